```python
import jax, jax.numpy as jnp
from jax import lax
import numpy as np

D_MODEL = 1024
BATCH = 4
SEQ = 4096
DEPTH = 1

CHUNK = 64
HEAD_DIM = 64
FOX_HEADS = 8
FOX_WIDTH = FOX_HEADS * HEAD_DIM
SGU_GROUPS = 8
SGU_GROUP_DIM = 64
SGU_WIDTH = SGU_GROUPS * SGU_GROUP_DIM
SGU_LEN = 128
Q_BLOCK = 128
N_BRANCH = 2
D_FF = 4 * D_MODEL
IN_COLS = 3 * FOX_WIDTH + FOX_HEADS + 2 * SGU_WIDTH + N_BRANCH * D_MODEL
EPS = 1e-6

kernel_name = "hybrid_fox_gmlp_gated_block"


def rmsnorm(x, g):
    xf = x.astype(jnp.float32)
    y = xf * lax.rsqrt(jnp.mean(xf * xf, axis=-1, keepdims=True) + EPS)
    return (y * g.astype(jnp.float32)).astype(x.dtype)


def layernorm(x, g, b):
    xf = x.astype(jnp.float32)
    mu = jnp.mean(xf, axis=-1, keepdims=True)
    xc = xf - mu
    y = xc * lax.rsqrt(jnp.mean(xc * xc, axis=-1, keepdims=True) + EPS)
    return (y * g.astype(jnp.float32) + b.astype(jnp.float32)).astype(x.dtype)


def forgetting_attention(q, k, v, logf):
    b, s, h, dh = q.shape
    c = jnp.cumsum(logf, axis=1)
    c_bh = jnp.transpose(c, (0, 2, 1))
    scale = dh ** -0.5
    outs = []
    for i in range(s // Q_BLOCK):
        s0 = i * Q_BLOCK
        e = s0 + Q_BLOCK
        logits = jnp.einsum('bqhd,bkhd->bhqk', q[:, s0:e], k[:, :e]).astype(jnp.float32) * scale
        logits = logits + c_bh[:, :, s0:e, None] - c_bh[:, :, None, :e]
        qpos = s0 + jnp.arange(Q_BLOCK)
        mask = qpos[:, None] >= jnp.arange(e)[None, :]
        logits = jnp.where(mask[None, None], logits, -jnp.inf)
        p = jax.nn.softmax(logits, axis=-1)
        outs.append(jnp.einsum('bhqk,bkhd->bqhd', p.astype(v.dtype), v[:, :e]))
    return jnp.concatenate(outs, axis=1)


def spatial_gating(u, v, ln_g, ln_b, w_s, b_s):
    b, s, _ = v.shape
    v = layernorm(v, ln_g, ln_b)
    vc = v.reshape(b, s // SGU_LEN, SGU_LEN, SGU_GROUPS, SGU_GROUP_DIM)
    chunk_id = jnp.arange(SGU_LEN) // CHUNK
    mask = chunk_id[None, :] <= chunk_id[:, None]
    w = jnp.where(mask[None], w_s, 0)
    mixed = jnp.einsum('gij,bnjgc->bnigc', w, vc) + jnp.transpose(b_s)[None, None, :, :, None]
    return u * mixed.reshape(b, s, SGU_WIDTH)


def setup_inputs(seed: int = 0) -> dict:
    key = jax.random.key(seed)
    ks = jax.random.split(key, 16)
    f32 = jnp.float32
    nrm = lambda k, shape, fan_in: jax.random.normal(k, shape, f32) * (fan_in ** -0.5)
    return {
        "x": jax.random.normal(ks[0], (BATCH, SEQ, D_MODEL), f32),
        "norm1_g": 1.0 + 0.02 * jax.random.normal(ks[1], (DEPTH, D_MODEL), f32),
        "w_in": nrm(ks[2], (DEPTH, D_MODEL, IN_COLS), D_MODEL),
        "b_f": jax.random.uniform(ks[3], (DEPTH, FOX_HEADS), f32, 1.0, 3.0),
        "ln_v_g": 1.0 + 0.02 * jax.random.normal(ks[4], (DEPTH, SGU_WIDTH), f32),
        "ln_v_b": 0.02 * jax.random.normal(ks[5], (DEPTH, SGU_WIDTH), f32),
        "w_sgu": nrm(ks[6], (DEPTH, SGU_GROUPS, SGU_LEN, SGU_LEN), SGU_LEN),
        "b_sgu": 1.0 + 0.1 * jax.random.normal(ks[7], (DEPTH, SGU_GROUPS, SGU_LEN), f32),
        "w_a": nrm(ks[8], (DEPTH, FOX_WIDTH, D_MODEL), FOX_WIDTH),
        "w_b": nrm(ks[9], (DEPTH, SGU_WIDTH, D_MODEL), SGU_WIDTH),
        "w_o": nrm(ks[10], (DEPTH, D_MODEL, D_MODEL), D_MODEL),
        "norm2_g": 1.0 + 0.02 * jax.random.normal(ks[11], (DEPTH, D_MODEL), f32),
        "w_up": nrm(ks[12], (DEPTH, D_MODEL, D_FF), D_MODEL),
        "w_down": nrm(ks[13], (DEPTH, D_FF, D_MODEL), D_FF),
        "normf_g": 1.0 + 0.02 * jax.random.normal(ks[14], (D_MODEL,), f32),
    }


def reference(x, norm1_g, w_in, b_f, ln_v_g, ln_v_b, w_sgu, b_sgu, w_a, w_b, w_o,
              norm2_g, w_up, w_down, normf_g):
    bsz, seq, _ = x.shape
    splits = np.cumsum([FOX_WIDTH, FOX_WIDTH, FOX_WIDTH, FOX_HEADS, SGU_WIDTH, SGU_WIDTH])
    for l in range(DEPTH):
        h = rmsnorm(x, norm1_g[l])
        p = h @ w_in[l]
        q, k, v, f_logit, u, sv, gates = jnp.split(p, splits, axis=-1)
        heads = (bsz, seq, FOX_HEADS, HEAD_DIM)
        logf = jax.nn.log_sigmoid((f_logit + b_f[l]).astype(jnp.float32))
        y_a = forgetting_attention(q.reshape(heads), k.reshape(heads), v.reshape(heads), logf)
        y_a = y_a.reshape(bsz, seq, FOX_WIDTH) @ w_a[l]
        y_b = spatial_gating(jax.nn.gelu(u), jax.nn.gelu(sv), ln_v_g[l], ln_v_b[l],
                             w_sgu[l], b_sgu[l]) @ w_b[l]
        g_a, g_b = jnp.split(gates, N_BRANCH, axis=-1)
        merged = jax.nn.sigmoid(g_a) * y_a + jax.nn.sigmoid(g_b) * y_b
        x = x + merged @ w_o[l]
        h2 = rmsnorm(x, norm2_g[l])
        x = x + jnp.square(jax.nn.relu(h2 @ w_up[l])) @ w_down[l]
    return rmsnorm(x, normf_g)
```

```python
import functools
import math

import jax
import jax.numpy as jnp
from jax import lax
from jax.experimental import pallas as pl
from jax.experimental.pallas import tpu as pltpu

D_MODEL = 1024
HEAD_DIM = 64
FOX_HEADS = 8
FOX_WIDTH = FOX_HEADS * HEAD_DIM
SGU_GROUPS = 8
SGU_WIDTH = 512
SGU_LEN = 128
CHUNK = 64
D_FF = 4 * D_MODEL
EPS = 1e-6

LANES = 128
TM = 512
TQ = 512
VMEM_LIMIT = 56 * 1024 * 1024

_Q0, _K0, _V0, _F0 = 0, FOX_WIDTH, 2 * FOX_WIDTH, 3 * FOX_WIDTH
_U0 = _F0 + LANES
_S0 = _U0 + SGU_WIDTH
_G0 = _S0 + SGU_WIDTH
_W_COLS = _G0 + 2 * D_MODEL

BF16 = jnp.bfloat16
F32 = jnp.float32


def _dot(a, b):
    return jnp.dot(a, b, preferred_element_type=F32)


def _gelu_tanh(x):
    c = math.sqrt(2.0 / math.pi)
    return 0.5 * x * (1.0 + jnp.tanh(c * (x + 0.044715 * (x * x * x))))


def _split3(x):
    hi = x.astype(BF16).astype(F32)
    r = x - hi
    mid = r.astype(BF16).astype(F32)
    return hi, mid, r - mid


def _pre_kernel(x_ref, g1_ref, w_ref, bf_ref, lng_ref, lnb_ref, wm_ref, bm_ref,
                qa_ref, ka_ref, vt_ref, sb_ref, gs_ref, carry_ref, *, tiles_per_seq):
    i = pl.program_id(0)

    @pl.when(i % tiles_per_seq == 0)
    def _():
        carry_ref[...] = jnp.zeros_like(carry_ref)

    x = x_ref[...]
    ms = jnp.mean(x * x, axis=-1, keepdims=True)
    h = (x * lax.rsqrt(ms + EPS) * g1_ref[...]).astype(BF16)

    z = _dot(h, w_ref[:, _F0:_F0 + LANES]) + bf_ref[...]
    logf = jnp.minimum(z, 0.0) - jnp.log(1.0 + jnp.exp(-jnp.abs(z)))
    r_i = lax.broadcasted_iota(jnp.int32, (TM, TM), 0)
    c_i = lax.broadcasted_iota(jnp.int32, (TM, TM), 1)
    tri = jnp.where(r_i >= c_i, 1.0, 0.0).astype(BF16)
    parts = _split3(logf)
    c = carry_ref[...] + sum(_dot(tri, p.astype(BF16)) for p in parts)
    carry_ref[...] = c[TM - 1:TM, :]

    lane = lax.broadcasted_iota(jnp.int32, (TM, LANES), 1)
    low = lane < HEAD_DIM
    trow = lax.broadcasted_iota(jnp.int32, (LANES, TM), 0)

    q_all = _dot(h, w_ref[:, _Q0:_Q0 + FOX_WIDTH]) * (HEAD_DIM ** -0.5)
    k_all = _dot(h, w_ref[:, _K0:_K0 + FOX_WIDTH])
    v_all = _dot(h, w_ref[:, _V0:_V0 + FOX_WIDTH])
    for jp in range(FOX_HEADS // 2):
        sl = slice(LANES * jp, LANES * (jp + 1))
        qc, kc, vc = q_all[:, sl], k_all[:, sl], v_all[:, sl]
        vt = vc.T
        for par in range(2):
            hd = 2 * jp + par
            a0 = HEAD_DIM if par == 0 else 0
            data = low if par == 0 else jnp.logical_not(low)
            in_aug = (lane >= a0) & (lane < a0 + 3)
            qa = jnp.where(data, qc, jnp.where(in_aug, 1.0, 0.0))
            nc = jnp.broadcast_to(-c[:, hd:hd + 1], (TM, LANES))
            hi, mid, lo = _split3(nc)
            aug = jnp.where(lane == a0, hi,
                            jnp.where(lane == a0 + 1, mid,
                                      jnp.where(lane == a0 + 2, lo, 0.0)))
            ka = jnp.where(data, kc, aug)
            qa_ref[0, hd] = qa.astype(BF16)
            ka_ref[0, hd] = ka.astype(BF16)
            if par == 0:
                vta = jnp.where(trow < HEAD_DIM, vt, jnp.where(trow == HEAD_DIM, 1.0, 0.0))
            else:
                vta = jnp.where(trow >= HEAD_DIM, vt, jnp.where(trow == 0, 1.0, 0.0))
            vt_ref[0, hd, 0] = vta.astype(BF16)

    u = _gelu_tanh(_dot(h, w_ref[:, _U0:_U0 + SGU_WIDTH]))
    sv = _gelu_tanh(_dot(h, w_ref[:, _S0:_S0 + SGU_WIDTH]))
    mu = jnp.mean(sv, axis=-1, keepdims=True)
    xc = sv - mu
    var = jnp.mean(xc * xc, axis=-1, keepdims=True)
    svn = (xc * lax.rsqrt(var + EPS) * lng_ref[...] + lnb_ref[...]).astype(BF16)

    n_win = TM // SGU_LEN
    wi = lax.broadcasted_iota(jnp.int32, (SGU_LEN, SGU_LEN), 0) // CHUNK
    wj = lax.broadcasted_iota(jnp.int32, (SGU_LEN, SGU_LEN), 1) // CHUNK
    wmask = wj <= wi
    lane_w = lax.broadcasted_iota(jnp.int32, (SGU_LEN, n_win * LANES), 1)
    low_w = (lane_w % LANES) < HEAD_DIM
    for jp in range(SGU_GROUPS // 2):
        sl = slice(LANES * jp, LANES * (jp + 1))
        chunk = svn[:, sl]
        rhs = jnp.concatenate(
            [chunk[SGU_LEN * w:SGU_LEN * (w + 1), :] for w in range(n_win)], axis=1)
        wa = jnp.where(wmask, wm_ref[2 * jp], 0.0).astype(BF16)
        wb = jnp.where(wmask, wm_ref[2 * jp + 1], 0.0).astype(BF16)
        mixed = jnp.where(low_w, _dot(wa, rhs), _dot(wb, rhs))
        bias = bm_ref[:, sl]
        mixed = mixed + jnp.concatenate([bias] * n_win, axis=1)
        mixed = jnp.concatenate(
            [mixed[:, LANES * w:LANES * (w + 1)] for w in range(n_win)], axis=0)
        sb_ref[:, sl] = (u[:, sl] * mixed).astype(BF16)

    gs_ref[...] = jax.nn.sigmoid(_dot(h, w_ref[:, _G0:_G0 + 2 * D_MODEL])).astype(BF16)


def _attn_kernel(qa_ref, ka_ref, vt_ref, o_ref, acc_ref, *, n_q):
    k_i = lax.broadcasted_iota(jnp.int32, (TQ, TQ), 0)
    q_i = lax.broadcasted_iota(jnp.int32, (TQ, TQ), 1)
    causal = k_i <= q_i
    orow = lax.broadcasted_iota(jnp.int32, (LANES, TQ), 0)
    neg_inf = jnp.full((1, TQ), -jnp.inf, F32)

    def q_body(qi, carry):
        q0 = pl.multiple_of(qi * TQ, TQ)
        qs = [qa_ref[0, hd, pl.ds(q0, TQ), :] for hd in range(2)]
        acc_ref[...] = jnp.zeros_like(acc_ref)

        def step(j, ms, masked):
            k0 = pl.multiple_of(j * TQ, TQ)
            out = []
            for hd in range(2):
                kj = ka_ref[0, hd, pl.ds(k0, TQ), :]
                st = lax.dot_general(kj, qs[hd], (((1,), (1,)), ((), ())),
                                     preferred_element_type=F32)
                if masked:
                    st = jnp.where(causal, st, -jnp.inf)
                m_new = jnp.maximum(ms[hd], jnp.max(st, axis=0, keepdims=True))
                alpha = jnp.exp(ms[hd] - m_new)
                p = jnp.exp(st - m_new).astype(BF16)
                pv = _dot(vt_ref[0, hd, j], p)
                acc_ref[hd] = acc_ref[hd] * alpha + pv
                out.append(m_new)
            return tuple(out)

        ms = lax.fori_loop(0, qi, lambda j, ms: step(j, ms, False), (neg_inf, neg_inf))
        step(qi, ms, True)

        acc_a, acc_b = acc_ref[0], acc_ref[1]
        o_t = jnp.where(orow < HEAD_DIM,
                        acc_a / acc_a[HEAD_DIM:HEAD_DIM + 1, :],
                        acc_b / acc_b[0:1, :])
        o_ref[0, pl.ds(q0, TQ), :] = o_t.T.astype(BF16)
        return carry

    lax.fori_loop(0, n_q, q_body, 0)


def _merge_kernel(x_ref, at_ref, sb_ref, gs_ref, wa_ref, wb_ref, wo_ref, o_ref):
    ya = _dot(at_ref[...], wa_ref[...])
    yb = _dot(sb_ref[...], wb_ref[...])
    merged = gs_ref[:, :D_MODEL].astype(F32) * ya + gs_ref[:, D_MODEL:].astype(F32) * yb
    o_ref[...] = x_ref[...] + _dot(merged.astype(BF16), wo_ref[...])


def _mlp_kernel(x_ref, g2_ref, wu_ref, wd_ref, gf_ref, o_ref, *, ff_chunk):
    x = x_ref[...]
    ms = jnp.mean(x * x, axis=-1, keepdims=True)
    h = (x * lax.rsqrt(ms + EPS) * g2_ref[...]).astype(BF16)
    y = x
    for c0 in range(0, D_FF, ff_chunk):
        a = jnp.maximum(_dot(h, wu_ref[:, c0:c0 + ff_chunk]), 0.0)
        y = y + _dot((a * a).astype(BF16), wd_ref[c0:c0 + ff_chunk, :])
    ms2 = jnp.mean(y * y, axis=-1, keepdims=True)
    o_ref[...] = y * lax.rsqrt(ms2 + EPS) * gf_ref[...]


def _params(n_axes):
    return pltpu.CompilerParams(dimension_semantics=("arbitrary",) * n_axes,
                                vmem_limit_bytes=VMEM_LIMIT)


def _const(shape):
    return pl.BlockSpec(shape, lambda *_: (0,) * len(shape))


def kernel(x, norm1_g, w_in, b_f, ln_v_g, ln_v_b, w_sgu, b_sgu, w_a, w_b, w_o,
           norm2_g, w_up, w_down, normf_g):
    bsz, seq, d = x.shape
    assert d == D_MODEL and seq % TM == 0 and seq % TQ == 0 and TM == TQ
    depth = norm1_g.shape[0]
    n_tok = bsz * seq
    n_tiles = n_tok // TM
    xt = x.reshape(n_tok, d)

    for l in range(depth):
        wl = w_in[l]
        w_f = jnp.pad(wl[:, _F0:_F0 + FOX_HEADS], ((0, 0), (0, LANES - FOX_HEADS)))
        w_all = jnp.concatenate(
            [wl[:, :_F0], w_f, wl[:, _F0 + FOX_HEADS:]], axis=1).astype(BF16)
        bf_pad = jnp.pad(b_f[l], (0, LANES - FOX_HEADS)).reshape(1, LANES)
        bm = jnp.repeat(jnp.transpose(b_sgu[l]), HEAD_DIM, axis=1)

        qa, ka, vt, sb, gs = pl.pallas_call(
            functools.partial(_pre_kernel, tiles_per_seq=seq // TM),
            grid=(n_tiles,),
            in_specs=[
                pl.BlockSpec((TM, d), lambda i: (i, 0)),
                _const((1, d)),
                _const((d, _W_COLS)),
                _const((1, LANES)),
                _const((1, SGU_WIDTH)),
                _const((1, SGU_WIDTH)),
                _const((SGU_GROUPS, SGU_LEN, SGU_LEN)),
                _const((SGU_LEN, SGU_WIDTH)),
            ],
            out_specs=[
                pl.BlockSpec((1, FOX_HEADS, TM, LANES),
                             lambda i: (i // (seq // TM), 0, i % (seq // TM), 0)),
                pl.BlockSpec((1, FOX_HEADS, TM, LANES),
                             lambda i: (i // (seq // TM), 0, i % (seq // TM), 0)),
                pl.BlockSpec((1, FOX_HEADS, 1, LANES, TM),
                             lambda i: (i // (seq // TM), 0, i % (seq // TM), 0, 0)),
                pl.BlockSpec((TM, SGU_WIDTH), lambda i: (i, 0)),
                pl.BlockSpec((TM, 2 * d), lambda i: (i, 0)),
            ],
            out_shape=[
                jax.ShapeDtypeStruct((bsz, FOX_HEADS, seq, LANES), BF16),
                jax.ShapeDtypeStruct((bsz, FOX_HEADS, seq, LANES), BF16),
                jax.ShapeDtypeStruct((bsz, FOX_HEADS, seq // TQ, LANES, TQ), BF16),
                jax.ShapeDtypeStruct((n_tok, SGU_WIDTH), BF16),
                jax.ShapeDtypeStruct((n_tok, 2 * d), BF16),
            ],
            scratch_shapes=[pltpu.VMEM((1, LANES), F32)],
            compiler_params=_params(1),
            name="pre",
        )(xt, norm1_g[l].reshape(1, d), w_all, bf_pad,
          ln_v_g[l].reshape(1, SGU_WIDTH), ln_v_b[l].reshape(1, SGU_WIDTH),
          w_sgu[l], bm)

        att = pl.pallas_call(
            functools.partial(_attn_kernel, n_q=seq // TQ),
            grid=(bsz, FOX_HEADS // 2),
            in_specs=[
                pl.BlockSpec((1, 2, seq, LANES), lambda b, p: (b, p, 0, 0)),
                pl.BlockSpec((1, 2, seq, LANES), lambda b, p: (b, p, 0, 0)),
                pl.BlockSpec((1, 2, seq // TQ, LANES, TQ), lambda b, p: (b, p, 0, 0, 0)),
            ],
            out_specs=pl.BlockSpec((1, seq, LANES), lambda b, p: (b, 0, p)),
            out_shape=jax.ShapeDtypeStruct((bsz, seq, FOX_WIDTH), BF16),
            scratch_shapes=[pltpu.VMEM((2, LANES, TQ), F32)],
            compiler_params=_params(2),
            name="attn",
        )(qa, ka, vt)

        xt = pl.pallas_call(
            _merge_kernel,
            grid=(n_tiles,),
            in_specs=[
                pl.BlockSpec((TM, d), lambda i: (i, 0)),
                pl.BlockSpec((TM, FOX_WIDTH), lambda i: (i, 0)),
                pl.BlockSpec((TM, SGU_WIDTH), lambda i: (i, 0)),
                pl.BlockSpec((TM, 2 * d), lambda i: (i, 0)),
                _const((FOX_WIDTH, d)),
                _const((SGU_WIDTH, d)),
                _const((d, d)),
            ],
            out_specs=pl.BlockSpec((TM, d), lambda i: (i, 0)),
            out_shape=jax.ShapeDtypeStruct((n_tok, d), F32),
            compiler_params=_params(1),
            name="merge",
        )(xt, att.reshape(n_tok, FOX_WIDTH), sb, gs,
          w_a[l].astype(BF16), w_b[l].astype(BF16), w_o[l].astype(BF16))

        last = l == depth - 1
        gf = normf_g if last else jnp.ones_like(normf_g)
        assert last, "final norm is fused into the last layer's mlp kernel"
        xt = pl.pallas_call(
            functools.partial(_mlp_kernel, ff_chunk=1024),
            grid=(n_tiles,),
            in_specs=[
                pl.BlockSpec((TM, d), lambda i: (i, 0)),
                _const((1, d)),
                _const((d, D_FF)),
                _const((D_FF, d)),
                _const((1, d)),
            ],
            out_specs=pl.BlockSpec((TM, d), lambda i: (i, 0)),
            out_shape=jax.ShapeDtypeStruct((n_tok, d), F32),
            compiler_params=_params(1),
            name="mlp",
        )(xt, norm2_g[l].reshape(1, d), w_up[l].astype(BF16), w_down[l].astype(BF16),
          gf.reshape(1, d))

    return xt.reshape(bsz, seq, d)
```

```python
import functools
import math

import jax
import jax.numpy as jnp
from jax import lax
from jax.experimental import pallas as pl
from jax.experimental.pallas import tpu as pltpu

D_MODEL = 1024
HEAD_DIM = 64
FOX_HEADS = 8
FOX_WIDTH = FOX_HEADS * HEAD_DIM
SGU_GROUPS = 8
SGU_WIDTH = 512
SGU_LEN = 128
CHUNK = 64
D_FF = 4 * D_MODEL
EPS = 1e-6

LANES = 128
TM = 512
TQ = 512
QC = 256
LOG2E = math.log2(math.e)
VMEM_LIMIT = 56 * 1024 * 1024

_Q0, _K0, _V0, _F0 = 0, FOX_WIDTH, 2 * FOX_WIDTH, 3 * FOX_WIDTH
_U0 = _F0 + LANES
_S0 = _U0 + SGU_WIDTH
_G0 = _S0 + SGU_WIDTH
_W_COLS = _G0 + 2 * D_MODEL

BF16 = jnp.bfloat16
F32 = jnp.float32


def _dot(a, b):
    return jnp.dot(a, b, preferred_element_type=F32)


def _gelu_tanh(x):
    c = math.sqrt(2.0 / math.pi)
    return 0.5 * x * (1.0 + jnp.tanh(c * (x + 0.044715 * (x * x * x))))


def _split3(x):
    hi = x.astype(BF16).astype(F32)
    r = x - hi
    mid = r.astype(BF16).astype(F32)
    return hi, mid, r - mid


def _pre_kernel(x_ref, g1_ref, w_ref, bf_ref, lng_ref, lnb_ref, wm_ref, bm_ref,
                qa_ref, ka_ref, vt_ref, sb_ref, gs_ref, carry_ref, *, tiles_per_seq):
    i = pl.program_id(0)

    @pl.when(i % tiles_per_seq == 0)
    def _():
        carry_ref[...] = jnp.zeros_like(carry_ref)

    x = x_ref[...]
    ms = jnp.mean(x * x, axis=-1, keepdims=True)
    h = (x * lax.rsqrt(ms + EPS) * g1_ref[...]).astype(BF16)

    z = _dot(h, w_ref[:, _F0:_F0 + LANES]) + bf_ref[...]
    logf = jnp.minimum(z, 0.0) - jnp.log(1.0 + jnp.exp(-jnp.abs(z)))
    r_i = lax.broadcasted_iota(jnp.int32, (TM, TM), 0)
    c_i = lax.broadcasted_iota(jnp.int32, (TM, TM), 1)
    tri = jnp.where(r_i >= c_i, 1.0, 0.0).astype(BF16)
    parts = _split3(logf)
    c = carry_ref[...] + sum(_dot(tri, p.astype(BF16)) for p in parts)
    carry_ref[...] = c[TM - 1:TM, :]

    lane = lax.broadcasted_iota(jnp.int32, (TM, LANES), 1)
    low = lane < HEAD_DIM
    trow = lax.broadcasted_iota(jnp.int32, (LANES, TM), 0)

    q_all = _dot(h, w_ref[:, _Q0:_Q0 + FOX_WIDTH]) * (HEAD_DIM ** -0.5 * LOG2E)
    k_all = _dot(h, w_ref[:, _K0:_K0 + FOX_WIDTH])
    v_all = _dot(h, w_ref[:, _V0:_V0 + FOX_WIDTH])
    for jp in range(FOX_HEADS // 2):
        sl = slice(LANES * jp, LANES * (jp + 1))
        qc, kc, vc = q_all[:, sl], k_all[:, sl], v_all[:, sl]
        vt = vc.T
        for par in range(2):
            hd = 2 * jp + par
            a0 = HEAD_DIM if par == 0 else 0
            data = low if par == 0 else jnp.logical_not(low)
            in_aug = (lane >= a0) & (lane < a0 + 3)
            qa = jnp.where(data, qc, jnp.where(in_aug, 1.0, 0.0))
            nc = jnp.broadcast_to(c[:, hd:hd + 1], (TM, LANES)) * (-LOG2E)
            hi, mid, lo = _split3(nc)
            aug = jnp.where(lane == a0, hi,
                            jnp.where(lane == a0 + 1, mid,
                                      jnp.where(lane == a0 + 2, lo, 0.0)))
            ka = jnp.where(data, kc, aug)
            qa_ref[0, hd] = qa.astype(BF16)
            ka_ref[0, hd] = ka.astype(BF16)
            if par == 0:
                vta = jnp.where(trow < HEAD_DIM, vt, jnp.where(trow == HEAD_DIM, 1.0, 0.0))
            else:
                vta = jnp.where(trow >= HEAD_DIM, vt, jnp.where(trow == 0, 1.0, 0.0))
            vt_ref[0, hd, 0] = vta.astype(BF16)

    u = _gelu_tanh(_dot(h, w_ref[:, _U0:_U0 + SGU_WIDTH]))
    sv = _gelu_tanh(_dot(h, w_ref[:, _S0:_S0 + SGU_WIDTH]))
    mu = jnp.mean(sv, axis=-1, keepdims=True)
    xc = sv - mu
    var = jnp.mean(xc * xc, axis=-1, keepdims=True)
    svn = (xc * lax.rsqrt(var + EPS) * lng_ref[...] + lnb_ref[...]).astype(BF16)

    n_win = TM // SGU_LEN
    wi = lax.broadcasted_iota(jnp.int32, (SGU_LEN, SGU_LEN), 0) // CHUNK
    wj = lax.broadcasted_iota(jnp.int32, (SGU_LEN, SGU_LEN), 1) // CHUNK
    wmask = wj <= wi
    lane_w = lax.broadcasted_iota(jnp.int32, (SGU_LEN, n_win * LANES), 1)
    low_w = (lane_w % LANES) < HEAD_DIM
    for jp in range(SGU_GROUPS // 2):
        sl = slice(LANES * jp, LANES * (jp + 1))
        chunk = svn[:, sl]
        rhs = jnp.concatenate(
            [chunk[SGU_LEN * w:SGU_LEN * (w + 1), :] for w in range(n_win)], axis=1)
        wa = jnp.where(wmask, wm_ref[2 * jp], 0.0).astype(BF16)
        wb = jnp.where(wmask, wm_ref[2 * jp + 1], 0.0).astype(BF16)
        mixed = jnp.where(low_w, _dot(wa, rhs), _dot(wb, rhs))
        bias = bm_ref[:, sl]
        mixed = mixed + jnp.concatenate([bias] * n_win, axis=1)
        mixed = jnp.concatenate(
            [mixed[:, LANES * w:LANES * (w + 1)] for w in range(n_win)], axis=0)
        sb_ref[:, sl] = (u[:, sl] * mixed).astype(BF16)

    gs_ref[...] = jax.nn.sigmoid(_dot(h, w_ref[:, _G0:_G0 + 2 * D_MODEL])).astype(BF16)


def _attn_kernel(qa_ref, ka_ref, vt_ref, o_ref, acc_ref, st_ref, *, n_q):
    k_i = lax.broadcasted_iota(jnp.int32, (QC, QC), 0)
    q_i = lax.broadcasted_iota(jnp.int32, (QC, QC), 1)
    causal = k_i <= q_i
    orow = lax.broadcasted_iota(jnp.int32, (LANES, QC), 0)
    neg_inf = jnp.full((1, QC), -jnp.inf, F32)
    halves = (slice(0, QC), slice(QC, TQ))

    def scores(ch, tile, blk, half):
        hd, c = divmod(ch, 2)
        q0 = pl.multiple_of(tile * TQ + c * QC, QC)
        k0 = pl.multiple_of(blk * TQ + half * QC, QC)
        st = lax.dot_general(ka_ref[0, hd, pl.ds(k0, QC), :], qa_ref[0, hd, pl.ds(q0, QC), :],
                             (((1,), (1,)), ((), ())), preferred_element_type=F32)
        st_ref[ch, halves[half], :] = st
        return jnp.max(st, axis=0, keepdims=True)

    def chain_step(ch, blk, m_old, cm, nxt_tile, nxt_blk, diag):
        hd, c = divmod(ch, 2)
        masked = [diag and c == 0, diag and c == 1]
        used = [True, not (diag and c == 0)]

        def load(half):
            st = st_ref[ch, halves[half], :]
            return jnp.where(causal, st, -jnp.inf) if masked[half] else st

        if diag:
            cm = load(0).max(axis=0, keepdims=True)
            if used[1]:
                cm = jnp.maximum(cm, load(1).max(axis=0, keepdims=True))
        m_new = jnp.maximum(m_old, cm)
        alpha = jnp.exp2(m_old - m_new)
        pv, cm_next = None, None
        for half in range(2):
            if used[half]:
                p = jnp.exp2(load(half) - m_new).astype(BF16)
            cm_h = scores(ch, nxt_tile, nxt_blk, half)
            cm_next = cm_h if cm_next is None else jnp.maximum(cm_next, cm_h)
            if used[half]:
                d = _dot(vt_ref[0, hd, blk, :, halves[half]], p)
                pv = d if pv is None else pv + d
        acc_ref[ch] = acc_ref[ch] * alpha + pv
        return m_new, cm_next

    def q_body(qi, cms):
        def full_step(j, state):
            ms, cms = state
            out = [chain_step(ch, j, ms[ch], cms[ch], qi, j + 1, False) for ch in range(4)]
            return tuple(o[0] for o in out), tuple(o[1] for o in out)

        ms, cms = lax.fori_loop(0, qi, full_step, ((neg_inf,) * 4, cms))
        nxt = jnp.minimum(qi + 1, n_q - 1)
        cms = tuple(chain_step(ch, qi, ms[ch], None, nxt, 0, True)[1] for ch in range(4))

        for c in range(2):
            acc_a, acc_b = acc_ref[c], acc_ref[2 + c]
            o_t = jnp.where(orow < HEAD_DIM,
                            acc_a / acc_a[HEAD_DIM:HEAD_DIM + 1, :],
                            acc_b / acc_b[0:1, :])
            q0 = pl.multiple_of(qi * TQ + c * QC, QC)
            o_ref[0, pl.ds(q0, QC), :] = o_t.T.astype(BF16)
        acc_ref[...] = jnp.zeros_like(acc_ref)
        return cms

    acc_ref[...] = jnp.zeros_like(acc_ref)
    first = tuple(jnp.maximum(scores(ch, 0, 0, 0), scores(ch, 0, 0, 1)) for ch in range(4))
    lax.fori_loop(0, n_q, q_body, first)


def _merge_kernel(x_ref, at_ref, sb_ref, gs_ref, wa_ref, wb_ref, wo_ref, o_ref):
    ya = _dot(at_ref[...], wa_ref[...])
    yb = _dot(sb_ref[...], wb_ref[...])
    merged = gs_ref[:, :D_MODEL].astype(F32) * ya + gs_ref[:, D_MODEL:].astype(F32) * yb
    o_ref[...] = x_ref[...] + _dot(merged.astype(BF16), wo_ref[...])


def _mlp_kernel(x_ref, g2_ref, wu_ref, wd_ref, gf_ref, o_ref, *, ff_chunk):
    x = x_ref[...]
    ms = jnp.mean(x * x, axis=-1, keepdims=True)
    h = (x * lax.rsqrt(ms + EPS) * g2_ref[...]).astype(BF16)
    y = x
    for c0 in range(0, D_FF, ff_chunk):
        a = jnp.maximum(_dot(h, wu_ref[:, c0:c0 + ff_chunk]), 0.0)
        y = y + _dot((a * a).astype(BF16), wd_ref[c0:c0 + ff_chunk, :])
    ms2 = jnp.mean(y * y, axis=-1, keepdims=True)
    o_ref[...] = y * lax.rsqrt(ms2 + EPS) * gf_ref[...]


def _params(n_axes):
    return pltpu.CompilerParams(dimension_semantics=("arbitrary",) * n_axes,
                                vmem_limit_bytes=VMEM_LIMIT)


def _const(shape):
    return pl.BlockSpec(shape, lambda *_: (0,) * len(shape))


def kernel(x, norm1_g, w_in, b_f, ln_v_g, ln_v_b, w_sgu, b_sgu, w_a, w_b, w_o,
           norm2_g, w_up, w_down, normf_g):
    bsz, seq, d = x.shape
    assert d == D_MODEL and seq % TM == 0 and seq % TQ == 0 and TM == TQ
    depth = norm1_g.shape[0]
    n_tok = bsz * seq
    n_tiles = n_tok // TM
    xt = x.reshape(n_tok, d)

    for l in range(depth):
        wl = w_in[l]
        w_f = jnp.pad(wl[:, _F0:_F0 + FOX_HEADS], ((0, 0), (0, LANES - FOX_HEADS)))
        w_all = jnp.concatenate(
            [wl[:, :_F0], w_f, wl[:, _F0 + FOX_HEADS:]], axis=1).astype(BF16)
        bf_pad = jnp.pad(b_f[l], (0, LANES - FOX_HEADS)).reshape(1, LANES)
        bm = jnp.repeat(jnp.transpose(b_sgu[l]), HEAD_DIM, axis=1)

        qa, ka, vt, sb, gs = pl.pallas_call(
            functools.partial(_pre_kernel, tiles_per_seq=seq // TM),
            grid=(n_tiles,),
            in_specs=[
                pl.BlockSpec((TM, d), lambda i: (i, 0)),
                _const((1, d)),
                _const((d, _W_COLS)),
                _const((1, LANES)),
                _const((1, SGU_WIDTH)),
                _const((1, SGU_WIDTH)),
                _const((SGU_GROUPS, SGU_LEN, SGU_LEN)),
                _const((SGU_LEN, SGU_WIDTH)),
            ],
            out_specs=[
                pl.BlockSpec((1, FOX_HEADS, TM, LANES),
                             lambda i: (i // (seq // TM), 0, i % (seq // TM), 0)),
                pl.BlockSpec((1, FOX_HEADS, TM, LANES),
                             lambda i: (i // (seq // TM), 0, i % (seq // TM), 0)),
                pl.BlockSpec((1, FOX_HEADS, 1, LANES, TM),
                             lambda i: (i // (seq // TM), 0, i % (seq // TM), 0, 0)),
                pl.BlockSpec((TM, SGU_WIDTH), lambda i: (i, 0)),
                pl.BlockSpec((TM, 2 * d), lambda i: (i, 0)),
            ],
            out_shape=[
                jax.ShapeDtypeStruct((bsz, FOX_HEADS, seq, LANES), BF16),
                jax.ShapeDtypeStruct((bsz, FOX_HEADS, seq, LANES), BF16),
                jax.ShapeDtypeStruct((bsz, FOX_HEADS, seq // TQ, LANES, TQ), BF16),
                jax.ShapeDtypeStruct((n_tok, SGU_WIDTH), BF16),
                jax.ShapeDtypeStruct((n_tok, 2 * d), BF16),
            ],
            scratch_shapes=[pltpu.VMEM((1, LANES), F32)],
            compiler_params=_params(1),
            name="pre",
        )(xt, norm1_g[l].reshape(1, d), w_all, bf_pad,
          ln_v_g[l].reshape(1, SGU_WIDTH), ln_v_b[l].reshape(1, SGU_WIDTH),
          w_sgu[l], bm)

        att = pl.pallas_call(
            functools.partial(_attn_kernel, n_q=seq // TQ),
            grid=(bsz, FOX_HEADS // 2),
            in_specs=[
                pl.BlockSpec((1, 2, seq, LANES), lambda b, p: (b, p, 0, 0)),
                pl.BlockSpec((1, 2, seq, LANES), lambda b, p: (b, p, 0, 0)),
                pl.BlockSpec((1, 2, seq // TQ, LANES, TQ), lambda b, p: (b, p, 0, 0, 0)),
            ],
            out_specs=pl.BlockSpec((1, seq, LANES), lambda b, p: (b, 0, p)),
            out_shape=jax.ShapeDtypeStruct((bsz, seq, FOX_WIDTH), BF16),
            scratch_shapes=[pltpu.VMEM((4, LANES, QC), F32), pltpu.VMEM((4, TQ, QC), F32)],
            compiler_params=_params(2),
            name="attn",
        )(qa, ka, vt)

        xt = pl.pallas_call(
            _merge_kernel,
            grid=(n_tiles,),
            in_specs=[
                pl.BlockSpec((TM, d), lambda i: (i, 0)),
                pl.BlockSpec((TM, FOX_WIDTH), lambda i: (i, 0)),
                pl.BlockSpec((TM, SGU_WIDTH), lambda i: (i, 0)),
                pl.BlockSpec((TM, 2 * d), lambda i: (i, 0)),
                _const((FOX_WIDTH, d)),
                _const((SGU_WIDTH, d)),
                _const((d, d)),
            ],
            out_specs=pl.BlockSpec((TM, d), lambda i: (i, 0)),
            out_shape=jax.ShapeDtypeStruct((n_tok, d), F32),
            compiler_params=_params(1),
            name="merge",
        )(xt, att.reshape(n_tok, FOX_WIDTH), sb, gs,
          w_a[l].astype(BF16), w_b[l].astype(BF16), w_o[l].astype(BF16))

        last = l == depth - 1
        gf = normf_g if last else jnp.ones_like(normf_g)
        assert last, "final norm is fused into the last layer's mlp kernel"
        xt = pl.pallas_call(
            functools.partial(_mlp_kernel, ff_chunk=1024),
            grid=(n_tiles,),
            in_specs=[
                pl.BlockSpec((TM, d), lambda i: (i, 0)),
                _const((1, d)),
                _const((d, D_FF)),
                _const((D_FF, d)),
                _const((1, d)),
            ],
            out_specs=pl.BlockSpec((TM, d), lambda i: (i, 0)),
            out_shape=jax.ShapeDtypeStruct((n_tok, d), F32),
            compiler_params=_params(1),
            name="mlp",
        )(xt, norm2_g[l].reshape(1, d), w_up[l].astype(BF16), w_down[l].astype(BF16),
          gf.reshape(1, d))

    return xt.reshape(bsz, seq, d)
```

```python
import functools
import math

import jax
import jax.numpy as jnp
from jax import lax
from jax.experimental import pallas as pl
from jax.experimental.pallas import tpu as pltpu

D_MODEL = 1024
HEAD_DIM = 64
FOX_HEADS = 8
FOX_WIDTH = FOX_HEADS * HEAD_DIM
SGU_GROUPS = 8
SGU_WIDTH = 512
SGU_LEN = 128
CHUNK = 64
D_FF = 4 * D_MODEL
EPS = 1e-6

LANES = 128
TM = 512
TQ = 512
QC = 256
SUB = 256
LOG2E = math.log2(math.e)
VMEM_LIMIT = 56 * 1024 * 1024

_Q0, _K0, _V0, _F0 = 0, FOX_WIDTH, 2 * FOX_WIDTH, 3 * FOX_WIDTH
_U0 = _F0 + LANES
_S0 = _U0 + SGU_WIDTH
_G0 = _S0 + SGU_WIDTH
_W_COLS = _G0 + 2 * D_MODEL

BF16 = jnp.bfloat16
F32 = jnp.float32


def _dot(a, b):
    return jnp.dot(a, b, preferred_element_type=F32)


def _gelu_tanh(x):
    c = math.sqrt(2.0 / math.pi)
    return 0.5 * x * (1.0 + jnp.tanh(c * (x + 0.044715 * (x * x * x))))


def _split3(x):
    hi = x.astype(BF16).astype(F32)
    r = x - hi
    mid = r.astype(BF16).astype(F32)
    return hi, mid, r - mid


def _sigmoid(x):
    return 0.5 * jnp.tanh(0.5 * x) + 0.5


def _pre_kernel(x_ref, g1_ref, w_ref, bf_ref, lng_ref, lnb_ref, wm_ref, bm_ref,
                qa_ref, ka_ref, vt_ref, sb_ref, gs_ref, carry_ref, *, tiles_per_seq):
    i = pl.program_id(0)

    @pl.when(i % tiles_per_seq == 0)
    def _():
        carry_ref[...] = jnp.zeros_like(carry_ref)

    r_i = lax.broadcasted_iota(jnp.int32, (SUB, SUB), 0)
    c_i = lax.broadcasted_iota(jnp.int32, (SUB, SUB), 1)
    tri = jnp.where(r_i >= c_i, 1.0, 0.0).astype(BF16)
    lane = lax.broadcasted_iota(jnp.int32, (SUB, LANES), 1)
    low = lane < HEAD_DIM
    trow = lax.broadcasted_iota(jnp.int32, (LANES, SUB), 0)
    n_win = SUB // SGU_LEN
    wi = lax.broadcasted_iota(jnp.int32, (SGU_LEN, SGU_LEN), 0) // CHUNK
    wj = lax.broadcasted_iota(jnp.int32, (SGU_LEN, SGU_LEN), 1) // CHUNK
    wmask = wj <= wi
    lane_w = lax.broadcasted_iota(jnp.int32, (SGU_LEN, n_win * LANES), 1)
    low_w = (lane_w % LANES) < HEAD_DIM

    def proj(h, c0, width):
        return _dot(h, w_ref[:, c0:c0 + width])

    carry = carry_ref[...]
    for sub in range(TM // SUB):
        rows = slice(sub * SUB, (sub + 1) * SUB)
        x = x_ref[rows, :]
        ms = jnp.mean(x * x, axis=-1, keepdims=True)
        h = (x * lax.rsqrt(ms + EPS) * g1_ref[...]).astype(BF16)

        z = proj(h, _F0, LANES) + bf_ref[...]
        logf = jnp.minimum(z, 0.0) - jnp.log(1.0 + jnp.exp(-jnp.abs(z)))
        hi, mid, lo = _split3(logf)
        part = jnp.where(lane < FOX_HEADS, hi, jnp.where(lane < 2 * FOX_HEADS, mid, lo))
        cl = _dot(tri, part.astype(BF16))
        c = carry + (cl + pltpu.roll(cl, LANES - FOX_HEADS, 1)
                     + pltpu.roll(cl, LANES - 2 * FOX_HEADS, 1))
        carry = c[SUB - 1:SUB, :]

        usv = proj(h, _U0, 2 * SGU_WIDTH)
        g_a = proj(h, _G0, D_MODEL)

        u = _gelu_tanh(usv[:, :SGU_WIDTH])
        sv = _gelu_tanh(usv[:, SGU_WIDTH:])
        mu = jnp.mean(sv, axis=-1, keepdims=True)
        xc = sv - mu
        var = jnp.mean(xc * xc, axis=-1, keepdims=True)
        svn = (xc * lax.rsqrt(var + EPS) * lng_ref[...] + lnb_ref[...]).astype(BF16)
        for jp in range(SGU_GROUPS // 2):
            sl = slice(LANES * jp, LANES * (jp + 1))
            chunk = svn[:, sl]
            rhs = jnp.concatenate(
                [chunk[SGU_LEN * w:SGU_LEN * (w + 1), :] for w in range(n_win)], axis=1)
            wa = jnp.where(wmask, wm_ref[2 * jp], 0.0).astype(BF16)
            wb = jnp.where(wmask, wm_ref[2 * jp + 1], 0.0).astype(BF16)
            mixed = jnp.where(low_w, _dot(wa, rhs), _dot(wb, rhs))
            mixed = mixed + jnp.concatenate([bm_ref[:, sl]] * n_win, axis=1)
            mixed = jnp.concatenate(
                [mixed[:, LANES * w:LANES * (w + 1)] for w in range(n_win)], axis=0)
            sb_ref[rows, sl] = (u[:, sl] * mixed).astype(BF16)
        gs_ref[rows, :D_MODEL] = _sigmoid(g_a).astype(BF16)

        g_b = proj(h, _G0 + D_MODEL, D_MODEL)
        k_all = proj(h, _K0, FOX_WIDTH)
        gs_ref[rows, D_MODEL:] = _sigmoid(g_b).astype(BF16)
        q_all = proj(h, _Q0, FOX_WIDTH) * (HEAD_DIM ** -0.5 * LOG2E)
        v_all = proj(h, _V0, FOX_WIDTH)
        for jp in range(FOX_HEADS // 2):
            sl = slice(LANES * jp, LANES * (jp + 1))
            qc, kc, vc = q_all[:, sl], k_all[:, sl], v_all[:, sl]
            vt = vc.T
            for par in range(2):
                hd = 2 * jp + par
                a0 = HEAD_DIM if par == 0 else 0
                data = low if par == 0 else jnp.logical_not(low)
                in_aug = (lane >= a0) & (lane < a0 + 3)
                qa = jnp.where(data, qc, jnp.where(in_aug, 1.0, 0.0))
                nc = jnp.broadcast_to(c[:, hd:hd + 1], (SUB, LANES)) * (-LOG2E)
                hi, mid, lo = _split3(nc)
                aug = jnp.where(lane == a0, hi,
                                jnp.where(lane == a0 + 1, mid,
                                          jnp.where(lane == a0 + 2, lo, 0.0)))
                ka = jnp.where(data, kc, aug)
                qa_ref[0, hd, rows, :] = qa.astype(BF16)
                ka_ref[0, hd, rows, :] = ka.astype(BF16)
                if par == 0:
                    vta = jnp.where(trow < HEAD_DIM, vt, jnp.where(trow == HEAD_DIM, 1.0, 0.0))
                else:
                    vta = jnp.where(trow >= HEAD_DIM, vt, jnp.where(trow == 0, 1.0, 0.0))
                vt_ref[0, hd, 0, :, rows] = vta.astype(BF16)
    carry_ref[...] = carry


def _attn_kernel(qa_ref, ka_ref, vt_ref, o_ref, acc_ref, st_ref, *, n_q):
    k_i = lax.broadcasted_iota(jnp.int32, (QC, QC), 0)
    q_i = lax.broadcasted_iota(jnp.int32, (QC, QC), 1)
    causal = k_i <= q_i
    orow = lax.broadcasted_iota(jnp.int32, (LANES, QC), 0)
    neg_inf = jnp.full((1, QC), -jnp.inf, F32)
    halves = (slice(0, QC), slice(QC, TQ))

    def scores(ch, tile, blk, half):
        hd, c = divmod(ch, 2)
        q0 = pl.multiple_of(tile * TQ + c * QC, QC)
        k0 = pl.multiple_of(blk * TQ + half * QC, QC)
        st = lax.dot_general(ka_ref[0, hd, pl.ds(k0, QC), :], qa_ref[0, hd, pl.ds(q0, QC), :],
                             (((1,), (1,)), ((), ())), preferred_element_type=F32)
        st_ref[ch, halves[half], :] = st
        return jnp.max(st, axis=0, keepdims=True)

    def chain_step(ch, blk, m_old, cm, nxt_tile, nxt_blk, diag):
        hd, c = divmod(ch, 2)
        masked = [diag and c == 0, diag and c == 1]
        used = [True, not (diag and c == 0)]

        def load(half):
            st = st_ref[ch, halves[half], :]
            return jnp.where(causal, st, -jnp.inf) if masked[half] else st

        if diag:
            cm = load(0).max(axis=0, keepdims=True)
            if used[1]:
                cm = jnp.maximum(cm, load(1).max(axis=0, keepdims=True))
        m_new = jnp.maximum(m_old, cm)
        alpha = jnp.exp2(m_old - m_new)
        pv, cm_next = None, None
        for half in range(2):
            if used[half]:
                p = jnp.exp2(load(half) - m_new).astype(BF16)
            cm_h = scores(ch, nxt_tile, nxt_blk, half)
            cm_next = cm_h if cm_next is None else jnp.maximum(cm_next, cm_h)
            if used[half]:
                d = _dot(vt_ref[0, hd, blk, :, halves[half]], p)
                pv = d if pv is None else pv + d
        acc_ref[ch] = acc_ref[ch] * alpha + pv
        return m_new, cm_next

    def q_body(qi, cms):
        def full_step(j, state):
            ms, cms = state
            out = [chain_step(ch, j, ms[ch], cms[ch], qi, j + 1, False) for ch in range(4)]
            return tuple(o[0] for o in out), tuple(o[1] for o in out)

        ms, cms = lax.fori_loop(0, qi, full_step, ((neg_inf,) * 4, cms))
        nxt = jnp.minimum(qi + 1, n_q - 1)
        cms = tuple(chain_step(ch, qi, ms[ch], None, nxt, 0, True)[1] for ch in range(4))

        for c in range(2):
            acc_a, acc_b = acc_ref[c], acc_ref[2 + c]
            o_t = jnp.where(orow < HEAD_DIM,
                            acc_a / acc_a[HEAD_DIM:HEAD_DIM + 1, :],
                            acc_b / acc_b[0:1, :])
            q0 = pl.multiple_of(qi * TQ + c * QC, QC)
            o_ref[0, pl.ds(q0, QC), :] = o_t.T.astype(BF16)
        acc_ref[...] = jnp.zeros_like(acc_ref)
        return cms

    acc_ref[...] = jnp.zeros_like(acc_ref)
    first = tuple(jnp.maximum(scores(ch, 0, 0, 0), scores(ch, 0, 0, 1)) for ch in range(4))
    lax.fori_loop(0, n_q, q_body, first)


def _post_kernel(x_ref, at_ref, sb_ref, gs_ref, wa_ref, wb_ref, wo_ref, g2_ref, wu_ref,
                 wd_ref, gf_ref, o_ref, *, ff_chunk):
    subs = [slice(s * SUB, (s + 1) * SUB) for s in range(TM // SUB)]
    merged = []
    for rows in subs:
        ya = _dot(at_ref[rows, :], wa_ref[...])
        yb = _dot(sb_ref[rows, :], wb_ref[...])
        merged.append((gs_ref[rows, :D_MODEL].astype(F32) * ya
                       + gs_ref[rows, D_MODEL:].astype(F32) * yb).astype(BF16))
    ys = [x_ref[rows, :] + _dot(m, wo_ref[...]) for rows, m in zip(subs, merged)]
    hs = []
    for y in ys:
        ms = jnp.mean(y * y, axis=-1, keepdims=True)
        hs.append((y * lax.rsqrt(ms + EPS) * g2_ref[...]).astype(BF16))
    for c0 in range(0, D_FF, ff_chunk):
        acts = []
        for h in hs:
            a = jnp.maximum(_dot(h, wu_ref[:, c0:c0 + ff_chunk]), 0.0)
            acts.append((a * a).astype(BF16))
        ys = [y + _dot(a, wd_ref[c0:c0 + ff_chunk, :]) for y, a in zip(ys, acts)]
    for rows, y in zip(subs, ys):
        ms = jnp.mean(y * y, axis=-1, keepdims=True)
        o_ref[rows, :] = y * lax.rsqrt(ms + EPS) * gf_ref[...]


def _params(n_axes):
    return pltpu.CompilerParams(dimension_semantics=("arbitrary",) * n_axes,
                                vmem_limit_bytes=VMEM_LIMIT)


def _const(shape):
    return pl.BlockSpec(shape, lambda *_: (0,) * len(shape), pipeline_mode=pl.Buffered(1))


def kernel(x, norm1_g, w_in, b_f, ln_v_g, ln_v_b, w_sgu, b_sgu, w_a, w_b, w_o,
           norm2_g, w_up, w_down, normf_g):
    bsz, seq, d = x.shape
    assert d == D_MODEL and seq % TM == 0 and TM == TQ and TM % SUB == 0
    assert norm1_g.shape[0] == 1, "single-layer block"
    n_tok = bsz * seq
    n_tiles = n_tok // TM
    tiles_per_seq = seq // TM
    xt = x.reshape(n_tok, d)

    wl = w_in[0]
    n_rep = 3
    w_f = jnp.pad(jnp.tile(wl[:, _F0:_F0 + FOX_HEADS], (1, n_rep)),
                  ((0, 0), (0, LANES - n_rep * FOX_HEADS)))
    w_all = jnp.concatenate([wl[:, :_F0], w_f, wl[:, _F0 + FOX_HEADS:]], axis=1).astype(BF16)
    bf_pad = jnp.pad(jnp.tile(b_f[0], n_rep), (0, LANES - n_rep * FOX_HEADS)).reshape(1, LANES)
    bm = jnp.repeat(jnp.transpose(b_sgu[0]), HEAD_DIM, axis=1)

    head_map = lambda i: (i // tiles_per_seq, 0, i % tiles_per_seq, 0)
    qa, ka, vt, sb, gs = pl.pallas_call(
        functools.partial(_pre_kernel, tiles_per_seq=tiles_per_seq),
        grid=(n_tiles,),
        in_specs=[
            pl.BlockSpec((TM, d), lambda i: (i, 0)),
            _const((1, d)),
            _const((d, _W_COLS)),
            _const((1, LANES)),
            _const((1, SGU_WIDTH)),
            _const((1, SGU_WIDTH)),
            _const((SGU_GROUPS, SGU_LEN, SGU_LEN)),
            _const((SGU_LEN, SGU_WIDTH)),
        ],
        out_specs=[
            pl.BlockSpec((1, FOX_HEADS, TM, LANES), head_map),
            pl.BlockSpec((1, FOX_HEADS, TM, LANES), head_map),
            pl.BlockSpec((1, FOX_HEADS, 1, LANES, TM), lambda i: head_map(i) + (0,)),
            pl.BlockSpec((TM, SGU_WIDTH), lambda i: (i, 0)),
            pl.BlockSpec((TM, 2 * d), lambda i: (i, 0)),
        ],
        out_shape=[
            jax.ShapeDtypeStruct((bsz, FOX_HEADS, seq, LANES), BF16),
            jax.ShapeDtypeStruct((bsz, FOX_HEADS, seq, LANES), BF16),
            jax.ShapeDtypeStruct((bsz, FOX_HEADS, seq // TQ, LANES, TQ), BF16),
            jax.ShapeDtypeStruct((n_tok, SGU_WIDTH), BF16),
            jax.ShapeDtypeStruct((n_tok, 2 * d), BF16),
        ],
        scratch_shapes=[pltpu.VMEM((1, LANES), F32)],
        compiler_params=_params(1),
        name="pre",
    )(xt, norm1_g.reshape(1, d), w_all, bf_pad, ln_v_g.reshape(1, SGU_WIDTH),
      ln_v_b.reshape(1, SGU_WIDTH), w_sgu[0], bm)

    att = pl.pallas_call(
        functools.partial(_attn_kernel, n_q=seq // TQ),
        grid=(bsz, FOX_HEADS // 2),
        in_specs=[
            pl.BlockSpec((1, 2, seq, LANES), lambda b, p: (b, p, 0, 0)),
            pl.BlockSpec((1, 2, seq, LANES), lambda b, p: (b, p, 0, 0)),
            pl.BlockSpec((1, 2, seq // TQ, LANES, TQ), lambda b, p: (b, p, 0, 0, 0)),
        ],
        out_specs=pl.BlockSpec((1, seq, LANES), lambda b, p: (b, 0, p)),
        out_shape=jax.ShapeDtypeStruct((bsz, seq, FOX_WIDTH), BF16),
        scratch_shapes=[pltpu.VMEM((4, LANES, QC), F32), pltpu.VMEM((4, TQ, QC), F32)],
        compiler_params=_params(2),
        name="attn",
    )(qa, ka, vt)

    out = pl.pallas_call(
        functools.partial(_post_kernel, ff_chunk=1024),
        grid=(n_tiles,),
        in_specs=[
            pl.BlockSpec((TM, d), lambda i: (i, 0)),
            pl.BlockSpec((TM, FOX_WIDTH), lambda i: (i, 0)),
            pl.BlockSpec((TM, SGU_WIDTH), lambda i: (i, 0)),
            pl.BlockSpec((TM, 2 * d), lambda i: (i, 0)),
            _const((FOX_WIDTH, d)),
            _const((SGU_WIDTH, d)),
            _const((d, d)),
            _const((1, d)),
            _const((d, D_FF)),
            _const((D_FF, d)),
            _const((1, d)),
        ],
        out_specs=pl.BlockSpec((TM, d), lambda i: (i, 0)),
        out_shape=jax.ShapeDtypeStruct((n_tok, d), F32),
        compiler_params=_params(1),
        name="post",
    )(xt, att.reshape(n_tok, FOX_WIDTH), sb, gs, w_a[0].astype(BF16), w_b[0].astype(BF16),
      w_o[0].astype(BF16), norm2_g.reshape(1, d), w_up[0].astype(BF16),
      w_down[0].astype(BF16), normf_g.reshape(1, d))

    return out.reshape(bsz, seq, d)
```

```python
import functools
import math

import jax
import jax.numpy as jnp
from jax import lax
from jax.experimental import pallas as pl
from jax.experimental.pallas import tpu as pltpu

D_MODEL = 1024
HEAD_DIM = 64
FOX_HEADS = 8
FOX_WIDTH = FOX_HEADS * HEAD_DIM
SGU_GROUPS = 8
SGU_WIDTH = 512
SGU_LEN = 128
CHUNK = 64
D_FF = 4 * D_MODEL
EPS = 1e-6

LANES = 128
TM = 512
TQ = 512
QC = 256
SUB = 256
LOG2E = math.log2(math.e)
VMEM_LIMIT = 56 * 1024 * 1024

_Q0, _K0, _V0, _F0 = 0, FOX_WIDTH, 2 * FOX_WIDTH, 3 * FOX_WIDTH
_U0 = _F0 + LANES
_S0 = _U0 + SGU_WIDTH
_G0 = _S0 + SGU_WIDTH
_W_COLS = _G0 + 2 * D_MODEL

BF16 = jnp.bfloat16
F32 = jnp.float32


def _dot(a, b):
    return jnp.dot(a, b, preferred_element_type=F32)


def _gelu_tanh(x):
    c = math.sqrt(2.0 / math.pi)
    return 0.5 * x * (1.0 + jnp.tanh(c * (x + 0.044715 * (x * x * x))))


def _split3(x):
    hi = x.astype(BF16).astype(F32)
    r = x - hi
    mid = r.astype(BF16).astype(F32)
    return hi, mid, r - mid


def _sigmoid(x):
    return 0.5 * jnp.tanh(0.5 * x) + 0.5


def _pre_kernel(x_ref, g1_ref, w_ref, bf_ref, lng_ref, lnb_ref, wm_ref, bm_ref,
                qa_ref, ka_ref, vt_ref, sb_ref, gs_ref, carry_ref, *, tiles_per_seq):
    i = pl.program_id(0)

    @pl.when(i % tiles_per_seq == 0)
    def _():
        carry_ref[...] = jnp.zeros_like(carry_ref)

    r_i = lax.broadcasted_iota(jnp.int32, (SUB, SUB), 0)
    c_i = lax.broadcasted_iota(jnp.int32, (SUB, SUB), 1)
    tri = jnp.where(r_i >= c_i, 1.0, 0.0).astype(BF16)
    lane = lax.broadcasted_iota(jnp.int32, (SUB, LANES), 1)
    low = lane < HEAD_DIM
    trow = lax.broadcasted_iota(jnp.int32, (LANES, SUB), 0)
    n_win = SUB // SGU_LEN
    wi = lax.broadcasted_iota(jnp.int32, (SGU_LEN, SGU_LEN), 0) // CHUNK
    wj = lax.broadcasted_iota(jnp.int32, (SGU_LEN, SGU_LEN), 1) // CHUNK
    wmask = wj <= wi
    lane_w = lax.broadcasted_iota(jnp.int32, (SGU_LEN, n_win * LANES), 1)
    low_w = (lane_w % LANES) < HEAD_DIM

    def proj(h, c0, width):
        return _dot(h, w_ref[:, c0:c0 + width])

    carry = carry_ref[...]
    for sub in range(TM // SUB):
        rows = slice(sub * SUB, (sub + 1) * SUB)
        x = x_ref[rows, :]
        ms = jnp.mean(x * x, axis=-1, keepdims=True)
        h = (x * lax.rsqrt(ms + EPS) * g1_ref[...]).astype(BF16)

        z = proj(h, _F0, LANES) + bf_ref[...]
        logf = jnp.minimum(z, 0.0) - jnp.log(1.0 + jnp.exp(-jnp.abs(z)))
        hi, mid, lo = _split3(logf)
        part = jnp.where(lane < FOX_HEADS, hi, jnp.where(lane < 2 * FOX_HEADS, mid, lo))
        cl = _dot(tri, part.astype(BF16))
        c = carry + (cl + pltpu.roll(cl, LANES - FOX_HEADS, 1)
                     + pltpu.roll(cl, LANES - 2 * FOX_HEADS, 1))
        carry = c[SUB - 1:SUB, :]

        usv = proj(h, _U0, 2 * SGU_WIDTH)
        g_a = proj(h, _G0, D_MODEL)

        u = _gelu_tanh(usv[:, :SGU_WIDTH])
        sv = _gelu_tanh(usv[:, SGU_WIDTH:])
        mu = jnp.mean(sv, axis=-1, keepdims=True)
        xc = sv - mu
        var = jnp.mean(xc * xc, axis=-1, keepdims=True)
        svn = (xc * lax.rsqrt(var + EPS) * lng_ref[...] + lnb_ref[...]).astype(BF16)
        for jp in range(SGU_GROUPS // 2):
            sl = slice(LANES * jp, LANES * (jp + 1))
            chunk = svn[:, sl]
            rhs = jnp.concatenate(
                [chunk[SGU_LEN * w:SGU_LEN * (w + 1), :] for w in range(n_win)], axis=1)
            wa = jnp.where(wmask, wm_ref[2 * jp], 0.0).astype(BF16)
            wb = jnp.where(wmask, wm_ref[2 * jp + 1], 0.0).astype(BF16)
            mixed = jnp.where(low_w, _dot(wa, rhs), _dot(wb, rhs))
            mixed = mixed + jnp.concatenate([bm_ref[:, sl]] * n_win, axis=1)
            mixed = jnp.concatenate(
                [mixed[:, LANES * w:LANES * (w + 1)] for w in range(n_win)], axis=0)
            sb_ref[rows, sl] = (u[:, sl] * mixed).astype(BF16)
        gs_ref[rows, :D_MODEL] = _sigmoid(g_a).astype(BF16)

        g_b = proj(h, _G0 + D_MODEL, D_MODEL)
        k_all = proj(h, _K0, FOX_WIDTH)
        gs_ref[rows, D_MODEL:] = _sigmoid(g_b).astype(BF16)
        q_all = proj(h, _Q0, FOX_WIDTH) * (HEAD_DIM ** -0.5 * LOG2E)
        v_all = proj(h, _V0, FOX_WIDTH)
        for jp in range(FOX_HEADS // 2):
            sl = slice(LANES * jp, LANES * (jp + 1))
            qc, kc, vc = q_all[:, sl], k_all[:, sl], v_all[:, sl]
            vt = vc.T
            for par in range(2):
                hd = 2 * jp + par
                a0 = HEAD_DIM if par == 0 else 0
                data = low if par == 0 else jnp.logical_not(low)
                in_aug = (lane >= a0) & (lane < a0 + 3)
                qa = jnp.where(data, qc, jnp.where(in_aug, 1.0, 0.0))
                nc = jnp.broadcast_to(c[:, hd:hd + 1], (SUB, LANES)) * (-LOG2E)
                hi, mid, lo = _split3(nc)
                aug = jnp.where(lane == a0, hi,
                                jnp.where(lane == a0 + 1, mid,
                                          jnp.where(lane == a0 + 2, lo, 0.0)))
                ka = jnp.where(data, kc, aug)
                qa_ref[0, hd, rows, :] = qa.astype(BF16)
                ka_ref[0, hd, rows, :] = ka.astype(BF16)
                if par == 0:
                    vta = jnp.where(trow < HEAD_DIM, vt, jnp.where(trow == HEAD_DIM, 1.0, 0.0))
                else:
                    vta = jnp.where(trow >= HEAD_DIM, vt, jnp.where(trow == 0, 1.0, 0.0))
                vt_ref[0, hd, 0, :, rows] = vta.astype(BF16)
    carry_ref[...] = carry


def _attn_kernel(qa_ref, ka_ref, vt_ref, o_ref, acc_ref, st_ref, *, n_q):
    k_i = lax.broadcasted_iota(jnp.int32, (QC, QC), 0)
    q_i = lax.broadcasted_iota(jnp.int32, (QC, QC), 1)
    causal = k_i <= q_i
    orow = lax.broadcasted_iota(jnp.int32, (LANES, QC), 0)
    neg_inf = jnp.full((1, QC), -jnp.inf, F32)
    halves = (slice(0, QC), slice(QC, TQ))

    def scores(ch, tile, blk, half):
        hd, c = divmod(ch, 2)
        q0 = tile * TQ + c * QC
        k0 = blk * TQ + half * QC
        st = lax.dot_general(ka_ref[0, hd, pl.ds(k0, QC), :], qa_ref[0, hd, pl.ds(q0, QC), :],
                             (((1,), (1,)), ((), ())), preferred_element_type=F32)
        st_ref[ch, halves[half], :] = st
        return jnp.max(st, axis=0, keepdims=True)

    def chain_step(ch, blk, m_old, cm, nxt_tile, nxt_blk, diag):
        hd, c = divmod(ch, 2)
        masked = [diag and c == 0, diag and c == 1]
        used = [True, not (diag and c == 0)]

        def load(half):
            st = st_ref[ch, halves[half], :]
            return jnp.where(causal, st, -jnp.inf) if masked[half] else st

        if diag:
            cm = load(0).max(axis=0, keepdims=True)
            if used[1]:
                cm = jnp.maximum(cm, load(1).max(axis=0, keepdims=True))
        m_new = jnp.maximum(m_old, cm)
        alpha = jnp.exp2(m_old - m_new)
        pv, cm_next = None, None
        for half in range(2):
            if used[half]:
                p = jnp.exp2(load(half) - m_new).astype(BF16)
            cm_h = scores(ch, nxt_tile, nxt_blk, half)
            cm_next = cm_h if cm_next is None else jnp.maximum(cm_next, cm_h)
            if used[half]:
                d = _dot(vt_ref[0, hd, blk, :, halves[half]], p)
                pv = d if pv is None else pv + d
        acc_ref[ch] = acc_ref[ch] * alpha + pv
        return m_new, cm_next

    acc_ref[...] = jnp.zeros_like(acc_ref)
    cms = [jnp.maximum(scores(ch, 0, 0, 0), scores(ch, 0, 0, 1)) for ch in range(4)]
    for qi in range(n_q):
        ms = [neg_inf] * 4
        for j in range(qi):
            for ch in range(4):
                ms[ch], cms[ch] = chain_step(ch, j, ms[ch], cms[ch], qi, j + 1, False)
        nxt = min(qi + 1, n_q - 1)
        for ch in range(4):
            cms[ch] = chain_step(ch, qi, ms[ch], None, nxt, 0, True)[1]

        for c in range(2):
            acc_a, acc_b = acc_ref[c], acc_ref[2 + c]
            o_t = jnp.where(orow < HEAD_DIM,
                            acc_a / acc_a[HEAD_DIM:HEAD_DIM + 1, :],
                            acc_b / acc_b[0:1, :])
            o_ref[0, pl.ds(qi * TQ + c * QC, QC), :] = o_t.T.astype(BF16)
        acc_ref[...] = jnp.zeros_like(acc_ref)


def _post_kernel(x_ref, at_ref, sb_ref, gs_ref, wa_ref, wb_ref, wo_ref, g2_ref, wu_ref,
                 wd_ref, gf_ref, o_ref, *, ff_chunk):
    subs = [slice(s * SUB, (s + 1) * SUB) for s in range(TM // SUB)]
    merged = []
    for rows in subs:
        ya = _dot(at_ref[rows, :], wa_ref[...])
        yb = _dot(sb_ref[rows, :], wb_ref[...])
        merged.append((gs_ref[rows, :D_MODEL].astype(F32) * ya
                       + gs_ref[rows, D_MODEL:].astype(F32) * yb).astype(BF16))
    ys = [x_ref[rows, :] + _dot(m, wo_ref[...]) for rows, m in zip(subs, merged)]
    hs = []
    for y in ys:
        ms = jnp.mean(y * y, axis=-1, keepdims=True)
        hs.append((y * lax.rsqrt(ms + EPS) * g2_ref[...]).astype(BF16))
    for c0 in range(0, D_FF, ff_chunk):
        acts = []
        for h in hs:
            a = jnp.maximum(_dot(h, wu_ref[:, c0:c0 + ff_chunk]), 0.0)
            acts.append((a * a).astype(BF16))
        ys = [y + _dot(a, wd_ref[c0:c0 + ff_chunk, :]) for y, a in zip(ys, acts)]
    for rows, y in zip(subs, ys):
        ms = jnp.mean(y * y, axis=-1, keepdims=True)
        o_ref[rows, :] = y * lax.rsqrt(ms + EPS) * gf_ref[...]


def _params(n_axes):
    return pltpu.CompilerParams(dimension_semantics=("arbitrary",) * n_axes,
                                vmem_limit_bytes=VMEM_LIMIT)


def _const(shape):
    return pl.BlockSpec(shape, lambda *_: (0,) * len(shape), pipeline_mode=pl.Buffered(1))


def kernel(x, norm1_g, w_in, b_f, ln_v_g, ln_v_b, w_sgu, b_sgu, w_a, w_b, w_o,
           norm2_g, w_up, w_down, normf_g):
    bsz, seq, d = x.shape
    assert d == D_MODEL and seq % TM == 0 and TM == TQ and TM % SUB == 0
    assert norm1_g.shape[0] == 1, "single-layer block"
    n_tok = bsz * seq
    n_tiles = n_tok // TM
    tiles_per_seq = seq // TM
    xt = x.reshape(n_tok, d)

    wl = w_in[0]
    n_rep = 3
    w_f = jnp.pad(jnp.tile(wl[:, _F0:_F0 + FOX_HEADS], (1, n_rep)),
                  ((0, 0), (0, LANES - n_rep * FOX_HEADS)))
    w_all = jnp.concatenate([wl[:, :_F0], w_f, wl[:, _F0 + FOX_HEADS:]], axis=1).astype(BF16)
    bf_pad = jnp.pad(jnp.tile(b_f[0], n_rep), (0, LANES - n_rep * FOX_HEADS)).reshape(1, LANES)
    bm = jnp.repeat(jnp.transpose(b_sgu[0]), HEAD_DIM, axis=1)

    head_map = lambda i: (i // tiles_per_seq, 0, i % tiles_per_seq, 0)
    qa, ka, vt, sb, gs = pl.pallas_call(
        functools.partial(_pre_kernel, tiles_per_seq=tiles_per_seq),
        grid=(n_tiles,),
        in_specs=[
            pl.BlockSpec((TM, d), lambda i: (i, 0)),
            _const((1, d)),
            _const((d, _W_COLS)),
            _const((1, LANES)),
            _const((1, SGU_WIDTH)),
            _const((1, SGU_WIDTH)),
            _const((SGU_GROUPS, SGU_LEN, SGU_LEN)),
            _const((SGU_LEN, SGU_WIDTH)),
        ],
        out_specs=[
            pl.BlockSpec((1, FOX_HEADS, TM, LANES), head_map),
            pl.BlockSpec((1, FOX_HEADS, TM, LANES), head_map),
            pl.BlockSpec((1, FOX_HEADS, 1, LANES, TM), lambda i: head_map(i) + (0,)),
            pl.BlockSpec((TM, SGU_WIDTH), lambda i: (i, 0)),
            pl.BlockSpec((TM, 2 * d), lambda i: (i, 0)),
        ],
        out_shape=[
            jax.ShapeDtypeStruct((bsz, FOX_HEADS, seq, LANES), BF16),
            jax.ShapeDtypeStruct((bsz, FOX_HEADS, seq, LANES), BF16),
            jax.ShapeDtypeStruct((bsz, FOX_HEADS, seq // TQ, LANES, TQ), BF16),
            jax.ShapeDtypeStruct((n_tok, SGU_WIDTH), BF16),
            jax.ShapeDtypeStruct((n_tok, 2 * d), BF16),
        ],
        scratch_shapes=[pltpu.VMEM((1, LANES), F32)],
        compiler_params=_params(1),
        name="pre",
    )(xt, norm1_g.reshape(1, d), w_all, bf_pad, ln_v_g.reshape(1, SGU_WIDTH),
      ln_v_b.reshape(1, SGU_WIDTH), w_sgu[0], bm)

    att = pl.pallas_call(
        functools.partial(_attn_kernel, n_q=seq // TQ),
        grid=(bsz, FOX_HEADS // 2),
        in_specs=[
            pl.BlockSpec((1, 2, seq, LANES), lambda b, p: (b, p, 0, 0)),
            pl.BlockSpec((1, 2, seq, LANES), lambda b, p: (b, p, 0, 0)),
            pl.BlockSpec((1, 2, seq // TQ, LANES, TQ), lambda b, p: (b, p, 0, 0, 0)),
        ],
        out_specs=pl.BlockSpec((1, seq, LANES), lambda b, p: (b, 0, p)),
        out_shape=jax.ShapeDtypeStruct((bsz, seq, FOX_WIDTH), BF16),
        scratch_shapes=[pltpu.VMEM((4, LANES, QC), F32), pltpu.VMEM((4, TQ, QC), F32)],
        compiler_params=_params(2),
        name="attn",
    )(qa, ka, vt)

    out = pl.pallas_call(
        functools.partial(_post_kernel, ff_chunk=1024),
        grid=(n_tiles,),
        in_specs=[
            pl.BlockSpec((TM, d), lambda i: (i, 0)),
            pl.BlockSpec((TM, FOX_WIDTH), lambda i: (i, 0)),
            pl.BlockSpec((TM, SGU_WIDTH), lambda i: (i, 0)),
            pl.BlockSpec((TM, 2 * d), lambda i: (i, 0)),
            _const((FOX_WIDTH, d)),
            _const((SGU_WIDTH, d)),
            _const((d, d)),
            _const((1, d)),
            _const((d, D_FF)),
            _const((D_FF, d)),
            _const((1, d)),
        ],
        out_specs=pl.BlockSpec((TM, d), lambda i: (i, 0)),
        out_shape=jax.ShapeDtypeStruct((n_tok, d), F32),
        compiler_params=_params(1),
        name="post",
    )(xt, att.reshape(n_tok, FOX_WIDTH), sb, gs, w_a[0].astype(BF16), w_b[0].astype(BF16),
      w_o[0].astype(BF16), norm2_g.reshape(1, d), w_up[0].astype(BF16),
      w_down[0].astype(BF16), normf_g.reshape(1, d))

    return out.reshape(bsz, seq, d)
```

```python
import functools
import math

import jax
import jax.numpy as jnp
from jax import lax
from jax.experimental import pallas as pl
from jax.experimental.pallas import tpu as pltpu

D_MODEL = 1024
HEAD_DIM = 64
FOX_HEADS = 8
FOX_WIDTH = FOX_HEADS * HEAD_DIM
SGU_GROUPS = 8
SGU_WIDTH = 512
SGU_LEN = 128
CHUNK = 64
D_FF = 4 * D_MODEL
EPS = 1e-6

LANES = 128
TM = 512
TQ = 512
QC = 256
SUB = 256
LOG2E = math.log2(math.e)
VMEM_LIMIT = 56 * 1024 * 1024

_F0 = 3 * FOX_WIDTH
_R0 = _F0 + FOX_HEADS
_R_COLS = 2 * SGU_WIDTH + 2 * D_MODEL

BF16 = jnp.bfloat16
F32 = jnp.float32


def _dot(a, b):
    return jnp.dot(a, b, preferred_element_type=F32)


def _gelu_tanh(x):
    c = math.sqrt(2.0 / math.pi)
    return 0.5 * x * (1.0 + jnp.tanh(c * (x + 0.044715 * (x * x * x))))


def _split3(x):
    hi = x.astype(BF16).astype(F32)
    r = x - hi
    mid = r.astype(BF16).astype(F32)
    return hi, mid, r - mid


def _sigmoid(x):
    return 0.5 * jnp.tanh(0.5 * x) + 0.5


def _pre_kernel(x_ref, g1_ref, wqkv_ref, wf_ref, wr_ref, bf_ref, lng_ref, lnb_ref, wm_ref, bm_ref,
                qa_ref, ka_ref, vt_ref, sb_ref, gs_ref, carry_ref, *, tiles_per_seq):
    i = pl.program_id(0)

    @pl.when(i % tiles_per_seq == 0)
    def _():
        carry_ref[...] = jnp.zeros_like(carry_ref)

    r_i = lax.broadcasted_iota(jnp.int32, (SUB, SUB), 0)
    c_i = lax.broadcasted_iota(jnp.int32, (SUB, SUB), 1)
    tri = jnp.where(r_i >= c_i, 1.0, 0.0).astype(BF16)
    lane = lax.broadcasted_iota(jnp.int32, (SUB, LANES), 1)
    low = lane < HEAD_DIM
    trow = lax.broadcasted_iota(jnp.int32, (LANES, SUB), 0)
    n_win = SUB // SGU_LEN
    wi = lax.broadcasted_iota(jnp.int32, (SGU_LEN, SGU_LEN), 0) // CHUNK
    wj = lax.broadcasted_iota(jnp.int32, (SGU_LEN, SGU_LEN), 1) // CHUNK
    wmask = wj <= wi
    lane_w = lax.broadcasted_iota(jnp.int32, (SGU_LEN, n_win * LANES), 1)
    low_w = (lane_w % LANES) < HEAD_DIM

    def proj(h, w_ref, c0, width):
        return _dot(h, w_ref[:, c0:c0 + width])

    carry = carry_ref[...]
    for sub in range(TM // SUB):
        rows = slice(sub * SUB, (sub + 1) * SUB)
        x = x_ref[rows, :]
        ms = jnp.mean(x * x, axis=-1, keepdims=True)
        h = (x * lax.rsqrt(ms + EPS) * g1_ref[...]).astype(BF16)

        z = _dot(h, wf_ref[...]) + bf_ref[...]
        logf = jnp.minimum(z, 0.0) - jnp.log(1.0 + jnp.exp(-jnp.abs(z)))
        hi, mid, lo = _split3(logf)
        part = jnp.where(lane < FOX_HEADS, hi, jnp.where(lane < 2 * FOX_HEADS, mid, lo))
        cl = _dot(tri, part.astype(BF16))
        c = carry + (cl + pltpu.roll(cl, LANES - FOX_HEADS, 1)
                     + pltpu.roll(cl, LANES - 2 * FOX_HEADS, 1))
        carry = c[SUB - 1:SUB, :]

        usv = proj(h, wr_ref, 0, 2 * SGU_WIDTH)
        g_a = proj(h, wr_ref, 2 * SGU_WIDTH, D_MODEL)

        u = _gelu_tanh(usv[:, :SGU_WIDTH])
        sv = _gelu_tanh(usv[:, SGU_WIDTH:])
        mu = jnp.mean(sv, axis=-1, keepdims=True)
        xc = sv - mu
        var = jnp.mean(xc * xc, axis=-1, keepdims=True)
        svn = (xc * lax.rsqrt(var + EPS) * lng_ref[...] + lnb_ref[...]).astype(BF16)
        for jp in range(SGU_GROUPS // 2):
            sl = slice(LANES * jp, LANES * (jp + 1))
            chunk = svn[:, sl]
            rhs = jnp.concatenate(
                [chunk[SGU_LEN * w:SGU_LEN * (w + 1), :] for w in range(n_win)], axis=1)
            wa = jnp.where(wmask, wm_ref[2 * jp], 0.0).astype(BF16)
            wb = jnp.where(wmask, wm_ref[2 * jp + 1], 0.0).astype(BF16)
            mixed = jnp.where(low_w, _dot(wa, rhs), _dot(wb, rhs))
            mixed = mixed + jnp.concatenate([bm_ref[:, sl]] * n_win, axis=1)
            mixed = jnp.concatenate(
                [mixed[:, LANES * w:LANES * (w + 1)] for w in range(n_win)], axis=0)
            sb_ref[rows, sl] = (u[:, sl] * mixed).astype(BF16)
        gs_ref[rows, :D_MODEL] = _sigmoid(g_a).astype(BF16)

        g_b = proj(h, wr_ref, 2 * SGU_WIDTH + D_MODEL, D_MODEL)
        k_all = proj(h, wqkv_ref, FOX_WIDTH, FOX_WIDTH)
        gs_ref[rows, D_MODEL:] = _sigmoid(g_b).astype(BF16)
        q_all = proj(h, wqkv_ref, 0, FOX_WIDTH) * (HEAD_DIM ** -0.5 * LOG2E)
        v_all = proj(h, wqkv_ref, 2 * FOX_WIDTH, FOX_WIDTH)
        for jp in range(FOX_HEADS // 2):
            sl = slice(LANES * jp, LANES * (jp + 1))
            qc, kc, vc = q_all[:, sl], k_all[:, sl], v_all[:, sl]
            vt = vc.T
            for par in range(2):
                hd = 2 * jp + par
                a0 = HEAD_DIM if par == 0 else 0
                data = low if par == 0 else jnp.logical_not(low)
                in_aug = (lane >= a0) & (lane < a0 + 3)
                qa = jnp.where(data, qc, jnp.where(in_aug, 1.0, 0.0))
                nc = jnp.broadcast_to(c[:, hd:hd + 1], (SUB, LANES)) * (-LOG2E)
                hi, mid, lo = _split3(nc)
                aug = jnp.where(lane == a0, hi,
                                jnp.where(lane == a0 + 1, mid,
                                          jnp.where(lane == a0 + 2, lo, 0.0)))
                ka = jnp.where(data, kc, aug)
                qa_ref[0, hd, rows, :] = qa.astype(BF16)
                ka_ref[0, hd, rows, :] = ka.astype(BF16)
                if par == 0:
                    vta = jnp.where(trow < HEAD_DIM, vt, jnp.where(trow == HEAD_DIM, 1.0, 0.0))
                else:
                    vta = jnp.where(trow >= HEAD_DIM, vt, jnp.where(trow == 0, 1.0, 0.0))
                vt_ref[0, hd, 0, :, rows] = vta.astype(BF16)
    carry_ref[...] = carry


def _attn_kernel(qa_ref, ka_ref, vt_ref, o_ref, acc_ref, st_ref, *, n_q):
    k_i = lax.broadcasted_iota(jnp.int32, (QC, QC), 0)
    q_i = lax.broadcasted_iota(jnp.int32, (QC, QC), 1)
    causal = k_i <= q_i
    orow = lax.broadcasted_iota(jnp.int32, (LANES, QC), 0)
    halves = (slice(0, QC), slice(QC, TQ))

    def needed(c, tile, blk, half):
        return not (blk == tile and c == 0 and half == 1)

    def score_half(ch, tile, blk, half):
        hd, c = divmod(ch, 2)
        q0 = tile * TQ + c * QC
        k0 = blk * TQ + half * QC
        st = lax.dot_general(ka_ref[0, hd, pl.ds(k0, QC), :], qa_ref[0, hd, pl.ds(q0, QC), :],
                             (((1,), (1,)), ((), ())), preferred_element_type=F32)
        st_ref[ch, halves[half], :] = st
        return jnp.max(st, axis=0, keepdims=True)

    def chain_step(ch, tile, blk, m_old, cm, nxt):
        hd, c = divmod(ch, 2)
        diag = blk == tile
        masked = [diag and c == 0, diag and c == 1]

        def load(half):
            st = st_ref[ch, halves[half], :]
            return jnp.where(causal, st, -jnp.inf) if masked[half] else st

        if diag:
            cm = functools.reduce(jnp.maximum, [load(half).max(axis=0, keepdims=True)
                                                for half in range(2) if needed(c, tile, blk, half)])
        m_new = cm if m_old is None else jnp.maximum(m_old, cm)
        pv, cm_next = [], []
        for half in range(2):
            use = needed(c, tile, blk, half)
            if use:
                p = jnp.exp2(load(half) - m_new).astype(BF16)
            if nxt is not None and needed(c, *nxt, half):
                cm_next.append(score_half(ch, *nxt, half))
            if use:
                pv.append(_dot(vt_ref[0, hd, blk, :, halves[half]], p))
        pv = sum(pv[1:], pv[0])
        if m_old is None:
            acc_ref[ch] = pv
        else:
            acc_ref[ch] = acc_ref[ch] * jnp.exp2(m_old - m_new) + pv
        return m_new, (functools.reduce(jnp.maximum, cm_next) if cm_next else None)

    steps = [(qi, j) for qi in range(n_q) for j in range(qi + 1)]
    ms = [None] * 4
    cms = [functools.reduce(jnp.maximum, [score_half(ch, *steps[0], half) for half in range(2)
                                          if needed(ch % 2, *steps[0], half)])
           for ch in range(4)]
    for n, (qi, j) in enumerate(steps):
        nxt = steps[n + 1] if n + 1 < len(steps) else None
        for ch in range(4):
            ms[ch], cms[ch] = chain_step(ch, qi, j, ms[ch], cms[ch], nxt)
        if j == qi:
            for c in range(2):
                acc_a, acc_b = acc_ref[c], acc_ref[2 + c]
                o_t = jnp.where(orow < HEAD_DIM,
                                acc_a / acc_a[HEAD_DIM:HEAD_DIM + 1, :],
                                acc_b / acc_b[0:1, :])
                o_ref[0, pl.ds(qi * TQ + c * QC, QC), :] = o_t.T.astype(BF16)
            ms = [None] * 4


def _post_kernel(x_ref, at_ref, sb_ref, gs_ref, wa_ref, wb_ref, wo_ref, g2_ref, wu_ref,
                 wd_ref, gf_ref, o_ref, *, ff_chunk):
    subs = [slice(s * SUB, (s + 1) * SUB) for s in range(TM // SUB)]
    merged = []
    for rows in subs:
        ya = _dot(at_ref[rows, :], wa_ref[...])
        yb = _dot(sb_ref[rows, :], wb_ref[...])
        merged.append((gs_ref[rows, :D_MODEL].astype(F32) * ya
                       + gs_ref[rows, D_MODEL:].astype(F32) * yb).astype(BF16))
    ys = [x_ref[rows, :] + _dot(m, wo_ref[...]) for rows, m in zip(subs, merged)]
    hs = []
    for y in ys:
        ms = jnp.mean(y * y, axis=-1, keepdims=True)
        hs.append((y * lax.rsqrt(ms + EPS) * g2_ref[...]).astype(BF16))
    for c0 in range(0, D_FF, ff_chunk):
        acts = []
        for h in hs:
            a = jnp.maximum(_dot(h, wu_ref[:, c0:c0 + ff_chunk]), 0.0)
            acts.append((a * a).astype(BF16))
        ys = [y + _dot(a, wd_ref[c0:c0 + ff_chunk, :]) for y, a in zip(ys, acts)]
    for rows, y in zip(subs, ys):
        ms = jnp.mean(y * y, axis=-1, keepdims=True)
        o_ref[rows, :] = y * lax.rsqrt(ms + EPS) * gf_ref[...]


def _params(n_axes):
    return pltpu.CompilerParams(dimension_semantics=("arbitrary",) * n_axes,
                                vmem_limit_bytes=VMEM_LIMIT)


def _const(shape):
    return pl.BlockSpec(shape, lambda *_: (0,) * len(shape), pipeline_mode=pl.Buffered(1))


def kernel(x, norm1_g, w_in, b_f, ln_v_g, ln_v_b, w_sgu, b_sgu, w_a, w_b, w_o,
           norm2_g, w_up, w_down, normf_g):
    bsz, seq, d = x.shape
    assert d == D_MODEL and seq % TM == 0 and TM == TQ and TM % SUB == 0
    assert norm1_g.shape[0] == 1, "single-layer block"
    n_tok = bsz * seq
    n_tiles = n_tok // TM
    tiles_per_seq = seq // TM
    xt = x.reshape(n_tok, d)

    wl = w_in[0]
    n_rep = 3
    w_f = jnp.pad(jnp.tile(wl[:, _F0:_R0], (1, n_rep)),
                  ((0, 0), (0, LANES - n_rep * FOX_HEADS))).astype(BF16)
    w_qkv = wl[:, :_F0].astype(BF16)
    w_rest = wl[:, _R0:].astype(BF16)
    bf_pad = jnp.pad(jnp.tile(b_f[0], n_rep), (0, LANES - n_rep * FOX_HEADS)).reshape(1, LANES)
    bm = jnp.repeat(jnp.transpose(b_sgu[0]), HEAD_DIM, axis=1)

    head_map = lambda i: (i // tiles_per_seq, 0, i % tiles_per_seq, 0)
    qa, ka, vt, sb, gs = pl.pallas_call(
        functools.partial(_pre_kernel, tiles_per_seq=tiles_per_seq),
        grid=(n_tiles,),
        in_specs=[
            pl.BlockSpec((TM, d), lambda i: (i, 0)),
            _const((1, d)),
            _const((d, _F0)),
            _const((d, LANES)),
            _const((d, _R_COLS)),
            _const((1, LANES)),
            _const((1, SGU_WIDTH)),
            _const((1, SGU_WIDTH)),
            _const((SGU_GROUPS, SGU_LEN, SGU_LEN)),
            _const((SGU_LEN, SGU_WIDTH)),
        ],
        out_specs=[
            pl.BlockSpec((1, FOX_HEADS, TM, LANES), head_map),
            pl.BlockSpec((1, FOX_HEADS, TM, LANES), head_map),
            pl.BlockSpec((1, FOX_HEADS, 1, LANES, TM), lambda i: head_map(i) + (0,)),
            pl.BlockSpec((TM, SGU_WIDTH), lambda i: (i, 0)),
            pl.BlockSpec((TM, 2 * d), lambda i: (i, 0)),
        ],
        out_shape=[
            jax.ShapeDtypeStruct((bsz, FOX_HEADS, seq, LANES), BF16),
            jax.ShapeDtypeStruct((bsz, FOX_HEADS, seq, LANES), BF16),
            jax.ShapeDtypeStruct((bsz, FOX_HEADS, seq // TQ, LANES, TQ), BF16),
            jax.ShapeDtypeStruct((n_tok, SGU_WIDTH), BF16),
            jax.ShapeDtypeStruct((n_tok, 2 * d), BF16),
        ],
        scratch_shapes=[pltpu.VMEM((1, LANES), F32)],
        compiler_params=_params(1),
        name="pre",
    )(xt, norm1_g.reshape(1, d), w_qkv, w_f, w_rest, bf_pad, ln_v_g.reshape(1, SGU_WIDTH),
      ln_v_b.reshape(1, SGU_WIDTH), w_sgu[0], bm)

    att = pl.pallas_call(
        functools.partial(_attn_kernel, n_q=seq // TQ),
        grid=(bsz, FOX_HEADS // 2),
        in_specs=[
            pl.BlockSpec((1, 2, seq, LANES), lambda b, p: (b, p, 0, 0)),
            pl.BlockSpec((1, 2, seq, LANES), lambda b, p: (b, p, 0, 0)),
            pl.BlockSpec((1, 2, seq // TQ, LANES, TQ), lambda b, p: (b, p, 0, 0, 0)),
        ],
        out_specs=pl.BlockSpec((1, seq, LANES), lambda b, p: (b, 0, p)),
        out_shape=jax.ShapeDtypeStruct((bsz, seq, FOX_WIDTH), BF16),
        scratch_shapes=[pltpu.VMEM((4, LANES, QC), F32), pltpu.VMEM((4, TQ, QC), F32)],
        compiler_params=_params(2),
        name="attn",
    )(qa, ka, vt)

    out = pl.pallas_call(
        functools.partial(_post_kernel, ff_chunk=1024),
        grid=(n_tiles,),
        in_specs=[
            pl.BlockSpec((TM, d), lambda i: (i, 0)),
            pl.BlockSpec((TM, FOX_WIDTH), lambda i: (i, 0)),
            pl.BlockSpec((TM, SGU_WIDTH), lambda i: (i, 0)),
            pl.BlockSpec((TM, 2 * d), lambda i: (i, 0)),
            _const((FOX_WIDTH, d)),
            _const((SGU_WIDTH, d)),
            _const((d, d)),
            _const((1, d)),
            _const((d, D_FF)),
            _const((D_FF, d)),
            _const((1, d)),
        ],
        out_specs=pl.BlockSpec((TM, d), lambda i: (i, 0)),
        out_shape=jax.ShapeDtypeStruct((n_tok, d), F32),
        compiler_params=_params(1),
        name="post",
    )(xt, att.reshape(n_tok, FOX_WIDTH), sb, gs, w_a[0].astype(BF16), w_b[0].astype(BF16),
      w_o[0].astype(BF16), norm2_g.reshape(1, d), w_up[0].astype(BF16),
      w_down[0].astype(BF16), normf_g.reshape(1, d))

    return out.reshape(bsz, seq, d)
```

```python
import functools
import math

import jax
import jax.numpy as jnp
from jax import lax
from jax.experimental import pallas as pl
from jax.experimental.pallas import tpu as pltpu

D_MODEL = 1024
HEAD_DIM = 64
FOX_HEADS = 8
FOX_WIDTH = FOX_HEADS * HEAD_DIM
SGU_GROUPS = 8
SGU_WIDTH = 512
SGU_LEN = 128
CHUNK = 64
D_FF = 4 * D_MODEL
EPS = 1e-6

LANES = 128
TM = 512
TQ = 512
QC = 256
SUB = 256
VT_ROWS = 80
PREP_ROWS = 128
LOG2E = math.log2(math.e)
VMEM_LIMIT = 56 * 1024 * 1024

_F0 = 3 * FOX_WIDTH
_R0 = _F0 + FOX_HEADS
_R_COLS = 2 * SGU_WIDTH + 2 * D_MODEL
N_PARTS = 3

BF16 = jnp.bfloat16
F32 = jnp.float32


def _dot(a, b):
    return jnp.dot(a, b, preferred_element_type=F32)


def _gelu_tanh(x):
    c = math.sqrt(2.0 / math.pi)
    return 0.5 * x * (1.0 + jnp.tanh(c * (x + 0.044715 * (x * x * x))))


def _sigmoid(x):
    return 0.5 * jnp.tanh(0.5 * x) + 0.5


def _split3(x):
    hi = x.astype(BF16).astype(F32)
    r = x - hi
    mid = r.astype(BF16).astype(F32)
    return hi, mid, r - mid


def _prep_kernel(w_ref, wqkv_ref, wf_ref, wr_ref):
    x = w_ref[...]
    wqkv_ref[...] = x[:, :_F0].astype(BF16)
    blk = x[:, _F0:_F0 + LANES]
    lane = lax.broadcasted_iota(jnp.int32, blk.shape, 1)
    wf = jnp.zeros_like(blk)
    for rep in reversed(range(N_PARTS)):
        shifted = blk if rep == 0 else pltpu.roll(blk, rep * FOX_HEADS, 1)
        wf = jnp.where(lane < (rep + 1) * FOX_HEADS, shifted, wf)
    wf_ref[...] = wf.astype(BF16)
    wr_ref[...] = x[:, _R0:_R0 + _R_COLS].astype(BF16)


def _pre_kernel(x_ref, g1_ref, wqkv_ref, wf_ref, wr_ref, bf_ref, lng_ref, lnb_ref, wm_ref, bm_ref,
                qa_ref, ka_ref, vt_ref, sb_ref, gs_ref, carry_ref, *, tiles_per_seq):
    i = pl.program_id(0)

    @pl.when(i % tiles_per_seq == 0)
    def _():
        carry_ref[...] = jnp.zeros_like(carry_ref)

    r_i = lax.broadcasted_iota(jnp.int32, (SUB, SUB), 0)
    c_i = lax.broadcasted_iota(jnp.int32, (SUB, SUB), 1)
    tri = jnp.where(r_i >= c_i, 1.0, 0.0).astype(BF16)
    lane = lax.broadcasted_iota(jnp.int32, (SUB, LANES), 1)
    low = lane < HEAD_DIM
    ones_row = jnp.where(lax.broadcasted_iota(jnp.int32, (VT_ROWS - HEAD_DIM, SUB), 0) == 0,
                         1.0, 0.0)
    n_win = SUB // SGU_LEN
    wi = lax.broadcasted_iota(jnp.int32, (SGU_LEN, SGU_LEN), 0) // CHUNK
    wj = lax.broadcasted_iota(jnp.int32, (SGU_LEN, SGU_LEN), 1) // CHUNK
    wmask = wj <= wi
    lane_w = lax.broadcasted_iota(jnp.int32, (SGU_LEN, n_win * LANES), 1)
    low_w = (lane_w % LANES) < HEAD_DIM

    def proj(h, w_ref, c0, width):
        return _dot(h, w_ref[:, c0:c0 + width])

    carry = carry_ref[...]
    for sub in range(TM // SUB):
        rows = slice(sub * SUB, (sub + 1) * SUB)
        x = x_ref[rows, :]
        ms = jnp.mean(x * x, axis=-1, keepdims=True)
        h = (x * lax.rsqrt(ms + EPS) * g1_ref[...]).astype(BF16)

        z = _dot(h, wf_ref[...]) + bf_ref[...]
        usv = proj(h, wr_ref, 0, 2 * SGU_WIDTH)
        k_all = proj(h, wqkv_ref, FOX_WIDTH, FOX_WIDTH)

        logf = jnp.minimum(z, 0.0) - jnp.log(1.0 + jnp.exp(-jnp.abs(z)))
        hi, mid, lo = _split3(logf)
        part = jnp.where(lane < FOX_HEADS, hi, jnp.where(lane < 2 * FOX_HEADS, mid, lo))
        cl = _dot(tri, part.astype(BF16))
        c = carry + (cl + pltpu.roll(cl, LANES - FOX_HEADS, 1)
                     + pltpu.roll(cl, LANES - 2 * FOX_HEADS, 1))
        carry = c[SUB - 1:SUB, :]

        q_all = proj(h, wqkv_ref, 0, FOX_WIDTH) * (HEAD_DIM ** -0.5 * LOG2E)
        g_a = proj(h, wr_ref, 2 * SGU_WIDTH, D_MODEL)

        u = _gelu_tanh(usv[:, :SGU_WIDTH])
        sv = _gelu_tanh(usv[:, SGU_WIDTH:])
        mu = jnp.mean(sv, axis=-1, keepdims=True)
        xc = sv - mu
        var = jnp.mean(xc * xc, axis=-1, keepdims=True)
        svn = (xc * lax.rsqrt(var + EPS) * lng_ref[...] + lnb_ref[...]).astype(BF16)
        for jp in range(SGU_GROUPS // 2):
            sl = slice(LANES * jp, LANES * (jp + 1))
            chunk = svn[:, sl]
            rhs = jnp.concatenate(
                [chunk[SGU_LEN * w:SGU_LEN * (w + 1), :] for w in range(n_win)], axis=1)
            wa = jnp.where(wmask, wm_ref[2 * jp], 0.0).astype(BF16)
            wb = jnp.where(wmask, wm_ref[2 * jp + 1], 0.0).astype(BF16)
            mixed = jnp.where(low_w, _dot(wa, rhs), _dot(wb, rhs))
            mixed = mixed + jnp.concatenate([bm_ref[:, sl]] * n_win, axis=1)
            mixed = jnp.concatenate(
                [mixed[:, LANES * w:LANES * (w + 1)] for w in range(n_win)], axis=0)
            sb_ref[rows, sl] = (u[:, sl] * mixed).astype(BF16)
        gs_ref[rows, :D_MODEL] = _sigmoid(g_a).astype(BF16)

        v_all = proj(h, wqkv_ref, 2 * FOX_WIDTH, FOX_WIDTH)
        g_b = proj(h, wr_ref, 2 * SGU_WIDTH + D_MODEL, D_MODEL)
        for jp in range(FOX_HEADS // 2):
            sl = slice(LANES * jp, LANES * (jp + 1))
            qc, kc = q_all[:, sl], k_all[:, sl]
            vt = v_all[:, sl].T
            for par in range(2):
                hd = 2 * jp + par
                a0 = HEAD_DIM if par == 0 else 0
                data = low if par == 0 else jnp.logical_not(low)
                in_aug = (lane >= a0) & (lane < a0 + N_PARTS)
                qa = jnp.where(data, qc, jnp.where(in_aug, 1.0, 0.0))
                nc = jnp.broadcast_to(c[:, hd:hd + 1], (SUB, LANES)) * (-LOG2E)
                hi, mid, lo = _split3(nc)
                aug = jnp.where(lane == a0, hi,
                                jnp.where(lane == a0 + 1, mid,
                                          jnp.where(lane == a0 + 2, lo, 0.0)))
                ka = jnp.where(data, kc, aug)
                qa_ref[0, hd, rows, :] = qa.astype(BF16)
                ka_ref[0, hd, rows, :] = ka.astype(BF16)
                vta = jnp.concatenate([vt[par * HEAD_DIM:(par + 1) * HEAD_DIM], ones_row], axis=0)
                vt_ref[0, hd, 0, :, rows] = vta.astype(BF16)
        gs_ref[rows, D_MODEL:] = _sigmoid(g_b).astype(BF16)
    carry_ref[...] = carry


def _attn_kernel(qa_ref, ka_ref, vt_ref, o_ref, acc_ref, st_ref, *, n_q):
    k_i = lax.broadcasted_iota(jnp.int32, (QC, QC), 0)
    q_i = lax.broadcasted_iota(jnp.int32, (QC, QC), 1)
    causal = k_i <= q_i
    halves = (slice(0, QC), slice(QC, TQ))

    def needed(c, tile, blk, half):
        return not (blk == tile and c == 0 and half == 1)

    def score_half(ch, tile, blk, half):
        hd, c = divmod(ch, 2)
        q0 = tile * TQ + c * QC
        k0 = blk * TQ + half * QC
        st = lax.dot_general(ka_ref[0, hd, pl.ds(k0, QC), :], qa_ref[0, hd, pl.ds(q0, QC), :],
                             (((1,), (1,)), ((), ())), preferred_element_type=F32)
        st_ref[ch, halves[half], :] = st
        return jnp.max(st, axis=0, keepdims=True)

    def chain_step(ch, tile, blk, m_old, cm, nxt):
        hd, c = divmod(ch, 2)
        diag = blk == tile
        masked = [diag and c == 0, diag and c == 1]

        def load(half):
            st = st_ref[ch, halves[half], :]
            return jnp.where(causal, st, -jnp.inf) if masked[half] else st

        if diag:
            cm = functools.reduce(jnp.maximum, [load(half).max(axis=0, keepdims=True)
                                                for half in range(2) if needed(c, tile, blk, half)])
        m_new = cm if m_old is None else jnp.maximum(m_old, cm)
        pv, cm_next = [], []
        for half in range(2):
            use = needed(c, tile, blk, half)
            if use:
                p = jnp.exp2(load(half) - m_new).astype(BF16)
            if nxt is not None and needed(c, *nxt, half):
                cm_next.append(score_half(ch, *nxt, half))
            if use:
                pv.append(_dot(vt_ref[0, hd, blk, :, halves[half]], p))
        pv = sum(pv[1:], pv[0])
        if m_old is None:
            acc_ref[ch] = pv
        else:
            acc_ref[ch] = acc_ref[ch] * jnp.exp2(m_old - m_new) + pv
        return m_new, (functools.reduce(jnp.maximum, cm_next) if cm_next else None)

    steps = [(qi, j) for qi in range(n_q) for j in range(qi + 1)]
    ms = [None] * 4
    cms = [functools.reduce(jnp.maximum, [score_half(ch, *steps[0], half) for half in range(2)
                                          if needed(ch % 2, *steps[0], half)])
           for ch in range(4)]
    for n, (qi, j) in enumerate(steps):
        nxt = steps[n + 1] if n + 1 < len(steps) else None
        for ch in range(4):
            ms[ch], cms[ch] = chain_step(ch, qi, j, ms[ch], cms[ch], nxt)
        if j == qi:
            for c in range(2):
                o_t = jnp.concatenate(
                    [acc_ref[2 * hd + c, :HEAD_DIM, :] / acc_ref[2 * hd + c, HEAD_DIM:HEAD_DIM + 1, :]
                     for hd in range(2)], axis=0)
                o_ref[0, pl.ds(qi * TQ + c * QC, QC), :] = o_t.T.astype(BF16)
            ms = [None] * 4


def _post_kernel(x_ref, at_ref, sb_ref, gs_ref, wa_ref, wb_ref, wo_ref, g2_ref, wu_ref,
                 wd_ref, gf_ref, o_ref, *, ff_chunk):
    subs = [slice(s * SUB, (s + 1) * SUB) for s in range(TM // SUB)]
    merged = []
    for rows in subs:
        ya = _dot(at_ref[rows, :], wa_ref[...])
        yb = _dot(sb_ref[rows, :], wb_ref[...])
        merged.append((gs_ref[rows, :D_MODEL].astype(F32) * ya
                       + gs_ref[rows, D_MODEL:].astype(F32) * yb).astype(BF16))
    ys = [x_ref[rows, :] + _dot(m, wo_ref[...]) for rows, m in zip(subs, merged)]
    hs = []
    for y in ys:
        ms = jnp.mean(y * y, axis=-1, keepdims=True)
        hs.append((y * lax.rsqrt(ms + EPS) * g2_ref[...]).astype(BF16))
    for c0 in range(0, D_FF, ff_chunk):
        acts = []
        for h in hs:
            a = jnp.maximum(_dot(h, wu_ref[:, c0:c0 + ff_chunk]), 0.0)
            acts.append((a * a).astype(BF16))
        ys = [y + _dot(a, wd_ref[c0:c0 + ff_chunk, :]) for y, a in zip(ys, acts)]
    for rows, y in zip(subs, ys):
        ms = jnp.mean(y * y, axis=-1, keepdims=True)
        o_ref[rows, :] = y * lax.rsqrt(ms + EPS) * gf_ref[...]


def _params(n_axes):
    return pltpu.CompilerParams(dimension_semantics=("arbitrary",) * n_axes,
                                vmem_limit_bytes=VMEM_LIMIT)


def _const(shape):
    return pl.BlockSpec(shape, lambda *_: (0,) * len(shape), pipeline_mode=pl.Buffered(1))


def _rows(width, rows=TM):
    return pl.BlockSpec((rows, width), lambda i: (i, 0))


def _prep(w):
    d, n = w.shape
    widths = (_F0, LANES, _R_COLS)
    return pl.pallas_call(
        _prep_kernel,
        grid=(d // PREP_ROWS,),
        in_specs=[_rows(n, PREP_ROWS)],
        out_specs=[_rows(width, PREP_ROWS) for width in widths],
        out_shape=[jax.ShapeDtypeStruct((d, width), BF16) for width in widths],
        compiler_params=_params(1),
        name="prep",
    )(w)


def _pre(xt, bsz, seq, g1, w_qkv, w_f, w_rest, bf_pad, ln_g, ln_b, w_sgu, bm):
    n_tok, d = xt.shape
    tiles_per_seq = seq // TM
    head_map = lambda i: (i // tiles_per_seq, 0, i % tiles_per_seq, 0)
    return pl.pallas_call(
        functools.partial(_pre_kernel, tiles_per_seq=tiles_per_seq),
        grid=(n_tok // TM,),
        in_specs=[
            _rows(d),
            _const((1, d)),
            _const((d, _F0)),
            _const((d, LANES)),
            _const((d, _R_COLS)),
            _const((1, LANES)),
            _const((1, SGU_WIDTH)),
            _const((1, SGU_WIDTH)),
            _const((SGU_GROUPS, SGU_LEN, SGU_LEN)),
            _const((SGU_LEN, SGU_WIDTH)),
        ],
        out_specs=[
            pl.BlockSpec((1, FOX_HEADS, TM, LANES), head_map),
            pl.BlockSpec((1, FOX_HEADS, TM, LANES), head_map),
            pl.BlockSpec((1, FOX_HEADS, 1, VT_ROWS, TM), lambda i: head_map(i) + (0,)),
            _rows(SGU_WIDTH),
            _rows(2 * d),
        ],
        out_shape=[
            jax.ShapeDtypeStruct((bsz, FOX_HEADS, seq, LANES), BF16),
            jax.ShapeDtypeStruct((bsz, FOX_HEADS, seq, LANES), BF16),
            jax.ShapeDtypeStruct((bsz, FOX_HEADS, seq // TQ, VT_ROWS, TQ), BF16),
            jax.ShapeDtypeStruct((n_tok, SGU_WIDTH), BF16),
            jax.ShapeDtypeStruct((n_tok, 2 * d), BF16),
        ],
        scratch_shapes=[pltpu.VMEM((1, LANES), F32)],
        compiler_params=_params(1),
        name="pre",
    )(xt, g1, w_qkv, w_f, w_rest, bf_pad, ln_g, ln_b, w_sgu, bm)


def _attn(qa, ka, vt):
    bsz, _, seq, _ = qa.shape
    return pl.pallas_call(
        functools.partial(_attn_kernel, n_q=seq // TQ),
        grid=(bsz, FOX_HEADS // 2),
        in_specs=[
            pl.BlockSpec((1, 2, seq, LANES), lambda b, p: (b, p, 0, 0)),
            pl.BlockSpec((1, 2, seq, LANES), lambda b, p: (b, p, 0, 0)),
            pl.BlockSpec((1, 2, seq // TQ, VT_ROWS, TQ), lambda b, p: (b, p, 0, 0, 0)),
        ],
        out_specs=pl.BlockSpec((1, seq, LANES), lambda b, p: (b, 0, p)),
        out_shape=jax.ShapeDtypeStruct((bsz, seq, FOX_WIDTH), BF16),
        scratch_shapes=[pltpu.VMEM((4, VT_ROWS, QC), F32), pltpu.VMEM((4, TQ, QC), F32)],
        compiler_params=_params(2),
        name="attn",
    )(qa, ka, vt)


def _post(xt, att, sb, gs, w_a, w_b, w_o, g2, w_up, w_down, gf):
    n_tok, d = xt.shape
    return pl.pallas_call(
        functools.partial(_post_kernel, ff_chunk=1024),
        grid=(n_tok // TM,),
        in_specs=[
            _rows(d),
            _rows(FOX_WIDTH),
            _rows(SGU_WIDTH),
            _rows(2 * d),
            _const((FOX_WIDTH, d)),
            _const((SGU_WIDTH, d)),
            _const((d, d)),
            _const((1, d)),
            _const((d, D_FF)),
            _const((D_FF, d)),
            _const((1, d)),
        ],
        out_specs=_rows(d),
        out_shape=jax.ShapeDtypeStruct((n_tok, d), F32),
        compiler_params=_params(1),
        name="post",
    )(xt, att, sb, gs, w_a, w_b, w_o, g2, w_up, w_down, gf)


def kernel(x, norm1_g, w_in, b_f, ln_v_g, ln_v_b, w_sgu, b_sgu, w_a, w_b, w_o,
           norm2_g, w_up, w_down, normf_g):
    bsz, seq, d = x.shape
    assert d == D_MODEL and seq % TM == 0 and TM == TQ and TM % SUB == 0
    assert norm1_g.shape[0] == 1, "single-layer block"
    n_tok = bsz * seq
    xt = x.reshape(n_tok, d)

    w_qkv, w_f, w_rest = _prep(w_in[0])
    bf_pad = jnp.pad(jnp.tile(b_f[0], N_PARTS), (0, LANES - N_PARTS * FOX_HEADS)).reshape(1, LANES)
    bm = jnp.repeat(jnp.transpose(b_sgu[0]), HEAD_DIM, axis=1)

    qa, ka, vt, sb, gs = _pre(xt, bsz, seq, norm1_g.reshape(1, d), w_qkv, w_f, w_rest, bf_pad,
                              ln_v_g.reshape(1, SGU_WIDTH), ln_v_b.reshape(1, SGU_WIDTH),
                              w_sgu[0], bm)
    att = _attn(qa, ka, vt)
    out = _post(xt, att.reshape(n_tok, FOX_WIDTH), sb, gs, w_a[0].astype(BF16),
                w_b[0].astype(BF16), w_o[0].astype(BF16), norm2_g.reshape(1, d),
                w_up[0].astype(BF16), w_down[0].astype(BF16), normf_g.reshape(1, d))
    return out.reshape(bsz, seq, d)
```

```python
import functools
import math

import jax
import jax.numpy as jnp
from jax import lax
from jax.experimental import pallas as pl
from jax.experimental.pallas import tpu as pltpu

D_MODEL = 1024
HEAD_DIM = 64
FOX_HEADS = 8
FOX_WIDTH = FOX_HEADS * HEAD_DIM
SGU_GROUPS = 8
SGU_WIDTH = 512
SGU_LEN = 128
CHUNK = 64
D_FF = 4 * D_MODEL
EPS = 1e-6

LANES = 128
TM = 512
TP = 1024
TQ = 512
QC = 256
SUB = 256
VT_ROWS = 80
PREP_ROWS = 128
LOG2E = math.log2(math.e)
VMEM_LIMIT = 56 * 1024 * 1024

_F0 = 3 * FOX_WIDTH
_R0 = _F0 + FOX_HEADS
_R_COLS = 2 * SGU_WIDTH + 2 * D_MODEL
N_PARTS = 3

BF16 = jnp.bfloat16
F32 = jnp.float32


def _dot(a, b):
    return jnp.dot(a, b, preferred_element_type=F32)


def _gelu_tanh(x):
    c = math.sqrt(2.0 / math.pi)
    return 0.5 * x * (1.0 + jnp.tanh(c * (x + 0.044715 * (x * x * x))))


def _sigmoid(x):
    return 0.5 * jnp.tanh(0.5 * x) + 0.5


def _split3(x):
    hi = x.astype(BF16).astype(F32)
    r = x - hi
    mid = r.astype(BF16).astype(F32)
    return hi, mid, r - mid


def _prep_kernel(w_ref, wqkv_ref, wf_ref, wr_ref):
    x = w_ref[...]
    wqkv_ref[...] = x[:, :_F0].astype(BF16)
    blk = x[:, _F0:_F0 + LANES]
    lane = lax.broadcasted_iota(jnp.int32, blk.shape, 1)
    wf = jnp.zeros_like(blk)
    for rep in reversed(range(N_PARTS)):
        shifted = blk if rep == 0 else pltpu.roll(blk, rep * FOX_HEADS, 1)
        wf = jnp.where(lane < (rep + 1) * FOX_HEADS, shifted, wf)
    wf_ref[...] = wf.astype(BF16)
    wr_ref[...] = x[:, _R0:_R0 + _R_COLS].astype(BF16)


def _pre_kernel(x_ref, g1_ref, wqkv_ref, wf_ref, wr_ref, bf_ref, lng_ref, lnb_ref, wm_ref, bm_ref,
                qa_ref, ka_ref, vt_ref, sb_ref, gs_ref, carry_ref, *, tiles_per_seq):
    i = pl.program_id(0)

    @pl.when(i % tiles_per_seq == 0)
    def _():
        carry_ref[...] = jnp.zeros_like(carry_ref)

    r_i = lax.broadcasted_iota(jnp.int32, (SUB, SUB), 0)
    c_i = lax.broadcasted_iota(jnp.int32, (SUB, SUB), 1)
    tri = jnp.where(r_i >= c_i, 1.0, 0.0).astype(BF16)
    lane = lax.broadcasted_iota(jnp.int32, (SUB, LANES), 1)
    low = lane < HEAD_DIM
    ones_row = jnp.where(lax.broadcasted_iota(jnp.int32, (VT_ROWS - HEAD_DIM, SUB), 0) == 0,
                         1.0, 0.0)
    n_win = SUB // SGU_LEN
    wi = lax.broadcasted_iota(jnp.int32, (SGU_LEN, SGU_LEN), 0) // CHUNK
    wj = lax.broadcasted_iota(jnp.int32, (SGU_LEN, SGU_LEN), 1) // CHUNK
    wmask = wj <= wi
    lane_w = lax.broadcasted_iota(jnp.int32, (SGU_LEN, n_win * LANES), 1)
    low_w = (lane_w % LANES) < HEAD_DIM

    def proj(h, w_ref, c0, width):
        return _dot(h, w_ref[:, c0:c0 + width])

    carry = carry_ref[...]
    for sub in range(TP // SUB):
        rows = slice(sub * SUB, (sub + 1) * SUB)
        kblk, kcols = divmod(sub * SUB, TQ)
        x = x_ref[rows, :]
        ms = jnp.mean(x * x, axis=-1, keepdims=True)
        h = (x * lax.rsqrt(ms + EPS) * g1_ref[...]).astype(BF16)

        z = _dot(h, wf_ref[...]) + bf_ref[...]
        usv = proj(h, wr_ref, 0, 2 * SGU_WIDTH)
        k_all = proj(h, wqkv_ref, FOX_WIDTH, FOX_WIDTH)

        logf = jnp.minimum(z, 0.0) - jnp.log(1.0 + jnp.exp(-jnp.abs(z)))
        hi, mid, lo = _split3(logf)
        part = jnp.where(lane < FOX_HEADS, hi, jnp.where(lane < 2 * FOX_HEADS, mid, lo))
        cl = _dot(tri, part.astype(BF16))
        c = carry + (cl + pltpu.roll(cl, LANES - FOX_HEADS, 1)
                     + pltpu.roll(cl, LANES - 2 * FOX_HEADS, 1))
        carry = c[SUB - 1:SUB, :]

        q_all = proj(h, wqkv_ref, 0, FOX_WIDTH) * (HEAD_DIM ** -0.5 * LOG2E)
        g_a = proj(h, wr_ref, 2 * SGU_WIDTH, D_MODEL)

        u = _gelu_tanh(usv[:, :SGU_WIDTH])
        sv = _gelu_tanh(usv[:, SGU_WIDTH:])
        mu = jnp.mean(sv, axis=-1, keepdims=True)
        xc = sv - mu
        var = jnp.mean(xc * xc, axis=-1, keepdims=True)
        svn = (xc * lax.rsqrt(var + EPS) * lng_ref[...] + lnb_ref[...]).astype(BF16)
        for jp in range(SGU_GROUPS // 2):
            sl = slice(LANES * jp, LANES * (jp + 1))
            chunk = svn[:, sl]
            rhs = jnp.concatenate(
                [chunk[SGU_LEN * w:SGU_LEN * (w + 1), :] for w in range(n_win)], axis=1)
            wa = jnp.where(wmask, wm_ref[2 * jp], 0.0).astype(BF16)
            wb = jnp.where(wmask, wm_ref[2 * jp + 1], 0.0).astype(BF16)
            mixed = jnp.where(low_w, _dot(wa, rhs), _dot(wb, rhs))
            mixed = mixed + jnp.concatenate([bm_ref[:, sl]] * n_win, axis=1)
            mixed = jnp.concatenate(
                [mixed[:, LANES * w:LANES * (w + 1)] for w in range(n_win)], axis=0)
            sb_ref[rows, sl] = (u[:, sl] * mixed).astype(BF16)
        gs_ref[rows, :D_MODEL] = _sigmoid(g_a).astype(BF16)

        v_all = proj(h, wqkv_ref, 2 * FOX_WIDTH, FOX_WIDTH)
        g_b = proj(h, wr_ref, 2 * SGU_WIDTH + D_MODEL, D_MODEL)
        for jp in range(FOX_HEADS // 2):
            sl = slice(LANES * jp, LANES * (jp + 1))
            qc, kc = q_all[:, sl], k_all[:, sl]
            vt = v_all[:, sl].T
            for par in range(2):
                hd = 2 * jp + par
                a0 = HEAD_DIM if par == 0 else 0
                data = low if par == 0 else jnp.logical_not(low)
                in_aug = (lane >= a0) & (lane < a0 + N_PARTS)
                qa = jnp.where(data, qc, jnp.where(in_aug, 1.0, 0.0))
                nc = jnp.broadcast_to(c[:, hd:hd + 1], (SUB, LANES)) * (-LOG2E)
                hi, mid, lo = _split3(nc)
                aug = jnp.where(lane == a0, hi,
                                jnp.where(lane == a0 + 1, mid,
                                          jnp.where(lane == a0 + 2, lo, 0.0)))
                ka = jnp.where(data, kc, aug)
                qa_ref[0, hd, rows, :] = qa.astype(BF16)
                ka_ref[0, hd, rows, :] = ka.astype(BF16)
                vta = jnp.concatenate([vt[par * HEAD_DIM:(par + 1) * HEAD_DIM], ones_row], axis=0)
                vt_ref[0, hd, kblk, :, kcols:kcols + SUB] = vta.astype(BF16)
        gs_ref[rows, D_MODEL:] = _sigmoid(g_b).astype(BF16)
    carry_ref[...] = carry


def _attn_kernel(qa_ref, ka_ref, vt_ref, o_ref, acc_ref, st_ref, *, n_q):
    k_i = lax.broadcasted_iota(jnp.int32, (QC, QC), 0)
    q_i = lax.broadcasted_iota(jnp.int32, (QC, QC), 1)
    causal = k_i <= q_i
    halves = (slice(0, QC), slice(QC, TQ))

    def needed(c, tile, blk, half):
        return not (blk == tile and c == 0 and half == 1)

    def score_half(ch, tile, blk, half):
        hd, c = divmod(ch, 2)
        q0 = tile * TQ + c * QC
        k0 = blk * TQ + half * QC
        st = lax.dot_general(ka_ref[0, hd, pl.ds(k0, QC), :], qa_ref[0, hd, pl.ds(q0, QC), :],
                             (((1,), (1,)), ((), ())), preferred_element_type=F32)
        st_ref[ch, halves[half], :] = st
        return jnp.max(st, axis=0, keepdims=True)

    def chain_step(ch, tile, blk, m_old, cm, nxt):
        hd, c = divmod(ch, 2)
        diag = blk == tile
        masked = [diag and c == 0, diag and c == 1]

        def load(half):
            st = st_ref[ch, halves[half], :]
            return jnp.where(causal, st, -jnp.inf) if masked[half] else st

        if diag:
            cm = functools.reduce(jnp.maximum, [load(half).max(axis=0, keepdims=True)
                                                for half in range(2) if needed(c, tile, blk, half)])
        m_new = cm if m_old is None else jnp.maximum(m_old, cm)
        pv, cm_next = [], []
        for half in range(2):
            use = needed(c, tile, blk, half)
            if use:
                p = jnp.exp2(load(half) - m_new).astype(BF16)
            if nxt is not None and needed(c, *nxt, half):
                cm_next.append(score_half(ch, *nxt, half))
            if use:
                pv.append(_dot(vt_ref[0, hd, blk, :, halves[half]], p))
        pv = sum(pv[1:], pv[0])
        if m_old is None:
            acc_ref[ch] = pv
        else:
            acc_ref[ch] = acc_ref[ch] * jnp.exp2(m_old - m_new) + pv
        return m_new, (functools.reduce(jnp.maximum, cm_next) if cm_next else None)

    steps = [(qi, j) for qi in range(n_q) for j in range(qi + 1)]
    ms = [None] * 4
    cms = [functools.reduce(jnp.maximum, [score_half(ch, *steps[0], half) for half in range(2)
                                          if needed(ch % 2, *steps[0], half)])
           for ch in range(4)]
    for n, (qi, j) in enumerate(steps):
        nxt = steps[n + 1] if n + 1 < len(steps) else None
        for ch in range(4):
            ms[ch], cms[ch] = chain_step(ch, qi, j, ms[ch], cms[ch], nxt)
        if j == qi:
            for c in range(2):
                o_t = jnp.concatenate(
                    [acc_ref[2 * hd + c, :HEAD_DIM, :] / acc_ref[2 * hd + c, HEAD_DIM:HEAD_DIM + 1, :]
                     for hd in range(2)], axis=0)
                o_ref[0, pl.ds(qi * TQ + c * QC, QC), :] = o_t.T.astype(BF16)
            ms = [None] * 4


def _post_kernel(x_ref, at_ref, sb_ref, gs_ref, wa_ref, wb_ref, wo_ref, g2_ref, wu_ref,
                 wd_ref, gf_ref, o_ref, *, ff_chunk):
    subs = [slice(s * SUB, (s + 1) * SUB) for s in range(TM // SUB)]
    merged = []
    for rows in subs:
        ya = _dot(at_ref[rows, :], wa_ref[...])
        yb = _dot(sb_ref[rows, :], wb_ref[...])
        merged.append((gs_ref[rows, :D_MODEL].astype(F32) * ya
                       + gs_ref[rows, D_MODEL:].astype(F32) * yb).astype(BF16))
    ys = [x_ref[rows, :] + _dot(m, wo_ref[...]) for rows, m in zip(subs, merged)]
    hs = []
    for y in ys:
        ms = jnp.mean(y * y, axis=-1, keepdims=True)
        hs.append((y * lax.rsqrt(ms + EPS) * g2_ref[...]).astype(BF16))
    for c0 in range(0, D_FF, ff_chunk):
        acts = []
        for h in hs:
            a = jnp.maximum(_dot(h, wu_ref[:, c0:c0 + ff_chunk]), 0.0)
            acts.append((a * a).astype(BF16))
        ys = [y + _dot(a, wd_ref[c0:c0 + ff_chunk, :]) for y, a in zip(ys, acts)]
    for rows, y in zip(subs, ys):
        ms = jnp.mean(y * y, axis=-1, keepdims=True)
        o_ref[rows, :] = y * lax.rsqrt(ms + EPS) * gf_ref[...]


def _params(n_axes):
    return pltpu.CompilerParams(dimension_semantics=("arbitrary",) * n_axes,
                                vmem_limit_bytes=VMEM_LIMIT)


def _const(shape):
    return pl.BlockSpec(shape, lambda *_: (0,) * len(shape), pipeline_mode=pl.Buffered(1))


def _rows(width, rows=TM):
    return pl.BlockSpec((rows, width), lambda i: (i, 0))


def _prep(w):
    _, d, n = w.shape
    widths = (_F0, LANES, _R_COLS)
    return pl.pallas_call(
        _prep_kernel,
        grid=(d // PREP_ROWS,),
        in_specs=[pl.BlockSpec((None, PREP_ROWS, n), lambda i: (0, i, 0))],
        out_specs=[_rows(width, PREP_ROWS) for width in widths],
        out_shape=[jax.ShapeDtypeStruct((d, width), BF16) for width in widths],
        compiler_params=_params(1),
        name="prep",
    )(w)


def _pre(xt, bsz, seq, g1, w_qkv, w_f, w_rest, bf_pad, ln_g, ln_b, w_sgu, bm):
    n_tok, d = xt.shape
    tiles_per_seq = seq // TP
    head_map = lambda i: (i // tiles_per_seq, 0, i % tiles_per_seq, 0)
    return pl.pallas_call(
        functools.partial(_pre_kernel, tiles_per_seq=tiles_per_seq),
        grid=(n_tok // TP,),
        in_specs=[
            _rows(d, TP),
            _const((1, d)),
            _const((d, _F0)),
            _const((d, LANES)),
            _const((d, _R_COLS)),
            _const((1, LANES)),
            _const((1, SGU_WIDTH)),
            _const((1, SGU_WIDTH)),
            _const((SGU_GROUPS, SGU_LEN, SGU_LEN)),
            _const((SGU_LEN, SGU_WIDTH)),
        ],
        out_specs=[
            pl.BlockSpec((1, FOX_HEADS, TP, LANES), head_map),
            pl.BlockSpec((1, FOX_HEADS, TP, LANES), head_map),
            pl.BlockSpec((1, FOX_HEADS, TP // TQ, VT_ROWS, TQ), lambda i: head_map(i) + (0,)),
            _rows(SGU_WIDTH, TP),
            _rows(2 * d, TP),
        ],
        out_shape=[
            jax.ShapeDtypeStruct((bsz, FOX_HEADS, seq, LANES), BF16),
            jax.ShapeDtypeStruct((bsz, FOX_HEADS, seq, LANES), BF16),
            jax.ShapeDtypeStruct((bsz, FOX_HEADS, seq // TQ, VT_ROWS, TQ), BF16),
            jax.ShapeDtypeStruct((n_tok, SGU_WIDTH), BF16),
            jax.ShapeDtypeStruct((n_tok, 2 * d), BF16),
        ],
        scratch_shapes=[pltpu.VMEM((1, LANES), F32)],
        compiler_params=_params(1),
        name="pre",
    )(xt, g1, w_qkv, w_f, w_rest, bf_pad, ln_g, ln_b, w_sgu, bm)


def _attn(qa, ka, vt):
    bsz, _, seq, _ = qa.shape
    return pl.pallas_call(
        functools.partial(_attn_kernel, n_q=seq // TQ),
        grid=(bsz, FOX_HEADS // 2),
        in_specs=[
            pl.BlockSpec((1, 2, seq, LANES), lambda b, p: (b, p, 0, 0)),
            pl.BlockSpec((1, 2, seq, LANES), lambda b, p: (b, p, 0, 0)),
            pl.BlockSpec((1, 2, seq // TQ, VT_ROWS, TQ), lambda b, p: (b, p, 0, 0, 0)),
        ],
        out_specs=pl.BlockSpec((1, seq, LANES), lambda b, p: (b, 0, p)),
        out_shape=jax.ShapeDtypeStruct((bsz, seq, FOX_WIDTH), BF16),
        scratch_shapes=[pltpu.VMEM((4, VT_ROWS, QC), F32), pltpu.VMEM((4, TQ, QC), F32)],
        compiler_params=_params(2),
        name="attn",
    )(qa, ka, vt)


def _post(xt, att, sb, gs, w_a, w_b, w_o, g2, w_up, w_down, gf):
    n_tok, d = xt.shape
    return pl.pallas_call(
        functools.partial(_post_kernel, ff_chunk=1024),
        grid=(n_tok // TM,),
        in_specs=[
            _rows(d),
            _rows(FOX_WIDTH),
            _rows(SGU_WIDTH),
            _rows(2 * d),
            _const((FOX_WIDTH, d)),
            _const((SGU_WIDTH, d)),
            _const((d, d)),
            _const((1, d)),
            _const((d, D_FF)),
            _const((D_FF, d)),
            _const((1, d)),
        ],
        out_specs=_rows(d),
        out_shape=jax.ShapeDtypeStruct((n_tok, d), F32),
        compiler_params=_params(1),
        name="post",
    )(xt, att, sb, gs, w_a, w_b, w_o, g2, w_up, w_down, gf)


def kernel(x, norm1_g, w_in, b_f, ln_v_g, ln_v_b, w_sgu, b_sgu, w_a, w_b, w_o,
           norm2_g, w_up, w_down, normf_g):
    bsz, seq, d = x.shape
    assert d == D_MODEL and seq % TP == 0 and TP % TQ == 0 and TQ % SUB == 0 and TM % SUB == 0
    assert norm1_g.shape[0] == 1, "single-layer block"
    n_tok = bsz * seq
    xt = x.reshape(n_tok, d)

    w_qkv, w_f, w_rest = _prep(w_in)
    bf_pad = jnp.pad(jnp.tile(b_f[0], N_PARTS), (0, LANES - N_PARTS * FOX_HEADS)).reshape(1, LANES)
    bm = jnp.repeat(jnp.transpose(b_sgu[0]), HEAD_DIM, axis=1)

    qa, ka, vt, sb, gs = _pre(xt, bsz, seq, norm1_g.reshape(1, d), w_qkv, w_f, w_rest, bf_pad,
                              ln_v_g.reshape(1, SGU_WIDTH), ln_v_b.reshape(1, SGU_WIDTH),
                              w_sgu[0], bm)
    att = _attn(qa, ka, vt)
    out = _post(xt, att.reshape(n_tok, FOX_WIDTH), sb, gs, w_a[0].astype(BF16),
                w_b[0].astype(BF16), w_o[0].astype(BF16), norm2_g.reshape(1, d),
                w_up[0].astype(BF16), w_down[0].astype(BF16), normf_g.reshape(1, d))
    return out.reshape(bsz, seq, d)
```

```python
import functools
import math

import jax
import jax.numpy as jnp
from jax import lax
from jax.experimental import pallas as pl
from jax.experimental.pallas import tpu as pltpu

D_MODEL = 1024
HEAD_DIM = 64
FOX_HEADS = 8
FOX_WIDTH = FOX_HEADS * HEAD_DIM
SGU_GROUPS = 8
SGU_WIDTH = 512
SGU_LEN = 128
CHUNK = 64
D_FF = 4 * D_MODEL
EPS = 1e-6

LANES = 128
TM = 512
TP = 1024
TQ = 512
QC = 256
SUB = 256
VT_ROWS = 80
PREP_ROWS = 128
LOG2E = math.log2(math.e)
VMEM_LIMIT = 56 * 1024 * 1024

_F0 = 3 * FOX_WIDTH
_R0 = _F0 + FOX_HEADS
_R_COLS = 2 * SGU_WIDTH + 2 * D_MODEL
N_PARTS = 3

BF16 = jnp.bfloat16
F32 = jnp.float32


def _dot(a, b):
    return jnp.dot(a, b, preferred_element_type=F32)


def _gelu_tanh(x):
    c = math.sqrt(2.0 / math.pi)
    return 0.5 * x * (1.0 + jnp.tanh(c * (x + 0.044715 * (x * x * x))))


def _sigmoid(x):
    return 0.5 * jnp.tanh(0.5 * x) + 0.5


def _split3(x):
    hi = x.astype(BF16).astype(F32)
    r = x - hi
    mid = r.astype(BF16).astype(F32)
    return hi, mid, r - mid


def _prep_kernel(w_ref, wqkv_ref, wf_ref, wr_ref):
    x = w_ref[...]
    wqkv_ref[...] = x[:, :_F0].astype(BF16)
    blk = x[:, _F0:_F0 + LANES]
    lane = lax.broadcasted_iota(jnp.int32, blk.shape, 1)
    wf = jnp.zeros_like(blk)
    for rep in reversed(range(N_PARTS)):
        shifted = blk if rep == 0 else pltpu.roll(blk, rep * FOX_HEADS, 1)
        wf = jnp.where(lane < (rep + 1) * FOX_HEADS, shifted, wf)
    wf_ref[...] = wf.astype(BF16)
    wr_ref[...] = x[:, _R0:_R0 + _R_COLS].astype(BF16)


def _pre_kernel(x_ref, g1_ref, wqkv_ref, wf_ref, wr_ref, bf_ref, lng_ref, lnb_ref, wm_ref, bm_ref,
                qa_ref, ka_ref, vt_ref, sb_ref, gs_ref, carry_ref, *, tiles_per_seq):
    i = pl.program_id(0)

    @pl.when(i % tiles_per_seq == 0)
    def _():
        carry_ref[...] = jnp.zeros_like(carry_ref)

    r_i = lax.broadcasted_iota(jnp.int32, (SUB, SUB), 0)
    c_i = lax.broadcasted_iota(jnp.int32, (SUB, SUB), 1)
    tri = jnp.where(r_i >= c_i, 1.0, 0.0).astype(BF16)
    lane = lax.broadcasted_iota(jnp.int32, (SUB, LANES), 1)
    low = lane < HEAD_DIM
    ones_row = jnp.where(lax.broadcasted_iota(jnp.int32, (VT_ROWS - HEAD_DIM, SUB), 0) == 0,
                         1.0, 0.0)
    n_win = SUB // SGU_LEN
    wi = lax.broadcasted_iota(jnp.int32, (SGU_LEN, SGU_LEN), 0) // CHUNK
    wj = lax.broadcasted_iota(jnp.int32, (SGU_LEN, SGU_LEN), 1) // CHUNK
    wmask = wj <= wi
    lane_w = lax.broadcasted_iota(jnp.int32, (SGU_LEN, n_win * LANES), 1)
    low_w = (lane_w % LANES) < HEAD_DIM

    def proj(h, w_ref, c0, width):
        return _dot(h, w_ref[:, c0:c0 + width])

    carry = carry_ref[...]
    for sub in range(TP // SUB):
        rows = slice(sub * SUB, (sub + 1) * SUB)
        kblk, kcols = divmod(sub * SUB, TQ)
        x = x_ref[rows, :]
        ms = jnp.mean(x * x, axis=-1, keepdims=True)
        h = (x * lax.rsqrt(ms + EPS) * g1_ref[...]).astype(BF16)

        z = _dot(h, wf_ref[...]) + bf_ref[...]
        usv = proj(h, wr_ref, 0, 2 * SGU_WIDTH)
        k_all = proj(h, wqkv_ref, FOX_WIDTH, FOX_WIDTH)

        logf = jnp.minimum(z, 0.0) - jnp.log(1.0 + jnp.exp(-jnp.abs(z)))
        hi, mid, lo = _split3(logf)
        part = jnp.where(lane < FOX_HEADS, hi, jnp.where(lane < 2 * FOX_HEADS, mid, lo))
        cl = _dot(tri, part.astype(BF16))
        c = carry + (cl + pltpu.roll(cl, LANES - FOX_HEADS, 1)
                     + pltpu.roll(cl, LANES - 2 * FOX_HEADS, 1))
        carry = c[SUB - 1:SUB, :]

        q_all = proj(h, wqkv_ref, 0, FOX_WIDTH) * (HEAD_DIM ** -0.5 * LOG2E)
        g_a = proj(h, wr_ref, 2 * SGU_WIDTH, D_MODEL)

        u = _gelu_tanh(usv[:, :SGU_WIDTH])
        sv = _gelu_tanh(usv[:, SGU_WIDTH:])
        mu = jnp.mean(sv, axis=-1, keepdims=True)
        xc = sv - mu
        var = jnp.mean(xc * xc, axis=-1, keepdims=True)
        svn = (xc * lax.rsqrt(var + EPS) * lng_ref[...] + lnb_ref[...]).astype(BF16)
        for jp in range(SGU_GROUPS // 2):
            sl = slice(LANES * jp, LANES * (jp + 1))
            chunk = svn[:, sl]
            rhs = jnp.concatenate(
                [chunk[SGU_LEN * w:SGU_LEN * (w + 1), :] for w in range(n_win)], axis=1)
            wa = jnp.where(wmask, wm_ref[2 * jp], 0.0).astype(BF16)
            wb = jnp.where(wmask, wm_ref[2 * jp + 1], 0.0).astype(BF16)
            mixed = jnp.where(low_w, _dot(wa, rhs), _dot(wb, rhs))
            mixed = mixed + jnp.concatenate([bm_ref[:, sl]] * n_win, axis=1)
            mixed = jnp.concatenate(
                [mixed[:, LANES * w:LANES * (w + 1)] for w in range(n_win)], axis=0)
            sb_ref[rows, sl] = (u[:, sl] * mixed).astype(BF16)
        gs_ref[rows, :D_MODEL] = _sigmoid(g_a).astype(BF16)

        v_all = proj(h, wqkv_ref, 2 * FOX_WIDTH, FOX_WIDTH)
        g_b = proj(h, wr_ref, 2 * SGU_WIDTH + D_MODEL, D_MODEL)
        for jp in range(FOX_HEADS // 2):
            sl = slice(LANES * jp, LANES * (jp + 1))
            qc, kc = q_all[:, sl], k_all[:, sl]
            vt = v_all[:, sl].T
            for par in range(2):
                hd = 2 * jp + par
                a0 = HEAD_DIM if par == 0 else 0
                data = low if par == 0 else jnp.logical_not(low)
                in_aug = (lane >= a0) & (lane < a0 + N_PARTS)
                qa = jnp.where(data, qc, jnp.where(in_aug, 1.0, 0.0))
                nc = jnp.broadcast_to(c[:, hd:hd + 1], (SUB, LANES)) * (-LOG2E)
                hi, mid, lo = _split3(nc)
                aug = jnp.where(lane == a0, hi,
                                jnp.where(lane == a0 + 1, mid,
                                          jnp.where(lane == a0 + 2, lo, 0.0)))
                ka = jnp.where(data, kc, aug)
                qa_ref[0, hd, rows, :] = qa.astype(BF16)
                ka_ref[0, hd, rows, :] = ka.astype(BF16)
                vta = jnp.concatenate([vt[par * HEAD_DIM:(par + 1) * HEAD_DIM], ones_row], axis=0)
                vt_ref[0, hd, kblk, :, kcols:kcols + SUB] = vta.astype(BF16)
        gs_ref[rows, D_MODEL:] = _sigmoid(g_b).astype(BF16)
    carry_ref[...] = carry


def _attn_kernel(qa_ref, ka_ref, vt_ref, o_ref, acc_ref, st_ref, *, n_q):
    k_i = lax.broadcasted_iota(jnp.int32, (QC, QC), 0)
    q_i = lax.broadcasted_iota(jnp.int32, (QC, QC), 1)
    causal = k_i <= q_i
    halves = (slice(0, QC), slice(QC, TQ))

    def needed(c, tile, blk, half):
        return not (blk == tile and c == 0 and half == 1)

    def score_half(ch, tile, blk, half):
        hd, c = divmod(ch, 2)
        q0 = tile * TQ + c * QC
        k0 = blk * TQ + half * QC
        st = lax.dot_general(ka_ref[0, hd, pl.ds(k0, QC), :], qa_ref[0, hd, pl.ds(q0, QC), :],
                             (((1,), (1,)), ((), ())), preferred_element_type=F32)
        st_ref[ch, halves[half], :] = st
        return jnp.max(st, axis=0, keepdims=True)

    def chain_step(ch, tile, blk, m_old, cm, nxt):
        hd, c = divmod(ch, 2)
        diag = blk == tile
        masked = [diag and c == 0, diag and c == 1]

        def load(half):
            st = st_ref[ch, halves[half], :]
            return jnp.where(causal, st, -jnp.inf) if masked[half] else st

        if diag:
            cm = functools.reduce(jnp.maximum, [load(half).max(axis=0, keepdims=True)
                                                for half in range(2) if needed(c, tile, blk, half)])
        m_new = cm if m_old is None else jnp.maximum(m_old, cm)
        pv, cm_next = [], []
        for half in range(2):
            use = needed(c, tile, blk, half)
            if use:
                p = jnp.exp2(load(half) - m_new).astype(BF16)
            if nxt is not None and needed(c, *nxt, half):
                cm_next.append(score_half(ch, *nxt, half))
            if use:
                pv.append(_dot(vt_ref[0, hd, blk, :, halves[half]], p))
        pv = sum(pv[1:], pv[0])
        if m_old is None:
            acc_ref[ch] = pv
        else:
            acc_ref[ch] = acc_ref[ch] * jnp.exp2(m_old - m_new) + pv
        return m_new, (functools.reduce(jnp.maximum, cm_next) if cm_next else None)

    steps = [(qi, j) for qi in range(n_q) for j in range(qi + 1)]
    ms = [None] * 4
    cms = [functools.reduce(jnp.maximum, [score_half(ch, *steps[0], half) for half in range(2)
                                          if needed(ch % 2, *steps[0], half)])
           for ch in range(4)]
    for n, (qi, j) in enumerate(steps):
        nxt = steps[n + 1] if n + 1 < len(steps) else None
        for ch in range(4):
            ms[ch], cms[ch] = chain_step(ch, qi, j, ms[ch], cms[ch], nxt)
        if j == qi:
            for c in range(2):
                o_t = jnp.concatenate(
                    [acc_ref[2 * hd + c, :HEAD_DIM, :] / acc_ref[2 * hd + c, HEAD_DIM:HEAD_DIM + 1, :]
                     for hd in range(2)], axis=0)
                o_ref[0, pl.ds(qi * TQ + c * QC, QC), :] = o_t.T.astype(BF16)
            ms = [None] * 4


def _post_kernel(x_ref, at_ref, sb_ref, gs_ref, wa_ref, wb_ref, wo_ref, g2_ref, wu_ref,
                 wd_ref, gf_ref, o_ref, *, ff_chunk):
    subs = [slice(s * SUB, (s + 1) * SUB) for s in range(TM // SUB)]
    merged = []
    for rows in subs:
        ya = _dot(at_ref[rows, :], wa_ref[...])
        yb = _dot(sb_ref[rows, :], wb_ref[...])
        merged.append((gs_ref[rows, :D_MODEL].astype(F32) * ya
                       + gs_ref[rows, D_MODEL:].astype(F32) * yb).astype(BF16))
    ys = [x_ref[rows, :] + _dot(m, wo_ref[...]) for rows, m in zip(subs, merged)]
    hs = []
    for y in ys:
        ms = jnp.mean(y * y, axis=-1, keepdims=True)
        hs.append((y * lax.rsqrt(ms + EPS) * g2_ref[...]).astype(BF16))
    for c0 in range(0, D_FF, ff_chunk):
        acts = []
        for h in hs:
            a = jnp.maximum(_dot(h, wu_ref[:, c0:c0 + ff_chunk]), 0.0)
            acts.append((a * a).astype(BF16))
        ys = [y + _dot(a, wd_ref[c0:c0 + ff_chunk, :]) for y, a in zip(ys, acts)]
    for rows, y in zip(subs, ys):
        ms = jnp.mean(y * y, axis=-1, keepdims=True)
        o_ref[rows, :] = y * lax.rsqrt(ms + EPS) * gf_ref[...]


def _params(n_axes):
    return pltpu.CompilerParams(dimension_semantics=("arbitrary",) * n_axes,
                                vmem_limit_bytes=VMEM_LIMIT)


def _const(shape):
    return pl.BlockSpec(shape, lambda *_: (0,) * len(shape), pipeline_mode=pl.Buffered(1))


def _rows(width, rows=TM):
    return pl.BlockSpec((rows, width), lambda i: (i, 0))


def _prep(w):
    d, n = w.shape
    widths = (_F0, LANES, _R_COLS)
    return pl.pallas_call(
        _prep_kernel,
        grid=(d // PREP_ROWS,),
        in_specs=[_rows(n, PREP_ROWS)],
        out_specs=[_rows(width, PREP_ROWS) for width in widths],
        out_shape=[jax.ShapeDtypeStruct((d, width), BF16) for width in widths],
        compiler_params=_params(1),
        name="prep",
    )(w)


def _pre(xt, bsz, seq, g1, w_qkv, w_f, w_rest, bf_pad, ln_g, ln_b, w_sgu, bm):
    n_tok, d = xt.shape
    tiles_per_seq = seq // TP
    head_map = lambda i: (i // tiles_per_seq, 0, i % tiles_per_seq, 0)
    return pl.pallas_call(
        functools.partial(_pre_kernel, tiles_per_seq=tiles_per_seq),
        grid=(n_tok // TP,),
        in_specs=[
            _rows(d, TP),
            _const((1, d)),
            _const((d, _F0)),
            _const((d, LANES)),
            _const((d, _R_COLS)),
            _const((1, LANES)),
            _const((1, SGU_WIDTH)),
            _const((1, SGU_WIDTH)),
            _const((SGU_GROUPS, SGU_LEN, SGU_LEN)),
            _const((SGU_LEN, SGU_WIDTH)),
        ],
        out_specs=[
            pl.BlockSpec((1, FOX_HEADS, TP, LANES), head_map),
            pl.BlockSpec((1, FOX_HEADS, TP, LANES), head_map),
            pl.BlockSpec((1, FOX_HEADS, TP // TQ, VT_ROWS, TQ), lambda i: head_map(i) + (0,)),
            _rows(SGU_WIDTH, TP),
            _rows(2 * d, TP),
        ],
        out_shape=[
            jax.ShapeDtypeStruct((bsz, FOX_HEADS, seq, LANES), BF16),
            jax.ShapeDtypeStruct((bsz, FOX_HEADS, seq, LANES), BF16),
            jax.ShapeDtypeStruct((bsz, FOX_HEADS, seq // TQ, VT_ROWS, TQ), BF16),
            jax.ShapeDtypeStruct((n_tok, SGU_WIDTH), BF16),
            jax.ShapeDtypeStruct((n_tok, 2 * d), BF16),
        ],
        scratch_shapes=[pltpu.VMEM((1, LANES), F32)],
        compiler_params=_params(1),
        name="pre",
    )(xt, g1, w_qkv, w_f, w_rest, bf_pad, ln_g, ln_b, w_sgu, bm)


def _attn(qa, ka, vt):
    bsz, _, seq, _ = qa.shape
    return pl.pallas_call(
        functools.partial(_attn_kernel, n_q=seq // TQ),
        grid=(bsz, FOX_HEADS // 2),
        in_specs=[
            pl.BlockSpec((1, 2, seq, LANES), lambda b, p: (b, p, 0, 0)),
            pl.BlockSpec((1, 2, seq, LANES), lambda b, p: (b, p, 0, 0)),
            pl.BlockSpec((1, 2, seq // TQ, VT_ROWS, TQ), lambda b, p: (b, p, 0, 0, 0)),
        ],
        out_specs=pl.BlockSpec((1, seq, LANES), lambda b, p: (b, 0, p)),
        out_shape=jax.ShapeDtypeStruct((bsz, seq, FOX_WIDTH), BF16),
        scratch_shapes=[pltpu.VMEM((4, VT_ROWS, QC), F32), pltpu.VMEM((4, TQ, QC), F32)],
        compiler_params=_params(2),
        name="attn",
    )(qa, ka, vt)


def _post(xt, att, sb, gs, w_a, w_b, w_o, g2, w_up, w_down, gf):
    n_tok, d = xt.shape
    return pl.pallas_call(
        functools.partial(_post_kernel, ff_chunk=1024),
        grid=(n_tok // TM,),
        in_specs=[
            _rows(d),
            _rows(FOX_WIDTH),
            _rows(SGU_WIDTH),
            _rows(2 * d),
            _const((FOX_WIDTH, d)),
            _const((SGU_WIDTH, d)),
            _const((d, d)),
            _const((1, d)),
            _const((d, D_FF)),
            _const((D_FF, d)),
            _const((1, d)),
        ],
        out_specs=_rows(d),
        out_shape=jax.ShapeDtypeStruct((n_tok, d), F32),
        compiler_params=_params(1),
        name="post",
    )(xt, att, sb, gs, w_a, w_b, w_o, g2, w_up, w_down, gf)


def kernel(x, norm1_g, w_in, b_f, ln_v_g, ln_v_b, w_sgu, b_sgu, w_a, w_b, w_o,
           norm2_g, w_up, w_down, normf_g):
    bsz, seq, d = x.shape
    assert d == D_MODEL and seq % TP == 0 and TP % TQ == 0 and TQ % SUB == 0 and TM % SUB == 0
    assert norm1_g.shape[0] == 1, "single-layer block"
    n_tok = bsz * seq
    xt = x.reshape(n_tok, d)

    w_qkv, w_f, w_rest = _prep(w_in.reshape(w_in.shape[1:]))
    bf_pad = jnp.pad(jnp.tile(b_f[0], N_PARTS), (0, LANES - N_PARTS * FOX_HEADS)).reshape(1, LANES)
    bm = jnp.repeat(jnp.transpose(b_sgu[0]), HEAD_DIM, axis=1)

    qa, ka, vt, sb, gs = _pre(xt, bsz, seq, norm1_g.reshape(1, d), w_qkv, w_f, w_rest, bf_pad,
                              ln_v_g.reshape(1, SGU_WIDTH), ln_v_b.reshape(1, SGU_WIDTH),
                              w_sgu[0], bm)
    att = _attn(qa, ka, vt)
    out = _post(xt, att.reshape(n_tok, FOX_WIDTH), sb, gs, w_a[0].astype(BF16),
                w_b[0].astype(BF16), w_o[0].astype(BF16), norm2_g.reshape(1, d),
                w_up[0].astype(BF16), w_down[0].astype(BF16), normf_g.reshape(1, d))
    return out.reshape(bsz, seq, d)
```

```python
import functools
import math

import jax
import jax.numpy as jnp
from jax import lax
from jax.experimental import pallas as pl
from jax.experimental.pallas import tpu as pltpu

D_MODEL = 1024
HEAD_DIM = 64
FOX_HEADS = 8
FOX_WIDTH = FOX_HEADS * HEAD_DIM
SGU_GROUPS = 8
SGU_WIDTH = 512
SGU_LEN = 128
CHUNK = 64
D_FF = 4 * D_MODEL
EPS = 1e-6

LANES = 128
TM = 512
TP = 512
TQ = 512
QC = 256
SUB = 256
VT_ROWS = 128
PREP_ROWS = 128
LOG2E = math.log2(math.e)
VMEM_LIMIT = 56 * 1024 * 1024

_F0 = 3 * FOX_WIDTH
_R0 = _F0 + FOX_HEADS
_R_COLS = 2 * SGU_WIDTH + 2 * D_MODEL
N_PARTS = 3

BF16 = jnp.bfloat16
F32 = jnp.float32


def _dot(a, b):
    return jnp.dot(a, b, preferred_element_type=F32)


def _gelu_tanh(x):
    c = math.sqrt(2.0 / math.pi)
    return 0.5 * x * (1.0 + jnp.tanh(c * (x + 0.044715 * (x * x * x))))


def _sigmoid(x):
    return 0.5 * jnp.tanh(0.5 * x) + 0.5


def _split3(x):
    hi = x.astype(BF16).astype(F32)
    r = x - hi
    mid = r.astype(BF16).astype(F32)
    return hi, mid, r - mid


def _prep_kernel(w_ref, wqkv_ref, wf_ref, wr_ref):
    x = w_ref[...].astype(F32)
    wqkv_ref[...] = x[:, :_F0].astype(BF16)
    blk = x[:, _F0:_F0 + LANES]
    lane = lax.broadcasted_iota(jnp.int32, blk.shape, 1)
    wf = jnp.zeros_like(blk)
    for rep in reversed(range(N_PARTS)):
        shifted = blk if rep == 0 else pltpu.roll(blk, rep * FOX_HEADS, 1)
        wf = jnp.where(lane < (rep + 1) * FOX_HEADS, shifted, wf)
    wf_ref[...] = wf.astype(BF16)
    wr_ref[...] = x[:, _R0:_R0 + _R_COLS].astype(BF16)


def _pre_kernel(x_ref, g1_ref, wqkv_ref, wf_ref, wr_ref, bf_ref, lng_ref, lnb_ref, wm_ref, bm_ref,
                qa_ref, ka_ref, vt_ref, sb_ref, gs_ref, carry_ref, *, tiles_per_seq):
    i = pl.program_id(0)

    @pl.when(i % tiles_per_seq == 0)
    def _():
        carry_ref[...] = jnp.zeros_like(carry_ref)

    r_i = lax.broadcasted_iota(jnp.int32, (SUB, SUB), 0)
    c_i = lax.broadcasted_iota(jnp.int32, (SUB, SUB), 1)
    tri = jnp.where(r_i >= c_i, 1.0, 0.0).astype(BF16)
    lane = lax.broadcasted_iota(jnp.int32, (SUB, LANES), 1)
    low = lane < HEAD_DIM
    ones_row = jnp.where(lax.broadcasted_iota(jnp.int32, (VT_ROWS - HEAD_DIM, SUB), 0) == 0,
                         1.0, 0.0)
    n_win = SUB // SGU_LEN
    wi = lax.broadcasted_iota(jnp.int32, (SGU_LEN, SGU_LEN), 0) // CHUNK
    wj = lax.broadcasted_iota(jnp.int32, (SGU_LEN, SGU_LEN), 1) // CHUNK
    wmask = wj <= wi
    lane_w = lax.broadcasted_iota(jnp.int32, (SGU_LEN, n_win * LANES), 1)
    low_w = (lane_w % LANES) < HEAD_DIM

    def proj(h, w_ref, c0, width):
        return _dot(h, w_ref[:, c0:c0 + width])

    carry = carry_ref[...]
    for sub in range(TP // SUB):
        rows = slice(sub * SUB, (sub + 1) * SUB)
        kblk, kcols = divmod(sub * SUB, TQ)
        x = x_ref[rows, :]
        ms = jnp.mean(x * x, axis=-1, keepdims=True)
        h = (x * lax.rsqrt(ms + EPS) * g1_ref[...]).astype(BF16)

        z = _dot(h, wf_ref[...]) + bf_ref[...]
        usv = proj(h, wr_ref, 0, 2 * SGU_WIDTH)
        k_all = proj(h, wqkv_ref, FOX_WIDTH, FOX_WIDTH)

        logf = jnp.minimum(z, 0.0) - jnp.log(1.0 + jnp.exp(-jnp.abs(z)))
        hi, mid, lo = _split3(logf)
        part = jnp.where(lane < FOX_HEADS, hi, jnp.where(lane < 2 * FOX_HEADS, mid, lo))
        cl = _dot(tri, part.astype(BF16))
        c = carry + (cl + pltpu.roll(cl, LANES - FOX_HEADS, 1)
                     + pltpu.roll(cl, LANES - 2 * FOX_HEADS, 1))
        carry = c[SUB - 1:SUB, :]

        q_all = proj(h, wqkv_ref, 0, FOX_WIDTH) * (HEAD_DIM ** -0.5 * LOG2E)
        g_a = proj(h, wr_ref, 2 * SGU_WIDTH, D_MODEL)

        u = _gelu_tanh(usv[:, :SGU_WIDTH])
        sv = _gelu_tanh(usv[:, SGU_WIDTH:])
        mu = jnp.mean(sv, axis=-1, keepdims=True)
        xc = sv - mu
        var = jnp.mean(xc * xc, axis=-1, keepdims=True)
        svn = (xc * lax.rsqrt(var + EPS) * lng_ref[...] + lnb_ref[...]).astype(BF16)
        for jp in range(SGU_GROUPS // 2):
            sl = slice(LANES * jp, LANES * (jp + 1))
            chunk = svn[:, sl]
            rhs = jnp.concatenate(
                [chunk[SGU_LEN * w:SGU_LEN * (w + 1), :] for w in range(n_win)], axis=1)
            wa = jnp.where(wmask, wm_ref[2 * jp], 0.0).astype(BF16)
            wb = jnp.where(wmask, wm_ref[2 * jp + 1], 0.0).astype(BF16)
            mixed = jnp.where(low_w, _dot(wa, rhs), _dot(wb, rhs))
            mixed = mixed + jnp.concatenate([bm_ref[:, sl]] * n_win, axis=1)
            mixed = jnp.concatenate(
                [mixed[:, LANES * w:LANES * (w + 1)] for w in range(n_win)], axis=0)
            sb_ref[rows, sl] = (u[:, sl] * mixed).astype(BF16)
        gs_ref[rows, :D_MODEL] = _sigmoid(g_a).astype(BF16)

        v_all = proj(h, wqkv_ref, 2 * FOX_WIDTH, FOX_WIDTH)
        g_b = proj(h, wr_ref, 2 * SGU_WIDTH + D_MODEL, D_MODEL)
        for jp in range(FOX_HEADS // 2):
            sl = slice(LANES * jp, LANES * (jp + 1))
            qc, kc = q_all[:, sl], k_all[:, sl]
            vt = v_all[:, sl].T
            for par in range(2):
                hd = 2 * jp + par
                a0 = HEAD_DIM if par == 0 else 0
                data = low if par == 0 else jnp.logical_not(low)
                in_aug = (lane >= a0) & (lane < a0 + N_PARTS)
                qa = jnp.where(data, qc, jnp.where(in_aug, 1.0, 0.0))
                nc = jnp.broadcast_to(c[:, hd:hd + 1], (SUB, LANES)) * (-LOG2E)
                hi, mid, lo = _split3(nc)
                aug = jnp.where(lane == a0, hi,
                                jnp.where(lane == a0 + 1, mid,
                                          jnp.where(lane == a0 + 2, lo, 0.0)))
                ka = jnp.where(data, kc, aug)
                qa_ref[0, hd, rows, :] = qa.astype(BF16)
                ka_ref[0, hd, rows, :] = ka.astype(BF16)
                vta = jnp.concatenate([vt[par * HEAD_DIM:(par + 1) * HEAD_DIM], ones_row], axis=0)
                vt_ref[0, hd, kblk, :, kcols:kcols + SUB] = vta.astype(BF16)
        gs_ref[rows, D_MODEL:] = _sigmoid(g_b).astype(BF16)
    carry_ref[...] = carry


def _attn_kernel(qa_ref, ka_ref, vt_ref, o_ref, acc_ref, st_ref, *, n_q):
    k_i = lax.broadcasted_iota(jnp.int32, (QC, QC), 0)
    q_i = lax.broadcasted_iota(jnp.int32, (QC, QC), 1)
    causal = k_i <= q_i
    halves = (slice(0, QC), slice(QC, TQ))

    def needed(c, tile, blk, half):
        return not (blk == tile and c == 0 and half == 1)

    def score_half(ch, tile, blk, half):
        hd, c = divmod(ch, 2)
        q0 = tile * TQ + c * QC
        k0 = blk * TQ + half * QC
        st = lax.dot_general(ka_ref[0, hd, pl.ds(k0, QC), :], qa_ref[0, hd, pl.ds(q0, QC), :],
                             (((1,), (1,)), ((), ())), preferred_element_type=F32)
        st_ref[ch, halves[half], :] = st
        return jnp.max(st, axis=0, keepdims=True)

    def chain_step(ch, tile, blk, m_old, cm, nxt):
        hd, c = divmod(ch, 2)
        diag = blk == tile
        masked = [diag and c == 0, diag and c == 1]

        def load(half):
            st = st_ref[ch, halves[half], :]
            return jnp.where(causal, st, -jnp.inf) if masked[half] else st

        if diag:
            cm = functools.reduce(jnp.maximum, [load(half).max(axis=0, keepdims=True)
                                                for half in range(2) if needed(c, tile, blk, half)])
        m_new = cm if m_old is None else jnp.maximum(m_old, cm)
        pv, cm_next = [], []
        for half in range(2):
            use = needed(c, tile, blk, half)
            if use:
                p = jnp.exp2(load(half) - m_new).astype(BF16)
            if nxt is not None and needed(c, *nxt, half):
                cm_next.append(score_half(ch, *nxt, half))
            if use:
                pv.append(_dot(vt_ref[0, hd, blk, :, halves[half]], p))
        pv = sum(pv[1:], pv[0])
        if m_old is None:
            acc_ref[ch] = pv
        else:
            acc_ref[ch] = acc_ref[ch] * jnp.exp2(m_old - m_new) + pv
        return m_new, (functools.reduce(jnp.maximum, cm_next) if cm_next else None)

    steps = [(qi, j) for qi in range(n_q) for j in range(qi + 1)]
    ms = [None] * 4
    cms = [functools.reduce(jnp.maximum, [score_half(ch, *steps[0], half) for half in range(2)
                                          if needed(ch % 2, *steps[0], half)])
           for ch in range(4)]
    for n, (qi, j) in enumerate(steps):
        nxt = steps[n + 1] if n + 1 < len(steps) else None
        for ch in range(4):
            ms[ch], cms[ch] = chain_step(ch, qi, j, ms[ch], cms[ch], nxt)
        if j == qi:
            for c in range(2):
                o_t = jnp.concatenate(
                    [acc_ref[2 * hd + c, :HEAD_DIM, :] / acc_ref[2 * hd + c, HEAD_DIM:HEAD_DIM + 1, :]
                     for hd in range(2)], axis=0)
                o_ref[0, pl.ds(qi * TQ + c * QC, QC), :] = o_t.T.astype(BF16)
            ms = [None] * 4


def _post_kernel(x_ref, at_ref, sb_ref, gs_ref, wa_ref, wb_ref, wo_ref, g2_ref, wu_ref,
                 wd_ref, gf_ref, o_ref, *, ff_chunk):
    subs = [slice(s * SUB, (s + 1) * SUB) for s in range(TM // SUB)]
    merged = []
    for rows in subs:
        ya = _dot(at_ref[rows, :], wa_ref[...])
        yb = _dot(sb_ref[rows, :], wb_ref[...])
        merged.append((gs_ref[rows, :D_MODEL].astype(F32) * ya
                       + gs_ref[rows, D_MODEL:].astype(F32) * yb).astype(BF16))
    ys = [x_ref[rows, :] + _dot(m, wo_ref[...]) for rows, m in zip(subs, merged)]
    hs = []
    for y in ys:
        ms = jnp.mean(y * y, axis=-1, keepdims=True)
        hs.append((y * lax.rsqrt(ms + EPS) * g2_ref[...]).astype(BF16))
    for c0 in range(0, D_FF, ff_chunk):
        acts = []
        for h in hs:
            a = jnp.maximum(_dot(h, wu_ref[:, c0:c0 + ff_chunk]), 0.0)
            acts.append((a * a).astype(BF16))
        ys = [y + _dot(a, wd_ref[c0:c0 + ff_chunk, :]) for y, a in zip(ys, acts)]
    for rows, y in zip(subs, ys):
        ms = jnp.mean(y * y, axis=-1, keepdims=True)
        o_ref[rows, :] = y * lax.rsqrt(ms + EPS) * gf_ref[...]


def _params(n_axes):
    return pltpu.CompilerParams(dimension_semantics=("arbitrary",) * n_axes,
                                vmem_limit_bytes=VMEM_LIMIT)


def _const(shape):
    return pl.BlockSpec(shape, lambda *_: (0,) * len(shape), pipeline_mode=pl.Buffered(1))


def _rows(width, rows=TM):
    return pl.BlockSpec((rows, width), lambda i: (i, 0))


def _prep(w):
    d, n = w.shape
    widths = (_F0, LANES, _R_COLS)
    return pl.pallas_call(
        _prep_kernel,
        grid=(d // PREP_ROWS,),
        in_specs=[_rows(n, PREP_ROWS)],
        out_specs=[_rows(width, PREP_ROWS) for width in widths],
        out_shape=[jax.ShapeDtypeStruct((d, width), BF16) for width in widths],
        compiler_params=_params(1),
        name="prep",
    )(w)


def _pre(xt, bsz, seq, g1, w_qkv, w_f, w_rest, bf_pad, ln_g, ln_b, w_sgu, bm):
    n_tok, d = xt.shape
    tiles_per_seq = seq // TP
    head_map = lambda i: (i // tiles_per_seq, 0, i % tiles_per_seq, 0)
    return pl.pallas_call(
        functools.partial(_pre_kernel, tiles_per_seq=tiles_per_seq),
        grid=(n_tok // TP,),
        in_specs=[
            _rows(d, TP),
            _const((1, d)),
            _const((d, _F0)),
            _const((d, LANES)),
            _const((d, _R_COLS)),
            _const((1, LANES)),
            _const((1, SGU_WIDTH)),
            _const((1, SGU_WIDTH)),
            _const((SGU_GROUPS, SGU_LEN, SGU_LEN)),
            _const((SGU_LEN, SGU_WIDTH)),
        ],
        out_specs=[
            pl.BlockSpec((1, FOX_HEADS, TP, LANES), head_map),
            pl.BlockSpec((1, FOX_HEADS, TP, LANES), head_map),
            pl.BlockSpec((1, FOX_HEADS, TP // TQ, VT_ROWS, TQ), lambda i: head_map(i) + (0,)),
            _rows(SGU_WIDTH, TP),
            _rows(2 * d, TP),
        ],
        out_shape=[
            jax.ShapeDtypeStruct((bsz, FOX_HEADS, seq, LANES), BF16),
            jax.ShapeDtypeStruct((bsz, FOX_HEADS, seq, LANES), BF16),
            jax.ShapeDtypeStruct((bsz, FOX_HEADS, seq // TQ, VT_ROWS, TQ), BF16),
            jax.ShapeDtypeStruct((n_tok, SGU_WIDTH), BF16),
            jax.ShapeDtypeStruct((n_tok, 2 * d), BF16),
        ],
        scratch_shapes=[pltpu.VMEM((1, LANES), F32)],
        compiler_params=_params(1),
        name="pre",
    )(xt, g1, w_qkv, w_f, w_rest, bf_pad, ln_g, ln_b, w_sgu, bm)


def _attn(qa, ka, vt):
    bsz, _, seq, _ = qa.shape
    return pl.pallas_call(
        functools.partial(_attn_kernel, n_q=seq // TQ),
        grid=(bsz, FOX_HEADS // 2),
        in_specs=[
            pl.BlockSpec((1, 2, seq, LANES), lambda b, p: (b, p, 0, 0)),
            pl.BlockSpec((1, 2, seq, LANES), lambda b, p: (b, p, 0, 0)),
            pl.BlockSpec((1, 2, seq // TQ, VT_ROWS, TQ), lambda b, p: (b, p, 0, 0, 0)),
        ],
        out_specs=pl.BlockSpec((1, seq, LANES), lambda b, p: (b, 0, p)),
        out_shape=jax.ShapeDtypeStruct((bsz, seq, FOX_WIDTH), BF16),
        scratch_shapes=[pltpu.VMEM((4, VT_ROWS, QC), F32), pltpu.VMEM((4, TQ, QC), F32)],
        compiler_params=_params(2),
        name="attn",
    )(qa, ka, vt)


def _post(xt, att, sb, gs, w_a, w_b, w_o, g2, w_up, w_down, gf):
    n_tok, d = xt.shape
    return pl.pallas_call(
        functools.partial(_post_kernel, ff_chunk=1024),
        grid=(n_tok // TM,),
        in_specs=[
            _rows(d),
            _rows(FOX_WIDTH),
            _rows(SGU_WIDTH),
            _rows(2 * d),
            _const((FOX_WIDTH, d)),
            _const((SGU_WIDTH, d)),
            _const((d, d)),
            _const((1, d)),
            _const((d, D_FF)),
            _const((D_FF, d)),
            _const((1, d)),
        ],
        out_specs=_rows(d),
        out_shape=jax.ShapeDtypeStruct((n_tok, d), F32),
        compiler_params=_params(1),
        name="post",
    )(xt, att, sb, gs, w_a, w_b, w_o, g2, w_up, w_down, gf)


def kernel(x, norm1_g, w_in, b_f, ln_v_g, ln_v_b, w_sgu, b_sgu, w_a, w_b, w_o,
           norm2_g, w_up, w_down, normf_g):
    bsz, seq, d = x.shape
    assert d == D_MODEL and seq % TP == 0 and TP % TQ == 0 and TQ % SUB == 0 and TM % SUB == 0
    assert norm1_g.shape[0] == 1, "single-layer block"
    n_tok = bsz * seq
    xt = x.reshape(n_tok, d)

    w_qkv, w_f, w_rest = _prep(w_in.astype(BF16).reshape(w_in.shape[1:]))
    bf_pad = jnp.pad(jnp.tile(b_f[0], N_PARTS), (0, LANES - N_PARTS * FOX_HEADS)).reshape(1, LANES)
    bm = jnp.repeat(jnp.transpose(b_sgu[0]), HEAD_DIM, axis=1)

    qa, ka, vt, sb, gs = _pre(xt, bsz, seq, norm1_g.reshape(1, d), w_qkv, w_f, w_rest, bf_pad,
                              ln_v_g.reshape(1, SGU_WIDTH), ln_v_b.reshape(1, SGU_WIDTH),
                              w_sgu[0], bm)
    att = _attn(qa, ka, vt)
    out = _post(xt, att.reshape(n_tok, FOX_WIDTH), sb, gs, w_a[0].astype(BF16),
                w_b[0].astype(BF16), w_o[0].astype(BF16), norm2_g.reshape(1, d),
                w_up[0].astype(BF16), w_down[0].astype(BF16), normf_g.reshape(1, d))
    return out.reshape(bsz, seq, d)
```

```python
import functools
import math

import jax
import jax.numpy as jnp
from jax import lax
from jax.experimental import pallas as pl
from jax.experimental.pallas import tpu as pltpu

D_MODEL = 1024
HEAD_DIM = 64
FOX_HEADS = 8
FOX_WIDTH = FOX_HEADS * HEAD_DIM
SGU_GROUPS = 8
SGU_WIDTH = 512
SGU_LEN = 128
CHUNK = 64
D_FF = 4 * D_MODEL
EPS = 1e-6

LANES = 128
TM = 512
TP = 512
TQ = 512
QC = 256
SUB = 256
VT_ROWS = 128
PREP_ROWS = 128
LOG2E = math.log2(math.e)
VMEM_LIMIT = 56 * 1024 * 1024

_F0 = 3 * FOX_WIDTH
_R0 = _F0 + FOX_HEADS
_R_COLS = 2 * SGU_WIDTH + 2 * D_MODEL
N_PARTS = 3
GATE_ROWS = 32

BF16 = jnp.bfloat16
F32 = jnp.float32


def _dot(a, b):
    return jnp.dot(a, b, preferred_element_type=F32)


def _gelu_tanh(x):
    c = math.sqrt(2.0 / math.pi)
    return 0.5 * x * (1.0 + jnp.tanh(c * (x + 0.044715 * (x * x * x))))


def _sigmoid(x):
    return 0.5 * jnp.tanh(0.5 * x) + 0.5


def _split3(x):
    hi = x.astype(BF16).astype(F32)
    r = x - hi
    mid = r.astype(BF16).astype(F32)
    return hi, mid, r - mid


def _prep_kernel(w_ref, wqkv_ref, wft_ref, wr_ref):
    x = w_ref[...].astype(F32)
    wqkv_ref[...] = x[:, :_F0].astype(BF16)
    blk = x[:, _F0:_F0 + LANES]
    lane = lax.broadcasted_iota(jnp.int32, blk.shape, 1)
    wf = jnp.zeros_like(blk)
    for rep in reversed(range(N_PARTS)):
        shifted = blk if rep == 0 else pltpu.roll(blk, rep * FOX_HEADS, 1)
        wf = jnp.where(lane < (rep + 1) * FOX_HEADS, shifted, wf)
    wft_ref[...] = wf.T[:GATE_ROWS].astype(BF16)
    wr_ref[...] = x[:, _R0:_R0 + _R_COLS].astype(BF16)


def _pre_kernel(x_ref, g1_ref, wqkv_ref, wft_ref, wr_ref, bf_ref, lng_ref, lnb_ref, wm_ref, bm_ref,
                qa_ref, ka_ref, vt_ref, sb_ref, gs_ref, carry_ref, *, tiles_per_seq):
    i = pl.program_id(0)

    @pl.when(i % tiles_per_seq == 0)
    def _():
        carry_ref[...] = jnp.zeros_like(carry_ref)

    s_i = lax.broadcasted_iota(jnp.int32, (SUB, SUB), 0)
    t_i = lax.broadcasted_iota(jnp.int32, (SUB, SUB), 1)
    tri = jnp.where(s_i <= t_i, 1.0, 0.0).astype(BF16)
    grow = lax.broadcasted_iota(jnp.int32, (GATE_ROWS, SUB), 0)
    head_pad = jnp.zeros((LANES - FOX_HEADS, SUB), F32)
    lane = lax.broadcasted_iota(jnp.int32, (SUB, LANES), 1)
    low = lane < HEAD_DIM
    ones_row = jnp.where(lax.broadcasted_iota(jnp.int32, (VT_ROWS - HEAD_DIM, SUB), 0) == 0,
                         1.0, 0.0)
    n_win = SUB // SGU_LEN
    wi = lax.broadcasted_iota(jnp.int32, (SGU_LEN, SGU_LEN), 0) // CHUNK
    wj = lax.broadcasted_iota(jnp.int32, (SGU_LEN, SGU_LEN), 1) // CHUNK
    wmask = wj <= wi
    lane_w = lax.broadcasted_iota(jnp.int32, (SGU_LEN, n_win * LANES), 1)
    low_w = (lane_w % LANES) < HEAD_DIM

    def proj(h, w_ref, c0, width):
        return _dot(h, w_ref[:, c0:c0 + width])

    carry = carry_ref[...]
    for sub in range(TP // SUB):
        rows = slice(sub * SUB, (sub + 1) * SUB)
        kblk, kcols = divmod(sub * SUB, TQ)
        x = x_ref[rows, :]
        ms = jnp.mean(x * x, axis=-1, keepdims=True)
        h = (x * lax.rsqrt(ms + EPS) * g1_ref[...]).astype(BF16)

        z = lax.dot_general(wft_ref[...], h, (((1,), (1,)), ((), ())),
                            preferred_element_type=F32) + bf_ref[...]
        usv = proj(h, wr_ref, 0, 2 * SGU_WIDTH)
        k_all = proj(h, wqkv_ref, FOX_WIDTH, FOX_WIDTH)

        logf = jnp.minimum(z, 0.0) - jnp.log(1.0 + jnp.exp(-jnp.abs(z)))
        hi, mid, lo = _split3(logf)
        part = jnp.where(grow < FOX_HEADS, hi, jnp.where(grow < 2 * FOX_HEADS, mid, lo))
        cl = _dot(part.astype(BF16), tri)
        ct = carry + (cl[:FOX_HEADS] + cl[FOX_HEADS:2 * FOX_HEADS]
                      + cl[2 * FOX_HEADS:3 * FOX_HEADS])
        carry = jnp.broadcast_to(ct[:, SUB - 1:SUB], ct.shape)
        c = jnp.concatenate([ct, head_pad], axis=0).T

        q_all = proj(h, wqkv_ref, 0, FOX_WIDTH) * (HEAD_DIM ** -0.5 * LOG2E)
        g_a = proj(h, wr_ref, 2 * SGU_WIDTH, D_MODEL)

        u = _gelu_tanh(usv[:, :SGU_WIDTH])
        sv = _gelu_tanh(usv[:, SGU_WIDTH:])
        mu = jnp.mean(sv, axis=-1, keepdims=True)
        xc = sv - mu
        var = jnp.mean(xc * xc, axis=-1, keepdims=True)
        svn = (xc * lax.rsqrt(var + EPS) * lng_ref[...] + lnb_ref[...]).astype(BF16)
        for jp in range(SGU_GROUPS // 2):
            sl = slice(LANES * jp, LANES * (jp + 1))
            chunk = svn[:, sl]
            rhs = jnp.concatenate(
                [chunk[SGU_LEN * w:SGU_LEN * (w + 1), :] for w in range(n_win)], axis=1)
            wa = jnp.where(wmask, wm_ref[2 * jp], 0.0).astype(BF16)
            wb = jnp.where(wmask, wm_ref[2 * jp + 1], 0.0).astype(BF16)
            zero = jnp.zeros_like(rhs)
            stacked = jnp.concatenate([jnp.where(low_w, rhs, zero), jnp.where(low_w, zero, rhs)],
                                      axis=0)
            mixed = _dot(jnp.concatenate([wa, wb], axis=1), stacked)
            mixed = mixed + jnp.concatenate([bm_ref[:, sl]] * n_win, axis=1)
            mixed = jnp.concatenate(
                [mixed[:, LANES * w:LANES * (w + 1)] for w in range(n_win)], axis=0)
            sb_ref[rows, sl] = (u[:, sl] * mixed).astype(BF16)
        gs_ref[rows, :D_MODEL] = g_a.astype(BF16)

        v_all = proj(h, wqkv_ref, 2 * FOX_WIDTH, FOX_WIDTH)
        g_b = proj(h, wr_ref, 2 * SGU_WIDTH + D_MODEL, D_MODEL)
        for jp in range(FOX_HEADS // 2):
            sl = slice(LANES * jp, LANES * (jp + 1))
            qc, kc = q_all[:, sl], k_all[:, sl]
            vt = v_all[:, sl].T
            for par in range(2):
                hd = 2 * jp + par
                a0 = HEAD_DIM if par == 0 else 0
                data = low if par == 0 else jnp.logical_not(low)
                in_aug = (lane >= a0) & (lane < a0 + N_PARTS)
                qa = jnp.where(data, qc, jnp.where(in_aug, 1.0, 0.0))
                nc = jnp.broadcast_to(c[:, hd:hd + 1], (SUB, LANES)) * (-LOG2E)
                hi, mid, lo = _split3(nc)
                aug = jnp.where(lane == a0, hi,
                                jnp.where(lane == a0 + 1, mid,
                                          jnp.where(lane == a0 + 2, lo, 0.0)))
                ka = jnp.where(data, kc, aug)
                qa_ref[0, hd, rows, :] = qa.astype(BF16)
                ka_ref[0, hd, rows, :] = ka.astype(BF16)
                vta = jnp.concatenate([vt[par * HEAD_DIM:(par + 1) * HEAD_DIM], ones_row], axis=0)
                vt_ref[0, hd, kblk, :, kcols:kcols + SUB] = vta.astype(BF16)
        gs_ref[rows, D_MODEL:] = g_b.astype(BF16)
    carry_ref[...] = carry


def _attn_kernel(qa_ref, ka_ref, vt_ref, o_ref, acc_ref, st_ref, *, n_q):
    k_i = lax.broadcasted_iota(jnp.int32, (QC, QC), 0)
    q_i = lax.broadcasted_iota(jnp.int32, (QC, QC), 1)
    causal = k_i <= q_i
    halves = (slice(0, QC), slice(QC, TQ))

    def needed(c, tile, blk, half):
        return not (blk == tile and c == 0 and half == 1)

    def score_half(ch, tile, blk, half):
        hd, c = divmod(ch, 2)
        q0 = tile * TQ + c * QC
        k0 = blk * TQ + half * QC
        st = lax.dot_general(ka_ref[0, hd, pl.ds(k0, QC), :], qa_ref[0, hd, pl.ds(q0, QC), :],
                             (((1,), (1,)), ((), ())), preferred_element_type=F32)
        st_ref[ch, halves[half], :] = st
        return jnp.max(st, axis=0, keepdims=True)

    def chain_step(ch, tile, blk, m_old, cm, nxt):
        hd, c = divmod(ch, 2)
        diag = blk == tile
        masked = [diag and c == 0, diag and c == 1]

        def load(half):
            st = st_ref[ch, halves[half], :]
            return jnp.where(causal, st, -jnp.inf) if masked[half] else st

        if diag:
            cm = functools.reduce(jnp.maximum, [load(half).max(axis=0, keepdims=True)
                                                for half in range(2) if needed(c, tile, blk, half)])
        m_new = cm if m_old is None else jnp.maximum(m_old, cm)
        pv, cm_next = [], []
        for half in range(2):
            use = needed(c, tile, blk, half)
            if use:
                p = jnp.exp2(load(half) - m_new).astype(BF16)
            if nxt is not None and needed(c, *nxt, half):
                cm_next.append(score_half(ch, *nxt, half))
            if use:
                pv.append(_dot(vt_ref[0, hd, blk, :, halves[half]], p))
        pv = sum(pv[1:], pv[0])
        if m_old is None:
            acc_ref[ch] = pv
        else:
            acc_ref[ch] = acc_ref[ch] * jnp.exp2(m_old - m_new) + pv
        return m_new, (functools.reduce(jnp.maximum, cm_next) if cm_next else None)

    steps = [(qi, j) for qi in range(n_q) for j in range(qi + 1)]
    ms = [None] * 4
    cms = [functools.reduce(jnp.maximum, [score_half(ch, *steps[0], half) for half in range(2)
                                          if needed(ch % 2, *steps[0], half)])
           for ch in range(4)]
    for n, (qi, j) in enumerate(steps):
        nxt = steps[n + 1] if n + 1 < len(steps) else None
        for ch in range(4):
            ms[ch], cms[ch] = chain_step(ch, qi, j, ms[ch], cms[ch], nxt)
        if j == qi:
            for c in range(2):
                o_t = jnp.concatenate(
                    [acc_ref[2 * hd + c, :HEAD_DIM, :] / acc_ref[2 * hd + c, HEAD_DIM:HEAD_DIM + 1, :]
                     for hd in range(2)], axis=0)
                o_ref[0, pl.ds(qi * TQ + c * QC, QC), :] = o_t.T.astype(BF16)
            ms = [None] * 4


def _post_kernel(x_ref, at_ref, sb_ref, gs_ref, wa_ref, wb_ref, wo_ref, g2_ref, wu_ref,
                 wd_ref, gf_ref, o_ref, *, ff_chunk):
    subs = [slice(s * SUB, (s + 1) * SUB) for s in range(TM // SUB)]
    merged = []
    for rows in subs:
        ya = _dot(at_ref[rows, :], wa_ref[...])
        yb = _dot(sb_ref[rows, :], wb_ref[...])
        merged.append((_sigmoid(gs_ref[rows, :D_MODEL].astype(F32)) * ya
                       + _sigmoid(gs_ref[rows, D_MODEL:].astype(F32)) * yb).astype(BF16))
    ys = [x_ref[rows, :] + _dot(m, wo_ref[...]) for rows, m in zip(subs, merged)]
    hs = []
    for y in ys:
        ms = jnp.mean(y * y, axis=-1, keepdims=True)
        hs.append((y * lax.rsqrt(ms + EPS) * g2_ref[...]).astype(BF16))
    for c0 in range(0, D_FF, ff_chunk):
        acts = []
        for h in hs:
            a = jnp.maximum(_dot(h, wu_ref[:, c0:c0 + ff_chunk]), 0.0)
            acts.append((a * a).astype(BF16))
        ys = [y + _dot(a, wd_ref[c0:c0 + ff_chunk, :]) for y, a in zip(ys, acts)]
    for rows, y in zip(subs, ys):
        ms = jnp.mean(y * y, axis=-1, keepdims=True)
        o_ref[rows, :] = y * lax.rsqrt(ms + EPS) * gf_ref[...]


def _params(n_axes):
    return pltpu.CompilerParams(dimension_semantics=("arbitrary",) * n_axes,
                                vmem_limit_bytes=VMEM_LIMIT)


def _const(shape):
    return pl.BlockSpec(shape, lambda *_: (0,) * len(shape), pipeline_mode=pl.Buffered(1))


def _rows(width, rows=TM):
    return pl.BlockSpec((rows, width), lambda i: (i, 0))


def _prep(w):
    d, n = w.shape
    return pl.pallas_call(
        _prep_kernel,
        grid=(d // PREP_ROWS,),
        in_specs=[_rows(n, PREP_ROWS)],
        out_specs=[_rows(_F0, PREP_ROWS),
                   pl.BlockSpec((GATE_ROWS, PREP_ROWS), lambda i: (0, i)),
                   _rows(_R_COLS, PREP_ROWS)],
        out_shape=[jax.ShapeDtypeStruct((d, _F0), BF16),
                   jax.ShapeDtypeStruct((GATE_ROWS, d), BF16),
                   jax.ShapeDtypeStruct((d, _R_COLS), BF16)],
        compiler_params=_params(1),
        name="prep",
    )(w)


def _pre(xt, bsz, seq, g1, w_qkv, w_f, w_rest, bf_pad, ln_g, ln_b, w_sgu, bm):
    n_tok, d = xt.shape
    tiles_per_seq = seq // TP
    head_map = lambda i: (i // tiles_per_seq, 0, i % tiles_per_seq, 0)
    return pl.pallas_call(
        functools.partial(_pre_kernel, tiles_per_seq=tiles_per_seq),
        grid=(n_tok // TP,),
        in_specs=[
            _rows(d, TP),
            _const((1, d)),
            _const((d, _F0)),
            _const((GATE_ROWS, d)),
            _const((d, _R_COLS)),
            _const((GATE_ROWS, SUB)),
            _const((1, SGU_WIDTH)),
            _const((1, SGU_WIDTH)),
            _const((SGU_GROUPS, SGU_LEN, SGU_LEN)),
            _const((SGU_LEN, SGU_WIDTH)),
        ],
        out_specs=[
            pl.BlockSpec((1, FOX_HEADS, TP, LANES), head_map),
            pl.BlockSpec((1, FOX_HEADS, TP, LANES), head_map),
            pl.BlockSpec((1, FOX_HEADS, TP // TQ, VT_ROWS, TQ), lambda i: head_map(i) + (0,)),
            _rows(SGU_WIDTH, TP),
            _rows(2 * d, TP),
        ],
        out_shape=[
            jax.ShapeDtypeStruct((bsz, FOX_HEADS, seq, LANES), BF16),
            jax.ShapeDtypeStruct((bsz, FOX_HEADS, seq, LANES), BF16),
            jax.ShapeDtypeStruct((bsz, FOX_HEADS, seq // TQ, VT_ROWS, TQ), BF16),
            jax.ShapeDtypeStruct((n_tok, SGU_WIDTH), BF16),
            jax.ShapeDtypeStruct((n_tok, 2 * d), BF16),
        ],
        scratch_shapes=[pltpu.VMEM((FOX_HEADS, SUB), F32)],
        compiler_params=_params(1),
        name="pre",
    )(xt, g1, w_qkv, w_f, w_rest, bf_pad, ln_g, ln_b, w_sgu, bm)


def _attn(qa, ka, vt):
    bsz, _, seq, _ = qa.shape
    return pl.pallas_call(
        functools.partial(_attn_kernel, n_q=seq // TQ),
        grid=(bsz, FOX_HEADS // 2),
        in_specs=[
            pl.BlockSpec((1, 2, seq, LANES), lambda b, p: (b, p, 0, 0)),
            pl.BlockSpec((1, 2, seq, LANES), lambda b, p: (b, p, 0, 0)),
            pl.BlockSpec((1, 2, seq // TQ, VT_ROWS, TQ), lambda b, p: (b, p, 0, 0, 0)),
        ],
        out_specs=pl.BlockSpec((1, seq, LANES), lambda b, p: (b, 0, p)),
        out_shape=jax.ShapeDtypeStruct((bsz, seq, FOX_WIDTH), BF16),
        scratch_shapes=[pltpu.VMEM((4, VT_ROWS, QC), F32), pltpu.VMEM((4, TQ, QC), F32)],
        compiler_params=_params(2),
        name="attn",
    )(qa, ka, vt)


def _post(xt, att, sb, gs, w_a, w_b, w_o, g2, w_up, w_down, gf):
    n_tok, d = xt.shape
    return pl.pallas_call(
        functools.partial(_post_kernel, ff_chunk=1024),
        grid=(n_tok // TM,),
        in_specs=[
            _rows(d),
            _rows(FOX_WIDTH),
            _rows(SGU_WIDTH),
            _rows(2 * d),
            _const((FOX_WIDTH, d)),
            _const((SGU_WIDTH, d)),
            _const((d, d)),
            _const((1, d)),
            _const((d, D_FF)),
            _const((D_FF, d)),
            _const((1, d)),
        ],
        out_specs=_rows(d),
        out_shape=jax.ShapeDtypeStruct((n_tok, d), F32),
        compiler_params=_params(1),
        name="post",
    )(xt, att, sb, gs, w_a, w_b, w_o, g2, w_up, w_down, gf)


def kernel(x, norm1_g, w_in, b_f, ln_v_g, ln_v_b, w_sgu, b_sgu, w_a, w_b, w_o,
           norm2_g, w_up, w_down, normf_g):
    bsz, seq, d = x.shape
    assert d == D_MODEL and seq % TP == 0 and TP % TQ == 0 and TQ % SUB == 0 and TM % SUB == 0
    assert norm1_g.shape[0] == 1, "single-layer block"
    n_tok = bsz * seq
    xt = x.reshape(n_tok, d)

    w_qkv, w_f, w_rest = _prep(w_in.astype(BF16).reshape(w_in.shape[1:]))
    bf_rows = jnp.pad(jnp.tile(b_f[0], N_PARTS), (0, GATE_ROWS - N_PARTS * FOX_HEADS))
    bf_pad = jnp.broadcast_to(bf_rows[:, None], (GATE_ROWS, SUB))
    bm = jnp.repeat(jnp.transpose(b_sgu[0]), HEAD_DIM, axis=1)

    qa, ka, vt, sb, gs = _pre(xt, bsz, seq, norm1_g.reshape(1, d), w_qkv, w_f, w_rest, bf_pad,
                              ln_v_g.reshape(1, SGU_WIDTH), ln_v_b.reshape(1, SGU_WIDTH),
                              w_sgu[0], bm)
    att = _attn(qa, ka, vt)
    out = _post(xt, att.reshape(n_tok, FOX_WIDTH), sb, gs, w_a[0].astype(BF16),
                w_b[0].astype(BF16), w_o[0].astype(BF16), norm2_g.reshape(1, d),
                w_up[0].astype(BF16), w_down[0].astype(BF16), normf_g.reshape(1, d))
    return out.reshape(bsz, seq, d)
```

```python
import functools
import math

import jax
import jax.numpy as jnp
from jax import lax
from jax.experimental import pallas as pl
from jax.experimental.pallas import tpu as pltpu

D_MODEL = 1024
HEAD_DIM = 64
FOX_HEADS = 8
FOX_WIDTH = FOX_HEADS * HEAD_DIM
SGU_GROUPS = 8
SGU_WIDTH = 512
SGU_LEN = 128
CHUNK = 64
D_FF = 4 * D_MODEL
EPS = 1e-6

LANES = 128
TM = 512
TP = 512
TQ = 512
QC = 256
SUB = 256
VT_ROWS = 128
PREP_COLS = 256
LOG2E = math.log2(math.e)
VMEM_LIMIT = 56 * 1024 * 1024

_F0 = 3 * FOX_WIDTH
_Q0, _K0, _V0 = 0, FOX_WIDTH, 2 * FOX_WIDTH
_U0 = _F0
_GA0 = _U0 + 2 * SGU_WIDTH
_GB0 = _GA0 + D_MODEL
_W_COLS = _GB0 + D_MODEL
N_PARTS = 3
GATE_ROWS = 32

BF16 = jnp.bfloat16
F32 = jnp.float32


def _dot(a, b):
    return jnp.dot(a, b, preferred_element_type=F32)


def _gelu_tanh(x):
    c = math.sqrt(2.0 / math.pi)
    return 0.5 * x * (1.0 + jnp.tanh(c * (x + 0.044715 * (x * x * x))))


def _sigmoid(x):
    return 0.5 * jnp.tanh(0.5 * x) + 0.5


def _split3(x):
    hi = x.astype(BF16).astype(F32)
    r = x - hi
    mid = r.astype(BF16).astype(F32)
    return hi, mid, r - mid


def _prep_kernel(a_ref, b_ref, g_ref, w_ref, wft_ref):
    a = a_ref[...]
    past_gates = jnp.concatenate([a[FOX_HEADS:], b_ref[...]], axis=0)
    rows = jnp.where(pl.program_id(0) < _F0 // PREP_COLS, a, past_gates)
    w_ref[...] = rows.T.astype(BF16)
    g = g_ref[...]
    pad = jnp.zeros((GATE_ROWS - N_PARTS * FOX_HEADS, g.shape[1]), g.dtype)
    wft_ref[...] = jnp.concatenate([g] * N_PARTS + [pad], axis=0).astype(BF16)


def _pre_kernel(x_ref, g1_ref, w_ref, wft_ref, bf_ref, lng_ref, lnb_ref, wm_ref, bm_ref,
                qa_ref, ka_ref, vt_ref, sb_ref, gs_ref, carry_ref, *, tiles_per_seq):
    i = pl.program_id(0)

    @pl.when(i % tiles_per_seq == 0)
    def _():
        carry_ref[...] = jnp.zeros_like(carry_ref)

    s_i = lax.broadcasted_iota(jnp.int32, (SUB, SUB), 0)
    t_i = lax.broadcasted_iota(jnp.int32, (SUB, SUB), 1)
    tri = jnp.where(s_i <= t_i, 1.0, 0.0).astype(BF16)
    grow = lax.broadcasted_iota(jnp.int32, (GATE_ROWS, SUB), 0)
    head_pad = jnp.zeros((LANES - FOX_HEADS, SUB), F32)
    lane = lax.broadcasted_iota(jnp.int32, (SUB, LANES), 1)
    low = lane < HEAD_DIM
    ones_row = jnp.where(lax.broadcasted_iota(jnp.int32, (VT_ROWS - HEAD_DIM, SUB), 0) == 0,
                         1.0, 0.0)
    n_win = SUB // SGU_LEN
    wi = lax.broadcasted_iota(jnp.int32, (SGU_LEN, SGU_LEN), 0) // CHUNK
    wj = lax.broadcasted_iota(jnp.int32, (SGU_LEN, SGU_LEN), 1) // CHUNK
    wmask = wj <= wi
    lane_w = lax.broadcasted_iota(jnp.int32, (SGU_LEN, n_win * LANES), 1)
    low_w = (lane_w % LANES) < HEAD_DIM

    def proj(h, c0, width):
        return _dot(h, w_ref[:, c0:c0 + width])

    carry = carry_ref[...]
    for sub in range(TP // SUB):
        rows = slice(sub * SUB, (sub + 1) * SUB)
        kblk, kcols = divmod(sub * SUB, TQ)
        x = x_ref[rows, :]
        ms = jnp.mean(x * x, axis=-1, keepdims=True)
        h = (x * lax.rsqrt(ms + EPS) * g1_ref[...]).astype(BF16)

        z = lax.dot_general(wft_ref[...], h, (((1,), (1,)), ((), ())),
                            preferred_element_type=F32) + bf_ref[...]
        usv = proj(h, _U0, 2 * SGU_WIDTH)
        k_all = proj(h, _K0, FOX_WIDTH)

        logf = jnp.minimum(z, 0.0) - jnp.log(1.0 + jnp.exp(-jnp.abs(z)))
        hi, mid, lo = _split3(logf)
        part = jnp.where(grow < FOX_HEADS, hi, jnp.where(grow < 2 * FOX_HEADS, mid, lo))
        cl = _dot(part.astype(BF16), tri)
        ct = carry + (cl[:FOX_HEADS] + cl[FOX_HEADS:2 * FOX_HEADS]
                      + cl[2 * FOX_HEADS:3 * FOX_HEADS])
        carry = jnp.broadcast_to(ct[:, SUB - 1:SUB], ct.shape)
        c = jnp.concatenate([ct, head_pad], axis=0).T

        q_all = proj(h, _Q0, FOX_WIDTH) * (HEAD_DIM ** -0.5 * LOG2E)
        g_a = proj(h, _GA0, D_MODEL)

        u = _gelu_tanh(usv[:, :SGU_WIDTH])
        sv = _gelu_tanh(usv[:, SGU_WIDTH:])
        mu = jnp.mean(sv, axis=-1, keepdims=True)
        xc = sv - mu
        var = jnp.mean(xc * xc, axis=-1, keepdims=True)
        svn = (xc * lax.rsqrt(var + EPS) * lng_ref[...] + lnb_ref[...]).astype(BF16)
        for jp in range(SGU_GROUPS // 2):
            sl = slice(LANES * jp, LANES * (jp + 1))
            chunk = svn[:, sl]
            rhs = jnp.concatenate(
                [chunk[SGU_LEN * w:SGU_LEN * (w + 1), :] for w in range(n_win)], axis=1)
            wa = jnp.where(wmask, wm_ref[2 * jp], 0.0).astype(BF16)
            wb = jnp.where(wmask, wm_ref[2 * jp + 1], 0.0).astype(BF16)
            zero = jnp.zeros_like(rhs)
            stacked = jnp.concatenate([jnp.where(low_w, rhs, zero), jnp.where(low_w, zero, rhs)],
                                      axis=0)
            mixed = _dot(jnp.concatenate([wa, wb], axis=1), stacked)
            mixed = mixed + jnp.concatenate([bm_ref[:, sl]] * n_win, axis=1)
            mixed = jnp.concatenate(
                [mixed[:, LANES * w:LANES * (w + 1)] for w in range(n_win)], axis=0)
            sb_ref[rows, sl] = (u[:, sl] * mixed).astype(BF16)
        gs_ref[rows, :D_MODEL] = g_a.astype(BF16)

        v_all = proj(h, _V0, FOX_WIDTH)
        g_b = proj(h, _GB0, D_MODEL)
        for jp in range(FOX_HEADS // 2):
            sl = slice(LANES * jp, LANES * (jp + 1))
            qc, kc = q_all[:, sl], k_all[:, sl]
            vt = v_all[:, sl].T
            for par in range(2):
                hd = 2 * jp + par
                a0 = HEAD_DIM if par == 0 else 0
                data = low if par == 0 else jnp.logical_not(low)
                in_aug = (lane >= a0) & (lane < a0 + N_PARTS)
                qa = jnp.where(data, qc, jnp.where(in_aug, 1.0, 0.0))
                nc = jnp.broadcast_to(c[:, hd:hd + 1], (SUB, LANES)) * (-LOG2E)
                hi, mid, lo = _split3(nc)
                aug = jnp.where(lane == a0, hi,
                                jnp.where(lane == a0 + 1, mid,
                                          jnp.where(lane == a0 + 2, lo, 0.0)))
                ka = jnp.where(data, kc, aug)
                qa_ref[0, hd, rows, :] = qa.astype(BF16)
                ka_ref[0, hd, rows, :] = ka.astype(BF16)
                vta = jnp.concatenate([vt[par * HEAD_DIM:(par + 1) * HEAD_DIM], ones_row], axis=0)
                vt_ref[0, hd, kblk, :, kcols:kcols + SUB] = vta.astype(BF16)
        gs_ref[rows, D_MODEL:] = g_b.astype(BF16)
    carry_ref[...] = carry


def _attn_kernel(qa_ref, ka_ref, vt_ref, o_ref, acc_ref, st_ref, *, n_q):
    k_i = lax.broadcasted_iota(jnp.int32, (QC, QC), 0)
    q_i = lax.broadcasted_iota(jnp.int32, (QC, QC), 1)
    causal = k_i <= q_i
    halves = (slice(0, QC), slice(QC, TQ))

    def needed(c, tile, blk, half):
        return not (blk == tile and c == 0 and half == 1)

    def score_half(ch, tile, blk, half):
        hd, c = divmod(ch, 2)
        q0 = tile * TQ + c * QC
        k0 = blk * TQ + half * QC
        st = lax.dot_general(ka_ref[0, hd, pl.ds(k0, QC), :], qa_ref[0, hd, pl.ds(q0, QC), :],
                             (((1,), (1,)), ((), ())), preferred_element_type=F32)
        st_ref[ch, halves[half], :] = st
        return jnp.max(st, axis=0, keepdims=True)

    def chain_step(ch, tile, blk, m_old, cm, nxt):
        hd, c = divmod(ch, 2)
        diag = blk == tile
        masked = [diag and c == 0, diag and c == 1]

        def load(half):
            st = st_ref[ch, halves[half], :]
            return jnp.where(causal, st, -jnp.inf) if masked[half] else st

        if diag:
            cm = functools.reduce(jnp.maximum, [load(half).max(axis=0, keepdims=True)
                                                for half in range(2) if needed(c, tile, blk, half)])
        m_new = cm if m_old is None else jnp.maximum(m_old, cm)
        pv, cm_next = [], []
        for half in range(2):
            use = needed(c, tile, blk, half)
            if use:
                p = jnp.exp2(load(half) - m_new).astype(BF16)
            if nxt is not None and needed(c, *nxt, half):
                cm_next.append(score_half(ch, *nxt, half))
            if use:
                pv.append(_dot(vt_ref[0, hd, blk, :, halves[half]], p))
        pv = sum(pv[1:], pv[0])
        if m_old is None:
            acc_ref[ch] = pv
        else:
            acc_ref[ch] = acc_ref[ch] * jnp.exp2(m_old - m_new) + pv
        return m_new, (functools.reduce(jnp.maximum, cm_next) if cm_next else None)

    steps = [(qi, j) for qi in range(n_q) for j in range(qi + 1)]
    ms = [None] * 4
    cms = [functools.reduce(jnp.maximum, [score_half(ch, *steps[0], half) for half in range(2)
                                          if needed(ch % 2, *steps[0], half)])
           for ch in range(4)]
    for n, (qi, j) in enumerate(steps):
        nxt = steps[n + 1] if n + 1 < len(steps) else None
        for ch in range(4):
            ms[ch], cms[ch] = chain_step(ch, qi, j, ms[ch], cms[ch], nxt)
        if j == qi:
            for c in range(2):
                o_t = jnp.concatenate(
                    [acc_ref[2 * hd + c, :HEAD_DIM, :] / acc_ref[2 * hd + c, HEAD_DIM:HEAD_DIM + 1, :]
                     for hd in range(2)], axis=0)
                o_ref[0, pl.ds(qi * TQ + c * QC, QC), :] = o_t.T.astype(BF16)
            ms = [None] * 4


def _post_kernel(x_ref, at_ref, sb_ref, gs_ref, wa_ref, wb_ref, wo_ref, g2_ref, wu_ref,
                 wd_ref, gf_ref, o_ref, *, ff_chunk):
    subs = [slice(s * SUB, (s + 1) * SUB) for s in range(TM // SUB)]
    merged = []
    for rows in subs:
        ya = _dot(at_ref[rows, :], wa_ref[...])
        yb = _dot(sb_ref[rows, :], wb_ref[...])
        merged.append((_sigmoid(gs_ref[rows, :D_MODEL].astype(F32)) * ya
                       + _sigmoid(gs_ref[rows, D_MODEL:].astype(F32)) * yb).astype(BF16))
    ys = [x_ref[rows, :] + _dot(m, wo_ref[...]) for rows, m in zip(subs, merged)]
    hs = []
    for y in ys:
        ms = jnp.mean(y * y, axis=-1, keepdims=True)
        hs.append((y * lax.rsqrt(ms + EPS) * g2_ref[...]).astype(BF16))
    for c0 in range(0, D_FF, ff_chunk):
        acts = []
        for h in hs:
            a = jnp.maximum(_dot(h, wu_ref[:, c0:c0 + ff_chunk]), 0.0)
            acts.append((a * a).astype(BF16))
        ys = [y + _dot(a, wd_ref[c0:c0 + ff_chunk, :]) for y, a in zip(ys, acts)]
    for rows, y in zip(subs, ys):
        ms = jnp.mean(y * y, axis=-1, keepdims=True)
        o_ref[rows, :] = y * lax.rsqrt(ms + EPS) * gf_ref[...]


def _params(n_axes):
    return pltpu.CompilerParams(dimension_semantics=("arbitrary",) * n_axes,
                                vmem_limit_bytes=VMEM_LIMIT)


def _const(shape):
    return pl.BlockSpec(shape, lambda *_: (0,) * len(shape), pipeline_mode=pl.Buffered(1))


def _rows(width, rows=TM):
    return pl.BlockSpec((rows, width), lambda i: (i, 0))


def _prep(wt):
    n, d = wt.shape
    assert n == _W_COLS + FOX_HEADS and _F0 % PREP_COLS == 0 and _W_COLS % PREP_COLS == 0
    per_tile = PREP_COLS // FOX_HEADS
    return pl.pallas_call(
        _prep_kernel,
        grid=(_W_COLS // PREP_COLS,),
        in_specs=[pl.BlockSpec((PREP_COLS, d), lambda t: (t, 0)),
                  pl.BlockSpec((FOX_HEADS, d), lambda t: ((t + 1) * per_tile, 0)),
                  pl.BlockSpec((FOX_HEADS, d), lambda t: (_F0 // FOX_HEADS, 0))],
        out_specs=[pl.BlockSpec((d, PREP_COLS), lambda t: (0, t)),
                   pl.BlockSpec((GATE_ROWS, d), lambda t: (0, 0))],
        out_shape=[jax.ShapeDtypeStruct((d, _W_COLS), BF16),
                   jax.ShapeDtypeStruct((GATE_ROWS, d), BF16)],
        compiler_params=_params(1),
        name="prep",
    )(wt, wt, wt)


def _pre(xt, bsz, seq, g1, w_all, w_ft, bf_pad, ln_g, ln_b, w_sgu, bm):
    n_tok, d = xt.shape
    tiles_per_seq = seq // TP
    head_map = lambda i: (i // tiles_per_seq, 0, i % tiles_per_seq, 0)
    return pl.pallas_call(
        functools.partial(_pre_kernel, tiles_per_seq=tiles_per_seq),
        grid=(n_tok // TP,),
        in_specs=[
            _rows(d, TP),
            _const((1, d)),
            _const((d, _W_COLS)),
            _const((GATE_ROWS, d)),
            _const((GATE_ROWS, SUB)),
            _const((1, SGU_WIDTH)),
            _const((1, SGU_WIDTH)),
            _const((SGU_GROUPS, SGU_LEN, SGU_LEN)),
            _const((SGU_LEN, SGU_WIDTH)),
        ],
        out_specs=[
            pl.BlockSpec((1, FOX_HEADS, TP, LANES), head_map),
            pl.BlockSpec((1, FOX_HEADS, TP, LANES), head_map),
            pl.BlockSpec((1, FOX_HEADS, TP // TQ, VT_ROWS, TQ), lambda i: head_map(i) + (0,)),
            _rows(SGU_WIDTH, TP),
            _rows(2 * d, TP),
        ],
        out_shape=[
            jax.ShapeDtypeStruct((bsz, FOX_HEADS, seq, LANES), BF16),
            jax.ShapeDtypeStruct((bsz, FOX_HEADS, seq, LANES), BF16),
            jax.ShapeDtypeStruct((bsz, FOX_HEADS, seq // TQ, VT_ROWS, TQ), BF16),
            jax.ShapeDtypeStruct((n_tok, SGU_WIDTH), BF16),
            jax.ShapeDtypeStruct((n_tok, 2 * d), BF16),
        ],
        scratch_shapes=[pltpu.VMEM((FOX_HEADS, SUB), F32)],
        compiler_params=_params(1),
        name="pre",
    )(xt, g1, w_all, w_ft, bf_pad, ln_g, ln_b, w_sgu, bm)


def _attn(qa, ka, vt):
    bsz, _, seq, _ = qa.shape
    return pl.pallas_call(
        functools.partial(_attn_kernel, n_q=seq // TQ),
        grid=(bsz, FOX_HEADS // 2),
        in_specs=[
            pl.BlockSpec((1, 2, seq, LANES), lambda b, p: (b, p, 0, 0)),
            pl.BlockSpec((1, 2, seq, LANES), lambda b, p: (b, p, 0, 0)),
            pl.BlockSpec((1, 2, seq // TQ, VT_ROWS, TQ), lambda b, p: (b, p, 0, 0, 0)),
        ],
        out_specs=pl.BlockSpec((1, seq, LANES), lambda b, p: (b, 0, p)),
        out_shape=jax.ShapeDtypeStruct((bsz, seq, FOX_WIDTH), BF16),
        scratch_shapes=[pltpu.VMEM((4, VT_ROWS, QC), F32), pltpu.VMEM((4, TQ, QC), F32)],
        compiler_params=_params(2),
        name="attn",
    )(qa, ka, vt)


def _post(xt, att, sb, gs, w_a, w_b, w_o, g2, w_up, w_down, gf):
    n_tok, d = xt.shape
    return pl.pallas_call(
        functools.partial(_post_kernel, ff_chunk=1024),
        grid=(n_tok // TM,),
        in_specs=[
            _rows(d),
            _rows(FOX_WIDTH),
            _rows(SGU_WIDTH),
            _rows(2 * d),
            _const((FOX_WIDTH, d)),
            _const((SGU_WIDTH, d)),
            _const((d, d)),
            _const((1, d)),
            _const((d, D_FF)),
            _const((D_FF, d)),
            _const((1, d)),
        ],
        out_specs=_rows(d),
        out_shape=jax.ShapeDtypeStruct((n_tok, d), F32),
        compiler_params=_params(1),
        name="post",
    )(xt, att, sb, gs, w_a, w_b, w_o, g2, w_up, w_down, gf)


def kernel(x, norm1_g, w_in, b_f, ln_v_g, ln_v_b, w_sgu, b_sgu, w_a, w_b, w_o,
           norm2_g, w_up, w_down, normf_g):
    bsz, seq, d = x.shape
    assert d == D_MODEL and seq % TP == 0 and TP % TQ == 0 and TQ % SUB == 0 and TM % SUB == 0
    assert norm1_g.shape[0] == 1, "single-layer block"
    n_tok = bsz * seq
    xt = x.reshape(n_tok, d)

    w_all, w_ft = _prep(jnp.swapaxes(w_in[0], 0, 1))
    bf_rows = jnp.pad(jnp.tile(b_f[0], N_PARTS), (0, GATE_ROWS - N_PARTS * FOX_HEADS))
    bf_pad = jnp.broadcast_to(bf_rows[:, None], (GATE_ROWS, SUB))
    bm = jnp.repeat(jnp.transpose(b_sgu[0]), HEAD_DIM, axis=1)

    qa, ka, vt, sb, gs = _pre(xt, bsz, seq, norm1_g.reshape(1, d), w_all, w_ft, bf_pad,
                              ln_v_g.reshape(1, SGU_WIDTH), ln_v_b.reshape(1, SGU_WIDTH),
                              w_sgu[0], bm)
    att = _attn(qa, ka, vt)
    out = _post(xt, att.reshape(n_tok, FOX_WIDTH), sb, gs, w_a[0].astype(BF16),
                w_b[0].astype(BF16), w_o[0].astype(BF16), norm2_g.reshape(1, d),
                w_up[0].astype(BF16), w_down[0].astype(BF16), normf_g.reshape(1, d))
    return out.reshape(bsz, seq, d)
```

```python
import functools
import math

import jax
import jax.numpy as jnp
from jax import lax
from jax.experimental import pallas as pl
from jax.experimental.pallas import tpu as pltpu

D_MODEL = 1024
HEAD_DIM = 64
FOX_HEADS = 8
FOX_WIDTH = FOX_HEADS * HEAD_DIM
SGU_GROUPS = 8
SGU_WIDTH = 512
SGU_LEN = 128
CHUNK = 64
D_FF = 4 * D_MODEL
EPS = 1e-6

LANES = 128
TM = 512
TP = 512
TQ = 512
QC = 256
SUB = 256
VT_ROWS = 128
PREP_COLS = 512
LOG2E = math.log2(math.e)
VMEM_LIMIT = 56 * 1024 * 1024

_F0 = 3 * FOX_WIDTH
_Q0, _K0, _V0 = 0, FOX_WIDTH, 2 * FOX_WIDTH
_U0 = _F0
_GA0 = _U0 + 2 * SGU_WIDTH
_GB0 = _GA0 + D_MODEL
_W_COLS = _GB0 + D_MODEL
N_PARTS = 3
GATE_ROWS = 32

BF16 = jnp.bfloat16
F32 = jnp.float32


def _dot(a, b):
    return jnp.dot(a, b, preferred_element_type=F32)


def _gelu_tanh(x):
    c = math.sqrt(2.0 / math.pi)
    return 0.5 * x * (1.0 + jnp.tanh(c * (x + 0.044715 * (x * x * x))))


def _sigmoid(x):
    return 0.5 * jnp.tanh(0.5 * x) + 0.5


def _split3(x):
    hi = x.astype(BF16).astype(F32)
    r = x - hi
    mid = r.astype(BF16).astype(F32)
    return hi, mid, r - mid


def _prep_kernel(a_ref, b_ref, g_ref, w_ref, wft_ref):
    a = a_ref[...]
    past_gates = jnp.concatenate([a[FOX_HEADS:], b_ref[...]], axis=0)
    rows = jnp.where(pl.program_id(0) < _F0 // PREP_COLS, a, past_gates)
    w_ref[...] = rows.T.astype(BF16)
    g = g_ref[...]
    pad = jnp.zeros((GATE_ROWS - N_PARTS * FOX_HEADS, g.shape[1]), g.dtype)
    wft_ref[...] = jnp.concatenate([g] * N_PARTS + [pad], axis=0).astype(BF16)


def _pre_kernel(x_ref, g1_ref, w_ref, wft_ref, bf_ref, lng_ref, lnb_ref, wm_ref, bm_ref,
                qa_ref, ka_ref, vt_ref, sb_ref, gs_ref, carry_ref, *, tiles_per_seq):
    i = pl.program_id(0)

    @pl.when(i % tiles_per_seq == 0)
    def _():
        carry_ref[...] = jnp.zeros_like(carry_ref)

    s_i = lax.broadcasted_iota(jnp.int32, (SUB, SUB), 0)
    t_i = lax.broadcasted_iota(jnp.int32, (SUB, SUB), 1)
    tri = jnp.where(s_i <= t_i, 1.0, 0.0).astype(BF16)
    grow = lax.broadcasted_iota(jnp.int32, (GATE_ROWS, SUB), 0)
    head_pad = jnp.zeros((LANES - FOX_HEADS, SUB), F32)
    lane = lax.broadcasted_iota(jnp.int32, (SUB, LANES), 1)
    low = lane < HEAD_DIM
    ones_row = jnp.where(lax.broadcasted_iota(jnp.int32, (VT_ROWS - HEAD_DIM, SUB), 0) == 0,
                         1.0, 0.0)
    n_win = SUB // SGU_LEN
    wi = lax.broadcasted_iota(jnp.int32, (SGU_LEN, SGU_LEN), 0) // CHUNK
    wj = lax.broadcasted_iota(jnp.int32, (SGU_LEN, SGU_LEN), 1) // CHUNK
    wmask = wj <= wi
    lane_w = lax.broadcasted_iota(jnp.int32, (SGU_LEN, n_win * LANES), 1)
    low_w = (lane_w % LANES) < HEAD_DIM

    def proj(h, c0, width):
        return _dot(h, w_ref[:, c0:c0 + width])

    carry = carry_ref[...]
    for sub in range(TP // SUB):
        rows = slice(sub * SUB, (sub + 1) * SUB)
        kblk, kcols = divmod(sub * SUB, TQ)
        x = x_ref[rows, :]
        ms = jnp.mean(x * x, axis=-1, keepdims=True)
        h = (x * lax.rsqrt(ms + EPS) * g1_ref[...]).astype(BF16)

        z = lax.dot_general(wft_ref[...], h, (((1,), (1,)), ((), ())),
                            preferred_element_type=F32) + bf_ref[...]
        usv = proj(h, _U0, 2 * SGU_WIDTH)
        k_all = proj(h, _K0, FOX_WIDTH)

        logf = jnp.minimum(z, 0.0) - jnp.log(1.0 + jnp.exp(-jnp.abs(z)))
        hi, mid, lo = _split3(logf)
        part = jnp.where(grow < FOX_HEADS, hi, jnp.where(grow < 2 * FOX_HEADS, mid, lo))
        cl = _dot(part.astype(BF16), tri)
        ct = carry + (cl[:FOX_HEADS] + cl[FOX_HEADS:2 * FOX_HEADS]
                      + cl[2 * FOX_HEADS:3 * FOX_HEADS])
        carry = jnp.broadcast_to(ct[:, SUB - 1:SUB], ct.shape)
        c = jnp.concatenate([ct, head_pad], axis=0).T

        q_all = proj(h, _Q0, FOX_WIDTH) * (HEAD_DIM ** -0.5 * LOG2E)
        g_a = proj(h, _GA0, D_MODEL)

        u = _gelu_tanh(usv[:, :SGU_WIDTH])
        sv = _gelu_tanh(usv[:, SGU_WIDTH:])
        mu = jnp.mean(sv, axis=-1, keepdims=True)
        xc = sv - mu
        var = jnp.mean(xc * xc, axis=-1, keepdims=True)
        svn = (xc * lax.rsqrt(var + EPS) * lng_ref[...] + lnb_ref[...]).astype(BF16)
        for jp in range(SGU_GROUPS // 2):
            sl = slice(LANES * jp, LANES * (jp + 1))
            chunk = svn[:, sl]
            rhs = jnp.concatenate(
                [chunk[SGU_LEN * w:SGU_LEN * (w + 1), :] for w in range(n_win)], axis=1)
            wa = jnp.where(wmask, wm_ref[2 * jp], 0.0).astype(BF16)
            wb = jnp.where(wmask, wm_ref[2 * jp + 1], 0.0).astype(BF16)
            zero = jnp.zeros_like(rhs)
            stacked = jnp.concatenate([jnp.where(low_w, rhs, zero), jnp.where(low_w, zero, rhs)],
                                      axis=0)
            mixed = _dot(jnp.concatenate([wa, wb], axis=1), stacked)
            mixed = mixed + jnp.concatenate([bm_ref[:, sl]] * n_win, axis=1)
            mixed = jnp.concatenate(
                [mixed[:, LANES * w:LANES * (w + 1)] for w in range(n_win)], axis=0)
            sb_ref[rows, sl] = (u[:, sl] * mixed).astype(BF16)
        gs_ref[rows, :D_MODEL] = g_a.astype(BF16)

        v_all = proj(h, _V0, FOX_WIDTH)
        g_b = proj(h, _GB0, D_MODEL)
        for jp in range(FOX_HEADS // 2):
            sl = slice(LANES * jp, LANES * (jp + 1))
            qc, kc = q_all[:, sl], k_all[:, sl]
            vt = v_all[:, sl].T
            for par in range(2):
                hd = 2 * jp + par
                a0 = HEAD_DIM if par == 0 else 0
                data = low if par == 0 else jnp.logical_not(low)
                in_aug = (lane >= a0) & (lane < a0 + N_PARTS)
                qa = jnp.where(data, qc, jnp.where(in_aug, 1.0, 0.0))
                nc = jnp.broadcast_to(c[:, hd:hd + 1], (SUB, LANES)) * (-LOG2E)
                hi, mid, lo = _split3(nc)
                aug = jnp.where(lane == a0, hi,
                                jnp.where(lane == a0 + 1, mid,
                                          jnp.where(lane == a0 + 2, lo, 0.0)))
                ka = jnp.where(data, kc, aug)
                qa_ref[0, hd, rows, :] = qa.astype(BF16)
                ka_ref[0, hd, rows, :] = ka.astype(BF16)
                vta = jnp.concatenate([vt[par * HEAD_DIM:(par + 1) * HEAD_DIM], ones_row], axis=0)
                vt_ref[0, hd, kblk, :, kcols:kcols + SUB] = vta.astype(BF16)
        gs_ref[rows, D_MODEL:] = g_b.astype(BF16)
    carry_ref[...] = carry


def _attn_kernel(qa_ref, ka_ref, vt_ref, o_ref, acc_ref, st_ref, *, n_q):
    k_i = lax.broadcasted_iota(jnp.int32, (QC, QC), 0)
    q_i = lax.broadcasted_iota(jnp.int32, (QC, QC), 1)
    causal = k_i <= q_i
    halves = (slice(0, QC), slice(QC, TQ))

    def needed(c, tile, blk, half):
        return not (blk == tile and c == 0 and half == 1)

    def score_half(ch, tile, blk, half):
        hd, c = divmod(ch, 2)
        q0 = tile * TQ + c * QC
        k0 = blk * TQ + half * QC
        st = lax.dot_general(ka_ref[0, hd, pl.ds(k0, QC), :], qa_ref[0, hd, pl.ds(q0, QC), :],
                             (((1,), (1,)), ((), ())), preferred_element_type=F32)
        st_ref[ch, halves[half], :] = st
        return jnp.max(st, axis=0, keepdims=True)

    def chain_step(ch, tile, blk, m_old, cm, nxt):
        hd, c = divmod(ch, 2)
        diag = blk == tile
        masked = [diag and c == 0, diag and c == 1]

        def load(half):
            st = st_ref[ch, halves[half], :]
            return jnp.where(causal, st, -jnp.inf) if masked[half] else st

        if diag:
            cm = functools.reduce(jnp.maximum, [load(half).max(axis=0, keepdims=True)
                                                for half in range(2) if needed(c, tile, blk, half)])
        m_new = cm if m_old is None else jnp.maximum(m_old, cm)
        pv, cm_next = [], []
        for half in range(2):
            use = needed(c, tile, blk, half)
            if use:
                p = jnp.exp2(load(half) - m_new).astype(BF16)
            if nxt is not None and needed(c, *nxt, half):
                cm_next.append(score_half(ch, *nxt, half))
            if use:
                pv.append(_dot(vt_ref[0, hd, blk, :, halves[half]], p))
        pv = sum(pv[1:], pv[0])
        if m_old is None:
            acc_ref[ch] = pv
        else:
            acc_ref[ch] = acc_ref[ch] * jnp.exp2(m_old - m_new) + pv
        return m_new, (functools.reduce(jnp.maximum, cm_next) if cm_next else None)

    steps = [(qi, j) for qi in range(n_q) for j in range(qi + 1)]
    ms = [None] * 4
    cms = [functools.reduce(jnp.maximum, [score_half(ch, *steps[0], half) for half in range(2)
                                          if needed(ch % 2, *steps[0], half)])
           for ch in range(4)]
    for n, (qi, j) in enumerate(steps):
        nxt = steps[n + 1] if n + 1 < len(steps) else None
        for ch in range(4):
            ms[ch], cms[ch] = chain_step(ch, qi, j, ms[ch], cms[ch], nxt)
        if j == qi:
            for c in range(2):
                o_t = jnp.concatenate(
                    [acc_ref[2 * hd + c, :HEAD_DIM, :] / acc_ref[2 * hd + c, HEAD_DIM:HEAD_DIM + 1, :]
                     for hd in range(2)], axis=0)
                o_ref[0, pl.ds(qi * TQ + c * QC, QC), :] = o_t.T.astype(BF16)
            ms = [None] * 4


def _post_kernel(x_ref, at_ref, sb_ref, gs_ref, wa_ref, wb_ref, wo_ref, g2_ref, wu_ref,
                 wd_ref, gf_ref, o_ref, *, ff_chunk):
    subs = [slice(s * SUB, (s + 1) * SUB) for s in range(TM // SUB)]
    merged = []
    for rows in subs:
        ya = _dot(at_ref[rows, :], wa_ref[...])
        yb = _dot(sb_ref[rows, :], wb_ref[...])
        merged.append((_sigmoid(gs_ref[rows, :D_MODEL].astype(F32)) * ya
                       + _sigmoid(gs_ref[rows, D_MODEL:].astype(F32)) * yb).astype(BF16))
    ys = [x_ref[rows, :] + _dot(m, wo_ref[...]) for rows, m in zip(subs, merged)]
    hs = []
    for y in ys:
        ms = jnp.mean(y * y, axis=-1, keepdims=True)
        hs.append((y * lax.rsqrt(ms + EPS) * g2_ref[...]).astype(BF16))
    for c0 in range(0, D_FF, ff_chunk):
        acts = []
        for h in hs:
            a = jnp.maximum(_dot(h, wu_ref[:, c0:c0 + ff_chunk]), 0.0)
            acts.append((a * a).astype(BF16))
        ys = [y + _dot(a, wd_ref[c0:c0 + ff_chunk, :]) for y, a in zip(ys, acts)]
    for rows, y in zip(subs, ys):
        ms = jnp.mean(y * y, axis=-1, keepdims=True)
        o_ref[rows, :] = y * lax.rsqrt(ms + EPS) * gf_ref[...]


def _params(n_axes):
    return pltpu.CompilerParams(dimension_semantics=("arbitrary",) * n_axes,
                                vmem_limit_bytes=VMEM_LIMIT)


def _const(shape):
    return pl.BlockSpec(shape, lambda *_: (0,) * len(shape), pipeline_mode=pl.Buffered(1))


def _rows(width, rows=TM):
    return pl.BlockSpec((rows, width), lambda i: (i, 0))


def _prep(wt):
    n, d = wt.shape
    assert n == _W_COLS + FOX_HEADS and _F0 % PREP_COLS == 0 and _W_COLS % PREP_COLS == 0
    per_tile = PREP_COLS // FOX_HEADS
    return pl.pallas_call(
        _prep_kernel,
        grid=(_W_COLS // PREP_COLS,),
        in_specs=[pl.BlockSpec((PREP_COLS, d), lambda t: (t, 0)),
                  pl.BlockSpec((FOX_HEADS, d), lambda t: ((t + 1) * per_tile, 0)),
                  pl.BlockSpec((FOX_HEADS, d), lambda t: (_F0 // FOX_HEADS, 0))],
        out_specs=[pl.BlockSpec((d, PREP_COLS), lambda t: (0, t)),
                   pl.BlockSpec((GATE_ROWS, d), lambda t: (0, 0))],
        out_shape=[jax.ShapeDtypeStruct((d, _W_COLS), BF16),
                   jax.ShapeDtypeStruct((GATE_ROWS, d), BF16)],
        compiler_params=_params(1),
        name="prep",
    )(wt, wt, wt)


def _pre(xt, bsz, seq, g1, w_all, w_ft, bf_pad, ln_g, ln_b, w_sgu, bm):
    n_tok, d = xt.shape
    tiles_per_seq = seq // TP
    head_map = lambda i: (i // tiles_per_seq, 0, i % tiles_per_seq, 0)
    return pl.pallas_call(
        functools.partial(_pre_kernel, tiles_per_seq=tiles_per_seq),
        grid=(n_tok // TP,),
        in_specs=[
            _rows(d, TP),
            _const((1, d)),
            _const((d, _W_COLS)),
            _const((GATE_ROWS, d)),
            _const((GATE_ROWS, SUB)),
            _const((1, SGU_WIDTH)),
            _const((1, SGU_WIDTH)),
            _const((SGU_GROUPS, SGU_LEN, SGU_LEN)),
            _const((SGU_LEN, SGU_WIDTH)),
        ],
        out_specs=[
            pl.BlockSpec((1, FOX_HEADS, TP, LANES), head_map),
            pl.BlockSpec((1, FOX_HEADS, TP, LANES), head_map),
            pl.BlockSpec((1, FOX_HEADS, TP // TQ, VT_ROWS, TQ), lambda i: head_map(i) + (0,)),
            _rows(SGU_WIDTH, TP),
            _rows(2 * d, TP),
        ],
        out_shape=[
            jax.ShapeDtypeStruct((bsz, FOX_HEADS, seq, LANES), BF16),
            jax.ShapeDtypeStruct((bsz, FOX_HEADS, seq, LANES), BF16),
            jax.ShapeDtypeStruct((bsz, FOX_HEADS, seq // TQ, VT_ROWS, TQ), BF16),
            jax.ShapeDtypeStruct((n_tok, SGU_WIDTH), BF16),
            jax.ShapeDtypeStruct((n_tok, 2 * d), BF16),
        ],
        scratch_shapes=[pltpu.VMEM((FOX_HEADS, SUB), F32)],
        compiler_params=_params(1),
        name="pre",
    )(xt, g1, w_all, w_ft, bf_pad, ln_g, ln_b, w_sgu, bm)


def _attn(qa, ka, vt):
    bsz, _, seq, _ = qa.shape
    return pl.pallas_call(
        functools.partial(_attn_kernel, n_q=seq // TQ),
        grid=(bsz, FOX_HEADS // 2),
        in_specs=[
            pl.BlockSpec((1, 2, seq, LANES), lambda b, p: (b, p, 0, 0)),
            pl.BlockSpec((1, 2, seq, LANES), lambda b, p: (b, p, 0, 0)),
            pl.BlockSpec((1, 2, seq // TQ, VT_ROWS, TQ), lambda b, p: (b, p, 0, 0, 0)),
        ],
        out_specs=pl.BlockSpec((1, seq, LANES), lambda b, p: (b, 0, p)),
        out_shape=jax.ShapeDtypeStruct((bsz, seq, FOX_WIDTH), BF16),
        scratch_shapes=[pltpu.VMEM((4, VT_ROWS, QC), F32), pltpu.VMEM((4, TQ, QC), F32)],
        compiler_params=_params(2),
        name="attn",
    )(qa, ka, vt)


def _post(xt, att, sb, gs, w_a, w_b, w_o, g2, w_up, w_down, gf):
    n_tok, d = xt.shape
    return pl.pallas_call(
        functools.partial(_post_kernel, ff_chunk=1024),
        grid=(n_tok // TM,),
        in_specs=[
            _rows(d),
            _rows(FOX_WIDTH),
            _rows(SGU_WIDTH),
            _rows(2 * d),
            _const((FOX_WIDTH, d)),
            _const((SGU_WIDTH, d)),
            _const((d, d)),
            _const((1, d)),
            _const((d, D_FF)),
            _const((D_FF, d)),
            _const((1, d)),
        ],
        out_specs=_rows(d),
        out_shape=jax.ShapeDtypeStruct((n_tok, d), F32),
        compiler_params=_params(1),
        name="post",
    )(xt, att, sb, gs, w_a, w_b, w_o, g2, w_up, w_down, gf)


def kernel(x, norm1_g, w_in, b_f, ln_v_g, ln_v_b, w_sgu, b_sgu, w_a, w_b, w_o,
           norm2_g, w_up, w_down, normf_g):
    bsz, seq, d = x.shape
    assert d == D_MODEL and seq % TP == 0 and TP % TQ == 0 and TQ % SUB == 0 and TM % SUB == 0
    assert norm1_g.shape[0] == 1, "single-layer block"
    n_tok = bsz * seq
    xt = x.reshape(n_tok, d)

    w_all, w_ft = _prep(jnp.swapaxes(w_in[0], 0, 1))
    bf_rows = jnp.pad(jnp.tile(b_f[0], N_PARTS), (0, GATE_ROWS - N_PARTS * FOX_HEADS))
    bf_pad = jnp.broadcast_to(bf_rows[:, None], (GATE_ROWS, SUB))
    bm = jnp.repeat(jnp.transpose(b_sgu[0]), HEAD_DIM, axis=1)

    qa, ka, vt, sb, gs = _pre(xt, bsz, seq, norm1_g.reshape(1, d), w_all, w_ft, bf_pad,
                              ln_v_g.reshape(1, SGU_WIDTH), ln_v_b.reshape(1, SGU_WIDTH),
                              w_sgu[0], bm)
    att = _attn(qa, ka, vt)
    out = _post(xt, att.reshape(n_tok, FOX_WIDTH), sb, gs, w_a[0].astype(BF16),
                w_b[0].astype(BF16), w_o[0].astype(BF16), norm2_g.reshape(1, d),
                w_up[0].astype(BF16), w_down[0].astype(BF16), normf_g.reshape(1, d))
    return out.reshape(bsz, seq, d)
```

```python
import functools
import math

import jax
import jax.numpy as jnp
from jax import lax
from jax.experimental import pallas as pl
from jax.experimental.pallas import tpu as pltpu

D_MODEL = 1024
HEAD_DIM = 64
FOX_HEADS = 8
FOX_WIDTH = FOX_HEADS * HEAD_DIM
SGU_GROUPS = 8
SGU_WIDTH = 512
SGU_LEN = 128
CHUNK = 64
D_FF = 4 * D_MODEL
EPS = 1e-6

LANES = 128
TM = 512
TP = 512
TQ = 512
QC = 256
SUB = 256
VT_ROWS = 128
PREP_COLS = 512
LOG2E = math.log2(math.e)
VMEM_LIMIT = 56 * 1024 * 1024

_F0 = 3 * FOX_WIDTH
_Q0, _K0, _V0 = 0, FOX_WIDTH, 2 * FOX_WIDTH
_U0 = _F0
_GA0 = _U0 + 2 * SGU_WIDTH
_GB0 = _GA0 + D_MODEL
_W_COLS = _GB0 + D_MODEL
N_PARTS = 3
GATE_ROWS = 32

BF16 = jnp.bfloat16
F32 = jnp.float32


def _dot(a, b):
    return jnp.dot(a, b, preferred_element_type=F32)


def _gelu_tanh(x):
    c = math.sqrt(2.0 / math.pi)
    half = 0.5 * x
    return half * jnp.tanh(x * ((x * x) * (c * 0.044715) + c)) + half


def _sigmoid(x):
    return 0.5 * jnp.tanh(0.5 * x) + 0.5


def _split3(x):
    hi = x.astype(BF16).astype(F32)
    r = x - hi
    mid = r.astype(BF16).astype(F32)
    return hi, mid, r - mid


def _prep_kernel(a_ref, b_ref, g_ref, w_ref, wft_ref):
    a = a_ref[...]
    past_gates = jnp.concatenate([a[FOX_HEADS:], b_ref[...]], axis=0)
    rows = jnp.where(pl.program_id(0) < _F0 // PREP_COLS, a, past_gates)
    w_ref[...] = rows.T.astype(BF16)
    g = g_ref[...]
    pad = jnp.zeros((GATE_ROWS - N_PARTS * FOX_HEADS, g.shape[1]), g.dtype)
    wft_ref[...] = jnp.concatenate([g] * N_PARTS + [pad], axis=0).astype(BF16)


def _pre_kernel(x_ref, g1_ref, w_ref, wft_ref, bf_ref, lng_ref, lnb_ref, wm_ref, bm_ref,
                qa_ref, ka_ref, vt_ref, sb_ref, gs_ref, carry_ref, *, tiles_per_seq):
    i = pl.program_id(0)

    @pl.when(i % tiles_per_seq == 0)
    def _():
        carry_ref[...] = jnp.zeros_like(carry_ref)

    s_i = lax.broadcasted_iota(jnp.int32, (SUB, SUB), 0)
    t_i = lax.broadcasted_iota(jnp.int32, (SUB, SUB), 1)
    tri = jnp.where(s_i <= t_i, 1.0, 0.0).astype(BF16)
    grow = lax.broadcasted_iota(jnp.int32, (GATE_ROWS, SUB), 0)
    head_pad = jnp.zeros((LANES - FOX_HEADS, SUB), F32)
    lane = lax.broadcasted_iota(jnp.int32, (SUB, LANES), 1)
    low = lane < HEAD_DIM
    ones_row = jnp.where(lax.broadcasted_iota(jnp.int32, (VT_ROWS - HEAD_DIM, SUB), 0) == 0,
                         1.0, 0.0)
    n_win = SUB // SGU_LEN
    wi = lax.broadcasted_iota(jnp.int32, (SGU_LEN, SGU_LEN), 0) // CHUNK
    wj = lax.broadcasted_iota(jnp.int32, (SGU_LEN, SGU_LEN), 1) // CHUNK
    wmask = wj <= wi
    lane_w = lax.broadcasted_iota(jnp.int32, (SGU_LEN, n_win * LANES), 1)
    low_w = (lane_w % LANES) < HEAD_DIM

    def proj(h, c0, width):
        return _dot(h, w_ref[:, c0:c0 + width])

    carry = carry_ref[...]
    for sub in range(TP // SUB):
        rows = slice(sub * SUB, (sub + 1) * SUB)
        kblk, kcols = divmod(sub * SUB, TQ)
        x = x_ref[rows, :]
        ms = jnp.mean(x * x, axis=-1, keepdims=True)
        h = (x * lax.rsqrt(ms + EPS) * g1_ref[...]).astype(BF16)

        z = lax.dot_general(wft_ref[...], h, (((1,), (1,)), ((), ())),
                            preferred_element_type=F32) + bf_ref[...]
        usv = proj(h, _U0, 2 * SGU_WIDTH)
        k_all = proj(h, _K0, FOX_WIDTH)

        logf = jnp.minimum(z, 0.0) - jnp.log(1.0 + jnp.exp(-jnp.abs(z)))
        hi, mid, lo = _split3(logf)
        part = jnp.where(grow < FOX_HEADS, hi, jnp.where(grow < 2 * FOX_HEADS, mid, lo))
        cl = _dot(part.astype(BF16), tri)
        ct = carry + (cl[:FOX_HEADS] + cl[FOX_HEADS:2 * FOX_HEADS]
                      + cl[2 * FOX_HEADS:3 * FOX_HEADS])
        carry = jnp.broadcast_to(ct[:, SUB - 1:SUB], ct.shape)
        c = jnp.concatenate([ct, head_pad], axis=0).T
        nc_parts = _split3(c * (-LOG2E))

        q_all = proj(h, _Q0, FOX_WIDTH) * (HEAD_DIM ** -0.5 * LOG2E)
        g_a = proj(h, _GA0, D_MODEL)

        u = _gelu_tanh(usv[:, :SGU_WIDTH])
        sv = _gelu_tanh(usv[:, SGU_WIDTH:])
        mu = jnp.mean(sv, axis=-1, keepdims=True)
        xc = sv - mu
        var = jnp.mean(xc * xc, axis=-1, keepdims=True)
        svn = (xc * lax.rsqrt(var + EPS) * lng_ref[...] + lnb_ref[...]).astype(BF16)
        for jp in range(SGU_GROUPS // 2):
            sl = slice(LANES * jp, LANES * (jp + 1))
            chunk = svn[:, sl]
            rhs = jnp.concatenate(
                [chunk[SGU_LEN * w:SGU_LEN * (w + 1), :] for w in range(n_win)], axis=1)
            wa = jnp.where(wmask, wm_ref[2 * jp], 0.0).astype(BF16)
            wb = jnp.where(wmask, wm_ref[2 * jp + 1], 0.0).astype(BF16)
            zero = jnp.zeros_like(rhs)
            stacked = jnp.concatenate([jnp.where(low_w, rhs, zero), jnp.where(low_w, zero, rhs)],
                                      axis=0)
            mixed = _dot(jnp.concatenate([wa, wb], axis=1), stacked)
            mixed = mixed + jnp.concatenate([bm_ref[:, sl]] * n_win, axis=1)
            mixed = jnp.concatenate(
                [mixed[:, LANES * w:LANES * (w + 1)] for w in range(n_win)], axis=0)
            sb_ref[rows, sl] = (u[:, sl] * mixed).astype(BF16)
        gs_ref[rows, :D_MODEL] = g_a.astype(BF16)

        v_all = proj(h, _V0, FOX_WIDTH)
        g_b = proj(h, _GB0, D_MODEL)
        for jp in range(FOX_HEADS // 2):
            sl = slice(LANES * jp, LANES * (jp + 1))
            qc, kc = q_all[:, sl], k_all[:, sl]
            vt = v_all[:, sl].T
            for par in range(2):
                hd = 2 * jp + par
                a0 = HEAD_DIM if par == 0 else 0
                data = low if par == 0 else jnp.logical_not(low)
                in_aug = (lane >= a0) & (lane < a0 + N_PARTS)
                qa = jnp.where(data, qc, jnp.where(in_aug, 1.0, 0.0))
                hi, mid, lo = [jnp.broadcast_to(p[:, hd:hd + 1], (SUB, LANES)) for p in nc_parts]
                aug = jnp.where(lane == a0, hi,
                                jnp.where(lane == a0 + 1, mid,
                                          jnp.where(lane == a0 + 2, lo, 0.0)))
                ka = jnp.where(data, kc, aug)
                qa_ref[0, hd, rows, :] = qa.astype(BF16)
                ka_ref[0, hd, rows, :] = ka.astype(BF16)
                vta = jnp.concatenate([vt[par * HEAD_DIM:(par + 1) * HEAD_DIM], ones_row], axis=0)
                vt_ref[0, hd, kblk, :, kcols:kcols + SUB] = vta.astype(BF16)
        gs_ref[rows, D_MODEL:] = g_b.astype(BF16)
    carry_ref[...] = carry


def _attn_kernel(qa_ref, ka_ref, vt_ref, o_ref, acc_ref, st_ref, *, n_q):
    k_i = lax.broadcasted_iota(jnp.int32, (QC, QC), 0)
    q_i = lax.broadcasted_iota(jnp.int32, (QC, QC), 1)
    causal = k_i <= q_i
    halves = (slice(0, QC), slice(QC, TQ))

    def needed(c, tile, blk, half):
        return not (blk == tile and c == 0 and half == 1)

    def score_half(ch, tile, blk, half):
        hd, c = divmod(ch, 2)
        q0 = tile * TQ + c * QC
        k0 = blk * TQ + half * QC
        st = lax.dot_general(ka_ref[0, hd, pl.ds(k0, QC), :], qa_ref[0, hd, pl.ds(q0, QC), :],
                             (((1,), (1,)), ((), ())), preferred_element_type=F32)
        st_ref[ch, halves[half], :] = st
        return jnp.max(st, axis=0, keepdims=True)

    def chain_step(ch, tile, blk, m_old, cm, nxt):
        hd, c = divmod(ch, 2)
        diag = blk == tile
        masked = [diag and c == 0, diag and c == 1]

        def load(half):
            st = st_ref[ch, halves[half], :]
            return jnp.where(causal, st, -jnp.inf) if masked[half] else st

        if diag:
            cm = functools.reduce(jnp.maximum, [load(half).max(axis=0, keepdims=True)
                                                for half in range(2) if needed(c, tile, blk, half)])
        m_new = cm if m_old is None else jnp.maximum(m_old, cm)
        pv, cm_next = [], []
        for half in range(2):
            use = needed(c, tile, blk, half)
            if use:
                p = jnp.exp2(load(half) - m_new).astype(BF16)
            if nxt is not None and needed(c, *nxt, half):
                cm_next.append(score_half(ch, *nxt, half))
            if use:
                pv.append(_dot(vt_ref[0, hd, blk, :, halves[half]], p))
        pv = sum(pv[1:], pv[0])
        if m_old is None:
            acc_ref[ch] = pv
        else:
            acc_ref[ch] = acc_ref[ch] * jnp.exp2(m_old - m_new) + pv
        return m_new, (functools.reduce(jnp.maximum, cm_next) if cm_next else None)

    steps = [(qi, j) for qi in range(n_q) for j in range(qi + 1)]
    ms = [None] * 4
    cms = [functools.reduce(jnp.maximum, [score_half(ch, *steps[0], half) for half in range(2)
                                          if needed(ch % 2, *steps[0], half)])
           for ch in range(4)]
    for n, (qi, j) in enumerate(steps):
        nxt = steps[n + 1] if n + 1 < len(steps) else None
        for ch in range(4):
            ms[ch], cms[ch] = chain_step(ch, qi, j, ms[ch], cms[ch], nxt)
        if j == qi:
            for c in range(2):
                o_t = jnp.concatenate(
                    [acc_ref[2 * hd + c, :HEAD_DIM, :] / acc_ref[2 * hd + c, HEAD_DIM:HEAD_DIM + 1, :]
                     for hd in range(2)], axis=0)
                o_ref[0, pl.ds(qi * TQ + c * QC, QC), :] = o_t.T.astype(BF16)
            ms = [None] * 4


def _post_kernel(x_ref, at_ref, sb_ref, gs_ref, wa_ref, wb_ref, wo_ref, g2_ref, wu_ref,
                 wd_ref, gf_ref, o_ref, *, ff_chunk):
    subs = [slice(s * SUB, (s + 1) * SUB) for s in range(TM // SUB)]
    merged = []
    for rows in subs:
        ya = _dot(at_ref[rows, :], wa_ref[...])
        yb = _dot(sb_ref[rows, :], wb_ref[...])
        merged.append((_sigmoid(gs_ref[rows, :D_MODEL].astype(F32)) * ya
                       + _sigmoid(gs_ref[rows, D_MODEL:].astype(F32)) * yb).astype(BF16))
    ys = [x_ref[rows, :] + _dot(m, wo_ref[...]) for rows, m in zip(subs, merged)]
    hs = []
    for y in ys:
        ms = jnp.mean(y * y, axis=-1, keepdims=True)
        hs.append((y * lax.rsqrt(ms + EPS) * g2_ref[...]).astype(BF16))
    for c0 in range(0, D_FF, ff_chunk):
        acts = []
        for h in hs:
            a = jnp.maximum(_dot(h, wu_ref[:, c0:c0 + ff_chunk]), 0.0)
            acts.append((a * a).astype(BF16))
        ys = [y + _dot(a, wd_ref[c0:c0 + ff_chunk, :]) for y, a in zip(ys, acts)]
    for rows, y in zip(subs, ys):
        ms = jnp.mean(y * y, axis=-1, keepdims=True)
        o_ref[rows, :] = y * lax.rsqrt(ms + EPS) * gf_ref[...]


def _params(n_axes):
    return pltpu.CompilerParams(dimension_semantics=("arbitrary",) * n_axes,
                                vmem_limit_bytes=VMEM_LIMIT)


def _const(shape):
    return pl.BlockSpec(shape, lambda *_: (0,) * len(shape), pipeline_mode=pl.Buffered(1))


def _rows(width, rows=TM):
    return pl.BlockSpec((rows, width), lambda i: (i, 0))


def _prep(wt):
    n, d = wt.shape
    assert n == _W_COLS + FOX_HEADS and _F0 % PREP_COLS == 0 and _W_COLS % PREP_COLS == 0
    per_tile = PREP_COLS // FOX_HEADS
    return pl.pallas_call(
        _prep_kernel,
        grid=(_W_COLS // PREP_COLS,),
        in_specs=[pl.BlockSpec((PREP_COLS, d), lambda t: (t, 0)),
                  pl.BlockSpec((FOX_HEADS, d), lambda t: ((t + 1) * per_tile, 0)),
                  pl.BlockSpec((FOX_HEADS, d), lambda t: (_F0 // FOX_HEADS, 0))],
        out_specs=[pl.BlockSpec((d, PREP_COLS), lambda t: (0, t)),
                   pl.BlockSpec((GATE_ROWS, d), lambda t: (0, 0))],
        out_shape=[jax.ShapeDtypeStruct((d, _W_COLS), BF16),
                   jax.ShapeDtypeStruct((GATE_ROWS, d), BF16)],
        compiler_params=_params(1),
        name="prep",
    )(wt, wt, wt)


def _pre(xt, bsz, seq, g1, w_all, w_ft, bf_pad, ln_g, ln_b, w_sgu, bm):
    n_tok, d = xt.shape
    tiles_per_seq = seq // TP
    head_map = lambda i: (i // tiles_per_seq, 0, i % tiles_per_seq, 0)
    return pl.pallas_call(
        functools.partial(_pre_kernel, tiles_per_seq=tiles_per_seq),
        grid=(n_tok // TP,),
        in_specs=[
            _rows(d, TP),
            _const((1, d)),
            _const((d, _W_COLS)),
            _const((GATE_ROWS, d)),
            _const((GATE_ROWS, SUB)),
            _const((1, SGU_WIDTH)),
            _const((1, SGU_WIDTH)),
            _const((SGU_GROUPS, SGU_LEN, SGU_LEN)),
            _const((SGU_LEN, SGU_WIDTH)),
        ],
        out_specs=[
            pl.BlockSpec((1, FOX_HEADS, TP, LANES), head_map),
            pl.BlockSpec((1, FOX_HEADS, TP, LANES), head_map),
            pl.BlockSpec((1, FOX_HEADS, TP // TQ, VT_ROWS, TQ), lambda i: head_map(i) + (0,)),
            _rows(SGU_WIDTH, TP),
            _rows(2 * d, TP),
        ],
        out_shape=[
            jax.ShapeDtypeStruct((bsz, FOX_HEADS, seq, LANES), BF16),
            jax.ShapeDtypeStruct((bsz, FOX_HEADS, seq, LANES), BF16),
            jax.ShapeDtypeStruct((bsz, FOX_HEADS, seq // TQ, VT_ROWS, TQ), BF16),
            jax.ShapeDtypeStruct((n_tok, SGU_WIDTH), BF16),
            jax.ShapeDtypeStruct((n_tok, 2 * d), BF16),
        ],
        scratch_shapes=[pltpu.VMEM((FOX_HEADS, SUB), F32)],
        compiler_params=_params(1),
        name="pre",
    )(xt, g1, w_all, w_ft, bf_pad, ln_g, ln_b, w_sgu, bm)


def _attn(qa, ka, vt):
    bsz, _, seq, _ = qa.shape
    return pl.pallas_call(
        functools.partial(_attn_kernel, n_q=seq // TQ),
        grid=(bsz, FOX_HEADS // 2),
        in_specs=[
            pl.BlockSpec((1, 2, seq, LANES), lambda b, p: (b, p, 0, 0)),
            pl.BlockSpec((1, 2, seq, LANES), lambda b, p: (b, p, 0, 0)),
            pl.BlockSpec((1, 2, seq // TQ, VT_ROWS, TQ), lambda b, p: (b, p, 0, 0, 0)),
        ],
        out_specs=pl.BlockSpec((1, seq, LANES), lambda b, p: (b, 0, p)),
        out_shape=jax.ShapeDtypeStruct((bsz, seq, FOX_WIDTH), BF16),
        scratch_shapes=[pltpu.VMEM((4, VT_ROWS, QC), F32), pltpu.VMEM((4, TQ, QC), F32)],
        compiler_params=_params(2),
        name="attn",
    )(qa, ka, vt)


def _post(xt, att, sb, gs, w_a, w_b, w_o, g2, w_up, w_down, gf):
    n_tok, d = xt.shape
    return pl.pallas_call(
        functools.partial(_post_kernel, ff_chunk=1024),
        grid=(n_tok // TM,),
        in_specs=[
            _rows(d),
            _rows(FOX_WIDTH),
            _rows(SGU_WIDTH),
            _rows(2 * d),
            _const((FOX_WIDTH, d)),
            _const((SGU_WIDTH, d)),
            _const((d, d)),
            _const((1, d)),
            _const((d, D_FF)),
            _const((D_FF, d)),
            _const((1, d)),
        ],
        out_specs=_rows(d),
        out_shape=jax.ShapeDtypeStruct((n_tok, d), F32),
        compiler_params=_params(1),
        name="post",
    )(xt, att, sb, gs, w_a, w_b, w_o, g2, w_up, w_down, gf)


def kernel(x, norm1_g, w_in, b_f, ln_v_g, ln_v_b, w_sgu, b_sgu, w_a, w_b, w_o,
           norm2_g, w_up, w_down, normf_g):
    bsz, seq, d = x.shape
    assert d == D_MODEL and seq % TP == 0 and TP % TQ == 0 and TQ % SUB == 0 and TM % SUB == 0
    assert norm1_g.shape[0] == 1, "single-layer block"
    n_tok = bsz * seq
    xt = x.reshape(n_tok, d)

    w_all, w_ft = _prep(jnp.swapaxes(w_in[0], 0, 1))
    bf_rows = jnp.pad(jnp.tile(b_f[0], N_PARTS), (0, GATE_ROWS - N_PARTS * FOX_HEADS))
    bf_pad = jnp.broadcast_to(bf_rows[:, None], (GATE_ROWS, SUB))
    bm = jnp.repeat(jnp.transpose(b_sgu[0]), HEAD_DIM, axis=1)

    qa, ka, vt, sb, gs = _pre(xt, bsz, seq, norm1_g.reshape(1, d), w_all, w_ft, bf_pad,
                              ln_v_g.reshape(1, SGU_WIDTH), ln_v_b.reshape(1, SGU_WIDTH),
                              w_sgu[0], bm)
    att = _attn(qa, ka, vt)
    out = _post(xt, att.reshape(n_tok, FOX_WIDTH), sb, gs, w_a[0].astype(BF16),
                w_b[0].astype(BF16), w_o[0].astype(BF16), norm2_g.reshape(1, d),
                w_up[0].astype(BF16), w_down[0].astype(BF16), normf_g.reshape(1, d))
    return out.reshape(bsz, seq, d)
```

```python
import functools
import math

import jax
import jax.numpy as jnp
from jax import lax
from jax.experimental import pallas as pl
from jax.experimental.pallas import tpu as pltpu

D_MODEL = 1024
HEAD_DIM = 64
FOX_HEADS = 8
FOX_WIDTH = FOX_HEADS * HEAD_DIM
SGU_GROUPS = 8
SGU_WIDTH = 512
SGU_LEN = 128
CHUNK = 64
D_FF = 4 * D_MODEL
EPS = 1e-6

LANES = 128
TM = 512
TP = 1024
TQ = 512
QC = 256
SUB = 256
VT_ROWS = 128
PREP_COLS = 512
LOG2E = math.log2(math.e)
VMEM_LIMIT = 56 * 1024 * 1024

_F0 = 3 * FOX_WIDTH
_Q0, _K0, _V0 = 0, FOX_WIDTH, 2 * FOX_WIDTH
_U0 = _F0
_GA0 = _U0 + 2 * SGU_WIDTH
_GB0 = _GA0 + D_MODEL
_W_COLS = _GB0 + D_MODEL
N_PARTS = 3
GATE_ROWS = 32

BF16 = jnp.bfloat16
F32 = jnp.float32


def _dot(a, b):
    return jnp.dot(a, b, preferred_element_type=F32)


def _gelu_tanh(x):
    c = math.sqrt(2.0 / math.pi)
    half = 0.5 * x
    return half * jnp.tanh(x * ((x * x) * (c * 0.044715) + c)) + half


def _sigmoid(x):
    return 0.5 * jnp.tanh(0.5 * x) + 0.5


def _split3(x):
    hi = x.astype(BF16).astype(F32)
    r = x - hi
    mid = r.astype(BF16).astype(F32)
    return hi, mid, r - mid


def _prep_kernel(a_ref, b_ref, g_ref, w_ref, wft_ref):
    a = a_ref[...]
    past_gates = jnp.concatenate([a[FOX_HEADS:], b_ref[...]], axis=0)
    rows = jnp.where(pl.program_id(0) < _F0 // PREP_COLS, a, past_gates)
    w_ref[...] = rows.T.astype(BF16)
    g = g_ref[...]
    pad = jnp.zeros((GATE_ROWS - N_PARTS * FOX_HEADS, g.shape[1]), g.dtype)
    wft_ref[...] = jnp.concatenate([g] * N_PARTS + [pad], axis=0).astype(BF16)


def _pre_kernel(x_ref, g1_ref, w_ref, wft_ref, bf_ref, lng_ref, lnb_ref, wm_ref, bm_ref,
                qa_ref, ka_ref, vt_ref, sb_ref, gs_ref, carry_ref, *, tiles_per_seq):
    i = pl.program_id(0)

    @pl.when(i % tiles_per_seq == 0)
    def _():
        carry_ref[...] = jnp.zeros_like(carry_ref)

    s_i = lax.broadcasted_iota(jnp.int32, (SUB, SUB), 0)
    t_i = lax.broadcasted_iota(jnp.int32, (SUB, SUB), 1)
    tri = jnp.where(s_i <= t_i, 1.0, 0.0).astype(BF16)
    grow = lax.broadcasted_iota(jnp.int32, (GATE_ROWS, SUB), 0)
    head_pad = jnp.zeros((LANES - FOX_HEADS, SUB), F32)
    lane = lax.broadcasted_iota(jnp.int32, (SUB, LANES), 1)
    low = lane < HEAD_DIM
    ones_row = jnp.where(lax.broadcasted_iota(jnp.int32, (VT_ROWS - HEAD_DIM, SUB), 0) == 0,
                         1.0, 0.0)
    n_win = SUB // SGU_LEN
    wi = lax.broadcasted_iota(jnp.int32, (SGU_LEN, SGU_LEN), 0) // CHUNK
    wj = lax.broadcasted_iota(jnp.int32, (SGU_LEN, SGU_LEN), 1) // CHUNK
    wmask = wj <= wi
    lane_w = lax.broadcasted_iota(jnp.int32, (SGU_LEN, n_win * LANES), 1)
    low_w = (lane_w % LANES) < HEAD_DIM

    def proj(h, c0, width):
        return _dot(h, w_ref[:, c0:c0 + width])

    carry = carry_ref[...]
    for sub in range(TP // SUB):
        rows = slice(sub * SUB, (sub + 1) * SUB)
        kblk, kcols = divmod(sub * SUB, TQ)
        x = x_ref[rows, :]
        ms = jnp.mean(x * x, axis=-1, keepdims=True)
        h = (x * lax.rsqrt(ms + EPS) * g1_ref[...]).astype(BF16)

        z = lax.dot_general(wft_ref[...], h, (((1,), (1,)), ((), ())),
                            preferred_element_type=F32) + bf_ref[...]
        usv = proj(h, _U0, 2 * SGU_WIDTH)
        k_all = proj(h, _K0, FOX_WIDTH)

        logf = jnp.minimum(z, 0.0) - jnp.log(1.0 + jnp.exp(-jnp.abs(z)))
        hi, mid, lo = _split3(logf)
        part = jnp.where(grow < FOX_HEADS, hi, jnp.where(grow < 2 * FOX_HEADS, mid, lo))
        cl = _dot(part.astype(BF16), tri)
        ct = carry + (cl[:FOX_HEADS] + cl[FOX_HEADS:2 * FOX_HEADS]
                      + cl[2 * FOX_HEADS:3 * FOX_HEADS])
        carry = jnp.broadcast_to(ct[:, SUB - 1:SUB], ct.shape)
        c = jnp.concatenate([ct, head_pad], axis=0).T
        nc_parts = _split3(c * (-LOG2E))

        q_all = proj(h, _Q0, FOX_WIDTH) * (HEAD_DIM ** -0.5 * LOG2E)
        g_a = proj(h, _GA0, D_MODEL)

        u = _gelu_tanh(usv[:, :SGU_WIDTH])
        sv = _gelu_tanh(usv[:, SGU_WIDTH:])
        mu = jnp.mean(sv, axis=-1, keepdims=True)
        xc = sv - mu
        var = jnp.mean(xc * xc, axis=-1, keepdims=True)
        svn = (xc * lax.rsqrt(var + EPS) * lng_ref[...] + lnb_ref[...]).astype(BF16)
        for jp in range(SGU_GROUPS // 2):
            sl = slice(LANES * jp, LANES * (jp + 1))
            chunk = svn[:, sl]
            rhs = jnp.concatenate(
                [chunk[SGU_LEN * w:SGU_LEN * (w + 1), :] for w in range(n_win)], axis=1)
            wa = jnp.where(wmask, wm_ref[2 * jp], 0.0).astype(BF16)
            wb = jnp.where(wmask, wm_ref[2 * jp + 1], 0.0).astype(BF16)
            zero = jnp.zeros_like(rhs)
            stacked = jnp.concatenate([jnp.where(low_w, rhs, zero), jnp.where(low_w, zero, rhs)],
                                      axis=0)
            mixed = _dot(jnp.concatenate([wa, wb], axis=1), stacked)
            mixed = mixed + jnp.concatenate([bm_ref[:, sl]] * n_win, axis=1)
            mixed = jnp.concatenate(
                [mixed[:, LANES * w:LANES * (w + 1)] for w in range(n_win)], axis=0)
            sb_ref[rows, sl] = (u[:, sl] * mixed).astype(BF16)
        gs_ref[rows, :D_MODEL] = g_a.astype(BF16)

        v_all = proj(h, _V0, FOX_WIDTH)
        g_b = proj(h, _GB0, D_MODEL)
        for jp in range(FOX_HEADS // 2):
            sl = slice(LANES * jp, LANES * (jp + 1))
            qc, kc = q_all[:, sl], k_all[:, sl]
            vt = v_all[:, sl].T
            for par in range(2):
                hd = 2 * jp + par
                a0 = HEAD_DIM if par == 0 else 0
                data = low if par == 0 else jnp.logical_not(low)
                in_aug = (lane >= a0) & (lane < a0 + N_PARTS)
                qa = jnp.where(data, qc, jnp.where(in_aug, 1.0, 0.0))
                hi, mid, lo = [jnp.broadcast_to(p[:, hd:hd + 1], (SUB, LANES)) for p in nc_parts]
                aug = jnp.where(lane == a0, hi,
                                jnp.where(lane == a0 + 1, mid,
                                          jnp.where(lane == a0 + 2, lo, 0.0)))
                ka = jnp.where(data, kc, aug)
                qa_ref[0, hd, rows, :] = qa.astype(BF16)
                ka_ref[0, hd, rows, :] = ka.astype(BF16)
                vta = jnp.concatenate([vt[par * HEAD_DIM:(par + 1) * HEAD_DIM], ones_row], axis=0)
                vt_ref[0, hd, kblk, :, kcols:kcols + SUB] = vta.astype(BF16)
        gs_ref[rows, D_MODEL:] = g_b.astype(BF16)
    carry_ref[...] = carry


def _attn_kernel(qa_ref, ka_ref, vt_ref, o_ref, acc_ref, st_ref, *, n_q):
    k_i = lax.broadcasted_iota(jnp.int32, (QC, QC), 0)
    q_i = lax.broadcasted_iota(jnp.int32, (QC, QC), 1)
    causal = k_i <= q_i
    halves = (slice(0, QC), slice(QC, TQ))

    def needed(c, tile, blk, half):
        return not (blk == tile and c == 0 and half == 1)

    def score_half(ch, tile, blk, half):
        hd, c = divmod(ch, 2)
        q0 = tile * TQ + c * QC
        k0 = blk * TQ + half * QC
        st = lax.dot_general(ka_ref[0, hd, pl.ds(k0, QC), :], qa_ref[0, hd, pl.ds(q0, QC), :],
                             (((1,), (1,)), ((), ())), preferred_element_type=F32)
        st_ref[ch, halves[half], :] = st
        return jnp.max(st, axis=0, keepdims=True)

    def chain_step(ch, tile, blk, m_old, cm, nxt):
        hd, c = divmod(ch, 2)
        diag = blk == tile
        masked = [diag and c == 0, diag and c == 1]

        def load(half):
            st = st_ref[ch, halves[half], :]
            return jnp.where(causal, st, -jnp.inf) if masked[half] else st

        if diag:
            cm = functools.reduce(jnp.maximum, [load(half).max(axis=0, keepdims=True)
                                                for half in range(2) if needed(c, tile, blk, half)])
        m_new = cm if m_old is None else jnp.maximum(m_old, cm)
        pv, cm_next = [], []
        for half in range(2):
            use = needed(c, tile, blk, half)
            if use:
                p = jnp.exp2(load(half) - m_new).astype(BF16)
            if nxt is not None and needed(c, *nxt, half):
                cm_next.append(score_half(ch, *nxt, half))
            if use:
                pv.append(_dot(vt_ref[0, hd, blk, :, halves[half]], p))
        pv = sum(pv[1:], pv[0])
        if m_old is None:
            acc_ref[ch] = pv
        else:
            acc_ref[ch] = acc_ref[ch] * jnp.exp2(m_old - m_new) + pv
        return m_new, (functools.reduce(jnp.maximum, cm_next) if cm_next else None)

    steps = [(qi, j) for qi in range(n_q) for j in range(qi + 1)]
    ms = [None] * 4
    cms = [functools.reduce(jnp.maximum, [score_half(ch, *steps[0], half) for half in range(2)
                                          if needed(ch % 2, *steps[0], half)])
           for ch in range(4)]
    for n, (qi, j) in enumerate(steps):
        nxt = steps[n + 1] if n + 1 < len(steps) else None
        for ch in range(4):
            ms[ch], cms[ch] = chain_step(ch, qi, j, ms[ch], cms[ch], nxt)
        if j == qi:
            for c in range(2):
                o_t = jnp.concatenate(
                    [acc_ref[2 * hd + c, :HEAD_DIM, :] / acc_ref[2 * hd + c, HEAD_DIM:HEAD_DIM + 1, :]
                     for hd in range(2)], axis=0)
                o_ref[0, pl.ds(qi * TQ + c * QC, QC), :] = o_t.T.astype(BF16)
            ms = [None] * 4


def _post_kernel(x_ref, at_ref, sb_ref, gs_ref, wa_ref, wb_ref, wo_ref, g2_ref, wu_ref,
                 wd_ref, gf_ref, o_ref, *, ff_chunk):
    subs = [slice(s * SUB, (s + 1) * SUB) for s in range(TM // SUB)]
    merged = []
    for rows in subs:
        ya = _dot(at_ref[rows, :], wa_ref[...])
        yb = _dot(sb_ref[rows, :], wb_ref[...])
        merged.append((_sigmoid(gs_ref[rows, :D_MODEL].astype(F32)) * ya
                       + _sigmoid(gs_ref[rows, D_MODEL:].astype(F32)) * yb).astype(BF16))
    ys = [x_ref[rows, :] + _dot(m, wo_ref[...]) for rows, m in zip(subs, merged)]
    hs = []
    for y in ys:
        ms = jnp.mean(y * y, axis=-1, keepdims=True)
        hs.append((y * lax.rsqrt(ms + EPS) * g2_ref[...]).astype(BF16))
    for c0 in range(0, D_FF, ff_chunk):
        acts = []
        for h in hs:
            a = jnp.maximum(_dot(h, wu_ref[:, c0:c0 + ff_chunk]), 0.0)
            acts.append((a * a).astype(BF16))
        ys = [y + _dot(a, wd_ref[c0:c0 + ff_chunk, :]) for y, a in zip(ys, acts)]
    for rows, y in zip(subs, ys):
        ms = jnp.mean(y * y, axis=-1, keepdims=True)
        o_ref[rows, :] = y * lax.rsqrt(ms + EPS) * gf_ref[...]


def _params(n_axes):
    return pltpu.CompilerParams(dimension_semantics=("arbitrary",) * n_axes,
                                vmem_limit_bytes=VMEM_LIMIT)


def _const(shape):
    return pl.BlockSpec(shape, lambda *_: (0,) * len(shape), pipeline_mode=pl.Buffered(1))


def _rows(width, rows=TM):
    return pl.BlockSpec((rows, width), lambda i: (i, 0))


def _prep(wt):
    n, d = wt.shape
    assert n == _W_COLS + FOX_HEADS and _F0 % PREP_COLS == 0 and _W_COLS % PREP_COLS == 0
    per_tile = PREP_COLS // FOX_HEADS
    return pl.pallas_call(
        _prep_kernel,
        grid=(_W_COLS // PREP_COLS,),
        in_specs=[pl.BlockSpec((PREP_COLS, d), lambda t: (t, 0)),
                  pl.BlockSpec((FOX_HEADS, d), lambda t: ((t + 1) * per_tile, 0)),
                  pl.BlockSpec((FOX_HEADS, d), lambda t: (_F0 // FOX_HEADS, 0))],
        out_specs=[pl.BlockSpec((d, PREP_COLS), lambda t: (0, t)),
                   pl.BlockSpec((GATE_ROWS, d), lambda t: (0, 0))],
        out_shape=[jax.ShapeDtypeStruct((d, _W_COLS), BF16),
                   jax.ShapeDtypeStruct((GATE_ROWS, d), BF16)],
        compiler_params=_params(1),
        name="prep",
    )(wt, wt, wt)


def _pre(xt, bsz, seq, g1, w_all, w_ft, bf_pad, ln_g, ln_b, w_sgu, bm):
    n_tok, d = xt.shape
    tiles_per_seq = seq // TP
    head_map = lambda i: (i // tiles_per_seq, 0, i % tiles_per_seq, 0)
    return pl.pallas_call(
        functools.partial(_pre_kernel, tiles_per_seq=tiles_per_seq),
        grid=(n_tok // TP,),
        in_specs=[
            _rows(d, TP),
            _const((1, d)),
            _const((d, _W_COLS)),
            _const((GATE_ROWS, d)),
            _const((GATE_ROWS, SUB)),
            _const((1, SGU_WIDTH)),
            _const((1, SGU_WIDTH)),
            _const((SGU_GROUPS, SGU_LEN, SGU_LEN)),
            _const((SGU_LEN, SGU_WIDTH)),
        ],
        out_specs=[
            pl.BlockSpec((1, FOX_HEADS, TP, LANES), head_map),
            pl.BlockSpec((1, FOX_HEADS, TP, LANES), head_map),
            pl.BlockSpec((1, FOX_HEADS, TP // TQ, VT_ROWS, TQ), lambda i: head_map(i) + (0,)),
            _rows(SGU_WIDTH, TP),
            _rows(2 * d, TP),
        ],
        out_shape=[
            jax.ShapeDtypeStruct((bsz, FOX_HEADS, seq, LANES), BF16),
            jax.ShapeDtypeStruct((bsz, FOX_HEADS, seq, LANES), BF16),
            jax.ShapeDtypeStruct((bsz, FOX_HEADS, seq // TQ, VT_ROWS, TQ), BF16),
            jax.ShapeDtypeStruct((n_tok, SGU_WIDTH), BF16),
            jax.ShapeDtypeStruct((n_tok, 2 * d), BF16),
        ],
        scratch_shapes=[pltpu.VMEM((FOX_HEADS, SUB), F32)],
        compiler_params=_params(1),
        name="pre",
    )(xt, g1, w_all, w_ft, bf_pad, ln_g, ln_b, w_sgu, bm)


def _attn(qa, ka, vt):
    bsz, _, seq, _ = qa.shape
    return pl.pallas_call(
        functools.partial(_attn_kernel, n_q=seq // TQ),
        grid=(bsz, FOX_HEADS // 2),
        in_specs=[
            pl.BlockSpec((1, 2, seq, LANES), lambda b, p: (b, p, 0, 0)),
            pl.BlockSpec((1, 2, seq, LANES), lambda b, p: (b, p, 0, 0)),
            pl.BlockSpec((1, 2, seq // TQ, VT_ROWS, TQ), lambda b, p: (b, p, 0, 0, 0)),
        ],
        out_specs=pl.BlockSpec((1, seq, LANES), lambda b, p: (b, 0, p)),
        out_shape=jax.ShapeDtypeStruct((bsz, seq, FOX_WIDTH), BF16),
        scratch_shapes=[pltpu.VMEM((4, VT_ROWS, QC), F32), pltpu.VMEM((4, TQ, QC), F32)],
        compiler_params=_params(2),
        name="attn",
    )(qa, ka, vt)


def _post(xt, att, sb, gs, w_a, w_b, w_o, g2, w_up, w_down, gf):
    n_tok, d = xt.shape
    return pl.pallas_call(
        functools.partial(_post_kernel, ff_chunk=1024),
        grid=(n_tok // TM,),
        in_specs=[
            _rows(d),
            _rows(FOX_WIDTH),
            _rows(SGU_WIDTH),
            _rows(2 * d),
            _const((FOX_WIDTH, d)),
            _const((SGU_WIDTH, d)),
            _const((d, d)),
            _const((1, d)),
            _const((d, D_FF)),
            _const((D_FF, d)),
            _const((1, d)),
        ],
        out_specs=_rows(d),
        out_shape=jax.ShapeDtypeStruct((n_tok, d), F32),
        compiler_params=_params(1),
        name="post",
    )(xt, att, sb, gs, w_a, w_b, w_o, g2, w_up, w_down, gf)


def kernel(x, norm1_g, w_in, b_f, ln_v_g, ln_v_b, w_sgu, b_sgu, w_a, w_b, w_o,
           norm2_g, w_up, w_down, normf_g):
    bsz, seq, d = x.shape
    assert d == D_MODEL and seq % TP == 0 and TP % TQ == 0 and TQ % SUB == 0 and TM % SUB == 0
    assert norm1_g.shape[0] == 1, "single-layer block"
    n_tok = bsz * seq
    xt = x.reshape(n_tok, d)

    w_all, w_ft = _prep(jnp.swapaxes(w_in[0], 0, 1))
    bf_rows = jnp.pad(jnp.tile(b_f[0], N_PARTS), (0, GATE_ROWS - N_PARTS * FOX_HEADS))
    bf_pad = jnp.broadcast_to(bf_rows[:, None], (GATE_ROWS, SUB))
    bm = jnp.repeat(jnp.transpose(b_sgu[0]), HEAD_DIM, axis=1)

    qa, ka, vt, sb, gs = _pre(xt, bsz, seq, norm1_g.reshape(1, d), w_all, w_ft, bf_pad,
                              ln_v_g.reshape(1, SGU_WIDTH), ln_v_b.reshape(1, SGU_WIDTH),
                              w_sgu[0], bm)
    att = _attn(qa, ka, vt)
    out = _post(xt, att.reshape(n_tok, FOX_WIDTH), sb, gs, w_a[0].astype(BF16),
                w_b[0].astype(BF16), w_o[0].astype(BF16), norm2_g.reshape(1, d),
                w_up[0].astype(BF16), w_down[0].astype(BF16), normf_g.reshape(1, d))
    return out.reshape(bsz, seq, d)
```

```python
import functools
import math

import jax
import jax.numpy as jnp
from jax import lax
from jax.experimental import pallas as pl
from jax.experimental.pallas import tpu as pltpu

D_MODEL = 1024
HEAD_DIM = 64
FOX_HEADS = 8
FOX_WIDTH = FOX_HEADS * HEAD_DIM
SGU_GROUPS = 8
SGU_WIDTH = 512
SGU_LEN = 128
CHUNK = 64
D_FF = 4 * D_MODEL
EPS = 1e-6

LANES = 128
TM = 512
TP = 512
TQ = 512
QC = 256
SUB = 256
VT_ROWS = 128
PREP_COLS = 512
LOG2E = math.log2(math.e)
VMEM_LIMIT = 56 * 1024 * 1024

_F0 = 3 * FOX_WIDTH
_Q0, _K0, _V0 = 0, FOX_WIDTH, 2 * FOX_WIDTH
_U0 = _F0
_GA0 = _U0 + 2 * SGU_WIDTH
_GB0 = _GA0 + D_MODEL
_W_COLS = _GB0 + D_MODEL
N_PARTS = 3
GATE_ROWS = 32

BF16 = jnp.bfloat16
F32 = jnp.float32


def _dot(a, b):
    return jnp.dot(a, b, preferred_element_type=F32)


def _gelu_tanh(x):
    c = math.sqrt(2.0 / math.pi)
    half = 0.5 * x
    return half * jnp.tanh(x * ((x * x) * (c * 0.044715) + c)) + half


def _sigmoid(x):
    return 0.5 * jnp.tanh(0.5 * x) + 0.5


def _split3(x):
    hi = x.astype(BF16).astype(F32)
    r = x - hi
    mid = r.astype(BF16).astype(F32)
    return hi, mid, r - mid


def _prep_kernel(a_ref, b_ref, g_ref, w_ref, wft_ref):
    a = a_ref[...]
    past_gates = jnp.concatenate([a[FOX_HEADS:], b_ref[...]], axis=0)
    rows = jnp.where(pl.program_id(0) < _F0 // PREP_COLS, a, past_gates)
    w_ref[...] = rows.T.astype(BF16)
    g = g_ref[...]
    pad = jnp.zeros((GATE_ROWS - N_PARTS * FOX_HEADS, g.shape[1]), g.dtype)
    wft_ref[...] = jnp.concatenate([g] * N_PARTS + [pad], axis=0).astype(BF16)


def _pre_kernel(x_ref, g1_ref, w_ref, wft_ref, bf_ref, lng_ref, lnb_ref, wm_ref, bm_ref,
                qa_ref, ka_ref, vt_ref, sb_ref, gs_ref, carry_ref, *, tiles_per_seq):
    i = pl.program_id(0)

    @pl.when(i % tiles_per_seq == 0)
    def _():
        carry_ref[...] = jnp.zeros_like(carry_ref)

    s_i = lax.broadcasted_iota(jnp.int32, (SUB, SUB), 0)
    t_i = lax.broadcasted_iota(jnp.int32, (SUB, SUB), 1)
    tri = jnp.where(s_i <= t_i, 1.0, 0.0).astype(BF16)
    grow = lax.broadcasted_iota(jnp.int32, (GATE_ROWS, SUB), 0)
    head_pad = jnp.zeros((LANES - FOX_HEADS, SUB), F32)
    lane = lax.broadcasted_iota(jnp.int32, (SUB, LANES), 1)
    low = lane < HEAD_DIM
    ones_row = jnp.where(lax.broadcasted_iota(jnp.int32, (VT_ROWS - HEAD_DIM, SUB), 0) == 0,
                         1.0, 0.0)
    n_win = SUB // SGU_LEN
    wi = lax.broadcasted_iota(jnp.int32, (SGU_LEN, SGU_LEN), 0) // CHUNK
    wj = lax.broadcasted_iota(jnp.int32, (SGU_LEN, SGU_LEN), 1) // CHUNK
    wmask = wj <= wi
    lane_w = lax.broadcasted_iota(jnp.int32, (SGU_LEN, n_win * LANES), 1)
    low_w = (lane_w % LANES) < HEAD_DIM

    def proj(h, c0, width):
        return _dot(h, w_ref[:, c0:c0 + width])

    carry = carry_ref[...]
    for sub in range(TP // SUB):
        rows = slice(sub * SUB, (sub + 1) * SUB)
        kblk, kcols = divmod(sub * SUB, TQ)
        x = x_ref[rows, :]
        ms = jnp.mean(x * x, axis=-1, keepdims=True)
        h = (x * lax.rsqrt(ms + EPS) * g1_ref[...]).astype(BF16)

        z = lax.dot_general(wft_ref[...], h, (((1,), (1,)), ((), ())),
                            preferred_element_type=F32) + bf_ref[...]
        usv = proj(h, _U0, 2 * SGU_WIDTH)
        k_all = proj(h, _K0, FOX_WIDTH)

        logf = jnp.minimum(z, 0.0) - jnp.log(1.0 + jnp.exp(-jnp.abs(z)))
        hi, mid, lo = _split3(logf)
        part = jnp.where(grow < FOX_HEADS, hi, jnp.where(grow < 2 * FOX_HEADS, mid, lo))
        cl = _dot(part.astype(BF16), tri)
        ct = carry + (cl[:FOX_HEADS] + cl[FOX_HEADS:2 * FOX_HEADS]
                      + cl[2 * FOX_HEADS:3 * FOX_HEADS])
        carry = jnp.broadcast_to(ct[:, SUB - 1:SUB], ct.shape)
        c = jnp.concatenate([ct, head_pad], axis=0).T
        nc_parts = _split3(c * (-LOG2E))

        q_all = proj(h, _Q0, FOX_WIDTH) * (HEAD_DIM ** -0.5 * LOG2E)
        g_a = proj(h, _GA0, D_MODEL)

        u = _gelu_tanh(usv[:, :SGU_WIDTH])
        sv = _gelu_tanh(usv[:, SGU_WIDTH:])
        mu = jnp.mean(sv, axis=-1, keepdims=True)
        xc = sv - mu
        var = jnp.mean(xc * xc, axis=-1, keepdims=True)
        svn = (xc * lax.rsqrt(var + EPS) * lng_ref[...] + lnb_ref[...]).astype(BF16)
        for jp in range(SGU_GROUPS // 2):
            sl = slice(LANES * jp, LANES * (jp + 1))
            chunk = svn[:, sl]
            rhs = jnp.concatenate(
                [chunk[SGU_LEN * w:SGU_LEN * (w + 1), :] for w in range(n_win)], axis=1)
            wa = jnp.where(wmask, wm_ref[2 * jp], 0.0).astype(BF16)
            wb = jnp.where(wmask, wm_ref[2 * jp + 1], 0.0).astype(BF16)
            zero = jnp.zeros_like(rhs)
            stacked = jnp.concatenate([jnp.where(low_w, rhs, zero), jnp.where(low_w, zero, rhs)],
                                      axis=0)
            mixed = _dot(jnp.concatenate([wa, wb], axis=1), stacked)
            mixed = mixed + jnp.concatenate([bm_ref[:, sl]] * n_win, axis=1)
            mixed = jnp.concatenate(
                [mixed[:, LANES * w:LANES * (w + 1)] for w in range(n_win)], axis=0)
            sb_ref[rows, sl] = (u[:, sl] * mixed).astype(BF16)
        gs_ref[rows, :D_MODEL] = g_a.astype(BF16)

        v_all = proj(h, _V0, FOX_WIDTH)
        g_b = proj(h, _GB0, D_MODEL)
        for jp in range(FOX_HEADS // 2):
            sl = slice(LANES * jp, LANES * (jp + 1))
            qc, kc = q_all[:, sl], k_all[:, sl]
            vt = v_all[:, sl].T
            for par in range(2):
                hd = 2 * jp + par
                a0 = HEAD_DIM if par == 0 else 0
                data = low if par == 0 else jnp.logical_not(low)
                in_aug = (lane >= a0) & (lane < a0 + N_PARTS)
                qa = jnp.where(data, qc, jnp.where(in_aug, 1.0, 0.0))
                hi, mid, lo = [jnp.broadcast_to(p[:, hd:hd + 1], (SUB, LANES)) for p in nc_parts]
                aug = jnp.where(lane == a0, hi,
                                jnp.where(lane == a0 + 1, mid,
                                          jnp.where(lane == a0 + 2, lo, 0.0)))
                ka = jnp.where(data, kc, aug)
                qa_ref[0, hd, rows, :] = qa.astype(BF16)
                ka_ref[0, hd, rows, :] = ka.astype(BF16)
                vta = jnp.concatenate([vt[par * HEAD_DIM:(par + 1) * HEAD_DIM], ones_row], axis=0)
                vt_ref[0, hd, kblk, :, kcols:kcols + SUB] = vta.astype(BF16)
        gs_ref[rows, D_MODEL:] = g_b.astype(BF16)
    carry_ref[...] = carry


def _attn_kernel(qa_ref, ka_ref, vt_ref, o_ref, acc_ref, st_ref, *, n_q):
    k_i = lax.broadcasted_iota(jnp.int32, (QC, QC), 0)
    q_i = lax.broadcasted_iota(jnp.int32, (QC, QC), 1)
    causal = k_i <= q_i
    halves = (slice(0, QC), slice(QC, TQ))

    def needed(c, tile, blk, half):
        return not (blk == tile and c == 0 and half == 1)

    def score_half(ch, tile, blk, half):
        hd, c = divmod(ch, 2)
        q0 = tile * TQ + c * QC
        k0 = blk * TQ + half * QC
        st = lax.dot_general(ka_ref[0, hd, pl.ds(k0, QC), :], qa_ref[0, hd, pl.ds(q0, QC), :],
                             (((1,), (1,)), ((), ())), preferred_element_type=F32)
        if blk == tile and half == c:
            st = jnp.where(causal, st, -jnp.inf)
        st_ref[ch, halves[half], :] = st
        return jnp.max(st, axis=0, keepdims=True)

    def chain_step(ch, tile, blk, m_old, cm, nxt):
        hd, c = divmod(ch, 2)
        m_new = cm if m_old is None else jnp.maximum(m_old, cm)
        pv, cm_next = [], []
        for half in range(2):
            use = needed(c, tile, blk, half)
            if use:
                p = jnp.exp2(st_ref[ch, halves[half], :] - m_new).astype(BF16)
            if nxt is not None and needed(c, *nxt, half):
                cm_next.append(score_half(ch, *nxt, half))
            if use:
                pv.append(_dot(vt_ref[0, hd, blk, :, halves[half]], p))
        pv = sum(pv[1:], pv[0])
        if m_old is None:
            acc_ref[ch] = pv
        else:
            acc_ref[ch] = acc_ref[ch] * jnp.exp2(m_old - m_new) + pv
        return m_new, (functools.reduce(jnp.maximum, cm_next) if cm_next else None)

    steps = [(qi, j) for qi in range(n_q) for j in range(qi + 1)]
    ms = [None] * 4
    cms = [functools.reduce(jnp.maximum, [score_half(ch, *steps[0], half) for half in range(2)
                                          if needed(ch % 2, *steps[0], half)])
           for ch in range(4)]
    for n, (qi, j) in enumerate(steps):
        nxt = steps[n + 1] if n + 1 < len(steps) else None
        for ch in range(4):
            ms[ch], cms[ch] = chain_step(ch, qi, j, ms[ch], cms[ch], nxt)
        if j == qi:
            for c in range(2):
                o_t = jnp.concatenate(
                    [acc_ref[2 * hd + c, :HEAD_DIM, :] / acc_ref[2 * hd + c, HEAD_DIM:HEAD_DIM + 1, :]
                     for hd in range(2)], axis=0)
                o_ref[0, pl.ds(qi * TQ + c * QC, QC), :] = o_t.T.astype(BF16)
            ms = [None] * 4


def _post_kernel(x_ref, at_ref, sb_ref, gs_ref, wa_ref, wb_ref, wo_ref, g2_ref, wu_ref,
                 wd_ref, gf_ref, o_ref, *, ff_chunk):
    subs = [slice(s * SUB, (s + 1) * SUB) for s in range(TM // SUB)]
    merged = []
    for rows in subs:
        ya = _dot(at_ref[rows, :], wa_ref[...])
        yb = _dot(sb_ref[rows, :], wb_ref[...])
        merged.append((_sigmoid(gs_ref[rows, :D_MODEL].astype(F32)) * ya
                       + _sigmoid(gs_ref[rows, D_MODEL:].astype(F32)) * yb).astype(BF16))
    ys = [x_ref[rows, :] + _dot(m, wo_ref[...]) for rows, m in zip(subs, merged)]
    hs = []
    for y in ys:
        ms = jnp.mean(y * y, axis=-1, keepdims=True)
        hs.append((y * lax.rsqrt(ms + EPS) * g2_ref[...]).astype(BF16))
    for c0 in range(0, D_FF, ff_chunk):
        acts = []
        for h in hs:
            a = jnp.maximum(_dot(h, wu_ref[:, c0:c0 + ff_chunk]), 0.0)
            acts.append((a * a).astype(BF16))
        ys = [y + _dot(a, wd_ref[c0:c0 + ff_chunk, :]) for y, a in zip(ys, acts)]
    for rows, y in zip(subs, ys):
        ms = jnp.mean(y * y, axis=-1, keepdims=True)
        o_ref[rows, :] = y * lax.rsqrt(ms + EPS) * gf_ref[...]


def _params(n_axes):
    return pltpu.CompilerParams(dimension_semantics=("arbitrary",) * n_axes,
                                vmem_limit_bytes=VMEM_LIMIT)


def _const(shape):
    return pl.BlockSpec(shape, lambda *_: (0,) * len(shape), pipeline_mode=pl.Buffered(1))


def _rows(width, rows=TM):
    return pl.BlockSpec((rows, width), lambda i: (i, 0))


def _prep(wt):
    n, d = wt.shape
    assert n == _W_COLS + FOX_HEADS and _F0 % PREP_COLS == 0 and _W_COLS % PREP_COLS == 0
    per_tile = PREP_COLS // FOX_HEADS
    return pl.pallas_call(
        _prep_kernel,
        grid=(_W_COLS // PREP_COLS,),
        in_specs=[pl.BlockSpec((PREP_COLS, d), lambda t: (t, 0)),
                  pl.BlockSpec((FOX_HEADS, d), lambda t: ((t + 1) * per_tile, 0)),
                  pl.BlockSpec((FOX_HEADS, d), lambda t: (_F0 // FOX_HEADS, 0))],
        out_specs=[pl.BlockSpec((d, PREP_COLS), lambda t: (0, t)),
                   pl.BlockSpec((GATE_ROWS, d), lambda t: (0, 0))],
        out_shape=[jax.ShapeDtypeStruct((d, _W_COLS), BF16),
                   jax.ShapeDtypeStruct((GATE_ROWS, d), BF16)],
        compiler_params=_params(1),
        name="prep",
    )(wt, wt, wt)


def _pre(xt, bsz, seq, g1, w_all, w_ft, bf_pad, ln_g, ln_b, w_sgu, bm):
    n_tok, d = xt.shape
    tiles_per_seq = seq // TP
    head_map = lambda i: (i // tiles_per_seq, 0, i % tiles_per_seq, 0)
    return pl.pallas_call(
        functools.partial(_pre_kernel, tiles_per_seq=tiles_per_seq),
        grid=(n_tok // TP,),
        in_specs=[
            _rows(d, TP),
            _const((1, d)),
            _const((d, _W_COLS)),
            _const((GATE_ROWS, d)),
            _const((GATE_ROWS, SUB)),
            _const((1, SGU_WIDTH)),
            _const((1, SGU_WIDTH)),
            _const((SGU_GROUPS, SGU_LEN, SGU_LEN)),
            _const((SGU_LEN, SGU_WIDTH)),
        ],
        out_specs=[
            pl.BlockSpec((1, FOX_HEADS, TP, LANES), head_map),
            pl.BlockSpec((1, FOX_HEADS, TP, LANES), head_map),
            pl.BlockSpec((1, FOX_HEADS, TP // TQ, VT_ROWS, TQ), lambda i: head_map(i) + (0,)),
            _rows(SGU_WIDTH, TP),
            _rows(2 * d, TP),
        ],
        out_shape=[
            jax.ShapeDtypeStruct((bsz, FOX_HEADS, seq, LANES), BF16),
            jax.ShapeDtypeStruct((bsz, FOX_HEADS, seq, LANES), BF16),
            jax.ShapeDtypeStruct((bsz, FOX_HEADS, seq // TQ, VT_ROWS, TQ), BF16),
            jax.ShapeDtypeStruct((n_tok, SGU_WIDTH), BF16),
            jax.ShapeDtypeStruct((n_tok, 2 * d), BF16),
        ],
        scratch_shapes=[pltpu.VMEM((FOX_HEADS, SUB), F32)],
        compiler_params=_params(1),
        name="pre",
    )(xt, g1, w_all, w_ft, bf_pad, ln_g, ln_b, w_sgu, bm)


def _attn(qa, ka, vt):
    bsz, _, seq, _ = qa.shape
    return pl.pallas_call(
        functools.partial(_attn_kernel, n_q=seq // TQ),
        grid=(bsz, FOX_HEADS // 2),
        in_specs=[
            pl.BlockSpec((1, 2, seq, LANES), lambda b, p: (b, p, 0, 0)),
            pl.BlockSpec((1, 2, seq, LANES), lambda b, p: (b, p, 0, 0)),
            pl.BlockSpec((1, 2, seq // TQ, VT_ROWS, TQ), lambda b, p: (b, p, 0, 0, 0)),
        ],
        out_specs=pl.BlockSpec((1, seq, LANES), lambda b, p: (b, 0, p)),
        out_shape=jax.ShapeDtypeStruct((bsz, seq, FOX_WIDTH), BF16),
        scratch_shapes=[pltpu.VMEM((4, VT_ROWS, QC), F32), pltpu.VMEM((4, TQ, QC), F32)],
        compiler_params=_params(2),
        name="attn",
    )(qa, ka, vt)


def _post(xt, att, sb, gs, w_a, w_b, w_o, g2, w_up, w_down, gf):
    n_tok, d = xt.shape
    return pl.pallas_call(
        functools.partial(_post_kernel, ff_chunk=1024),
        grid=(n_tok // TM,),
        in_specs=[
            _rows(d),
            _rows(FOX_WIDTH),
            _rows(SGU_WIDTH),
            _rows(2 * d),
            _const((FOX_WIDTH, d)),
            _const((SGU_WIDTH, d)),
            _const((d, d)),
            _const((1, d)),
            _const((d, D_FF)),
            _const((D_FF, d)),
            _const((1, d)),
        ],
        out_specs=_rows(d),
        out_shape=jax.ShapeDtypeStruct((n_tok, d), F32),
        compiler_params=_params(1),
        name="post",
    )(xt, att, sb, gs, w_a, w_b, w_o, g2, w_up, w_down, gf)


def kernel(x, norm1_g, w_in, b_f, ln_v_g, ln_v_b, w_sgu, b_sgu, w_a, w_b, w_o,
           norm2_g, w_up, w_down, normf_g):
    bsz, seq, d = x.shape
    assert d == D_MODEL and seq % TP == 0 and TP % TQ == 0 and TQ % SUB == 0 and TM % SUB == 0
    assert norm1_g.shape[0] == 1, "single-layer block"
    n_tok = bsz * seq
    xt = x.reshape(n_tok, d)

    w_all, w_ft = _prep(jnp.swapaxes(w_in[0], 0, 1))
    bf_rows = jnp.pad(jnp.tile(b_f[0], N_PARTS), (0, GATE_ROWS - N_PARTS * FOX_HEADS))
    bf_pad = jnp.broadcast_to(bf_rows[:, None], (GATE_ROWS, SUB))
    bm = jnp.repeat(jnp.transpose(b_sgu[0]), HEAD_DIM, axis=1)

    qa, ka, vt, sb, gs = _pre(xt, bsz, seq, norm1_g.reshape(1, d), w_all, w_ft, bf_pad,
                              ln_v_g.reshape(1, SGU_WIDTH), ln_v_b.reshape(1, SGU_WIDTH),
                              w_sgu[0], bm)
    att = _attn(qa, ka, vt)
    out = _post(xt, att.reshape(n_tok, FOX_WIDTH), sb, gs, w_a[0].astype(BF16),
                w_b[0].astype(BF16), w_o[0].astype(BF16), norm2_g.reshape(1, d),
                w_up[0].astype(BF16), w_down[0].astype(BF16), normf_g.reshape(1, d))
    return out.reshape(bsz, seq, d)
```

```python
import functools
import math

import jax
import jax.numpy as jnp
from jax import lax
from jax.experimental import pallas as pl
from jax.experimental.pallas import tpu as pltpu

D_MODEL = 1024
HEAD_DIM = 64
FOX_HEADS = 8
FOX_WIDTH = FOX_HEADS * HEAD_DIM
SGU_GROUPS = 8
SGU_WIDTH = 512
SGU_LEN = 128
CHUNK = 64
D_FF = 4 * D_MODEL
EPS = 1e-6

LANES = 128
TM = 512
TP = 512
TQ = 512
QC = 256
SUB = 256
VT_ROWS = 128
PREP_COLS = 512
LOG2E = math.log2(math.e)
VMEM_LIMIT = 56 * 1024 * 1024

_F0 = 3 * FOX_WIDTH
_Q0, _K0, _V0 = 0, FOX_WIDTH, 2 * FOX_WIDTH
_U0 = _F0
_GA0 = _U0 + 2 * SGU_WIDTH
_GB0 = _GA0 + D_MODEL
_W_COLS = _GB0 + D_MODEL
N_PARTS = 3
GATE_ROWS = 32

BF16 = jnp.bfloat16
F32 = jnp.float32


def _dot(a, b):
    return jnp.dot(a, b, preferred_element_type=F32)


def _gelu_tanh(x):
    c = math.sqrt(2.0 / math.pi)
    half = 0.5 * x
    return half * jnp.tanh(x * ((x * x) * (c * 0.044715) + c)) + half


def _sigmoid(x):
    return 0.5 * jnp.tanh(0.5 * x) + 0.5


def _split3(x):
    hi = x.astype(BF16).astype(F32)
    r = x - hi
    mid = r.astype(BF16).astype(F32)
    return hi, mid, r - mid


def _prep_kernel(a_ref, b_ref, g_ref, w_ref, wft_ref):
    a = a_ref[...]
    past_gates = jnp.concatenate([a[FOX_HEADS:], b_ref[...]], axis=0)
    rows = jnp.where(pl.program_id(0) < _F0 // PREP_COLS, a, past_gates)
    w_ref[...] = rows.T.astype(BF16)
    g = g_ref[...]
    pad = jnp.zeros((GATE_ROWS - N_PARTS * FOX_HEADS, g.shape[1]), g.dtype)
    wft_ref[...] = jnp.concatenate([g] * N_PARTS + [pad], axis=0).astype(BF16)


def _pre_kernel(x_ref, g1_ref, w_ref, wft_ref, bf_ref, lng_ref, lnb_ref, wm_ref, bm_ref,
                qa_ref, ka_ref, vt_ref, sb_ref, gs_ref, carry_ref, *, tiles_per_seq):
    i = pl.program_id(0)

    @pl.when(i % tiles_per_seq == 0)
    def _():
        carry_ref[...] = jnp.zeros_like(carry_ref)

    s_i = lax.broadcasted_iota(jnp.int32, (SUB, SUB), 0)
    t_i = lax.broadcasted_iota(jnp.int32, (SUB, SUB), 1)
    tri = jnp.where(s_i <= t_i, 1.0, 0.0).astype(BF16)
    grow = lax.broadcasted_iota(jnp.int32, (GATE_ROWS, SUB), 0)
    head_pad = jnp.zeros((LANES - FOX_HEADS, SUB), F32)
    lane = lax.broadcasted_iota(jnp.int32, (SUB, LANES), 1)
    low = lane < HEAD_DIM
    ones_row = jnp.where(lax.broadcasted_iota(jnp.int32, (VT_ROWS - HEAD_DIM, SUB), 0) == 0,
                         1.0, 0.0)
    n_win = SUB // SGU_LEN
    wi = lax.broadcasted_iota(jnp.int32, (SGU_LEN, SGU_LEN), 0) // CHUNK
    wj = lax.broadcasted_iota(jnp.int32, (SGU_LEN, SGU_LEN), 1) // CHUNK
    wmask = wj <= wi
    lane_w = lax.broadcasted_iota(jnp.int32, (SGU_LEN, n_win * LANES), 1)
    low_w = (lane_w % LANES) < HEAD_DIM

    def proj(h, c0, width):
        return _dot(h, w_ref[:, c0:c0 + width])

    carry = carry_ref[...]
    for sub in range(TP // SUB):
        rows = slice(sub * SUB, (sub + 1) * SUB)
        kblk, kcols = divmod(sub * SUB, TQ)
        x = x_ref[rows, :]
        ms = jnp.mean(x * x, axis=-1, keepdims=True)
        h = (x * lax.rsqrt(ms + EPS) * g1_ref[...]).astype(BF16)

        z = lax.dot_general(wft_ref[...], h, (((1,), (1,)), ((), ())),
                            preferred_element_type=F32) + bf_ref[...]
        usv = proj(h, _U0, 2 * SGU_WIDTH)
        k_all = proj(h, _K0, FOX_WIDTH)

        logf = jnp.minimum(z, 0.0) - jnp.log(1.0 + jnp.exp(-jnp.abs(z)))
        hi, mid, lo = _split3(logf)
        part = jnp.where(grow < FOX_HEADS, hi, jnp.where(grow < 2 * FOX_HEADS, mid, lo))
        cl = _dot(part.astype(BF16), tri)
        ct = carry + (cl[:FOX_HEADS] + cl[FOX_HEADS:2 * FOX_HEADS]
                      + cl[2 * FOX_HEADS:3 * FOX_HEADS])
        carry = jnp.broadcast_to(ct[:, SUB - 1:SUB], ct.shape)
        c = jnp.concatenate([ct, head_pad], axis=0).T
        nc_parts = _split3(c * (-LOG2E))

        q_all = proj(h, _Q0, FOX_WIDTH) * (HEAD_DIM ** -0.5 * LOG2E)
        g_a = proj(h, _GA0, D_MODEL)
        v_all = proj(h, _V0, FOX_WIDTH)

        u = _gelu_tanh(usv[:, :SGU_WIDTH])
        sv = _gelu_tanh(usv[:, SGU_WIDTH:])
        mu = jnp.mean(sv, axis=-1, keepdims=True)
        xc = sv - mu
        var = jnp.mean(xc * xc, axis=-1, keepdims=True)
        svn = (xc * lax.rsqrt(var + EPS) * lng_ref[...] + lnb_ref[...]).astype(BF16)
        for jp in range(SGU_GROUPS // 2):
            sl = slice(LANES * jp, LANES * (jp + 1))
            chunk = svn[:, sl]
            rhs = jnp.concatenate(
                [chunk[SGU_LEN * w:SGU_LEN * (w + 1), :] for w in range(n_win)], axis=1)
            wa = jnp.where(wmask, wm_ref[2 * jp], 0.0).astype(BF16)
            wb = jnp.where(wmask, wm_ref[2 * jp + 1], 0.0).astype(BF16)
            zero = jnp.zeros_like(rhs)
            stacked = jnp.concatenate([jnp.where(low_w, rhs, zero), jnp.where(low_w, zero, rhs)],
                                      axis=0)
            mixed = _dot(jnp.concatenate([wa, wb], axis=1), stacked)
            mixed = mixed + jnp.concatenate([bm_ref[:, sl]] * n_win, axis=1)
            mixed = jnp.concatenate(
                [mixed[:, LANES * w:LANES * (w + 1)] for w in range(n_win)], axis=0)
            sb_ref[rows, sl] = (u[:, sl] * mixed).astype(BF16)
        gs_ref[rows, :D_MODEL] = g_a.astype(BF16)

        g_b = proj(h, _GB0, D_MODEL)
        for jp in range(FOX_HEADS // 2):
            sl = slice(LANES * jp, LANES * (jp + 1))
            qc, kc = q_all[:, sl], k_all[:, sl]
            vt = v_all[:, sl].T
            for par in range(2):
                hd = 2 * jp + par
                a0 = HEAD_DIM if par == 0 else 0
                data = low if par == 0 else jnp.logical_not(low)
                in_aug = (lane >= a0) & (lane < a0 + N_PARTS)
                qa = jnp.where(data, qc, jnp.where(in_aug, 1.0, 0.0))
                hi, mid, lo = [jnp.broadcast_to(p[:, hd:hd + 1], (SUB, LANES)) for p in nc_parts]
                aug = jnp.where(lane == a0, hi,
                                jnp.where(lane == a0 + 1, mid,
                                          jnp.where(lane == a0 + 2, lo, 0.0)))
                ka = jnp.where(data, kc, aug)
                qa_ref[0, hd, rows, :] = qa.astype(BF16)
                ka_ref[0, hd, rows, :] = ka.astype(BF16)
                vta = jnp.concatenate([vt[par * HEAD_DIM:(par + 1) * HEAD_DIM], ones_row], axis=0)
                vt_ref[0, hd, kblk, :, kcols:kcols + SUB] = vta.astype(BF16)
        gs_ref[rows, D_MODEL:] = g_b.astype(BF16)
    carry_ref[...] = carry


def _attn_kernel(qa_ref, ka_ref, vt_ref, o_ref, acc_ref, st_ref, *, n_q):
    k_i = lax.broadcasted_iota(jnp.int32, (QC, QC), 0)
    q_i = lax.broadcasted_iota(jnp.int32, (QC, QC), 1)
    causal = k_i <= q_i
    halves = (slice(0, QC), slice(QC, TQ))

    def needed(c, tile, blk, half):
        return not (blk == tile and c == 0 and half == 1)

    def score_half(ch, tile, blk, half):
        hd, c = divmod(ch, 2)
        q0 = tile * TQ + c * QC
        k0 = blk * TQ + half * QC
        st = lax.dot_general(ka_ref[0, hd, pl.ds(k0, QC), :], qa_ref[0, hd, pl.ds(q0, QC), :],
                             (((1,), (1,)), ((), ())), preferred_element_type=F32)
        if blk == tile and half == c:
            st = jnp.where(causal, st, -jnp.inf)
        st_ref[ch, halves[half], :] = st
        return jnp.max(st, axis=0, keepdims=True)

    def chain_step(ch, tile, blk, m_old, cm, nxt):
        hd, c = divmod(ch, 2)
        m_new = cm if m_old is None else jnp.maximum(m_old, cm)
        pv, cm_next = [], []
        for half in range(2):
            use = needed(c, tile, blk, half)
            if use:
                p = jnp.exp2(st_ref[ch, halves[half], :] - m_new).astype(BF16)
            if nxt is not None and needed(c, *nxt, half):
                cm_next.append(score_half(ch, *nxt, half))
            if use:
                pv.append(_dot(vt_ref[0, hd, blk, :, halves[half]], p))
        pv = sum(pv[1:], pv[0])
        if m_old is None:
            acc_ref[ch] = pv
        else:
            acc_ref[ch] = acc_ref[ch] * jnp.exp2(m_old - m_new) + pv
        return m_new, (functools.reduce(jnp.maximum, cm_next) if cm_next else None)

    steps = [(qi, j) for qi in range(n_q) for j in range(qi + 1)]
    ms = [None] * 4
    cms = [functools.reduce(jnp.maximum, [score_half(ch, *steps[0], half) for half in range(2)
                                          if needed(ch % 2, *steps[0], half)])
           for ch in range(4)]
    for n, (qi, j) in enumerate(steps):
        nxt = steps[n + 1] if n + 1 < len(steps) else None
        for ch in range(4):
            ms[ch], cms[ch] = chain_step(ch, qi, j, ms[ch], cms[ch], nxt)
        if j == qi:
            for c in range(2):
                o_t = jnp.concatenate(
                    [acc_ref[2 * hd + c, :HEAD_DIM, :] / acc_ref[2 * hd + c, HEAD_DIM:HEAD_DIM + 1, :]
                     for hd in range(2)], axis=0)
                o_ref[0, pl.ds(qi * TQ + c * QC, QC), :] = o_t.T.astype(BF16)
            ms = [None] * 4


def _post_kernel(x_ref, at_ref, sb_ref, gs_ref, wa_ref, wb_ref, wo_ref, g2_ref, wu_ref,
                 wd_ref, gf_ref, o_ref, *, ff_chunk):
    subs = [slice(s * SUB, (s + 1) * SUB) for s in range(TM // SUB)]
    merged = []
    for rows in subs:
        ya = _dot(at_ref[rows, :], wa_ref[...])
        yb = _dot(sb_ref[rows, :], wb_ref[...])
        merged.append((_sigmoid(gs_ref[rows, :D_MODEL].astype(F32)) * ya
                       + _sigmoid(gs_ref[rows, D_MODEL:].astype(F32)) * yb).astype(BF16))
    ys = [x_ref[rows, :] + _dot(m, wo_ref[...]) for rows, m in zip(subs, merged)]
    hs = []
    for y in ys:
        ms = jnp.mean(y * y, axis=-1, keepdims=True)
        hs.append((y * lax.rsqrt(ms + EPS) * g2_ref[...]).astype(BF16))
    for c0 in range(0, D_FF, ff_chunk):
        acts = []
        for h in hs:
            a = jnp.maximum(_dot(h, wu_ref[:, c0:c0 + ff_chunk]), 0.0)
            acts.append((a * a).astype(BF16))
        ys = [y + _dot(a, wd_ref[c0:c0 + ff_chunk, :]) for y, a in zip(ys, acts)]
    for rows, y in zip(subs, ys):
        ms = jnp.mean(y * y, axis=-1, keepdims=True)
        o_ref[rows, :] = y * lax.rsqrt(ms + EPS) * gf_ref[...]


def _params(n_axes):
    return pltpu.CompilerParams(dimension_semantics=("arbitrary",) * n_axes,
                                vmem_limit_bytes=VMEM_LIMIT)


def _const(shape):
    return pl.BlockSpec(shape, lambda *_: (0,) * len(shape), pipeline_mode=pl.Buffered(1))


def _rows(width, rows=TM):
    return pl.BlockSpec((rows, width), lambda i: (i, 0))


def _prep(wt):
    n, d = wt.shape
    assert n == _W_COLS + FOX_HEADS and _F0 % PREP_COLS == 0 and _W_COLS % PREP_COLS == 0
    per_tile = PREP_COLS // FOX_HEADS
    return pl.pallas_call(
        _prep_kernel,
        grid=(_W_COLS // PREP_COLS,),
        in_specs=[pl.BlockSpec((PREP_COLS, d), lambda t: (t, 0)),
                  pl.BlockSpec((FOX_HEADS, d), lambda t: ((t + 1) * per_tile, 0)),
                  pl.BlockSpec((FOX_HEADS, d), lambda t: (_F0 // FOX_HEADS, 0))],
        out_specs=[pl.BlockSpec((d, PREP_COLS), lambda t: (0, t)),
                   pl.BlockSpec((GATE_ROWS, d), lambda t: (0, 0))],
        out_shape=[jax.ShapeDtypeStruct((d, _W_COLS), BF16),
                   jax.ShapeDtypeStruct((GATE_ROWS, d), BF16)],
        compiler_params=_params(1),
        name="prep",
    )(wt, wt, wt)


def _pre(xt, bsz, seq, g1, w_all, w_ft, bf_pad, ln_g, ln_b, w_sgu, bm):
    n_tok, d = xt.shape
    tiles_per_seq = seq // TP
    head_map = lambda i: (i // tiles_per_seq, 0, i % tiles_per_seq, 0)
    return pl.pallas_call(
        functools.partial(_pre_kernel, tiles_per_seq=tiles_per_seq),
        grid=(n_tok // TP,),
        in_specs=[
            _rows(d, TP),
            _const((1, d)),
            _const((d, _W_COLS)),
            _const((GATE_ROWS, d)),
            _const((GATE_ROWS, SUB)),
            _const((1, SGU_WIDTH)),
            _const((1, SGU_WIDTH)),
            _const((SGU_GROUPS, SGU_LEN, SGU_LEN)),
            _const((SGU_LEN, SGU_WIDTH)),
        ],
        out_specs=[
            pl.BlockSpec((1, FOX_HEADS, TP, LANES), head_map),
            pl.BlockSpec((1, FOX_HEADS, TP, LANES), head_map),
            pl.BlockSpec((1, FOX_HEADS, TP // TQ, VT_ROWS, TQ), lambda i: head_map(i) + (0,)),
            _rows(SGU_WIDTH, TP),
            _rows(2 * d, TP),
        ],
        out_shape=[
            jax.ShapeDtypeStruct((bsz, FOX_HEADS, seq, LANES), BF16),
            jax.ShapeDtypeStruct((bsz, FOX_HEADS, seq, LANES), BF16),
            jax.ShapeDtypeStruct((bsz, FOX_HEADS, seq // TQ, VT_ROWS, TQ), BF16),
            jax.ShapeDtypeStruct((n_tok, SGU_WIDTH), BF16),
            jax.ShapeDtypeStruct((n_tok, 2 * d), BF16),
        ],
        scratch_shapes=[pltpu.VMEM((FOX_HEADS, SUB), F32)],
        compiler_params=_params(1),
        name="pre",
    )(xt, g1, w_all, w_ft, bf_pad, ln_g, ln_b, w_sgu, bm)


def _attn(qa, ka, vt):
    bsz, _, seq, _ = qa.shape
    return pl.pallas_call(
        functools.partial(_attn_kernel, n_q=seq // TQ),
        grid=(bsz, FOX_HEADS // 2),
        in_specs=[
            pl.BlockSpec((1, 2, seq, LANES), lambda b, p: (b, p, 0, 0)),
            pl.BlockSpec((1, 2, seq, LANES), lambda b, p: (b, p, 0, 0)),
            pl.BlockSpec((1, 2, seq // TQ, VT_ROWS, TQ), lambda b, p: (b, p, 0, 0, 0)),
        ],
        out_specs=pl.BlockSpec((1, seq, LANES), lambda b, p: (b, 0, p)),
        out_shape=jax.ShapeDtypeStruct((bsz, seq, FOX_WIDTH), BF16),
        scratch_shapes=[pltpu.VMEM((4, VT_ROWS, QC), F32), pltpu.VMEM((4, TQ, QC), F32)],
        compiler_params=_params(2),
        name="attn",
    )(qa, ka, vt)


def _post(xt, att, sb, gs, w_a, w_b, w_o, g2, w_up, w_down, gf):
    n_tok, d = xt.shape
    return pl.pallas_call(
        functools.partial(_post_kernel, ff_chunk=1024),
        grid=(n_tok // TM,),
        in_specs=[
            _rows(d),
            _rows(FOX_WIDTH),
            _rows(SGU_WIDTH),
            _rows(2 * d),
            _const((FOX_WIDTH, d)),
            _const((SGU_WIDTH, d)),
            _const((d, d)),
            _const((1, d)),
            _const((d, D_FF)),
            _const((D_FF, d)),
            _const((1, d)),
        ],
        out_specs=_rows(d),
        out_shape=jax.ShapeDtypeStruct((n_tok, d), F32),
        compiler_params=_params(1),
        name="post",
    )(xt, att, sb, gs, w_a, w_b, w_o, g2, w_up, w_down, gf)


def kernel(x, norm1_g, w_in, b_f, ln_v_g, ln_v_b, w_sgu, b_sgu, w_a, w_b, w_o,
           norm2_g, w_up, w_down, normf_g):
    bsz, seq, d = x.shape
    assert d == D_MODEL and seq % TP == 0 and TP % TQ == 0 and TQ % SUB == 0 and TM % SUB == 0
    assert norm1_g.shape[0] == 1, "single-layer block"
    n_tok = bsz * seq
    xt = x.reshape(n_tok, d)

    w_all, w_ft = _prep(jnp.swapaxes(w_in[0], 0, 1))
    bf_rows = jnp.pad(jnp.tile(b_f[0], N_PARTS), (0, GATE_ROWS - N_PARTS * FOX_HEADS))
    bf_pad = jnp.broadcast_to(bf_rows[:, None], (GATE_ROWS, SUB))
    bm = jnp.repeat(jnp.transpose(b_sgu[0]), HEAD_DIM, axis=1)

    qa, ka, vt, sb, gs = _pre(xt, bsz, seq, norm1_g.reshape(1, d), w_all, w_ft, bf_pad,
                              ln_v_g.reshape(1, SGU_WIDTH), ln_v_b.reshape(1, SGU_WIDTH),
                              w_sgu[0], bm)
    att = _attn(qa, ka, vt)
    out = _post(xt, att.reshape(n_tok, FOX_WIDTH), sb, gs, w_a[0].astype(BF16),
                w_b[0].astype(BF16), w_o[0].astype(BF16), norm2_g.reshape(1, d),
                w_up[0].astype(BF16), w_down[0].astype(BF16), normf_g.reshape(1, d))
    return out.reshape(bsz, seq, d)
```

```python
import functools
import math

import jax
import jax.numpy as jnp
from jax import lax
from jax.experimental import pallas as pl
from jax.experimental.pallas import tpu as pltpu

D_MODEL = 1024
HEAD_DIM = 64
FOX_HEADS = 8
FOX_WIDTH = FOX_HEADS * HEAD_DIM
SGU_GROUPS = 8
SGU_WIDTH = 512
SGU_LEN = 128
CHUNK = 64
D_FF = 4 * D_MODEL
EPS = 1e-6

LANES = 128
TM = 1024
TP = 512
TQ = 512
QC = 256
SUB = 256
VT_ROWS = 128
PREP_COLS = 512
LOG2E = math.log2(math.e)
VMEM_LIMIT = 58 * 1024 * 1024

_F0 = 3 * FOX_WIDTH
_Q0, _K0, _V0 = 0, FOX_WIDTH, 2 * FOX_WIDTH
_U0 = _F0
_GA0 = _U0 + 2 * SGU_WIDTH
_GB0 = _GA0 + D_MODEL
_W_COLS = _GB0 + D_MODEL
N_PARTS = 3
GATE_ROWS = 32

BF16 = jnp.bfloat16
F32 = jnp.float32


def _dot(a, b):
    return jnp.dot(a, b, preferred_element_type=F32)


def _gelu_tanh(x):
    c = math.sqrt(2.0 / math.pi)
    half = 0.5 * x
    return half * jnp.tanh(x * ((x * x) * (c * 0.044715) + c)) + half


def _sigmoid(x):
    return 0.5 * jnp.tanh(0.5 * x) + 0.5


def _split3(x):
    hi = x.astype(BF16).astype(F32)
    r = x - hi
    mid = r.astype(BF16).astype(F32)
    return hi, mid, r - mid


def _prep_kernel(a_ref, b_ref, g_ref, w_ref, wft_ref):
    a = a_ref[...]
    past_gates = jnp.concatenate([a[FOX_HEADS:], b_ref[...]], axis=0)
    rows = jnp.where(pl.program_id(0) < _F0 // PREP_COLS, a, past_gates)
    w_ref[...] = rows.T.astype(BF16)
    g = g_ref[...]
    pad = jnp.zeros((GATE_ROWS - N_PARTS * FOX_HEADS, g.shape[1]), g.dtype)
    wft_ref[...] = jnp.concatenate([g] * N_PARTS + [pad], axis=0).astype(BF16)


def _pre_kernel(x_ref, g1_ref, w_ref, wft_ref, bf_ref, lng_ref, lnb_ref, wm_ref, bm_ref,
                qa_ref, ka_ref, vt_ref, sb_ref, gs_ref, carry_ref, *, tiles_per_seq):
    i = pl.program_id(0)

    @pl.when(i % tiles_per_seq == 0)
    def _():
        carry_ref[...] = jnp.zeros_like(carry_ref)

    s_i = lax.broadcasted_iota(jnp.int32, (SUB, SUB), 0)
    t_i = lax.broadcasted_iota(jnp.int32, (SUB, SUB), 1)
    tri = jnp.where(s_i <= t_i, 1.0, 0.0).astype(BF16)
    grow = lax.broadcasted_iota(jnp.int32, (GATE_ROWS, SUB), 0)
    head_pad = jnp.zeros((LANES - FOX_HEADS, SUB), F32)
    lane = lax.broadcasted_iota(jnp.int32, (SUB, LANES), 1)
    low = lane < HEAD_DIM
    ones_row = jnp.where(lax.broadcasted_iota(jnp.int32, (VT_ROWS - HEAD_DIM, SUB), 0) == 0,
                         1.0, 0.0)
    n_win = SUB // SGU_LEN
    wi = lax.broadcasted_iota(jnp.int32, (SGU_LEN, SGU_LEN), 0) // CHUNK
    wj = lax.broadcasted_iota(jnp.int32, (SGU_LEN, SGU_LEN), 1) // CHUNK
    wmask = wj <= wi
    lane_w = lax.broadcasted_iota(jnp.int32, (SGU_LEN, n_win * LANES), 1)
    low_w = (lane_w % LANES) < HEAD_DIM

    def proj(h, c0, width):
        return _dot(h, w_ref[:, c0:c0 + width])

    carry = carry_ref[...]
    for sub in range(TP // SUB):
        rows = slice(sub * SUB, (sub + 1) * SUB)
        kblk, kcols = divmod(sub * SUB, TQ)
        x = x_ref[rows, :]
        ms = jnp.mean(x * x, axis=-1, keepdims=True)
        h = (x * lax.rsqrt(ms + EPS) * g1_ref[...]).astype(BF16)

        z = lax.dot_general(wft_ref[...], h, (((1,), (1,)), ((), ())),
                            preferred_element_type=F32) + bf_ref[...]
        usv = proj(h, _U0, 2 * SGU_WIDTH)
        k_all = proj(h, _K0, FOX_WIDTH)

        logf = jnp.minimum(z, 0.0) - jnp.log(1.0 + jnp.exp(-jnp.abs(z)))
        hi, mid, lo = _split3(logf)
        part = jnp.where(grow < FOX_HEADS, hi, jnp.where(grow < 2 * FOX_HEADS, mid, lo))
        cl = _dot(part.astype(BF16), tri)
        ct = carry + (cl[:FOX_HEADS] + cl[FOX_HEADS:2 * FOX_HEADS]
                      + cl[2 * FOX_HEADS:3 * FOX_HEADS])
        carry = jnp.broadcast_to(ct[:, SUB - 1:SUB], ct.shape)
        c = jnp.concatenate([ct, head_pad], axis=0).T
        nc_parts = _split3(c * (-LOG2E))

        q_all = proj(h, _Q0, FOX_WIDTH) * (HEAD_DIM ** -0.5 * LOG2E)
        g_a = proj(h, _GA0, D_MODEL)
        v_all = proj(h, _V0, FOX_WIDTH)

        u = _gelu_tanh(usv[:, :SGU_WIDTH])
        sv = _gelu_tanh(usv[:, SGU_WIDTH:])
        mu = jnp.mean(sv, axis=-1, keepdims=True)
        xc = sv - mu
        var = jnp.mean(xc * xc, axis=-1, keepdims=True)
        svn = (xc * lax.rsqrt(var + EPS) * lng_ref[...] + lnb_ref[...]).astype(BF16)
        for jp in range(SGU_GROUPS // 2):
            sl = slice(LANES * jp, LANES * (jp + 1))
            chunk = svn[:, sl]
            rhs = jnp.concatenate(
                [chunk[SGU_LEN * w:SGU_LEN * (w + 1), :] for w in range(n_win)], axis=1)
            wa = jnp.where(wmask, wm_ref[2 * jp], 0.0).astype(BF16)
            wb = jnp.where(wmask, wm_ref[2 * jp + 1], 0.0).astype(BF16)
            zero = jnp.zeros_like(rhs)
            stacked = jnp.concatenate([jnp.where(low_w, rhs, zero), jnp.where(low_w, zero, rhs)],
                                      axis=0)
            mixed = _dot(jnp.concatenate([wa, wb], axis=1), stacked)
            mixed = mixed + jnp.concatenate([bm_ref[:, sl]] * n_win, axis=1)
            mixed = jnp.concatenate(
                [mixed[:, LANES * w:LANES * (w + 1)] for w in range(n_win)], axis=0)
            sb_ref[rows, sl] = (u[:, sl] * mixed).astype(BF16)
        gs_ref[rows, :D_MODEL] = g_a.astype(BF16)

        g_b = proj(h, _GB0, D_MODEL)
        for jp in range(FOX_HEADS // 2):
            sl = slice(LANES * jp, LANES * (jp + 1))
            qc, kc = q_all[:, sl], k_all[:, sl]
            vt = v_all[:, sl].T
            for par in range(2):
                hd = 2 * jp + par
                a0 = HEAD_DIM if par == 0 else 0
                data = low if par == 0 else jnp.logical_not(low)
                in_aug = (lane >= a0) & (lane < a0 + N_PARTS)
                qa = jnp.where(data, qc, jnp.where(in_aug, 1.0, 0.0))
                hi, mid, lo = [jnp.broadcast_to(p[:, hd:hd + 1], (SUB, LANES)) for p in nc_parts]
                aug = jnp.where(lane == a0, hi,
                                jnp.where(lane == a0 + 1, mid,
                                          jnp.where(lane == a0 + 2, lo, 0.0)))
                ka = jnp.where(data, kc, aug)
                qa_ref[0, hd, rows, :] = qa.astype(BF16)
                ka_ref[0, hd, rows, :] = ka.astype(BF16)
                vta = jnp.concatenate([vt[par * HEAD_DIM:(par + 1) * HEAD_DIM], ones_row], axis=0)
                vt_ref[0, hd, kblk, :, kcols:kcols + SUB] = vta.astype(BF16)
        gs_ref[rows, D_MODEL:] = g_b.astype(BF16)
    carry_ref[...] = carry


def _attn_kernel(qa_ref, ka_ref, vt_ref, o_ref, acc_ref, st_ref, *, n_q):
    k_i = lax.broadcasted_iota(jnp.int32, (QC, QC), 0)
    q_i = lax.broadcasted_iota(jnp.int32, (QC, QC), 1)
    causal = k_i <= q_i
    halves = (slice(0, QC), slice(QC, TQ))

    def needed(c, tile, blk, half):
        return not (blk == tile and c == 0 and half == 1)

    def score_half(ch, tile, blk, half):
        hd, c = divmod(ch, 2)
        q0 = tile * TQ + c * QC
        k0 = blk * TQ + half * QC
        st = lax.dot_general(ka_ref[0, hd, pl.ds(k0, QC), :], qa_ref[0, hd, pl.ds(q0, QC), :],
                             (((1,), (1,)), ((), ())), preferred_element_type=F32)
        if blk == tile and half == c:
            st = jnp.where(causal, st, -jnp.inf)
        st_ref[ch, halves[half], :] = st
        return jnp.max(st, axis=0, keepdims=True)

    def chain_step(ch, tile, blk, m_old, cm, nxt):
        hd, c = divmod(ch, 2)
        m_new = cm if m_old is None else jnp.maximum(m_old, cm)
        pv, cm_next = [], []
        for half in range(2):
            use = needed(c, tile, blk, half)
            if use:
                p = jnp.exp2(st_ref[ch, halves[half], :] - m_new).astype(BF16)
            if nxt is not None and needed(c, *nxt, half):
                cm_next.append(score_half(ch, *nxt, half))
            if use:
                pv.append(_dot(vt_ref[0, hd, blk, :, halves[half]], p))
        pv = sum(pv[1:], pv[0])
        if m_old is None:
            acc_ref[ch] = pv
        else:
            acc_ref[ch] = acc_ref[ch] * jnp.exp2(m_old - m_new) + pv
        return m_new, (functools.reduce(jnp.maximum, cm_next) if cm_next else None)

    steps = [(qi, j) for qi in range(n_q) for j in range(qi + 1)]
    ms = [None] * 4
    cms = [functools.reduce(jnp.maximum, [score_half(ch, *steps[0], half) for half in range(2)
                                          if needed(ch % 2, *steps[0], half)])
           for ch in range(4)]
    for n, (qi, j) in enumerate(steps):
        nxt = steps[n + 1] if n + 1 < len(steps) else None
        for ch in range(4):
            ms[ch], cms[ch] = chain_step(ch, qi, j, ms[ch], cms[ch], nxt)
        if j == qi:
            for c in range(2):
                o_t = jnp.concatenate(
                    [acc_ref[2 * hd + c, :HEAD_DIM, :] / acc_ref[2 * hd + c, HEAD_DIM:HEAD_DIM + 1, :]
                     for hd in range(2)], axis=0)
                o_ref[0, pl.ds(qi * TQ + c * QC, QC), :] = o_t.T.astype(BF16)
            ms = [None] * 4


def _post_kernel(x_ref, at_ref, sb_ref, gs_ref, wa_ref, wb_ref, wo_ref, g2_ref, wu_ref,
                 wd_ref, gf_ref, o_ref, *, ff_chunk):
    subs = [slice(s * SUB, (s + 1) * SUB) for s in range(TM // SUB)]
    merged = []
    for rows in subs:
        ya = _dot(at_ref[rows, :], wa_ref[...])
        yb = _dot(sb_ref[rows, :], wb_ref[...])
        merged.append((_sigmoid(gs_ref[rows, :D_MODEL].astype(F32)) * ya
                       + _sigmoid(gs_ref[rows, D_MODEL:].astype(F32)) * yb).astype(BF16))
    ys = [x_ref[rows, :] + _dot(m, wo_ref[...]) for rows, m in zip(subs, merged)]
    hs = []
    for y in ys:
        ms = jnp.mean(y * y, axis=-1, keepdims=True)
        hs.append((y * lax.rsqrt(ms + EPS) * g2_ref[...]).astype(BF16))
    for c0 in range(0, D_FF, ff_chunk):
        acts = []
        for h in hs:
            a = jnp.maximum(_dot(h, wu_ref[:, c0:c0 + ff_chunk]), 0.0)
            acts.append((a * a).astype(BF16))
        ys = [y + _dot(a, wd_ref[c0:c0 + ff_chunk, :]) for y, a in zip(ys, acts)]
    for rows, y in zip(subs, ys):
        ms = jnp.mean(y * y, axis=-1, keepdims=True)
        o_ref[rows, :] = y * lax.rsqrt(ms + EPS) * gf_ref[...]


def _params(n_axes):
    return pltpu.CompilerParams(dimension_semantics=("arbitrary",) * n_axes,
                                vmem_limit_bytes=VMEM_LIMIT)


def _const(shape):
    return pl.BlockSpec(shape, lambda *_: (0,) * len(shape), pipeline_mode=pl.Buffered(1))


def _rows(width, rows=TM):
    return pl.BlockSpec((rows, width), lambda i: (i, 0))


def _prep(wt):
    n, d = wt.shape
    assert n == _W_COLS + FOX_HEADS and _F0 % PREP_COLS == 0 and _W_COLS % PREP_COLS == 0
    per_tile = PREP_COLS // FOX_HEADS
    return pl.pallas_call(
        _prep_kernel,
        grid=(_W_COLS // PREP_COLS,),
        in_specs=[pl.BlockSpec((PREP_COLS, d), lambda t: (t, 0)),
                  pl.BlockSpec((FOX_HEADS, d), lambda t: ((t + 1) * per_tile, 0)),
                  pl.BlockSpec((FOX_HEADS, d), lambda t: (_F0 // FOX_HEADS, 0))],
        out_specs=[pl.BlockSpec((d, PREP_COLS), lambda t: (0, t)),
                   pl.BlockSpec((GATE_ROWS, d), lambda t: (0, 0))],
        out_shape=[jax.ShapeDtypeStruct((d, _W_COLS), BF16),
                   jax.ShapeDtypeStruct((GATE_ROWS, d), BF16)],
        compiler_params=_params(1),
        name="prep",
    )(wt, wt, wt)


def _pre(xt, bsz, seq, g1, w_all, w_ft, bf_pad, ln_g, ln_b, w_sgu, bm):
    n_tok, d = xt.shape
    tiles_per_seq = seq // TP
    head_map = lambda i: (i // tiles_per_seq, 0, i % tiles_per_seq, 0)
    return pl.pallas_call(
        functools.partial(_pre_kernel, tiles_per_seq=tiles_per_seq),
        grid=(n_tok // TP,),
        in_specs=[
            _rows(d, TP),
            _const((1, d)),
            _const((d, _W_COLS)),
            _const((GATE_ROWS, d)),
            _const((GATE_ROWS, SUB)),
            _const((1, SGU_WIDTH)),
            _const((1, SGU_WIDTH)),
            _const((SGU_GROUPS, SGU_LEN, SGU_LEN)),
            _const((SGU_LEN, SGU_WIDTH)),
        ],
        out_specs=[
            pl.BlockSpec((1, FOX_HEADS, TP, LANES), head_map),
            pl.BlockSpec((1, FOX_HEADS, TP, LANES), head_map),
            pl.BlockSpec((1, FOX_HEADS, TP // TQ, VT_ROWS, TQ), lambda i: head_map(i) + (0,)),
            _rows(SGU_WIDTH, TP),
            _rows(2 * d, TP),
        ],
        out_shape=[
            jax.ShapeDtypeStruct((bsz, FOX_HEADS, seq, LANES), BF16),
            jax.ShapeDtypeStruct((bsz, FOX_HEADS, seq, LANES), BF16),
            jax.ShapeDtypeStruct((bsz, FOX_HEADS, seq // TQ, VT_ROWS, TQ), BF16),
            jax.ShapeDtypeStruct((n_tok, SGU_WIDTH), BF16),
            jax.ShapeDtypeStruct((n_tok, 2 * d), BF16),
        ],
        scratch_shapes=[pltpu.VMEM((FOX_HEADS, SUB), F32)],
        compiler_params=_params(1),
        name="pre",
    )(xt, g1, w_all, w_ft, bf_pad, ln_g, ln_b, w_sgu, bm)


def _attn(qa, ka, vt):
    bsz, _, seq, _ = qa.shape
    return pl.pallas_call(
        functools.partial(_attn_kernel, n_q=seq // TQ),
        grid=(bsz, FOX_HEADS // 2),
        in_specs=[
            pl.BlockSpec((1, 2, seq, LANES), lambda b, p: (b, p, 0, 0)),
            pl.BlockSpec((1, 2, seq, LANES), lambda b, p: (b, p, 0, 0)),
            pl.BlockSpec((1, 2, seq // TQ, VT_ROWS, TQ), lambda b, p: (b, p, 0, 0, 0)),
        ],
        out_specs=pl.BlockSpec((1, seq, LANES), lambda b, p: (b, 0, p)),
        out_shape=jax.ShapeDtypeStruct((bsz, seq, FOX_WIDTH), BF16),
        scratch_shapes=[pltpu.VMEM((4, VT_ROWS, QC), F32), pltpu.VMEM((4, TQ, QC), F32)],
        compiler_params=_params(2),
        name="attn",
    )(qa, ka, vt)


def _post(xt, att, sb, gs, w_a, w_b, w_o, g2, w_up, w_down, gf):
    n_tok, d = xt.shape
    return pl.pallas_call(
        functools.partial(_post_kernel, ff_chunk=1024),
        grid=(n_tok // TM,),
        in_specs=[
            _rows(d),
            _rows(FOX_WIDTH),
            _rows(SGU_WIDTH),
            _rows(2 * d),
            _const((FOX_WIDTH, d)),
            _const((SGU_WIDTH, d)),
            _const((d, d)),
            _const((1, d)),
            _const((d, D_FF)),
            _const((D_FF, d)),
            _const((1, d)),
        ],
        out_specs=pl.BlockSpec((TM, d), lambda i: (i, 0), pipeline_mode=pl.Buffered(1)),
        out_shape=jax.ShapeDtypeStruct((n_tok, d), F32),
        compiler_params=_params(1),
        name="post",
    )(xt, att, sb, gs, w_a, w_b, w_o, g2, w_up, w_down, gf)


def kernel(x, norm1_g, w_in, b_f, ln_v_g, ln_v_b, w_sgu, b_sgu, w_a, w_b, w_o,
           norm2_g, w_up, w_down, normf_g):
    bsz, seq, d = x.shape
    assert d == D_MODEL and seq % TP == 0 and TP % TQ == 0 and TQ % SUB == 0 and TM % SUB == 0
    assert norm1_g.shape[0] == 1, "single-layer block"
    n_tok = bsz * seq
    xt = x.reshape(n_tok, d)

    w_all, w_ft = _prep(jnp.swapaxes(w_in[0], 0, 1))
    bf_rows = jnp.pad(jnp.tile(b_f[0], N_PARTS), (0, GATE_ROWS - N_PARTS * FOX_HEADS))
    bf_pad = jnp.broadcast_to(bf_rows[:, None], (GATE_ROWS, SUB))
    bm = jnp.repeat(jnp.transpose(b_sgu[0]), HEAD_DIM, axis=1)

    qa, ka, vt, sb, gs = _pre(xt, bsz, seq, norm1_g.reshape(1, d), w_all, w_ft, bf_pad,
                              ln_v_g.reshape(1, SGU_WIDTH), ln_v_b.reshape(1, SGU_WIDTH),
                              w_sgu[0], bm)
    att = _attn(qa, ka, vt)
    out = _post(xt, att.reshape(n_tok, FOX_WIDTH), sb, gs, w_a[0].astype(BF16),
                w_b[0].astype(BF16), w_o[0].astype(BF16), norm2_g.reshape(1, d),
                w_up[0].astype(BF16), w_down[0].astype(BF16), normf_g.reshape(1, d))
    return out.reshape(bsz, seq, d)
```

```python
import functools
import math

import jax
import jax.numpy as jnp
from jax import lax
from jax.experimental import pallas as pl
from jax.experimental.pallas import tpu as pltpu

D_MODEL = 1024
HEAD_DIM = 64
FOX_HEADS = 8
FOX_WIDTH = FOX_HEADS * HEAD_DIM
SGU_GROUPS = 8
SGU_WIDTH = 512
SGU_LEN = 128
CHUNK = 64
D_FF = 4 * D_MODEL
EPS = 1e-6

LANES = 128
TM = 512
TP = 512
TQ = 512
QC = 256
SUB = 256
VT_ROWS = 128
PREP_COLS = 512
LOG2E = math.log2(math.e)
VMEM_LIMIT = 56 * 1024 * 1024

_F0 = 3 * FOX_WIDTH
_Q0, _K0, _V0 = 0, FOX_WIDTH, 2 * FOX_WIDTH
_U0 = _F0
_GA0 = _U0 + 2 * SGU_WIDTH
_GB0 = _GA0 + D_MODEL
_W_COLS = _GB0 + D_MODEL
N_PARTS = 3
GATE_ROWS = 32

BF16 = jnp.bfloat16
F32 = jnp.float32


def _dot(a, b):
    return jnp.dot(a, b, preferred_element_type=F32)


def _gelu_tanh(x):
    c = math.sqrt(2.0 / math.pi)
    half = 0.5 * x
    return half * jnp.tanh(x * ((x * x) * (c * 0.044715) + c)) + half


def _sigmoid(x):
    return 0.5 * jnp.tanh(0.5 * x) + 0.5


def _split3(x):
    hi = x.astype(BF16).astype(F32)
    r = x - hi
    mid = r.astype(BF16).astype(F32)
    return hi, mid, r - mid


def _prep_kernel(a_ref, b_ref, g_ref, w_ref, wft_ref):
    a = a_ref[...]
    past_gates = jnp.concatenate([a[FOX_HEADS:], b_ref[...]], axis=0)
    rows = jnp.where(pl.program_id(0) < _F0 // PREP_COLS, a, past_gates)
    w_ref[...] = rows.T.astype(BF16)
    g = g_ref[...]
    pad = jnp.zeros((GATE_ROWS - N_PARTS * FOX_HEADS, g.shape[1]), g.dtype)
    wft_ref[...] = jnp.concatenate([g] * N_PARTS + [pad], axis=0).astype(BF16)


def _pre_kernel(x_ref, g1_ref, w_ref, wft_ref, bf_ref, lng_ref, lnb_ref, wm_ref, bm_ref,
                qa_ref, ka_ref, vt_ref, sb_ref, gs_ref, carry_ref, *, tiles_per_seq):
    i = pl.program_id(0)

    @pl.when(i % tiles_per_seq == 0)
    def _():
        carry_ref[...] = jnp.zeros_like(carry_ref)

    s_i = lax.broadcasted_iota(jnp.int32, (SUB, SUB), 0)
    t_i = lax.broadcasted_iota(jnp.int32, (SUB, SUB), 1)
    tri = jnp.where(s_i <= t_i, 1.0, 0.0).astype(BF16)
    grow = lax.broadcasted_iota(jnp.int32, (GATE_ROWS, SUB), 0)
    head_pad = jnp.zeros((LANES - FOX_HEADS, SUB), F32)
    lane = lax.broadcasted_iota(jnp.int32, (SUB, LANES), 1)
    low = lane < HEAD_DIM
    ones_row = jnp.where(lax.broadcasted_iota(jnp.int32, (VT_ROWS - HEAD_DIM, SUB), 0) == 0,
                         1.0, 0.0)
    n_win = SUB // SGU_LEN
    wi = lax.broadcasted_iota(jnp.int32, (SGU_LEN, SGU_LEN), 0) // CHUNK
    wj = lax.broadcasted_iota(jnp.int32, (SGU_LEN, SGU_LEN), 1) // CHUNK
    wmask = wj <= wi
    lane_w = lax.broadcasted_iota(jnp.int32, (SGU_LEN, n_win * LANES), 1)
    low_w = (lane_w % LANES) < HEAD_DIM

    def proj(h, c0, width):
        return _dot(h, w_ref[:, c0:c0 + width])

    carry = carry_ref[...]
    for sub in range(TP // SUB):
        rows = slice(sub * SUB, (sub + 1) * SUB)
        kblk, kcols = divmod(sub * SUB, TQ)
        x = x_ref[rows, :]
        ms = jnp.mean(x * x, axis=-1, keepdims=True)
        h = (x * lax.rsqrt(ms + EPS) * g1_ref[...]).astype(BF16)

        z = lax.dot_general(wft_ref[...], h, (((1,), (1,)), ((), ())),
                            preferred_element_type=F32) + bf_ref[...]
        usv = proj(h, _U0, 2 * SGU_WIDTH)
        k_all = proj(h, _K0, FOX_WIDTH)

        logf = jnp.minimum(z, 0.0) - jnp.log(1.0 + jnp.exp(-jnp.abs(z)))
        hi, mid, lo = _split3(logf)
        part = jnp.where(grow < FOX_HEADS, hi, jnp.where(grow < 2 * FOX_HEADS, mid, lo))
        cl = _dot(part.astype(BF16), tri)
        ct = carry + (cl[:FOX_HEADS] + cl[FOX_HEADS:2 * FOX_HEADS]
                      + cl[2 * FOX_HEADS:3 * FOX_HEADS])
        carry = jnp.broadcast_to(ct[:, SUB - 1:SUB], ct.shape)
        c = jnp.concatenate([ct, head_pad], axis=0).T
        nc_parts = _split3(c * (-LOG2E))

        q_all = proj(h, _Q0, FOX_WIDTH) * (HEAD_DIM ** -0.5 * LOG2E)
        g_a = proj(h, _GA0, D_MODEL)
        v_all = proj(h, _V0, FOX_WIDTH)

        u = _gelu_tanh(usv[:, :SGU_WIDTH])
        sv = _gelu_tanh(usv[:, SGU_WIDTH:])
        mu = jnp.mean(sv, axis=-1, keepdims=True)
        xc = sv - mu
        var = jnp.mean(xc * xc, axis=-1, keepdims=True)
        svn = (xc * lax.rsqrt(var + EPS) * lng_ref[...] + lnb_ref[...]).astype(BF16)
        for jp in range(SGU_GROUPS // 2):
            sl = slice(LANES * jp, LANES * (jp + 1))
            chunk = svn[:, sl]
            rhs = jnp.concatenate(
                [chunk[SGU_LEN * w:SGU_LEN * (w + 1), :] for w in range(n_win)], axis=1)
            wa = jnp.where(wmask, wm_ref[2 * jp], 0.0).astype(BF16)
            wb = jnp.where(wmask, wm_ref[2 * jp + 1], 0.0).astype(BF16)
            zero = jnp.zeros_like(rhs)
            stacked = jnp.concatenate([jnp.where(low_w, rhs, zero), jnp.where(low_w, zero, rhs)],
                                      axis=0)
            mixed = _dot(jnp.concatenate([wa, wb], axis=1), stacked)
            mixed = mixed + jnp.concatenate([bm_ref[:, sl]] * n_win, axis=1)
            mixed = jnp.concatenate(
                [mixed[:, LANES * w:LANES * (w + 1)] for w in range(n_win)], axis=0)
            sb_ref[rows, sl] = (u[:, sl] * mixed).astype(BF16)
        gs_ref[rows, :D_MODEL] = g_a.astype(BF16)

        g_b = proj(h, _GB0, D_MODEL)
        for jp in range(FOX_HEADS // 2):
            sl = slice(LANES * jp, LANES * (jp + 1))
            qc, kc = q_all[:, sl], k_all[:, sl]
            vt = v_all[:, sl].T
            for par in range(2):
                hd = 2 * jp + par
                a0 = HEAD_DIM if par == 0 else 0
                data = low if par == 0 else jnp.logical_not(low)
                in_aug = (lane >= a0) & (lane < a0 + N_PARTS)
                qa = jnp.where(data, qc, jnp.where(in_aug, 1.0, 0.0))
                hi, mid, lo = [jnp.broadcast_to(p[:, hd:hd + 1], (SUB, LANES)) for p in nc_parts]
                aug = jnp.where(lane == a0, hi,
                                jnp.where(lane == a0 + 1, mid,
                                          jnp.where(lane == a0 + 2, lo, 0.0)))
                ka = jnp.where(data, kc, aug)
                qa_ref[0, hd, rows, :] = qa.astype(BF16)
                ka_ref[0, hd, rows, :] = ka.astype(BF16)
                vta = jnp.concatenate([vt[par * HEAD_DIM:(par + 1) * HEAD_DIM], ones_row], axis=0)
                vt_ref[0, hd, kblk, :, kcols:kcols + SUB] = vta.astype(BF16)
        gs_ref[rows, D_MODEL:] = g_b.astype(BF16)
    carry_ref[...] = carry


def _attn_kernel(qa_ref, ka_ref, vt_ref, *refs, n_q, n_cast):
    cast_in, (o_ref, *cast_out), (acc_ref, st_ref) = (
        refs[:n_cast], refs[n_cast:2 * n_cast + 1], refs[2 * n_cast + 1:])
    for src, dst in zip(cast_in, cast_out):
        dst[...] = src[...].astype(BF16)

    k_i = lax.broadcasted_iota(jnp.int32, (QC, QC), 0)
    q_i = lax.broadcasted_iota(jnp.int32, (QC, QC), 1)
    causal = k_i <= q_i
    halves = (slice(0, QC), slice(QC, TQ))

    def needed(c, tile, blk, half):
        return not (blk == tile and c == 0 and half == 1)

    def score_half(ch, tile, blk, half):
        hd, c = divmod(ch, 2)
        q0 = tile * TQ + c * QC
        k0 = blk * TQ + half * QC
        st = lax.dot_general(ka_ref[0, hd, pl.ds(k0, QC), :], qa_ref[0, hd, pl.ds(q0, QC), :],
                             (((1,), (1,)), ((), ())), preferred_element_type=F32)
        if blk == tile and half == c:
            st = jnp.where(causal, st, -jnp.inf)
        st_ref[ch, halves[half], :] = st
        return jnp.max(st, axis=0, keepdims=True)

    def chain_step(ch, tile, blk, m_old, cm, nxt):
        hd, c = divmod(ch, 2)
        m_new = cm if m_old is None else jnp.maximum(m_old, cm)
        pv, cm_next = [], []
        for half in range(2):
            use = needed(c, tile, blk, half)
            if use:
                p = jnp.exp2(st_ref[ch, halves[half], :] - m_new).astype(BF16)
            if nxt is not None and needed(c, *nxt, half):
                cm_next.append(score_half(ch, *nxt, half))
            if use:
                pv.append(_dot(vt_ref[0, hd, blk, :, halves[half]], p))
        pv = sum(pv[1:], pv[0])
        if m_old is None:
            acc_ref[ch] = pv
        else:
            acc_ref[ch] = acc_ref[ch] * jnp.exp2(m_old - m_new) + pv
        return m_new, (functools.reduce(jnp.maximum, cm_next) if cm_next else None)

    steps = [(qi, j) for qi in range(n_q) for j in range(qi + 1)]
    ms = [None] * 4
    cms = [functools.reduce(jnp.maximum, [score_half(ch, *steps[0], half) for half in range(2)
                                          if needed(ch % 2, *steps[0], half)])
           for ch in range(4)]
    for n, (qi, j) in enumerate(steps):
        nxt = steps[n + 1] if n + 1 < len(steps) else None
        for ch in range(4):
            ms[ch], cms[ch] = chain_step(ch, qi, j, ms[ch], cms[ch], nxt)
        if j == qi:
            for c in range(2):
                o_t = jnp.concatenate(
                    [acc_ref[2 * hd + c, :HEAD_DIM, :] / acc_ref[2 * hd + c, HEAD_DIM:HEAD_DIM + 1, :]
                     for hd in range(2)], axis=0)
                o_ref[0, pl.ds(qi * TQ + c * QC, QC), :] = o_t.T.astype(BF16)
            ms = [None] * 4


def _post_kernel(x_ref, at_ref, sb_ref, gs_ref, wa_ref, wb_ref, wo_ref, g2_ref, wu_ref,
                 wd_ref, gf_ref, o_ref, *, ff_chunk):
    subs = [slice(s * SUB, (s + 1) * SUB) for s in range(TM // SUB)]
    merged = []
    for rows in subs:
        ya = _dot(at_ref[rows, :], wa_ref[...])
        yb = _dot(sb_ref[rows, :], wb_ref[...])
        merged.append((_sigmoid(gs_ref[rows, :D_MODEL].astype(F32)) * ya
                       + _sigmoid(gs_ref[rows, D_MODEL:].astype(F32)) * yb).astype(BF16))
    ys = [x_ref[rows, :] + _dot(m, wo_ref[...]) for rows, m in zip(subs, merged)]
    hs = []
    for y in ys:
        ms = jnp.mean(y * y, axis=-1, keepdims=True)
        hs.append((y * lax.rsqrt(ms + EPS) * g2_ref[...]).astype(BF16))
    for c0 in range(0, D_FF, ff_chunk):
        acts = []
        for h in hs:
            a = jnp.maximum(_dot(h, wu_ref[:, c0:c0 + ff_chunk]), 0.0)
            acts.append((a * a).astype(BF16))
        ys = [y + _dot(a, wd_ref[c0:c0 + ff_chunk, :]) for y, a in zip(ys, acts)]
    for rows, y in zip(subs, ys):
        ms = jnp.mean(y * y, axis=-1, keepdims=True)
        o_ref[rows, :] = y * lax.rsqrt(ms + EPS) * gf_ref[...]


def _params(n_axes):
    return pltpu.CompilerParams(dimension_semantics=("arbitrary",) * n_axes,
                                vmem_limit_bytes=VMEM_LIMIT)


def _const(shape):
    return pl.BlockSpec(shape, lambda *_: (0,) * len(shape), pipeline_mode=pl.Buffered(1))


def _rows(width, rows=TM):
    return pl.BlockSpec((rows, width), lambda i: (i, 0))


def _prep(wt):
    n, d = wt.shape
    assert n == _W_COLS + FOX_HEADS and _F0 % PREP_COLS == 0 and _W_COLS % PREP_COLS == 0
    per_tile = PREP_COLS // FOX_HEADS
    return pl.pallas_call(
        _prep_kernel,
        grid=(_W_COLS // PREP_COLS,),
        in_specs=[pl.BlockSpec((PREP_COLS, d), lambda t: (t, 0)),
                  pl.BlockSpec((FOX_HEADS, d), lambda t: ((t + 1) * per_tile, 0)),
                  pl.BlockSpec((FOX_HEADS, d), lambda t: (_F0 // FOX_HEADS, 0))],
        out_specs=[pl.BlockSpec((d, PREP_COLS), lambda t: (0, t)),
                   pl.BlockSpec((GATE_ROWS, d), lambda t: (0, 0))],
        out_shape=[jax.ShapeDtypeStruct((d, _W_COLS), BF16),
                   jax.ShapeDtypeStruct((GATE_ROWS, d), BF16)],
        compiler_params=_params(1),
        name="prep",
    )(wt, wt, wt)


def _pre(xt, bsz, seq, g1, w_all, w_ft, bf_pad, ln_g, ln_b, w_sgu, bm):
    n_tok, d = xt.shape
    tiles_per_seq = seq // TP
    head_map = lambda i: (i // tiles_per_seq, 0, i % tiles_per_seq, 0)
    return pl.pallas_call(
        functools.partial(_pre_kernel, tiles_per_seq=tiles_per_seq),
        grid=(n_tok // TP,),
        in_specs=[
            _rows(d, TP),
            _const((1, d)),
            _const((d, _W_COLS)),
            _const((GATE_ROWS, d)),
            _const((GATE_ROWS, SUB)),
            _const((1, SGU_WIDTH)),
            _const((1, SGU_WIDTH)),
            _const((SGU_GROUPS, SGU_LEN, SGU_LEN)),
            _const((SGU_LEN, SGU_WIDTH)),
        ],
        out_specs=[
            pl.BlockSpec((1, FOX_HEADS, TP, LANES), head_map),
            pl.BlockSpec((1, FOX_HEADS, TP, LANES), head_map),
            pl.BlockSpec((1, FOX_HEADS, TP // TQ, VT_ROWS, TQ), lambda i: head_map(i) + (0,)),
            _rows(SGU_WIDTH, TP),
            _rows(2 * d, TP),
        ],
        out_shape=[
            jax.ShapeDtypeStruct((bsz, FOX_HEADS, seq, LANES), BF16),
            jax.ShapeDtypeStruct((bsz, FOX_HEADS, seq, LANES), BF16),
            jax.ShapeDtypeStruct((bsz, FOX_HEADS, seq // TQ, VT_ROWS, TQ), BF16),
            jax.ShapeDtypeStruct((n_tok, SGU_WIDTH), BF16),
            jax.ShapeDtypeStruct((n_tok, 2 * d), BF16),
        ],
        scratch_shapes=[pltpu.VMEM((FOX_HEADS, SUB), F32)],
        compiler_params=_params(1),
        name="pre",
    )(xt, g1, w_all, w_ft, bf_pad, ln_g, ln_b, w_sgu, bm)


def _attn(qa, ka, vt, weights):
    bsz, _, seq, _ = qa.shape
    pairs = FOX_HEADS // 2
    n_steps = bsz * pairs
    assert all(w.shape[0] % (n_steps * 16) == 0 for w in weights)
    slab = lambda w: pl.BlockSpec((w.shape[0] // n_steps, w.shape[1]),
                                  lambda b, p: (b * pairs + p, 0))
    att, *cast = pl.pallas_call(
        functools.partial(_attn_kernel, n_q=seq // TQ, n_cast=len(weights)),
        grid=(bsz, pairs),
        in_specs=[
            pl.BlockSpec((1, 2, seq, LANES), lambda b, p: (b, p, 0, 0)),
            pl.BlockSpec((1, 2, seq, LANES), lambda b, p: (b, p, 0, 0)),
            pl.BlockSpec((1, 2, seq // TQ, VT_ROWS, TQ), lambda b, p: (b, p, 0, 0, 0)),
        ] + [slab(w) for w in weights],
        out_specs=[pl.BlockSpec((1, seq, LANES), lambda b, p: (b, 0, p))]
        + [slab(w) for w in weights],
        out_shape=[jax.ShapeDtypeStruct((bsz, seq, FOX_WIDTH), BF16)]
        + [jax.ShapeDtypeStruct(w.shape, BF16) for w in weights],
        scratch_shapes=[pltpu.VMEM((4, VT_ROWS, QC), F32), pltpu.VMEM((4, TQ, QC), F32)],
        compiler_params=_params(2),
        name="attn",
    )(qa, ka, vt, *weights)
    return att, cast


def _post(xt, att, sb, gs, w_a, w_b, w_o, g2, w_up, w_down, gf):
    n_tok, d = xt.shape
    return pl.pallas_call(
        functools.partial(_post_kernel, ff_chunk=1024),
        grid=(n_tok // TM,),
        in_specs=[
            _rows(d),
            _rows(FOX_WIDTH),
            _rows(SGU_WIDTH),
            _rows(2 * d),
            _const((FOX_WIDTH, d)),
            _const((SGU_WIDTH, d)),
            _const((d, d)),
            _const((1, d)),
            _const((d, D_FF)),
            _const((D_FF, d)),
            _const((1, d)),
        ],
        out_specs=_rows(d),
        out_shape=jax.ShapeDtypeStruct((n_tok, d), F32),
        compiler_params=_params(1),
        name="post",
    )(xt, att, sb, gs, w_a, w_b, w_o, g2, w_up, w_down, gf)


def kernel(x, norm1_g, w_in, b_f, ln_v_g, ln_v_b, w_sgu, b_sgu, w_a, w_b, w_o,
           norm2_g, w_up, w_down, normf_g):
    bsz, seq, d = x.shape
    assert d == D_MODEL and seq % TP == 0 and TP % TQ == 0 and TQ % SUB == 0 and TM % SUB == 0
    assert norm1_g.shape[0] == 1, "single-layer block"
    n_tok = bsz * seq
    xt = x.reshape(n_tok, d)

    w_all, w_ft = _prep(jnp.swapaxes(w_in[0], 0, 1))
    bf_rows = jnp.pad(jnp.tile(b_f[0], N_PARTS), (0, GATE_ROWS - N_PARTS * FOX_HEADS))
    bf_pad = jnp.broadcast_to(bf_rows[:, None], (GATE_ROWS, SUB))
    bm = jnp.repeat(jnp.transpose(b_sgu[0]), HEAD_DIM, axis=1)

    qa, ka, vt, sb, gs = _pre(xt, bsz, seq, norm1_g.reshape(1, d), w_all, w_ft, bf_pad,
                              ln_v_g.reshape(1, SGU_WIDTH), ln_v_b.reshape(1, SGU_WIDTH),
                              w_sgu[0], bm)
    att, (wa16, wb16, wo16, wu16, wd16) = _attn(
        qa, ka, vt, [w_a[0], w_b[0], w_o[0], w_up[0], w_down[0]])
    out = _post(xt, att.reshape(n_tok, FOX_WIDTH), sb, gs, wa16, wb16, wo16,
                norm2_g.reshape(1, d), wu16, wd16, normf_g.reshape(1, d))
    return out.reshape(bsz, seq, d)
```

```python
import functools
import math

import jax
import jax.numpy as jnp
from jax import lax
from jax.experimental import pallas as pl
from jax.experimental.pallas import tpu as pltpu

D_MODEL = 1024
HEAD_DIM = 64
FOX_HEADS = 8
FOX_WIDTH = FOX_HEADS * HEAD_DIM
SGU_GROUPS = 8
SGU_WIDTH = 512
SGU_LEN = 128
CHUNK = 64
D_FF = 4 * D_MODEL
EPS = 1e-6

LANES = 128
TM = 512
TP = 512
TQ = 512
QC = 256
SUB = 256
VT_ROWS = 128
PREP_COLS = 512
LOG2E = math.log2(math.e)
VMEM_LIMIT = 56 * 1024 * 1024

_F0 = 3 * FOX_WIDTH
_Q0, _K0, _V0 = 0, FOX_WIDTH, 2 * FOX_WIDTH
_U0 = _F0
_GA0 = _U0 + 2 * SGU_WIDTH
_GB0 = _GA0 + D_MODEL
_W_COLS = _GB0 + D_MODEL
N_PARTS = 3
GATE_ROWS = 32

BF16 = jnp.bfloat16
F32 = jnp.float32


def _dot(a, b):
    return jnp.dot(a, b, preferred_element_type=F32)


def _gelu_tanh(x):
    c = math.sqrt(2.0 / math.pi)
    half = 0.5 * x
    return half * jnp.tanh(x * ((x * x) * (c * 0.044715) + c)) + half


def _sigmoid(x):
    return 0.5 * jnp.tanh(0.5 * x) + 0.5


def _split3(x):
    hi = x.astype(BF16).astype(F32)
    r = x - hi
    mid = r.astype(BF16).astype(F32)
    return hi, mid, r - mid


def _prep_kernel(a_ref, b_ref, g_ref, w_ref, wft_ref):
    a = a_ref[...]
    past_gates = jnp.concatenate([a[FOX_HEADS:], b_ref[...]], axis=0)
    rows = jnp.where(pl.program_id(0) < _F0 // PREP_COLS, a, past_gates)
    w_ref[...] = rows.T.astype(BF16)
    g = g_ref[...]
    pad = jnp.zeros((GATE_ROWS - N_PARTS * FOX_HEADS, g.shape[1]), g.dtype)
    wft_ref[...] = jnp.concatenate([g] * N_PARTS + [pad], axis=0).astype(BF16)


def _pre_kernel(x_ref, g1_ref, w_ref, wft_ref, bf_ref, lng_ref, lnb_ref, wm_ref, bm_ref,
                qk_ref, vt_ref, sg_ref, carry_ref, *, tiles_per_seq):
    i = pl.program_id(0)

    @pl.when(i % tiles_per_seq == 0)
    def _():
        carry_ref[...] = jnp.zeros_like(carry_ref)

    s_i = lax.broadcasted_iota(jnp.int32, (SUB, SUB), 0)
    t_i = lax.broadcasted_iota(jnp.int32, (SUB, SUB), 1)
    tri = jnp.where(s_i <= t_i, 1.0, 0.0).astype(BF16)
    grow = lax.broadcasted_iota(jnp.int32, (GATE_ROWS, SUB), 0)
    head_pad = jnp.zeros((LANES - FOX_HEADS, SUB), F32)
    lane = lax.broadcasted_iota(jnp.int32, (SUB, LANES), 1)
    low = lane < HEAD_DIM
    ones_row = jnp.where(lax.broadcasted_iota(jnp.int32, (VT_ROWS - HEAD_DIM, SUB), 0) == 0,
                         1.0, 0.0)
    n_win = SUB // SGU_LEN
    wi = lax.broadcasted_iota(jnp.int32, (SGU_LEN, SGU_LEN), 0) // CHUNK
    wj = lax.broadcasted_iota(jnp.int32, (SGU_LEN, SGU_LEN), 1) // CHUNK
    wmask = wj <= wi
    lane_w = lax.broadcasted_iota(jnp.int32, (SGU_LEN, n_win * LANES), 1)
    low_w = (lane_w % LANES) < HEAD_DIM

    def proj(h, c0, width):
        return _dot(h, w_ref[:, c0:c0 + width])

    carry = carry_ref[...]
    for sub in range(TP // SUB):
        rows = slice(sub * SUB, (sub + 1) * SUB)
        kblk, kcols = divmod(sub * SUB, TQ)
        x = x_ref[rows, :]
        ms = jnp.mean(x * x, axis=-1, keepdims=True)
        h = (x * lax.rsqrt(ms + EPS) * g1_ref[...]).astype(BF16)

        z = lax.dot_general(wft_ref[...], h, (((1,), (1,)), ((), ())),
                            preferred_element_type=F32) + bf_ref[...]
        usv = proj(h, _U0, 2 * SGU_WIDTH)
        k_all = proj(h, _K0, FOX_WIDTH)

        logf = jnp.minimum(z, 0.0) - jnp.log(1.0 + jnp.exp(-jnp.abs(z)))
        hi, mid, lo = _split3(logf)
        part = jnp.where(grow < FOX_HEADS, hi, jnp.where(grow < 2 * FOX_HEADS, mid, lo))
        cl = _dot(part.astype(BF16), tri)
        ct = carry + (cl[:FOX_HEADS] + cl[FOX_HEADS:2 * FOX_HEADS]
                      + cl[2 * FOX_HEADS:3 * FOX_HEADS])
        carry = jnp.broadcast_to(ct[:, SUB - 1:SUB], ct.shape)
        c = jnp.concatenate([ct, head_pad], axis=0).T
        nc_parts = _split3(c * (-LOG2E))

        q_all = proj(h, _Q0, FOX_WIDTH) * (HEAD_DIM ** -0.5 * LOG2E)
        g_a = proj(h, _GA0, D_MODEL)
        v_all = proj(h, _V0, FOX_WIDTH)

        u = _gelu_tanh(usv[:, :SGU_WIDTH])
        sv = _gelu_tanh(usv[:, SGU_WIDTH:])
        mu = jnp.mean(sv, axis=-1, keepdims=True)
        xc = sv - mu
        var = jnp.mean(xc * xc, axis=-1, keepdims=True)
        svn = (xc * lax.rsqrt(var + EPS) * lng_ref[...] + lnb_ref[...]).astype(BF16)
        for jp in range(SGU_GROUPS // 2):
            sl = slice(LANES * jp, LANES * (jp + 1))
            chunk = svn[:, sl]
            rhs = jnp.concatenate(
                [chunk[SGU_LEN * w:SGU_LEN * (w + 1), :] for w in range(n_win)], axis=1)
            wa = jnp.where(wmask, wm_ref[2 * jp], 0.0).astype(BF16)
            wb = jnp.where(wmask, wm_ref[2 * jp + 1], 0.0).astype(BF16)
            zero = jnp.zeros_like(rhs)
            stacked = jnp.concatenate([jnp.where(low_w, rhs, zero), jnp.where(low_w, zero, rhs)],
                                      axis=0)
            mixed = _dot(jnp.concatenate([wa, wb], axis=1), stacked)
            mixed = mixed + jnp.concatenate([bm_ref[:, sl]] * n_win, axis=1)
            mixed = jnp.concatenate(
                [mixed[:, LANES * w:LANES * (w + 1)] for w in range(n_win)], axis=0)
            sg_ref[rows, sl] = (u[:, sl] * mixed).astype(BF16)
        sg_ref[rows, SGU_WIDTH:SGU_WIDTH + D_MODEL] = g_a.astype(BF16)

        g_b = proj(h, _GB0, D_MODEL)
        for jp in range(FOX_HEADS // 2):
            sl = slice(LANES * jp, LANES * (jp + 1))
            qc, kc = q_all[:, sl], k_all[:, sl]
            vt = v_all[:, sl].T
            for par in range(2):
                hd = 2 * jp + par
                a0 = HEAD_DIM if par == 0 else 0
                data = low if par == 0 else jnp.logical_not(low)
                in_aug = (lane >= a0) & (lane < a0 + N_PARTS)
                qa = jnp.where(data, qc, jnp.where(in_aug, 1.0, 0.0))
                hi, mid, lo = [jnp.broadcast_to(p[:, hd:hd + 1], (SUB, LANES)) for p in nc_parts]
                aug = jnp.where(lane == a0, hi,
                                jnp.where(lane == a0 + 1, mid,
                                          jnp.where(lane == a0 + 2, lo, 0.0)))
                ka = jnp.where(data, kc, aug)
                qk_ref[0, hd, rows, :LANES] = qa.astype(BF16)
                qk_ref[0, hd, rows, LANES:] = ka.astype(BF16)
                vta = jnp.concatenate([vt[par * HEAD_DIM:(par + 1) * HEAD_DIM], ones_row], axis=0)
                vt_ref[0, hd, kblk, :, kcols:kcols + SUB] = vta.astype(BF16)
        sg_ref[rows, SGU_WIDTH + D_MODEL:] = g_b.astype(BF16)
    carry_ref[...] = carry


def _attn_kernel(qk_ref, vt_ref, *refs, n_q, n_cast):
    cast_in, (o_ref, *cast_out), (acc_ref, st_ref) = (
        refs[:n_cast], refs[n_cast:2 * n_cast + 1], refs[2 * n_cast + 1:])
    for src, dst in zip(cast_in, cast_out):
        dst[...] = src[...].astype(BF16)

    k_i = lax.broadcasted_iota(jnp.int32, (QC, QC), 0)
    q_i = lax.broadcasted_iota(jnp.int32, (QC, QC), 1)
    causal = k_i <= q_i
    halves = (slice(0, QC), slice(QC, TQ))

    def needed(c, tile, blk, half):
        return not (blk == tile and c == 0 and half == 1)

    def score_half(ch, tile, blk, half):
        hd, c = divmod(ch, 2)
        q0 = tile * TQ + c * QC
        k0 = blk * TQ + half * QC
        st = lax.dot_general(qk_ref[0, hd, pl.ds(k0, QC), LANES:], qk_ref[0, hd, pl.ds(q0, QC), :LANES],
                             (((1,), (1,)), ((), ())), preferred_element_type=F32)
        if blk == tile and half == c:
            st = jnp.where(causal, st, -jnp.inf)
        st_ref[ch, halves[half], :] = st
        return jnp.max(st, axis=0, keepdims=True)

    def chain_step(ch, tile, blk, m_old, cm, nxt):
        hd, c = divmod(ch, 2)
        m_new = cm if m_old is None else jnp.maximum(m_old, cm)
        pv, cm_next = [], []
        for half in range(2):
            use = needed(c, tile, blk, half)
            if use:
                p = jnp.exp2(st_ref[ch, halves[half], :] - m_new).astype(BF16)
            if nxt is not None and needed(c, *nxt, half):
                cm_next.append(score_half(ch, *nxt, half))
            if use:
                pv.append(_dot(vt_ref[0, hd, blk, :, halves[half]], p))
        pv = sum(pv[1:], pv[0])
        if m_old is None:
            acc_ref[ch] = pv
        else:
            acc_ref[ch] = acc_ref[ch] * jnp.exp2(m_old - m_new) + pv
        return m_new, (functools.reduce(jnp.maximum, cm_next) if cm_next else None)

    steps = [(qi, j) for qi in range(n_q) for j in range(qi + 1)]
    ms = [None] * 4
    cms = [functools.reduce(jnp.maximum, [score_half(ch, *steps[0], half) for half in range(2)
                                          if needed(ch % 2, *steps[0], half)])
           for ch in range(4)]
    for n, (qi, j) in enumerate(steps):
        nxt = steps[n + 1] if n + 1 < len(steps) else None
        for ch in range(4):
            ms[ch], cms[ch] = chain_step(ch, qi, j, ms[ch], cms[ch], nxt)
        if j == qi:
            for c in range(2):
                o_t = jnp.concatenate(
                    [acc_ref[2 * hd + c, :HEAD_DIM, :] / acc_ref[2 * hd + c, HEAD_DIM:HEAD_DIM + 1, :]
                     for hd in range(2)], axis=0)
                o_ref[0, pl.ds(qi * TQ + c * QC, QC), :] = o_t.T.astype(BF16)
            ms = [None] * 4


def _post_kernel(x_ref, at_ref, sg_ref, wa_ref, wb_ref, wo_ref, g2_ref, wu_ref,
                 wd_ref, gf_ref, o_ref, *, ff_chunk):
    subs = [slice(s * SUB, (s + 1) * SUB) for s in range(TM // SUB)]
    merged = []
    for rows in subs:
        ya = _dot(at_ref[rows, :], wa_ref[...])
        yb = _dot(sg_ref[rows, :SGU_WIDTH], wb_ref[...])
        gates = _sigmoid(sg_ref[rows, SGU_WIDTH:].astype(F32))
        merged.append((gates[:, :D_MODEL] * ya + gates[:, D_MODEL:] * yb).astype(BF16))
    ys = [x_ref[rows, :] + _dot(m, wo_ref[...]) for rows, m in zip(subs, merged)]
    hs = []
    for y in ys:
        ms = jnp.mean(y * y, axis=-1, keepdims=True)
        hs.append((y * lax.rsqrt(ms + EPS) * g2_ref[...]).astype(BF16))
    for c0 in range(0, D_FF, ff_chunk):
        acts = []
        for h in hs:
            a = jnp.maximum(_dot(h, wu_ref[:, c0:c0 + ff_chunk]), 0.0)
            acts.append((a * a).astype(BF16))
        ys = [y + _dot(a, wd_ref[c0:c0 + ff_chunk, :]) for y, a in zip(ys, acts)]
    for rows, y in zip(subs, ys):
        ms = jnp.mean(y * y, axis=-1, keepdims=True)
        o_ref[rows, :] = y * lax.rsqrt(ms + EPS) * gf_ref[...]


def _params(n_axes):
    return pltpu.CompilerParams(dimension_semantics=("arbitrary",) * n_axes,
                                vmem_limit_bytes=VMEM_LIMIT)


def _const(shape):
    return pl.BlockSpec(shape, lambda *_: (0,) * len(shape), pipeline_mode=pl.Buffered(1))


def _rows(width, rows=TM):
    return pl.BlockSpec((rows, width), lambda i: (i, 0))


def _prep(wt):
    n, d = wt.shape
    assert n == _W_COLS + FOX_HEADS and _F0 % PREP_COLS == 0 and _W_COLS % PREP_COLS == 0
    per_tile = PREP_COLS // FOX_HEADS
    return pl.pallas_call(
        _prep_kernel,
        grid=(_W_COLS // PREP_COLS,),
        in_specs=[pl.BlockSpec((PREP_COLS, d), lambda t: (t, 0)),
                  pl.BlockSpec((FOX_HEADS, d), lambda t: ((t + 1) * per_tile, 0)),
                  pl.BlockSpec((FOX_HEADS, d), lambda t: (_F0 // FOX_HEADS, 0))],
        out_specs=[pl.BlockSpec((d, PREP_COLS), lambda t: (0, t)),
                   pl.BlockSpec((GATE_ROWS, d), lambda t: (0, 0))],
        out_shape=[jax.ShapeDtypeStruct((d, _W_COLS), BF16),
                   jax.ShapeDtypeStruct((GATE_ROWS, d), BF16)],
        compiler_params=_params(1),
        name="prep",
    )(wt, wt, wt)


def _pre(xt, bsz, seq, g1, w_all, w_ft, bf_pad, ln_g, ln_b, w_sgu, bm):
    n_tok, d = xt.shape
    tiles_per_seq = seq // TP
    head_map = lambda i: (i // tiles_per_seq, 0, i % tiles_per_seq, 0)
    return pl.pallas_call(
        functools.partial(_pre_kernel, tiles_per_seq=tiles_per_seq),
        grid=(n_tok // TP,),
        in_specs=[
            _rows(d, TP),
            _const((1, d)),
            _const((d, _W_COLS)),
            _const((GATE_ROWS, d)),
            _const((GATE_ROWS, SUB)),
            _const((1, SGU_WIDTH)),
            _const((1, SGU_WIDTH)),
            _const((SGU_GROUPS, SGU_LEN, SGU_LEN)),
            _const((SGU_LEN, SGU_WIDTH)),
        ],
        out_specs=[
            pl.BlockSpec((1, FOX_HEADS, TP, 2 * LANES), head_map),
            pl.BlockSpec((1, FOX_HEADS, TP // TQ, VT_ROWS, TQ), lambda i: head_map(i) + (0,)),
            _rows(SGU_WIDTH + 2 * d, TP),
        ],
        out_shape=[
            jax.ShapeDtypeStruct((bsz, FOX_HEADS, seq, 2 * LANES), BF16),
            jax.ShapeDtypeStruct((bsz, FOX_HEADS, seq // TQ, VT_ROWS, TQ), BF16),
            jax.ShapeDtypeStruct((n_tok, SGU_WIDTH + 2 * d), BF16),
        ],
        scratch_shapes=[pltpu.VMEM((FOX_HEADS, SUB), F32)],
        compiler_params=_params(1),
        name="pre",
    )(xt, g1, w_all, w_ft, bf_pad, ln_g, ln_b, w_sgu, bm)


def _attn(qk, vt, weights):
    bsz, _, seq, _ = qk.shape
    pairs = FOX_HEADS // 2
    n_steps = bsz * pairs
    assert all(w.shape[0] % (n_steps * 16) == 0 for w in weights)
    slab = lambda w: pl.BlockSpec((w.shape[0] // n_steps, w.shape[1]),
                                  lambda b, p: (b * pairs + p, 0))
    att, *cast = pl.pallas_call(
        functools.partial(_attn_kernel, n_q=seq // TQ, n_cast=len(weights)),
        grid=(bsz, pairs),
        in_specs=[
            pl.BlockSpec((1, 2, seq, 2 * LANES), lambda b, p: (b, p, 0, 0)),
            pl.BlockSpec((1, 2, seq // TQ, VT_ROWS, TQ), lambda b, p: (b, p, 0, 0, 0)),
        ] + [slab(w) for w in weights],
        out_specs=[pl.BlockSpec((1, seq, LANES), lambda b, p: (b, 0, p))]
        + [slab(w) for w in weights],
        out_shape=[jax.ShapeDtypeStruct((bsz, seq, FOX_WIDTH), BF16)]
        + [jax.ShapeDtypeStruct(w.shape, BF16) for w in weights],
        scratch_shapes=[pltpu.VMEM((4, VT_ROWS, QC), F32), pltpu.VMEM((4, TQ, QC), F32)],
        compiler_params=_params(2),
        name="attn",
    )(qk, vt, *weights)
    return att, cast


def _post(xt, att, sg, w_a, w_b, w_o, g2, w_up, w_down, gf):
    n_tok, d = xt.shape
    return pl.pallas_call(
        functools.partial(_post_kernel, ff_chunk=1024),
        grid=(n_tok // TM,),
        in_specs=[
            _rows(d),
            _rows(FOX_WIDTH),
            _rows(SGU_WIDTH + 2 * d),
            _const((FOX_WIDTH, d)),
            _const((SGU_WIDTH, d)),
            _const((d, d)),
            _const((1, d)),
            _const((d, D_FF)),
            _const((D_FF, d)),
            _const((1, d)),
        ],
        out_specs=_rows(d),
        out_shape=jax.ShapeDtypeStruct((n_tok, d), F32),
        compiler_params=_params(1),
        name="post",
    )(xt, att, sg, w_a, w_b, w_o, g2, w_up, w_down, gf)


def kernel(x, norm1_g, w_in, b_f, ln_v_g, ln_v_b, w_sgu, b_sgu, w_a, w_b, w_o,
           norm2_g, w_up, w_down, normf_g):
    bsz, seq, d = x.shape
    assert d == D_MODEL and seq % TP == 0 and TP % TQ == 0 and TQ % SUB == 0 and TM % SUB == 0
    assert norm1_g.shape[0] == 1, "single-layer block"
    n_tok = bsz * seq
    xt = x.reshape(n_tok, d)

    w_all, w_ft = _prep(jnp.swapaxes(w_in[0], 0, 1))
    bf_rows = jnp.pad(jnp.tile(b_f[0], N_PARTS), (0, GATE_ROWS - N_PARTS * FOX_HEADS))
    bf_pad = jnp.broadcast_to(bf_rows[:, None], (GATE_ROWS, SUB))
    bm = jnp.repeat(jnp.transpose(b_sgu[0]), HEAD_DIM, axis=1)

    qk, vt, sg = _pre(xt, bsz, seq, norm1_g.reshape(1, d), w_all, w_ft, bf_pad,
                              ln_v_g.reshape(1, SGU_WIDTH), ln_v_b.reshape(1, SGU_WIDTH),
                              w_sgu[0], bm)
    att, (wa16, wb16, wo16, wu16, wd16) = _attn(
        qk, vt, [w_a[0], w_b[0], w_o[0], w_up[0], w_down[0]])
    out = _post(xt, att.reshape(n_tok, FOX_WIDTH), sg, wa16, wb16, wo16,
                norm2_g.reshape(1, d), wu16, wd16, normf_g.reshape(1, d))
    return out.reshape(bsz, seq, d)
```

```python
import functools
import math

import jax
import jax.numpy as jnp
from jax import lax
from jax.experimental import pallas as pl
from jax.experimental.pallas import tpu as pltpu

D_MODEL = 1024
HEAD_DIM = 64
FOX_HEADS = 8
FOX_WIDTH = FOX_HEADS * HEAD_DIM
SGU_GROUPS = 8
SGU_WIDTH = 512
SGU_LEN = 128
CHUNK = 64
D_FF = 4 * D_MODEL
EPS = 1e-6

LANES = 128
TM = 512
TP = 512
TQ = 512
QC = 256
SUB = 256
VT_ROWS = 128
PREP_COLS = 1536
LOG2E = math.log2(math.e)
VMEM_LIMIT = 56 * 1024 * 1024

_F0 = 3 * FOX_WIDTH
_Q0, _K0, _V0 = 0, FOX_WIDTH, 2 * FOX_WIDTH
_U0 = _F0
_GA0 = _U0 + 2 * SGU_WIDTH
_GB0 = _GA0 + D_MODEL
_W_COLS = _GB0 + D_MODEL
N_PARTS = 3
GATE_ROWS = 32

BF16 = jnp.bfloat16
F32 = jnp.float32


def _dot(a, b):
    return jnp.dot(a, b, preferred_element_type=F32)


def _gelu_tanh(x):
    c = math.sqrt(2.0 / math.pi)
    half = 0.5 * x
    return half * jnp.tanh(x * ((x * x) * (c * 0.044715) + c)) + half


def _sigmoid(x):
    return 0.5 * jnp.tanh(0.5 * x) + 0.5


def _split3(x):
    hi = x.astype(BF16).astype(F32)
    r = x - hi
    mid = r.astype(BF16).astype(F32)
    return hi, mid, r - mid


def _prep_kernel(a_ref, b_ref, g_ref, w_ref, wft_ref):
    a = a_ref[...]
    past_gates = jnp.concatenate([a[FOX_HEADS:], b_ref[...]], axis=0)
    rows = jnp.where(pl.program_id(0) < _F0 // PREP_COLS, a, past_gates)
    w_ref[...] = rows.T.astype(BF16)
    g = g_ref[...]
    pad = jnp.zeros((GATE_ROWS - N_PARTS * FOX_HEADS, g.shape[1]), g.dtype)
    wft_ref[...] = jnp.concatenate([g] * N_PARTS + [pad], axis=0).astype(BF16)


def _pre_kernel(x_ref, g1_ref, w_ref, wft_ref, bf_ref, lng_ref, lnb_ref, wm_ref, bm_ref,
                qa_ref, ka_ref, vt_ref, sb_ref, gs_ref, carry_ref, *, tiles_per_seq):
    i = pl.program_id(0)

    @pl.when(i % tiles_per_seq == 0)
    def _():
        carry_ref[...] = jnp.zeros_like(carry_ref)

    s_i = lax.broadcasted_iota(jnp.int32, (SUB, SUB), 0)
    t_i = lax.broadcasted_iota(jnp.int32, (SUB, SUB), 1)
    tri = jnp.where(s_i <= t_i, 1.0, 0.0).astype(BF16)
    grow = lax.broadcasted_iota(jnp.int32, (GATE_ROWS, SUB), 0)
    head_pad = jnp.zeros((LANES - FOX_HEADS, SUB), F32)
    lane = lax.broadcasted_iota(jnp.int32, (SUB, LANES), 1)
    low = lane < HEAD_DIM
    ones_row = jnp.where(lax.broadcasted_iota(jnp.int32, (VT_ROWS - HEAD_DIM, SUB), 0) == 0,
                         1.0, 0.0)
    n_win = SUB // SGU_LEN
    wi = lax.broadcasted_iota(jnp.int32, (SGU_LEN, SGU_LEN), 0) // CHUNK
    wj = lax.broadcasted_iota(jnp.int32, (SGU_LEN, SGU_LEN), 1) // CHUNK
    wmask = wj <= wi
    lane_w = lax.broadcasted_iota(jnp.int32, (SGU_LEN, n_win * LANES), 1)
    low_w = (lane_w % LANES) < HEAD_DIM

    def proj(h, c0, width):
        return _dot(h, w_ref[:, c0:c0 + width])

    carry = carry_ref[...]
    for sub in range(TP // SUB):
        rows = slice(sub * SUB, (sub + 1) * SUB)
        kblk, kcols = divmod(sub * SUB, TQ)
        x = x_ref[rows, :]
        ms = jnp.mean(x * x, axis=-1, keepdims=True)
        h = (x * lax.rsqrt(ms + EPS) * g1_ref[...]).astype(BF16)

        z = lax.dot_general(wft_ref[...], h, (((1,), (1,)), ((), ())),
                            preferred_element_type=F32) + bf_ref[...]
        usv = proj(h, _U0, 2 * SGU_WIDTH)
        k_all = proj(h, _K0, FOX_WIDTH)

        logf = jnp.minimum(z, 0.0) - jnp.log(1.0 + jnp.exp(-jnp.abs(z)))
        hi, mid, lo = _split3(logf)
        part = jnp.where(grow < FOX_HEADS, hi, jnp.where(grow < 2 * FOX_HEADS, mid, lo))
        cl = _dot(part.astype(BF16), tri)
        ct = carry + (cl[:FOX_HEADS] + cl[FOX_HEADS:2 * FOX_HEADS]
                      + cl[2 * FOX_HEADS:3 * FOX_HEADS])
        carry = jnp.broadcast_to(ct[:, SUB - 1:SUB], ct.shape)
        c = jnp.concatenate([ct, head_pad], axis=0).T
        nc_parts = _split3(c * (-LOG2E))

        q_all = proj(h, _Q0, FOX_WIDTH) * (HEAD_DIM ** -0.5 * LOG2E)
        g_a = proj(h, _GA0, D_MODEL)
        v_all = proj(h, _V0, FOX_WIDTH)

        u = _gelu_tanh(usv[:, :SGU_WIDTH])
        sv = _gelu_tanh(usv[:, SGU_WIDTH:])
        mu = jnp.mean(sv, axis=-1, keepdims=True)
        xc = sv - mu
        var = jnp.mean(xc * xc, axis=-1, keepdims=True)
        svn = (xc * lax.rsqrt(var + EPS) * lng_ref[...] + lnb_ref[...]).astype(BF16)
        for jp in range(SGU_GROUPS // 2):
            sl = slice(LANES * jp, LANES * (jp + 1))
            chunk = svn[:, sl]
            rhs = jnp.concatenate(
                [chunk[SGU_LEN * w:SGU_LEN * (w + 1), :] for w in range(n_win)], axis=1)
            wa = jnp.where(wmask, wm_ref[2 * jp], 0.0).astype(BF16)
            wb = jnp.where(wmask, wm_ref[2 * jp + 1], 0.0).astype(BF16)
            zero = jnp.zeros_like(rhs)
            stacked = jnp.concatenate([jnp.where(low_w, rhs, zero), jnp.where(low_w, zero, rhs)],
                                      axis=0)
            mixed = _dot(jnp.concatenate([wa, wb], axis=1), stacked)
            mixed = mixed + jnp.concatenate([bm_ref[:, sl]] * n_win, axis=1)
            mixed = jnp.concatenate(
                [mixed[:, LANES * w:LANES * (w + 1)] for w in range(n_win)], axis=0)
            sb_ref[rows, sl] = (u[:, sl] * mixed).astype(BF16)
        gs_ref[rows, :D_MODEL] = g_a.astype(BF16)

        g_b = proj(h, _GB0, D_MODEL)
        for jp in range(FOX_HEADS // 2):
            sl = slice(LANES * jp, LANES * (jp + 1))
            qc, kc = q_all[:, sl], k_all[:, sl]
            vt = v_all[:, sl].T
            for par in range(2):
                hd = 2 * jp + par
                a0 = HEAD_DIM if par == 0 else 0
                data = low if par == 0 else jnp.logical_not(low)
                in_aug = (lane >= a0) & (lane < a0 + N_PARTS)
                qa = jnp.where(data, qc, jnp.where(in_aug, 1.0, 0.0))
                hi, mid, lo = [jnp.broadcast_to(p[:, hd:hd + 1], (SUB, LANES)) for p in nc_parts]
                aug = jnp.where(lane == a0, hi,
                                jnp.where(lane == a0 + 1, mid,
                                          jnp.where(lane == a0 + 2, lo, 0.0)))
                ka = jnp.where(data, kc, aug)
                qa_ref[0, hd, rows, :] = qa.astype(BF16)
                ka_ref[0, hd, rows, :] = ka.astype(BF16)
                vta = jnp.concatenate([vt[par * HEAD_DIM:(par + 1) * HEAD_DIM], ones_row], axis=0)
                vt_ref[0, hd, kblk, :, kcols:kcols + SUB] = vta.astype(BF16)
        gs_ref[rows, D_MODEL:] = g_b.astype(BF16)
    carry_ref[...] = carry


def _attn_kernel(qa_ref, ka_ref, vt_ref, *refs, n_q, n_cast):
    cast_in, (o_ref, *cast_out), (acc_ref, st_ref) = (
        refs[:n_cast], refs[n_cast:2 * n_cast + 1], refs[2 * n_cast + 1:])
    for src, dst in zip(cast_in, cast_out):
        dst[...] = src[...].astype(BF16)

    k_i = lax.broadcasted_iota(jnp.int32, (QC, QC), 0)
    q_i = lax.broadcasted_iota(jnp.int32, (QC, QC), 1)
    causal = k_i <= q_i
    halves = (slice(0, QC), slice(QC, TQ))

    def needed(c, tile, blk, half):
        return not (blk == tile and c == 0 and half == 1)

    def score_half(ch, tile, blk, half):
        hd, c = divmod(ch, 2)
        q0 = tile * TQ + c * QC
        k0 = blk * TQ + half * QC
        st = lax.dot_general(ka_ref[0, hd, pl.ds(k0, QC), :], qa_ref[0, hd, pl.ds(q0, QC), :],
                             (((1,), (1,)), ((), ())), preferred_element_type=F32)
        if blk == tile and half == c:
            st = jnp.where(causal, st, -jnp.inf)
        st_ref[ch, halves[half], :] = st
        return jnp.max(st, axis=0, keepdims=True)

    def chain_step(ch, tile, blk, m_old, cm, nxt):
        hd, c = divmod(ch, 2)
        m_new = cm if m_old is None else jnp.maximum(m_old, cm)
        pv, cm_next = [], []
        for half in range(2):
            use = needed(c, tile, blk, half)
            if use:
                p = jnp.exp2(st_ref[ch, halves[half], :] - m_new).astype(BF16)
            if nxt is not None and needed(c, *nxt, half):
                cm_next.append(score_half(ch, *nxt, half))
            if use:
                pv.append(_dot(vt_ref[0, hd, blk, :, halves[half]], p))
        pv = sum(pv[1:], pv[0])
        if m_old is None:
            acc_ref[ch] = pv
        else:
            acc_ref[ch] = acc_ref[ch] * jnp.exp2(m_old - m_new) + pv
        return m_new, (functools.reduce(jnp.maximum, cm_next) if cm_next else None)

    steps = [(qi, j) for qi in range(n_q) for j in range(qi + 1)]
    ms = [None] * 4
    cms = [functools.reduce(jnp.maximum, [score_half(ch, *steps[0], half) for half in range(2)
                                          if needed(ch % 2, *steps[0], half)])
           for ch in range(4)]
    for n, (qi, j) in enumerate(steps):
        nxt = steps[n + 1] if n + 1 < len(steps) else None
        for ch in range(4):
            ms[ch], cms[ch] = chain_step(ch, qi, j, ms[ch], cms[ch], nxt)
        if j == qi:
            for c in range(2):
                o_t = jnp.concatenate(
                    [acc_ref[2 * hd + c, :HEAD_DIM, :] / acc_ref[2 * hd + c, HEAD_DIM:HEAD_DIM + 1, :]
                     for hd in range(2)], axis=0)
                o_ref[0, pl.ds(qi * TQ + c * QC, QC), :] = o_t.T.astype(BF16)
            ms = [None] * 4


def _post_kernel(x_ref, at_ref, sb_ref, gs_ref, wa_ref, wb_ref, wo_ref, g2_ref, wu_ref,
                 wd_ref, gf_ref, o_ref, *, ff_chunk):
    subs = [slice(s * SUB, (s + 1) * SUB) for s in range(TM // SUB)]
    merged = []
    for rows in subs:
        ya = _dot(at_ref[rows, :], wa_ref[...])
        yb = _dot(sb_ref[rows, :], wb_ref[...])
        merged.append((_sigmoid(gs_ref[rows, :D_MODEL].astype(F32)) * ya
                       + _sigmoid(gs_ref[rows, D_MODEL:].astype(F32)) * yb).astype(BF16))
    ys = [x_ref[rows, :] + _dot(m, wo_ref[...]) for rows, m in zip(subs, merged)]
    hs = []
    for y in ys:
        ms = jnp.mean(y * y, axis=-1, keepdims=True)
        hs.append((y * lax.rsqrt(ms + EPS) * g2_ref[...]).astype(BF16))
    for c0 in range(0, D_FF, ff_chunk):
        acts = []
        for h in hs:
            a = jnp.maximum(_dot(h, wu_ref[:, c0:c0 + ff_chunk]), 0.0)
            acts.append((a * a).astype(BF16))
        ys = [y + _dot(a, wd_ref[c0:c0 + ff_chunk, :]) for y, a in zip(ys, acts)]
    for rows, y in zip(subs, ys):
        ms = jnp.mean(y * y, axis=-1, keepdims=True)
        o_ref[rows, :] = y * lax.rsqrt(ms + EPS) * gf_ref[...]


def _params(n_axes):
    return pltpu.CompilerParams(dimension_semantics=("arbitrary",) * n_axes,
                                vmem_limit_bytes=VMEM_LIMIT)


def _const(shape):
    return pl.BlockSpec(shape, lambda *_: (0,) * len(shape), pipeline_mode=pl.Buffered(1))


def _rows(width, rows=TM):
    return pl.BlockSpec((rows, width), lambda i: (i, 0))


def _prep(wt):
    n, d = wt.shape
    assert n == _W_COLS + FOX_HEADS and _F0 % PREP_COLS == 0 and _W_COLS % PREP_COLS == 0
    per_tile = PREP_COLS // FOX_HEADS
    return pl.pallas_call(
        _prep_kernel,
        grid=(_W_COLS // PREP_COLS,),
        in_specs=[pl.BlockSpec((PREP_COLS, d), lambda t: (t, 0)),
                  pl.BlockSpec((FOX_HEADS, d), lambda t: ((t + 1) * per_tile, 0)),
                  pl.BlockSpec((FOX_HEADS, d), lambda t: (_F0 // FOX_HEADS, 0))],
        out_specs=[pl.BlockSpec((d, PREP_COLS), lambda t: (0, t)),
                   pl.BlockSpec((GATE_ROWS, d), lambda t: (0, 0))],
        out_shape=[jax.ShapeDtypeStruct((d, _W_COLS), BF16),
                   jax.ShapeDtypeStruct((GATE_ROWS, d), BF16)],
        compiler_params=_params(1),
        name="prep",
    )(wt, wt, wt)


def _pre(xt, bsz, seq, g1, w_all, w_ft, bf_pad, ln_g, ln_b, w_sgu, bm):
    n_tok, d = xt.shape
    tiles_per_seq = seq // TP
    head_map = lambda i: (i // tiles_per_seq, 0, i % tiles_per_seq, 0)
    return pl.pallas_call(
        functools.partial(_pre_kernel, tiles_per_seq=tiles_per_seq),
        grid=(n_tok // TP,),
        in_specs=[
            _rows(d, TP),
            _const((1, d)),
            _const((d, _W_COLS)),
            _const((GATE_ROWS, d)),
            _const((GATE_ROWS, SUB)),
            _const((1, SGU_WIDTH)),
            _const((1, SGU_WIDTH)),
            _const((SGU_GROUPS, SGU_LEN, SGU_LEN)),
            _const((SGU_LEN, SGU_WIDTH)),
        ],
        out_specs=[
            pl.BlockSpec((1, FOX_HEADS, TP, LANES), head_map),
            pl.BlockSpec((1, FOX_HEADS, TP, LANES), head_map),
            pl.BlockSpec((1, FOX_HEADS, TP // TQ, VT_ROWS, TQ), lambda i: head_map(i) + (0,)),
            _rows(SGU_WIDTH, TP),
            _rows(2 * d, TP),
        ],
        out_shape=[
            jax.ShapeDtypeStruct((bsz, FOX_HEADS, seq, LANES), BF16),
            jax.ShapeDtypeStruct((bsz, FOX_HEADS, seq, LANES), BF16),
            jax.ShapeDtypeStruct((bsz, FOX_HEADS, seq // TQ, VT_ROWS, TQ), BF16),
            jax.ShapeDtypeStruct((n_tok, SGU_WIDTH), BF16),
            jax.ShapeDtypeStruct((n_tok, 2 * d), BF16),
        ],
        scratch_shapes=[pltpu.VMEM((FOX_HEADS, SUB), F32)],
        compiler_params=_params(1),
        name="pre",
    )(xt, g1, w_all, w_ft, bf_pad, ln_g, ln_b, w_sgu, bm)


def _attn(qa, ka, vt, weights):
    bsz, _, seq, _ = qa.shape
    pairs = FOX_HEADS // 2
    n_steps = bsz * pairs
    assert all(w.shape[0] % (n_steps * 16) == 0 for w in weights)
    slab = lambda w: pl.BlockSpec((w.shape[0] // n_steps, w.shape[1]),
                                  lambda b, p: (b * pairs + p, 0))
    att, *cast = pl.pallas_call(
        functools.partial(_attn_kernel, n_q=seq // TQ, n_cast=len(weights)),
        grid=(bsz, pairs),
        in_specs=[
            pl.BlockSpec((1, 2, seq, LANES), lambda b, p: (b, p, 0, 0)),
            pl.BlockSpec((1, 2, seq, LANES), lambda b, p: (b, p, 0, 0)),
            pl.BlockSpec((1, 2, seq // TQ, VT_ROWS, TQ), lambda b, p: (b, p, 0, 0, 0)),
        ] + [slab(w) for w in weights],
        out_specs=[pl.BlockSpec((1, seq, LANES), lambda b, p: (b, 0, p))]
        + [slab(w) for w in weights],
        out_shape=[jax.ShapeDtypeStruct((bsz, seq, FOX_WIDTH), BF16)]
        + [jax.ShapeDtypeStruct(w.shape, BF16) for w in weights],
        scratch_shapes=[pltpu.VMEM((4, VT_ROWS, QC), F32), pltpu.VMEM((4, TQ, QC), F32)],
        compiler_params=_params(2),
        name="attn",
    )(qa, ka, vt, *weights)
    return att, cast


def _post(xt, att, sb, gs, w_a, w_b, w_o, g2, w_up, w_down, gf):
    n_tok, d = xt.shape
    return pl.pallas_call(
        functools.partial(_post_kernel, ff_chunk=1024),
        grid=(n_tok // TM,),
        in_specs=[
            _rows(d),
            _rows(FOX_WIDTH),
            _rows(SGU_WIDTH),
            _rows(2 * d),
            _const((FOX_WIDTH, d)),
            _const((SGU_WIDTH, d)),
            _const((d, d)),
            _const((1, d)),
            _const((d, D_FF)),
            _const((D_FF, d)),
            _const((1, d)),
        ],
        out_specs=_rows(d),
        out_shape=jax.ShapeDtypeStruct((n_tok, d), F32),
        compiler_params=_params(1),
        name="post",
    )(xt, att, sb, gs, w_a, w_b, w_o, g2, w_up, w_down, gf)


def kernel(x, norm1_g, w_in, b_f, ln_v_g, ln_v_b, w_sgu, b_sgu, w_a, w_b, w_o,
           norm2_g, w_up, w_down, normf_g):
    bsz, seq, d = x.shape
    assert d == D_MODEL and seq % TP == 0 and TP % TQ == 0 and TQ % SUB == 0 and TM % SUB == 0
    assert norm1_g.shape[0] == 1, "single-layer block"
    n_tok = bsz * seq
    xt = x.reshape(n_tok, d)

    w_all, w_ft = _prep(jnp.swapaxes(w_in[0], 0, 1))
    bf_rows = jnp.pad(jnp.tile(b_f[0], N_PARTS), (0, GATE_ROWS - N_PARTS * FOX_HEADS))
    bf_pad = jnp.broadcast_to(bf_rows[:, None], (GATE_ROWS, SUB))
    bm = jnp.repeat(jnp.transpose(b_sgu[0]), HEAD_DIM, axis=1)

    qa, ka, vt, sb, gs = _pre(xt, bsz, seq, norm1_g.reshape(1, d), w_all, w_ft, bf_pad,
                              ln_v_g.reshape(1, SGU_WIDTH), ln_v_b.reshape(1, SGU_WIDTH),
                              w_sgu[0], bm)
    att, (wa16, wb16, wo16, wu16, wd16) = _attn(
        qa, ka, vt, [w_a[0], w_b[0], w_o[0], w_up[0], w_down[0]])
    out = _post(xt, att.reshape(n_tok, FOX_WIDTH), sb, gs, wa16, wb16, wo16,
                norm2_g.reshape(1, d), wu16, wd16, normf_g.reshape(1, d))
    return out.reshape(bsz, seq, d)
```

```python
import functools
import math

import jax
import jax.numpy as jnp
from jax import lax
from jax.experimental import pallas as pl
from jax.experimental.pallas import tpu as pltpu

D_MODEL = 1024
HEAD_DIM = 64
FOX_HEADS = 8
FOX_WIDTH = FOX_HEADS * HEAD_DIM
SGU_GROUPS = 8
SGU_WIDTH = 512
SGU_LEN = 128
CHUNK = 64
D_FF = 4 * D_MODEL
EPS = 1e-6

LANES = 128
TM = 512
TP = 512
TQ = 512
QC = 256
PAIR = 2
N_CHUNKS = TQ // QC
N_CHAINS = PAIR * N_CHUNKS
BF16_SUBLANES = 16
SUB = 256
VT_ROWS = 128
PREP_COLS = 1536
LOG2E = math.log2(math.e)
VMEM_LIMIT = 56 * 1024 * 1024

_F0 = 3 * FOX_WIDTH
_Q0, _K0, _V0 = 0, FOX_WIDTH, 2 * FOX_WIDTH
_U0 = _F0
_GA0 = _U0 + 2 * SGU_WIDTH
_GB0 = _GA0 + D_MODEL
_W_COLS = _GB0 + D_MODEL
N_PARTS = 3
GATE_ROWS = 32

BF16 = jnp.bfloat16
F32 = jnp.float32


def _dot(a, b):
    return jnp.dot(a, b, preferred_element_type=F32)


def _gelu_tanh(x):
    c = math.sqrt(2.0 / math.pi)
    half = 0.5 * x
    return half * jnp.tanh(x * ((x * x) * (c * 0.044715) + c)) + half


def _sigmoid(x):
    return 0.5 * jnp.tanh(0.5 * x) + 0.5


def _split3(x):
    hi = x.astype(BF16).astype(F32)
    r = x - hi
    mid = r.astype(BF16).astype(F32)
    return hi, mid, r - mid


def _prep_kernel(a_ref, b_ref, g_ref, w_ref, wft_ref):
    a = a_ref[...]
    past_gates = jnp.concatenate([a[FOX_HEADS:], b_ref[...]], axis=0)
    rows = jnp.where(pl.program_id(0) < _F0 // PREP_COLS, a, past_gates)
    w_ref[...] = rows.T.astype(BF16)
    g = g_ref[...]
    pad = jnp.zeros((GATE_ROWS - N_PARTS * FOX_HEADS, g.shape[1]), g.dtype)
    wft_ref[...] = jnp.concatenate([g] * N_PARTS + [pad], axis=0).astype(BF16)


def _pre_kernel(x_ref, g1_ref, w_ref, wft_ref, bf_ref, lng_ref, lnb_ref, wm_ref, bm_ref,
                qa_ref, ka_ref, vt_ref, sb_ref, gs_ref, carry_ref, *, tiles_per_seq):
    i = pl.program_id(0)

    @pl.when(i % tiles_per_seq == 0)
    def _():
        carry_ref[...] = jnp.zeros_like(carry_ref)

    s_i = lax.broadcasted_iota(jnp.int32, (SUB, SUB), 0)
    t_i = lax.broadcasted_iota(jnp.int32, (SUB, SUB), 1)
    tri = jnp.where(s_i <= t_i, 1.0, 0.0).astype(BF16)
    grow = lax.broadcasted_iota(jnp.int32, (GATE_ROWS, SUB), 0)
    head_pad = jnp.zeros((LANES - FOX_HEADS, SUB), F32)
    lane = lax.broadcasted_iota(jnp.int32, (SUB, LANES), 1)
    low = lane < HEAD_DIM
    ones_row = jnp.where(lax.broadcasted_iota(jnp.int32, (VT_ROWS - HEAD_DIM, SUB), 0) == 0,
                         1.0, 0.0)
    n_win = SUB // SGU_LEN
    wi = lax.broadcasted_iota(jnp.int32, (SGU_LEN, SGU_LEN), 0) // CHUNK
    wj = lax.broadcasted_iota(jnp.int32, (SGU_LEN, SGU_LEN), 1) // CHUNK
    wmask = wj <= wi
    lane_w = lax.broadcasted_iota(jnp.int32, (SGU_LEN, n_win * LANES), 1)
    low_w = (lane_w % LANES) < HEAD_DIM

    def proj(h, c0, width):
        return _dot(h, w_ref[:, c0:c0 + width])

    carry = carry_ref[...]
    for sub in range(TP // SUB):
        rows = slice(sub * SUB, (sub + 1) * SUB)
        kblk, kcols = divmod(sub * SUB, TQ)
        x = x_ref[rows, :]
        ms = jnp.mean(x * x, axis=-1, keepdims=True)
        h = (x * lax.rsqrt(ms + EPS) * g1_ref[...]).astype(BF16)

        z = lax.dot_general(wft_ref[...], h, (((1,), (1,)), ((), ())),
                            preferred_element_type=F32) + bf_ref[...]
        usv = proj(h, _U0, 2 * SGU_WIDTH)
        k_all = proj(h, _K0, FOX_WIDTH)

        logf = jnp.minimum(z, 0.0) - jnp.log(1.0 + jnp.exp(-jnp.abs(z)))
        hi, mid, lo = _split3(logf)
        part = jnp.where(grow < FOX_HEADS, hi, jnp.where(grow < 2 * FOX_HEADS, mid, lo))
        cl = _dot(part.astype(BF16), tri)
        ct = carry + (cl[:FOX_HEADS] + cl[FOX_HEADS:2 * FOX_HEADS]
                      + cl[2 * FOX_HEADS:3 * FOX_HEADS])
        carry = jnp.broadcast_to(ct[:, SUB - 1:SUB], ct.shape)
        c = jnp.concatenate([ct, head_pad], axis=0).T
        nc_parts = _split3(c * (-LOG2E))

        q_all = proj(h, _Q0, FOX_WIDTH) * (HEAD_DIM ** -0.5 * LOG2E)
        g_a = proj(h, _GA0, D_MODEL)
        v_all = proj(h, _V0, FOX_WIDTH)

        u = _gelu_tanh(usv[:, :SGU_WIDTH])
        sv = _gelu_tanh(usv[:, SGU_WIDTH:])
        mu = jnp.mean(sv, axis=-1, keepdims=True)
        xc = sv - mu
        var = jnp.mean(xc * xc, axis=-1, keepdims=True)
        svn = (xc * lax.rsqrt(var + EPS) * lng_ref[...] + lnb_ref[...]).astype(BF16)
        for jp in range(SGU_GROUPS // 2):
            sl = slice(LANES * jp, LANES * (jp + 1))
            chunk = svn[:, sl]
            rhs = jnp.concatenate(
                [chunk[SGU_LEN * w:SGU_LEN * (w + 1), :] for w in range(n_win)], axis=1)
            wa = jnp.where(wmask, wm_ref[2 * jp], 0.0).astype(BF16)
            wb = jnp.where(wmask, wm_ref[2 * jp + 1], 0.0).astype(BF16)
            zero = jnp.zeros_like(rhs)
            stacked = jnp.concatenate([jnp.where(low_w, rhs, zero), jnp.where(low_w, zero, rhs)],
                                      axis=0)
            mixed = _dot(jnp.concatenate([wa, wb], axis=1), stacked)
            mixed = mixed + jnp.concatenate([bm_ref[:, sl]] * n_win, axis=1)
            mixed = jnp.concatenate(
                [mixed[:, LANES * w:LANES * (w + 1)] for w in range(n_win)], axis=0)
            sb_ref[rows, sl] = (u[:, sl] * mixed).astype(BF16)
        gs_ref[rows, :D_MODEL] = g_a.astype(BF16)

        g_b = proj(h, _GB0, D_MODEL)
        for jp in range(FOX_HEADS // 2):
            sl = slice(LANES * jp, LANES * (jp + 1))
            qc, kc = q_all[:, sl], k_all[:, sl]
            vt = v_all[:, sl].T
            for par in range(2):
                hd = 2 * jp + par
                a0 = HEAD_DIM if par == 0 else 0
                data = low if par == 0 else jnp.logical_not(low)
                in_aug = (lane >= a0) & (lane < a0 + N_PARTS)
                qa = jnp.where(data, qc, jnp.where(in_aug, 1.0, 0.0))
                hi, mid, lo = [jnp.broadcast_to(p[:, hd:hd + 1], (SUB, LANES)) for p in nc_parts]
                aug = jnp.where(lane == a0, hi,
                                jnp.where(lane == a0 + 1, mid,
                                          jnp.where(lane == a0 + 2, lo, 0.0)))
                ka = jnp.where(data, kc, aug)
                qa_ref[0, hd, rows, :] = qa.astype(BF16)
                ka_ref[0, hd, rows, :] = ka.astype(BF16)
                vta = jnp.concatenate([vt[par * HEAD_DIM:(par + 1) * HEAD_DIM], ones_row], axis=0)
                vt_ref[0, hd, kblk, :, kcols:kcols + SUB] = vta.astype(BF16)
        gs_ref[rows, D_MODEL:] = g_b.astype(BF16)
    carry_ref[...] = carry


def _attn_kernel(qa_ref, ka_ref, vt_ref, *refs, n_q, n_cast):
    cast_in, (o_ref, *cast_out), (acc_ref, st_ref) = (
        refs[:n_cast], refs[n_cast:2 * n_cast + 1], refs[2 * n_cast + 1:])
    for src, dst in zip(cast_in, cast_out):
        dst[...] = src[...].astype(BF16)

    k_i = lax.broadcasted_iota(jnp.int32, (QC, QC), 0)
    q_i = lax.broadcasted_iota(jnp.int32, (QC, QC), 1)
    causal = k_i <= q_i
    halves = [slice(h * QC, (h + 1) * QC) for h in range(N_CHUNKS)]

    def needed(c, tile, blk, half):
        return not (blk == tile and half > c)

    def score_half(ch, tile, blk, half):
        hd, c = divmod(ch, N_CHUNKS)
        q0 = tile * TQ + c * QC
        k0 = blk * TQ + half * QC
        st = lax.dot_general(ka_ref[0, hd, pl.ds(k0, QC), :], qa_ref[0, hd, pl.ds(q0, QC), :],
                             (((1,), (1,)), ((), ())), preferred_element_type=F32)
        if blk == tile and half == c:
            st = jnp.where(causal, st, -jnp.inf)
        st_ref[ch, halves[half], :] = st
        return jnp.max(st, axis=0, keepdims=True)

    def chain_step(ch, tile, blk, m_old, cm, nxt):
        hd, c = divmod(ch, N_CHUNKS)
        m_new = cm if m_old is None else jnp.maximum(m_old, cm)
        pv, cm_next = [], []
        for half in range(N_CHUNKS):
            use = needed(c, tile, blk, half)
            if use:
                p = jnp.exp2(st_ref[ch, halves[half], :] - m_new).astype(BF16)
            if nxt is not None and needed(c, *nxt, half):
                cm_next.append(score_half(ch, *nxt, half))
            if use:
                pv.append(_dot(vt_ref[0, hd, blk, :, halves[half]], p))
        pv = sum(pv[1:], pv[0])
        if m_old is None:
            acc_ref[ch] = pv
        else:
            acc_ref[ch] = acc_ref[ch] * jnp.exp2(m_old - m_new) + pv
        return m_new, (functools.reduce(jnp.maximum, cm_next) if cm_next else None)

    steps = [(qi, j) for qi in range(n_q) for j in range(qi + 1)]
    ms = [None] * N_CHAINS
    cms = [functools.reduce(jnp.maximum,
                            [score_half(ch, *steps[0], half) for half in range(N_CHUNKS)
                             if needed(ch % N_CHUNKS, *steps[0], half)])
           for ch in range(N_CHAINS)]
    for n, (qi, j) in enumerate(steps):
        nxt = steps[n + 1] if n + 1 < len(steps) else None
        for ch in range(N_CHAINS):
            ms[ch], cms[ch] = chain_step(ch, qi, j, ms[ch], cms[ch], nxt)
        if j == qi:
            for c in range(N_CHUNKS):
                chains = [N_CHUNKS * hd + c for hd in range(PAIR)]
                o_t = jnp.concatenate(
                    [acc_ref[ch, :HEAD_DIM, :] / acc_ref[ch, HEAD_DIM:HEAD_DIM + 1, :]
                     for ch in chains], axis=0)
                o_ref[0, pl.ds(qi * TQ + c * QC, QC), :] = o_t.T.astype(BF16)
            ms = [None] * N_CHAINS


def _post_kernel(x_ref, at_ref, sb_ref, gs_ref, wa_ref, wb_ref, wo_ref, g2_ref, wu_ref,
                 wd_ref, gf_ref, o_ref, *, ff_chunk):
    subs = [slice(s * SUB, (s + 1) * SUB) for s in range(TM // SUB)]
    merged = []
    for rows in subs:
        ya = _dot(at_ref[rows, :], wa_ref[...])
        yb = _dot(sb_ref[rows, :], wb_ref[...])
        merged.append((_sigmoid(gs_ref[rows, :D_MODEL].astype(F32)) * ya
                       + _sigmoid(gs_ref[rows, D_MODEL:].astype(F32)) * yb).astype(BF16))
    ys = [x_ref[rows, :] + _dot(m, wo_ref[...]) for rows, m in zip(subs, merged)]
    hs = []
    for y in ys:
        ms = jnp.mean(y * y, axis=-1, keepdims=True)
        hs.append((y * lax.rsqrt(ms + EPS) * g2_ref[...]).astype(BF16))
    for c0 in range(0, D_FF, ff_chunk):
        acts = []
        for h in hs:
            a = jnp.maximum(_dot(h, wu_ref[:, c0:c0 + ff_chunk]), 0.0)
            acts.append((a * a).astype(BF16))
        ys = [y + _dot(a, wd_ref[c0:c0 + ff_chunk, :]) for y, a in zip(ys, acts)]
    for rows, y in zip(subs, ys):
        ms = jnp.mean(y * y, axis=-1, keepdims=True)
        o_ref[rows, :] = y * lax.rsqrt(ms + EPS) * gf_ref[...]


def _params(n_axes):
    return pltpu.CompilerParams(dimension_semantics=("arbitrary",) * n_axes,
                                vmem_limit_bytes=VMEM_LIMIT)


def _const(shape):
    return pl.BlockSpec(shape, lambda *_: (0,) * len(shape), pipeline_mode=pl.Buffered(1))


def _rows(width, rows=TM):
    return pl.BlockSpec((rows, width), lambda i: (i, 0))


def _prep(wt):
    n, d = wt.shape
    assert n == _W_COLS + FOX_HEADS and _F0 % PREP_COLS == 0 and _W_COLS % PREP_COLS == 0
    per_tile = PREP_COLS // FOX_HEADS
    return pl.pallas_call(
        _prep_kernel,
        grid=(_W_COLS // PREP_COLS,),
        in_specs=[pl.BlockSpec((PREP_COLS, d), lambda t: (t, 0)),
                  pl.BlockSpec((FOX_HEADS, d), lambda t: ((t + 1) * per_tile, 0)),
                  pl.BlockSpec((FOX_HEADS, d), lambda t: (_F0 // FOX_HEADS, 0))],
        out_specs=[pl.BlockSpec((d, PREP_COLS), lambda t: (0, t)),
                   pl.BlockSpec((GATE_ROWS, d), lambda t: (0, 0))],
        out_shape=[jax.ShapeDtypeStruct((d, _W_COLS), BF16),
                   jax.ShapeDtypeStruct((GATE_ROWS, d), BF16)],
        compiler_params=_params(1),
        name="prep",
    )(wt, wt, wt)


def _pre(xt, bsz, seq, g1, w_all, w_ft, bf_pad, ln_g, ln_b, w_sgu, bm):
    n_tok, d = xt.shape
    tiles_per_seq = seq // TP
    head_map = lambda i: (i // tiles_per_seq, 0, i % tiles_per_seq, 0)
    return pl.pallas_call(
        functools.partial(_pre_kernel, tiles_per_seq=tiles_per_seq),
        grid=(n_tok // TP,),
        in_specs=[
            _rows(d, TP),
            _const((1, d)),
            _const((d, _W_COLS)),
            _const((GATE_ROWS, d)),
            _const((GATE_ROWS, SUB)),
            _const((1, SGU_WIDTH)),
            _const((1, SGU_WIDTH)),
            _const((SGU_GROUPS, SGU_LEN, SGU_LEN)),
            _const((SGU_LEN, SGU_WIDTH)),
        ],
        out_specs=[
            pl.BlockSpec((1, FOX_HEADS, TP, LANES), head_map),
            pl.BlockSpec((1, FOX_HEADS, TP, LANES), head_map),
            pl.BlockSpec((1, FOX_HEADS, TP // TQ, VT_ROWS, TQ), lambda i: head_map(i) + (0,)),
            _rows(SGU_WIDTH, TP),
            _rows(2 * d, TP),
        ],
        out_shape=[
            jax.ShapeDtypeStruct((bsz, FOX_HEADS, seq, LANES), BF16),
            jax.ShapeDtypeStruct((bsz, FOX_HEADS, seq, LANES), BF16),
            jax.ShapeDtypeStruct((bsz, FOX_HEADS, seq // TQ, VT_ROWS, TQ), BF16),
            jax.ShapeDtypeStruct((n_tok, SGU_WIDTH), BF16),
            jax.ShapeDtypeStruct((n_tok, 2 * d), BF16),
        ],
        scratch_shapes=[pltpu.VMEM((FOX_HEADS, SUB), F32)],
        compiler_params=_params(1),
        name="pre",
    )(xt, g1, w_all, w_ft, bf_pad, ln_g, ln_b, w_sgu, bm)


def _attn(qa, ka, vt, weights):
    bsz, _, seq, _ = qa.shape
    pairs = FOX_HEADS // PAIR
    n_steps = bsz * pairs
    assert all(w.shape[0] % (n_steps * BF16_SUBLANES) == 0 for w in weights)
    slab = lambda w: pl.BlockSpec((w.shape[0] // n_steps, w.shape[1]),
                                  lambda b, p: (b * pairs + p, 0))
    att, *cast = pl.pallas_call(
        functools.partial(_attn_kernel, n_q=seq // TQ, n_cast=len(weights)),
        grid=(bsz, pairs),
        in_specs=[
            pl.BlockSpec((1, PAIR, seq, LANES), lambda b, p: (b, p, 0, 0)),
            pl.BlockSpec((1, PAIR, seq, LANES), lambda b, p: (b, p, 0, 0)),
            pl.BlockSpec((1, PAIR, seq // TQ, VT_ROWS, TQ), lambda b, p: (b, p, 0, 0, 0)),
        ] + [slab(w) for w in weights],
        out_specs=[pl.BlockSpec((1, seq, LANES), lambda b, p: (b, 0, p))]
        + [slab(w) for w in weights],
        out_shape=[jax.ShapeDtypeStruct((bsz, seq, FOX_WIDTH), BF16)]
        + [jax.ShapeDtypeStruct(w.shape, BF16) for w in weights],
        scratch_shapes=[pltpu.VMEM((N_CHAINS, VT_ROWS, QC), F32),
                        pltpu.VMEM((N_CHAINS, TQ, QC), F32)],
        compiler_params=_params(2),
        name="attn",
    )(qa, ka, vt, *weights)
    return att, cast


def _post(xt, att, sb, gs, w_a, w_b, w_o, g2, w_up, w_down, gf):
    n_tok, d = xt.shape
    return pl.pallas_call(
        functools.partial(_post_kernel, ff_chunk=1024),
        grid=(n_tok // TM,),
        in_specs=[
            _rows(d),
            _rows(FOX_WIDTH),
            _rows(SGU_WIDTH),
            _rows(2 * d),
            _const((FOX_WIDTH, d)),
            _const((SGU_WIDTH, d)),
            _const((d, d)),
            _const((1, d)),
            _const((d, D_FF)),
            _const((D_FF, d)),
            _const((1, d)),
        ],
        out_specs=_rows(d),
        out_shape=jax.ShapeDtypeStruct((n_tok, d), F32),
        compiler_params=_params(1),
        name="post",
    )(xt, att, sb, gs, w_a, w_b, w_o, g2, w_up, w_down, gf)


def kernel(x, norm1_g, w_in, b_f, ln_v_g, ln_v_b, w_sgu, b_sgu, w_a, w_b, w_o,
           norm2_g, w_up, w_down, normf_g):
    bsz, seq, d = x.shape
    assert d == D_MODEL and seq % TP == 0 and TP % TQ == 0 and TQ % SUB == 0 and TM % SUB == 0
    assert norm1_g.shape[0] == 1, "single-layer block"
    n_tok = bsz * seq
    xt = x.reshape(n_tok, d)

    w_all, w_ft = _prep(jnp.swapaxes(w_in[0], 0, 1))
    bf_rows = jnp.pad(jnp.tile(b_f[0], N_PARTS), (0, GATE_ROWS - N_PARTS * FOX_HEADS))
    bf_pad = jnp.broadcast_to(bf_rows[:, None], (GATE_ROWS, SUB))
    bm = jnp.repeat(jnp.transpose(b_sgu[0]), HEAD_DIM, axis=1)

    qa, ka, vt, sb, gs = _pre(xt, bsz, seq, norm1_g.reshape(1, d), w_all, w_ft, bf_pad,
                              ln_v_g.reshape(1, SGU_WIDTH), ln_v_b.reshape(1, SGU_WIDTH),
                              w_sgu[0], bm)
    att, (wa16, wb16, wo16, wu16, wd16) = _attn(
        qa, ka, vt, [w_a[0], w_b[0], w_o[0], w_up[0], w_down[0]])
    out = _post(xt, att.reshape(n_tok, FOX_WIDTH), sb, gs, wa16, wb16, wo16,
                norm2_g.reshape(1, d), wu16, wd16, normf_g.reshape(1, d))
    return out.reshape(bsz, seq, d)
```

```python
import functools
import math

import jax
import jax.numpy as jnp
from jax import lax
from jax.experimental import pallas as pl
from jax.experimental.pallas import tpu as pltpu

D_MODEL = 1024
HEAD_DIM = 64
FOX_HEADS = 8
FOX_WIDTH = FOX_HEADS * HEAD_DIM
SGU_GROUPS = 8
SGU_WIDTH = 512
SGU_LEN = 128
CHUNK = 64
D_FF = 4 * D_MODEL
EPS = 1e-6

LANES = 128
TM = 512
TP = 512
TQ = 512
QC = 512
PAIR = 2
N_CHUNKS = TQ // QC
N_CHAINS = PAIR * N_CHUNKS
BF16_SUBLANES = 16
SUB = 256
VT_ROWS = 128
PREP_COLS = 1536
LOG2E = math.log2(math.e)
VMEM_LIMIT = 56 * 1024 * 1024

_F0 = 3 * FOX_WIDTH
_Q0, _K0, _V0 = 0, FOX_WIDTH, 2 * FOX_WIDTH
_U0 = _F0
_GA0 = _U0 + 2 * SGU_WIDTH
_GB0 = _GA0 + D_MODEL
_W_COLS = _GB0 + D_MODEL
N_PARTS = 3
GATE_ROWS = 32

BF16 = jnp.bfloat16
F32 = jnp.float32


def _dot(a, b):
    return jnp.dot(a, b, preferred_element_type=F32)


def _gelu_tanh(x):
    c = math.sqrt(2.0 / math.pi)
    half = 0.5 * x
    return half * jnp.tanh(x * ((x * x) * (c * 0.044715) + c)) + half


def _sigmoid(x):
    return 0.5 * jnp.tanh(0.5 * x) + 0.5


def _split3(x):
    hi = x.astype(BF16).astype(F32)
    r = x - hi
    mid = r.astype(BF16).astype(F32)
    return hi, mid, r - mid


def _prep_kernel(a_ref, b_ref, g_ref, w_ref, wft_ref):
    a = a_ref[...]
    past_gates = jnp.concatenate([a[FOX_HEADS:], b_ref[...]], axis=0)
    rows = jnp.where(pl.program_id(0) < _F0 // PREP_COLS, a, past_gates)
    w_ref[...] = rows.T.astype(BF16)
    g = g_ref[...]
    pad = jnp.zeros((GATE_ROWS - N_PARTS * FOX_HEADS, g.shape[1]), g.dtype)
    wft_ref[...] = jnp.concatenate([g] * N_PARTS + [pad], axis=0).astype(BF16)


def _pre_kernel(x_ref, g1_ref, w_ref, wft_ref, bf_ref, lng_ref, lnb_ref, wm_ref, bm_ref,
                qa_ref, ka_ref, vt_ref, sb_ref, gs_ref, carry_ref, *, tiles_per_seq):
    i = pl.program_id(0)

    @pl.when(i % tiles_per_seq == 0)
    def _():
        carry_ref[...] = jnp.zeros_like(carry_ref)

    s_i = lax.broadcasted_iota(jnp.int32, (SUB, SUB), 0)
    t_i = lax.broadcasted_iota(jnp.int32, (SUB, SUB), 1)
    tri = jnp.where(s_i <= t_i, 1.0, 0.0).astype(BF16)
    grow = lax.broadcasted_iota(jnp.int32, (GATE_ROWS, SUB), 0)
    head_pad = jnp.zeros((LANES - FOX_HEADS, SUB), F32)
    lane = lax.broadcasted_iota(jnp.int32, (SUB, LANES), 1)
    low = lane < HEAD_DIM
    ones_row = jnp.where(lax.broadcasted_iota(jnp.int32, (VT_ROWS - HEAD_DIM, SUB), 0) == 0,
                         1.0, 0.0)
    n_win = SUB // SGU_LEN
    wi = lax.broadcasted_iota(jnp.int32, (SGU_LEN, SGU_LEN), 0) // CHUNK
    wj = lax.broadcasted_iota(jnp.int32, (SGU_LEN, SGU_LEN), 1) // CHUNK
    wmask = wj <= wi
    lane_w = lax.broadcasted_iota(jnp.int32, (SGU_LEN, n_win * LANES), 1)
    low_w = (lane_w % LANES) < HEAD_DIM

    def proj(h, c0, width):
        return _dot(h, w_ref[:, c0:c0 + width])

    carry = carry_ref[...]
    for sub in range(TP // SUB):
        rows = slice(sub * SUB, (sub + 1) * SUB)
        kblk, kcols = divmod(sub * SUB, TQ)
        x = x_ref[rows, :]
        ms = jnp.mean(x * x, axis=-1, keepdims=True)
        h = (x * lax.rsqrt(ms + EPS) * g1_ref[...]).astype(BF16)

        z = lax.dot_general(wft_ref[...], h, (((1,), (1,)), ((), ())),
                            preferred_element_type=F32) + bf_ref[...]
        usv = proj(h, _U0, 2 * SGU_WIDTH)
        k_all = proj(h, _K0, FOX_WIDTH)

        logf = jnp.minimum(z, 0.0) - jnp.log(1.0 + jnp.exp(-jnp.abs(z)))
        hi, mid, lo = _split3(logf)
        part = jnp.where(grow < FOX_HEADS, hi, jnp.where(grow < 2 * FOX_HEADS, mid, lo))
        cl = _dot(part.astype(BF16), tri)
        ct = carry + (cl[:FOX_HEADS] + cl[FOX_HEADS:2 * FOX_HEADS]
                      + cl[2 * FOX_HEADS:3 * FOX_HEADS])
        carry = jnp.broadcast_to(ct[:, SUB - 1:SUB], ct.shape)
        c = jnp.concatenate([ct, head_pad], axis=0).T
        nc_parts = _split3(c * (-LOG2E))

        q_all = proj(h, _Q0, FOX_WIDTH) * (HEAD_DIM ** -0.5 * LOG2E)
        g_a = proj(h, _GA0, D_MODEL)
        v_all = proj(h, _V0, FOX_WIDTH)

        u = _gelu_tanh(usv[:, :SGU_WIDTH])
        sv = _gelu_tanh(usv[:, SGU_WIDTH:])
        mu = jnp.mean(sv, axis=-1, keepdims=True)
        xc = sv - mu
        var = jnp.mean(xc * xc, axis=-1, keepdims=True)
        svn = (xc * lax.rsqrt(var + EPS) * lng_ref[...] + lnb_ref[...]).astype(BF16)
        for jp in range(SGU_GROUPS // 2):
            sl = slice(LANES * jp, LANES * (jp + 1))
            chunk = svn[:, sl]
            rhs = jnp.concatenate(
                [chunk[SGU_LEN * w:SGU_LEN * (w + 1), :] for w in range(n_win)], axis=1)
            wa = jnp.where(wmask, wm_ref[2 * jp], 0.0).astype(BF16)
            wb = jnp.where(wmask, wm_ref[2 * jp + 1], 0.0).astype(BF16)
            zero = jnp.zeros_like(rhs)
            stacked = jnp.concatenate([jnp.where(low_w, rhs, zero), jnp.where(low_w, zero, rhs)],
                                      axis=0)
            mixed = _dot(jnp.concatenate([wa, wb], axis=1), stacked)
            mixed = mixed + jnp.concatenate([bm_ref[:, sl]] * n_win, axis=1)
            mixed = jnp.concatenate(
                [mixed[:, LANES * w:LANES * (w + 1)] for w in range(n_win)], axis=0)
            sb_ref[rows, sl] = (u[:, sl] * mixed).astype(BF16)
        gs_ref[rows, :D_MODEL] = g_a.astype(BF16)

        g_b = proj(h, _GB0, D_MODEL)
        for jp in range(FOX_HEADS // 2):
            sl = slice(LANES * jp, LANES * (jp + 1))
            qc, kc = q_all[:, sl], k_all[:, sl]
            vt = v_all[:, sl].T
            for par in range(2):
                hd = 2 * jp + par
                a0 = HEAD_DIM if par == 0 else 0
                data = low if par == 0 else jnp.logical_not(low)
                in_aug = (lane >= a0) & (lane < a0 + N_PARTS)
                qa = jnp.where(data, qc, jnp.where(in_aug, 1.0, 0.0))
                hi, mid, lo = [jnp.broadcast_to(p[:, hd:hd + 1], (SUB, LANES)) for p in nc_parts]
                aug = jnp.where(lane == a0, hi,
                                jnp.where(lane == a0 + 1, mid,
                                          jnp.where(lane == a0 + 2, lo, 0.0)))
                ka = jnp.where(data, kc, aug)
                qa_ref[0, hd, rows, :] = qa.astype(BF16)
                ka_ref[0, hd, rows, :] = ka.astype(BF16)
                vta = jnp.concatenate([vt[par * HEAD_DIM:(par + 1) * HEAD_DIM], ones_row], axis=0)
                vt_ref[0, hd, kblk, :, kcols:kcols + SUB] = vta.astype(BF16)
        gs_ref[rows, D_MODEL:] = g_b.astype(BF16)
    carry_ref[...] = carry


def _attn_kernel(qa_ref, ka_ref, vt_ref, *refs, n_q, n_cast):
    cast_in, (o_ref, *cast_out), (acc_ref, st_ref) = (
        refs[:n_cast], refs[n_cast:2 * n_cast + 1], refs[2 * n_cast + 1:])
    for src, dst in zip(cast_in, cast_out):
        dst[...] = src[...].astype(BF16)

    k_i = lax.broadcasted_iota(jnp.int32, (QC, QC), 0)
    q_i = lax.broadcasted_iota(jnp.int32, (QC, QC), 1)
    causal = k_i <= q_i
    halves = [slice(h * QC, (h + 1) * QC) for h in range(N_CHUNKS)]

    def needed(c, tile, blk, half):
        return not (blk == tile and half > c)

    def score_half(ch, tile, blk, half):
        hd, c = divmod(ch, N_CHUNKS)
        q0 = tile * TQ + c * QC
        k0 = blk * TQ + half * QC
        st = lax.dot_general(ka_ref[0, hd, pl.ds(k0, QC), :], qa_ref[0, hd, pl.ds(q0, QC), :],
                             (((1,), (1,)), ((), ())), preferred_element_type=F32)
        if blk == tile and half == c:
            st = jnp.where(causal, st, -jnp.inf)
        st_ref[ch, halves[half], :] = st
        return jnp.max(st, axis=0, keepdims=True)

    def chain_step(ch, tile, blk, m_old, cm, nxt):
        hd, c = divmod(ch, N_CHUNKS)
        m_new = cm if m_old is None else jnp.maximum(m_old, cm)
        pv, cm_next = [], []
        for half in range(N_CHUNKS):
            use = needed(c, tile, blk, half)
            if use:
                p = jnp.exp2(st_ref[ch, halves[half], :] - m_new).astype(BF16)
            if nxt is not None and needed(c, *nxt, half):
                cm_next.append(score_half(ch, *nxt, half))
            if use:
                pv.append(_dot(vt_ref[0, hd, blk, :, halves[half]], p))
        pv = sum(pv[1:], pv[0])
        if m_old is None:
            acc_ref[ch] = pv
        else:
            acc_ref[ch] = acc_ref[ch] * jnp.exp2(m_old - m_new) + pv
        return m_new, (functools.reduce(jnp.maximum, cm_next) if cm_next else None)

    steps = [(qi, j) for qi in range(n_q) for j in range(qi + 1)]
    ms = [None] * N_CHAINS
    cms = [functools.reduce(jnp.maximum,
                            [score_half(ch, *steps[0], half) for half in range(N_CHUNKS)
                             if needed(ch % N_CHUNKS, *steps[0], half)])
           for ch in range(N_CHAINS)]
    for n, (qi, j) in enumerate(steps):
        nxt = steps[n + 1] if n + 1 < len(steps) else None
        for ch in range(N_CHAINS):
            ms[ch], cms[ch] = chain_step(ch, qi, j, ms[ch], cms[ch], nxt)
        if j == qi:
            for c in range(N_CHUNKS):
                chains = [N_CHUNKS * hd + c for hd in range(PAIR)]
                o_t = jnp.concatenate(
                    [acc_ref[ch, :HEAD_DIM, :] / acc_ref[ch, HEAD_DIM:HEAD_DIM + 1, :]
                     for ch in chains], axis=0)
                o_ref[0, pl.ds(qi * TQ + c * QC, QC), :] = o_t.T.astype(BF16)
            ms = [None] * N_CHAINS


def _post_kernel(x_ref, at_ref, sb_ref, gs_ref, wa_ref, wb_ref, wo_ref, g2_ref, wu_ref,
                 wd_ref, gf_ref, o_ref, *, ff_chunk):
    subs = [slice(s * SUB, (s + 1) * SUB) for s in range(TM // SUB)]
    merged = []
    for rows in subs:
        ya = _dot(at_ref[rows, :], wa_ref[...])
        yb = _dot(sb_ref[rows, :], wb_ref[...])
        merged.append((_sigmoid(gs_ref[rows, :D_MODEL].astype(F32)) * ya
                       + _sigmoid(gs_ref[rows, D_MODEL:].astype(F32)) * yb).astype(BF16))
    ys = [x_ref[rows, :] + _dot(m, wo_ref[...]) for rows, m in zip(subs, merged)]
    hs = []
    for y in ys:
        ms = jnp.mean(y * y, axis=-1, keepdims=True)
        hs.append((y * lax.rsqrt(ms + EPS) * g2_ref[...]).astype(BF16))
    for c0 in range(0, D_FF, ff_chunk):
        acts = []
        for h in hs:
            a = jnp.maximum(_dot(h, wu_ref[:, c0:c0 + ff_chunk]), 0.0)
            acts.append((a * a).astype(BF16))
        ys = [y + _dot(a, wd_ref[c0:c0 + ff_chunk, :]) for y, a in zip(ys, acts)]
    for rows, y in zip(subs, ys):
        ms = jnp.mean(y * y, axis=-1, keepdims=True)
        o_ref[rows, :] = y * lax.rsqrt(ms + EPS) * gf_ref[...]


def _params(n_axes):
    return pltpu.CompilerParams(dimension_semantics=("arbitrary",) * n_axes,
                                vmem_limit_bytes=VMEM_LIMIT)


def _const(shape):
    return pl.BlockSpec(shape, lambda *_: (0,) * len(shape), pipeline_mode=pl.Buffered(1))


def _rows(width, rows=TM):
    return pl.BlockSpec((rows, width), lambda i: (i, 0))


def _prep(wt):
    n, d = wt.shape
    assert n == _W_COLS + FOX_HEADS and _F0 % PREP_COLS == 0 and _W_COLS % PREP_COLS == 0
    per_tile = PREP_COLS // FOX_HEADS
    return pl.pallas_call(
        _prep_kernel,
        grid=(_W_COLS // PREP_COLS,),
        in_specs=[pl.BlockSpec((PREP_COLS, d), lambda t: (t, 0)),
                  pl.BlockSpec((FOX_HEADS, d), lambda t: ((t + 1) * per_tile, 0)),
                  pl.BlockSpec((FOX_HEADS, d), lambda t: (_F0 // FOX_HEADS, 0))],
        out_specs=[pl.BlockSpec((d, PREP_COLS), lambda t: (0, t)),
                   pl.BlockSpec((GATE_ROWS, d), lambda t: (0, 0))],
        out_shape=[jax.ShapeDtypeStruct((d, _W_COLS), BF16),
                   jax.ShapeDtypeStruct((GATE_ROWS, d), BF16)],
        compiler_params=_params(1),
        name="prep",
    )(wt, wt, wt)


def _pre(xt, bsz, seq, g1, w_all, w_ft, bf_pad, ln_g, ln_b, w_sgu, bm):
    n_tok, d = xt.shape
    tiles_per_seq = seq // TP
    head_map = lambda i: (i // tiles_per_seq, 0, i % tiles_per_seq, 0)
    return pl.pallas_call(
        functools.partial(_pre_kernel, tiles_per_seq=tiles_per_seq),
        grid=(n_tok // TP,),
        in_specs=[
            _rows(d, TP),
            _const((1, d)),
            _const((d, _W_COLS)),
            _const((GATE_ROWS, d)),
            _const((GATE_ROWS, SUB)),
            _const((1, SGU_WIDTH)),
            _const((1, SGU_WIDTH)),
            _const((SGU_GROUPS, SGU_LEN, SGU_LEN)),
            _const((SGU_LEN, SGU_WIDTH)),
        ],
        out_specs=[
            pl.BlockSpec((1, FOX_HEADS, TP, LANES), head_map),
            pl.BlockSpec((1, FOX_HEADS, TP, LANES), head_map),
            pl.BlockSpec((1, FOX_HEADS, TP // TQ, VT_ROWS, TQ), lambda i: head_map(i) + (0,)),
            _rows(SGU_WIDTH, TP),
            _rows(2 * d, TP),
        ],
        out_shape=[
            jax.ShapeDtypeStruct((bsz, FOX_HEADS, seq, LANES), BF16),
            jax.ShapeDtypeStruct((bsz, FOX_HEADS, seq, LANES), BF16),
            jax.ShapeDtypeStruct((bsz, FOX_HEADS, seq // TQ, VT_ROWS, TQ), BF16),
            jax.ShapeDtypeStruct((n_tok, SGU_WIDTH), BF16),
            jax.ShapeDtypeStruct((n_tok, 2 * d), BF16),
        ],
        scratch_shapes=[pltpu.VMEM((FOX_HEADS, SUB), F32)],
        compiler_params=_params(1),
        name="pre",
    )(xt, g1, w_all, w_ft, bf_pad, ln_g, ln_b, w_sgu, bm)


def _attn(qa, ka, vt, weights):
    bsz, _, seq, _ = qa.shape
    pairs = FOX_HEADS // PAIR
    n_steps = bsz * pairs
    assert all(w.shape[0] % (n_steps * BF16_SUBLANES) == 0 for w in weights)
    slab = lambda w: pl.BlockSpec((w.shape[0] // n_steps, w.shape[1]),
                                  lambda b, p: (b * pairs + p, 0))
    att, *cast = pl.pallas_call(
        functools.partial(_attn_kernel, n_q=seq // TQ, n_cast=len(weights)),
        grid=(bsz, pairs),
        in_specs=[
            pl.BlockSpec((1, PAIR, seq, LANES), lambda b, p: (b, p, 0, 0)),
            pl.BlockSpec((1, PAIR, seq, LANES), lambda b, p: (b, p, 0, 0)),
            pl.BlockSpec((1, PAIR, seq // TQ, VT_ROWS, TQ), lambda b, p: (b, p, 0, 0, 0)),
        ] + [slab(w) for w in weights],
        out_specs=[pl.BlockSpec((1, seq, LANES), lambda b, p: (b, 0, p))]
        + [slab(w) for w in weights],
        out_shape=[jax.ShapeDtypeStruct((bsz, seq, FOX_WIDTH), BF16)]
        + [jax.ShapeDtypeStruct(w.shape, BF16) for w in weights],
        scratch_shapes=[pltpu.VMEM((N_CHAINS, VT_ROWS, QC), F32),
                        pltpu.VMEM((N_CHAINS, TQ, QC), F32)],
        compiler_params=_params(2),
        name="attn",
    )(qa, ka, vt, *weights)
    return att, cast


def _post(xt, att, sb, gs, w_a, w_b, w_o, g2, w_up, w_down, gf):
    n_tok, d = xt.shape
    return pl.pallas_call(
        functools.partial(_post_kernel, ff_chunk=1024),
        grid=(n_tok // TM,),
        in_specs=[
            _rows(d),
            _rows(FOX_WIDTH),
            _rows(SGU_WIDTH),
            _rows(2 * d),
            _const((FOX_WIDTH, d)),
            _const((SGU_WIDTH, d)),
            _const((d, d)),
            _const((1, d)),
            _const((d, D_FF)),
            _const((D_FF, d)),
            _const((1, d)),
        ],
        out_specs=_rows(d),
        out_shape=jax.ShapeDtypeStruct((n_tok, d), F32),
        compiler_params=_params(1),
        name="post",
    )(xt, att, sb, gs, w_a, w_b, w_o, g2, w_up, w_down, gf)


def kernel(x, norm1_g, w_in, b_f, ln_v_g, ln_v_b, w_sgu, b_sgu, w_a, w_b, w_o,
           norm2_g, w_up, w_down, normf_g):
    bsz, seq, d = x.shape
    assert d == D_MODEL and seq % TP == 0 and TP % TQ == 0 and TQ % SUB == 0 and TM % SUB == 0
    assert norm1_g.shape[0] == 1, "single-layer block"
    n_tok = bsz * seq
    xt = x.reshape(n_tok, d)

    w_all, w_ft = _prep(jnp.swapaxes(w_in[0], 0, 1))
    bf_rows = jnp.pad(jnp.tile(b_f[0], N_PARTS), (0, GATE_ROWS - N_PARTS * FOX_HEADS))
    bf_pad = jnp.broadcast_to(bf_rows[:, None], (GATE_ROWS, SUB))
    bm = jnp.repeat(jnp.transpose(b_sgu[0]), HEAD_DIM, axis=1)

    qa, ka, vt, sb, gs = _pre(xt, bsz, seq, norm1_g.reshape(1, d), w_all, w_ft, bf_pad,
                              ln_v_g.reshape(1, SGU_WIDTH), ln_v_b.reshape(1, SGU_WIDTH),
                              w_sgu[0], bm)
    att, (wa16, wb16, wo16, wu16, wd16) = _attn(
        qa, ka, vt, [w_a[0], w_b[0], w_o[0], w_up[0], w_down[0]])
    out = _post(xt, att.reshape(n_tok, FOX_WIDTH), sb, gs, wa16, wb16, wo16,
                norm2_g.reshape(1, d), wu16, wd16, normf_g.reshape(1, d))
    return out.reshape(bsz, seq, d)
```

```python
import functools
import math

import jax
import jax.numpy as jnp
from jax import lax
from jax.experimental import pallas as pl
from jax.experimental.pallas import tpu as pltpu

D_MODEL = 1024
HEAD_DIM = 64
FOX_HEADS = 8
FOX_WIDTH = FOX_HEADS * HEAD_DIM
SGU_GROUPS = 8
SGU_WIDTH = 512
SGU_LEN = 128
CHUNK = 64
D_FF = 4 * D_MODEL
EPS = 1e-6

LANES = 128
TM = 512
TP = 512
TQ = 512
QC = 256
PAIR = 4
N_CHUNKS = TQ // QC
N_CHAINS = PAIR * N_CHUNKS
BF16_SUBLANES = 16
SUB = 256
VT_ROWS = 128
PREP_COLS = 1536
LOG2E = math.log2(math.e)
VMEM_LIMIT = 56 * 1024 * 1024

_F0 = 3 * FOX_WIDTH
_Q0, _K0, _V0 = 0, FOX_WIDTH, 2 * FOX_WIDTH
_U0 = _F0
_GA0 = _U0 + 2 * SGU_WIDTH
_GB0 = _GA0 + D_MODEL
_W_COLS = _GB0 + D_MODEL
N_PARTS = 3
GATE_ROWS = 32

BF16 = jnp.bfloat16
F32 = jnp.float32


def _dot(a, b):
    return jnp.dot(a, b, preferred_element_type=F32)


def _gelu_tanh(x):
    c = math.sqrt(2.0 / math.pi)
    half = 0.5 * x
    return half * jnp.tanh(x * ((x * x) * (c * 0.044715) + c)) + half


def _sigmoid(x):
    return 0.5 * jnp.tanh(0.5 * x) + 0.5


def _split3(x):
    hi = x.astype(BF16).astype(F32)
    r = x - hi
    mid = r.astype(BF16).astype(F32)
    return hi, mid, r - mid


def _prep_kernel(a_ref, b_ref, g_ref, w_ref, wft_ref):
    a = a_ref[...]
    past_gates = jnp.concatenate([a[FOX_HEADS:], b_ref[...]], axis=0)
    rows = jnp.where(pl.program_id(0) < _F0 // PREP_COLS, a, past_gates)
    w_ref[...] = rows.T.astype(BF16)
    g = g_ref[...]
    pad = jnp.zeros((GATE_ROWS - N_PARTS * FOX_HEADS, g.shape[1]), g.dtype)
    wft_ref[...] = jnp.concatenate([g] * N_PARTS + [pad], axis=0).astype(BF16)


def _pre_kernel(x_ref, g1_ref, w_ref, wft_ref, bf_ref, lng_ref, lnb_ref, wm_ref, bm_ref,
                qa_ref, ka_ref, vt_ref, sb_ref, gs_ref, carry_ref, *, tiles_per_seq):
    i = pl.program_id(0)

    @pl.when(i % tiles_per_seq == 0)
    def _():
        carry_ref[...] = jnp.zeros_like(carry_ref)

    s_i = lax.broadcasted_iota(jnp.int32, (SUB, SUB), 0)
    t_i = lax.broadcasted_iota(jnp.int32, (SUB, SUB), 1)
    tri = jnp.where(s_i <= t_i, 1.0, 0.0).astype(BF16)
    grow = lax.broadcasted_iota(jnp.int32, (GATE_ROWS, SUB), 0)
    head_pad = jnp.zeros((LANES - FOX_HEADS, SUB), F32)
    lane = lax.broadcasted_iota(jnp.int32, (SUB, LANES), 1)
    low = lane < HEAD_DIM
    ones_row = jnp.where(lax.broadcasted_iota(jnp.int32, (VT_ROWS - HEAD_DIM, SUB), 0) == 0,
                         1.0, 0.0)
    n_win = SUB // SGU_LEN
    wi = lax.broadcasted_iota(jnp.int32, (SGU_LEN, SGU_LEN), 0) // CHUNK
    wj = lax.broadcasted_iota(jnp.int32, (SGU_LEN, SGU_LEN), 1) // CHUNK
    wmask = wj <= wi
    lane_w = lax.broadcasted_iota(jnp.int32, (SGU_LEN, n_win * LANES), 1)
    low_w = (lane_w % LANES) < HEAD_DIM

    def proj(h, c0, width):
        return _dot(h, w_ref[:, c0:c0 + width])

    carry = carry_ref[...]
    for sub in range(TP // SUB):
        rows = slice(sub * SUB, (sub + 1) * SUB)
        kblk, kcols = divmod(sub * SUB, TQ)
        x = x_ref[rows, :]
        ms = jnp.mean(x * x, axis=-1, keepdims=True)
        h = (x * lax.rsqrt(ms + EPS) * g1_ref[...]).astype(BF16)

        z = lax.dot_general(wft_ref[...], h, (((1,), (1,)), ((), ())),
                            preferred_element_type=F32) + bf_ref[...]
        usv = proj(h, _U0, 2 * SGU_WIDTH)
        k_all = proj(h, _K0, FOX_WIDTH)

        logf = jnp.minimum(z, 0.0) - jnp.log(1.0 + jnp.exp(-jnp.abs(z)))
        hi, mid, lo = _split3(logf)
        part = jnp.where(grow < FOX_HEADS, hi, jnp.where(grow < 2 * FOX_HEADS, mid, lo))
        cl = _dot(part.astype(BF16), tri)
        ct = carry + (cl[:FOX_HEADS] + cl[FOX_HEADS:2 * FOX_HEADS]
                      + cl[2 * FOX_HEADS:3 * FOX_HEADS])
        carry = jnp.broadcast_to(ct[:, SUB - 1:SUB], ct.shape)
        c = jnp.concatenate([ct, head_pad], axis=0).T
        nc_parts = _split3(c * (-LOG2E))

        q_all = proj(h, _Q0, FOX_WIDTH) * (HEAD_DIM ** -0.5 * LOG2E)
        g_a = proj(h, _GA0, D_MODEL)
        v_all = proj(h, _V0, FOX_WIDTH)

        u = _gelu_tanh(usv[:, :SGU_WIDTH])
        sv = _gelu_tanh(usv[:, SGU_WIDTH:])
        mu = jnp.mean(sv, axis=-1, keepdims=True)
        xc = sv - mu
        var = jnp.mean(xc * xc, axis=-1, keepdims=True)
        svn = (xc * lax.rsqrt(var + EPS) * lng_ref[...] + lnb_ref[...]).astype(BF16)
        for jp in range(SGU_GROUPS // 2):
            sl = slice(LANES * jp, LANES * (jp + 1))
            chunk = svn[:, sl]
            rhs = jnp.concatenate(
                [chunk[SGU_LEN * w:SGU_LEN * (w + 1), :] for w in range(n_win)], axis=1)
            wa = jnp.where(wmask, wm_ref[2 * jp], 0.0).astype(BF16)
            wb = jnp.where(wmask, wm_ref[2 * jp + 1], 0.0).astype(BF16)
            zero = jnp.zeros_like(rhs)
            stacked = jnp.concatenate([jnp.where(low_w, rhs, zero), jnp.where(low_w, zero, rhs)],
                                      axis=0)
            mixed = _dot(jnp.concatenate([wa, wb], axis=1), stacked)
            mixed = mixed + jnp.concatenate([bm_ref[:, sl]] * n_win, axis=1)
            mixed = jnp.concatenate(
                [mixed[:, LANES * w:LANES * (w + 1)] for w in range(n_win)], axis=0)
            sb_ref[rows, sl] = (u[:, sl] * mixed).astype(BF16)
        gs_ref[rows, :D_MODEL] = g_a.astype(BF16)

        g_b = proj(h, _GB0, D_MODEL)
        for jp in range(FOX_HEADS // 2):
            sl = slice(LANES * jp, LANES * (jp + 1))
            qc, kc = q_all[:, sl], k_all[:, sl]
            vt = v_all[:, sl].T
            for par in range(2):
                hd = 2 * jp + par
                a0 = HEAD_DIM if par == 0 else 0
                data = low if par == 0 else jnp.logical_not(low)
                in_aug = (lane >= a0) & (lane < a0 + N_PARTS)
                qa = jnp.where(data, qc, jnp.where(in_aug, 1.0, 0.0))
                hi, mid, lo = [jnp.broadcast_to(p[:, hd:hd + 1], (SUB, LANES)) for p in nc_parts]
                aug = jnp.where(lane == a0, hi,
                                jnp.where(lane == a0 + 1, mid,
                                          jnp.where(lane == a0 + 2, lo, 0.0)))
                ka = jnp.where(data, kc, aug)
                qa_ref[0, hd, rows, :] = qa.astype(BF16)
                ka_ref[0, hd, rows, :] = ka.astype(BF16)
                vta = jnp.concatenate([vt[par * HEAD_DIM:(par + 1) * HEAD_DIM], ones_row], axis=0)
                vt_ref[0, hd, kblk, :, kcols:kcols + SUB] = vta.astype(BF16)
        gs_ref[rows, D_MODEL:] = g_b.astype(BF16)
    carry_ref[...] = carry


def _attn_kernel(qa_ref, ka_ref, vt_ref, *refs, n_q, n_cast):
    cast_in, (o_ref, *cast_out), (acc_ref, st_ref) = (
        refs[:n_cast], refs[n_cast:2 * n_cast + 1], refs[2 * n_cast + 1:])
    for src, dst in zip(cast_in, cast_out):
        dst[...] = src[...].astype(BF16)

    k_i = lax.broadcasted_iota(jnp.int32, (QC, QC), 0)
    q_i = lax.broadcasted_iota(jnp.int32, (QC, QC), 1)
    causal = k_i <= q_i
    halves = [slice(h * QC, (h + 1) * QC) for h in range(N_CHUNKS)]

    def needed(c, tile, blk, half):
        return not (blk == tile and half > c)

    def score_half(ch, tile, blk, half):
        hd, c = divmod(ch, N_CHUNKS)
        q0 = tile * TQ + c * QC
        k0 = blk * TQ + half * QC
        st = lax.dot_general(ka_ref[0, hd, pl.ds(k0, QC), :], qa_ref[0, hd, pl.ds(q0, QC), :],
                             (((1,), (1,)), ((), ())), preferred_element_type=F32)
        if blk == tile and half == c:
            st = jnp.where(causal, st, -jnp.inf)
        st_ref[ch, halves[half], :] = st
        return jnp.max(st, axis=0, keepdims=True)

    def chain_step(ch, tile, blk, m_old, cm, nxt):
        hd, c = divmod(ch, N_CHUNKS)
        m_new = cm if m_old is None else jnp.maximum(m_old, cm)
        pv, cm_next = [], []
        for half in range(N_CHUNKS):
            use = needed(c, tile, blk, half)
            if use:
                p = jnp.exp2(st_ref[ch, halves[half], :] - m_new).astype(BF16)
            if nxt is not None and needed(c, *nxt, half):
                cm_next.append(score_half(ch, *nxt, half))
            if use:
                pv.append(_dot(vt_ref[0, hd, blk, :, halves[half]], p))
        pv = sum(pv[1:], pv[0])
        if m_old is None:
            acc_ref[ch] = pv
        else:
            acc_ref[ch] = acc_ref[ch] * jnp.exp2(m_old - m_new) + pv
        return m_new, (functools.reduce(jnp.maximum, cm_next) if cm_next else None)

    steps = [(qi, j) for qi in range(n_q) for j in range(qi + 1)]
    ms = [None] * N_CHAINS
    cms = [functools.reduce(jnp.maximum,
                            [score_half(ch, *steps[0], half) for half in range(N_CHUNKS)
                             if needed(ch % N_CHUNKS, *steps[0], half)])
           for ch in range(N_CHAINS)]
    for n, (qi, j) in enumerate(steps):
        nxt = steps[n + 1] if n + 1 < len(steps) else None
        for ch in range(N_CHAINS):
            ms[ch], cms[ch] = chain_step(ch, qi, j, ms[ch], cms[ch], nxt)
        if j == qi:
            for c in range(N_CHUNKS):
                chains = [N_CHUNKS * hd + c for hd in range(PAIR)]
                o_t = jnp.concatenate(
                    [acc_ref[ch, :HEAD_DIM, :] / acc_ref[ch, HEAD_DIM:HEAD_DIM + 1, :]
                     for ch in chains], axis=0)
                o_ref[0, pl.ds(qi * TQ + c * QC, QC), :] = o_t.T.astype(BF16)
            ms = [None] * N_CHAINS


def _post_kernel(x_ref, at_ref, sb_ref, gs_ref, wa_ref, wb_ref, wo_ref, g2_ref, wu_ref,
                 wd_ref, gf_ref, o_ref, *, ff_chunk):
    subs = [slice(s * SUB, (s + 1) * SUB) for s in range(TM // SUB)]
    merged = []
    for rows in subs:
        ya = _dot(at_ref[rows, :], wa_ref[...])
        yb = _dot(sb_ref[rows, :], wb_ref[...])
        merged.append((_sigmoid(gs_ref[rows, :D_MODEL].astype(F32)) * ya
                       + _sigmoid(gs_ref[rows, D_MODEL:].astype(F32)) * yb).astype(BF16))
    ys = [x_ref[rows, :] + _dot(m, wo_ref[...]) for rows, m in zip(subs, merged)]
    hs = []
    for y in ys:
        ms = jnp.mean(y * y, axis=-1, keepdims=True)
        hs.append((y * lax.rsqrt(ms + EPS) * g2_ref[...]).astype(BF16))
    for c0 in range(0, D_FF, ff_chunk):
        acts = []
        for h in hs:
            a = jnp.maximum(_dot(h, wu_ref[:, c0:c0 + ff_chunk]), 0.0)
            acts.append((a * a).astype(BF16))
        ys = [y + _dot(a, wd_ref[c0:c0 + ff_chunk, :]) for y, a in zip(ys, acts)]
    for rows, y in zip(subs, ys):
        ms = jnp.mean(y * y, axis=-1, keepdims=True)
        o_ref[rows, :] = y * lax.rsqrt(ms + EPS) * gf_ref[...]


def _params(n_axes):
    return pltpu.CompilerParams(dimension_semantics=("arbitrary",) * n_axes,
                                vmem_limit_bytes=VMEM_LIMIT)


def _const(shape):
    return pl.BlockSpec(shape, lambda *_: (0,) * len(shape), pipeline_mode=pl.Buffered(1))


def _rows(width, rows=TM):
    return pl.BlockSpec((rows, width), lambda i: (i, 0))


def _prep(wt):
    n, d = wt.shape
    assert n == _W_COLS + FOX_HEADS and _F0 % PREP_COLS == 0 and _W_COLS % PREP_COLS == 0
    per_tile = PREP_COLS // FOX_HEADS
    return pl.pallas_call(
        _prep_kernel,
        grid=(_W_COLS // PREP_COLS,),
        in_specs=[pl.BlockSpec((PREP_COLS, d), lambda t: (t, 0)),
                  pl.BlockSpec((FOX_HEADS, d), lambda t: ((t + 1) * per_tile, 0)),
                  pl.BlockSpec((FOX_HEADS, d), lambda t: (_F0 // FOX_HEADS, 0))],
        out_specs=[pl.BlockSpec((d, PREP_COLS), lambda t: (0, t)),
                   pl.BlockSpec((GATE_ROWS, d), lambda t: (0, 0))],
        out_shape=[jax.ShapeDtypeStruct((d, _W_COLS), BF16),
                   jax.ShapeDtypeStruct((GATE_ROWS, d), BF16)],
        compiler_params=_params(1),
        name="prep",
    )(wt, wt, wt)


def _pre(xt, bsz, seq, g1, w_all, w_ft, bf_pad, ln_g, ln_b, w_sgu, bm):
    n_tok, d = xt.shape
    tiles_per_seq = seq // TP
    head_map = lambda i: (i // tiles_per_seq, 0, i % tiles_per_seq, 0)
    return pl.pallas_call(
        functools.partial(_pre_kernel, tiles_per_seq=tiles_per_seq),
        grid=(n_tok // TP,),
        in_specs=[
            _rows(d, TP),
            _const((1, d)),
            _const((d, _W_COLS)),
            _const((GATE_ROWS, d)),
            _const((GATE_ROWS, SUB)),
            _const((1, SGU_WIDTH)),
            _const((1, SGU_WIDTH)),
            _const((SGU_GROUPS, SGU_LEN, SGU_LEN)),
            _const((SGU_LEN, SGU_WIDTH)),
        ],
        out_specs=[
            pl.BlockSpec((1, FOX_HEADS, TP, LANES), head_map),
            pl.BlockSpec((1, FOX_HEADS, TP, LANES), head_map),
            pl.BlockSpec((1, FOX_HEADS, TP // TQ, VT_ROWS, TQ), lambda i: head_map(i) + (0,)),
            _rows(SGU_WIDTH, TP),
            _rows(2 * d, TP),
        ],
        out_shape=[
            jax.ShapeDtypeStruct((bsz, FOX_HEADS, seq, LANES), BF16),
            jax.ShapeDtypeStruct((bsz, FOX_HEADS, seq, LANES), BF16),
            jax.ShapeDtypeStruct((bsz, FOX_HEADS, seq // TQ, VT_ROWS, TQ), BF16),
            jax.ShapeDtypeStruct((n_tok, SGU_WIDTH), BF16),
            jax.ShapeDtypeStruct((n_tok, 2 * d), BF16),
        ],
        scratch_shapes=[pltpu.VMEM((FOX_HEADS, SUB), F32)],
        compiler_params=_params(1),
        name="pre",
    )(xt, g1, w_all, w_ft, bf_pad, ln_g, ln_b, w_sgu, bm)


def _attn(qa, ka, vt, weights):
    bsz, _, seq, _ = qa.shape
    pairs = FOX_HEADS // PAIR
    n_steps = bsz * pairs
    assert all(w.shape[0] % (n_steps * BF16_SUBLANES) == 0 for w in weights)
    slab = lambda w: pl.BlockSpec((w.shape[0] // n_steps, w.shape[1]),
                                  lambda b, p: (b * pairs + p, 0))
    att, *cast = pl.pallas_call(
        functools.partial(_attn_kernel, n_q=seq // TQ, n_cast=len(weights)),
        grid=(bsz, pairs),
        in_specs=[
            pl.BlockSpec((1, PAIR, seq, LANES), lambda b, p: (b, p, 0, 0)),
            pl.BlockSpec((1, PAIR, seq, LANES), lambda b, p: (b, p, 0, 0)),
            pl.BlockSpec((1, PAIR, seq // TQ, VT_ROWS, TQ), lambda b, p: (b, p, 0, 0, 0)),
        ] + [slab(w) for w in weights],
        out_specs=[pl.BlockSpec((1, seq, PAIR * HEAD_DIM), lambda b, p: (b, 0, p))]
        + [slab(w) for w in weights],
        out_shape=[jax.ShapeDtypeStruct((bsz, seq, FOX_WIDTH), BF16)]
        + [jax.ShapeDtypeStruct(w.shape, BF16) for w in weights],
        scratch_shapes=[pltpu.VMEM((N_CHAINS, VT_ROWS, QC), F32),
                        pltpu.VMEM((N_CHAINS, TQ, QC), F32)],
        compiler_params=_params(2),
        name="attn",
    )(qa, ka, vt, *weights)
    return att, cast


def _post(xt, att, sb, gs, w_a, w_b, w_o, g2, w_up, w_down, gf):
    n_tok, d = xt.shape
    return pl.pallas_call(
        functools.partial(_post_kernel, ff_chunk=1024),
        grid=(n_tok // TM,),
        in_specs=[
            _rows(d),
            _rows(FOX_WIDTH),
            _rows(SGU_WIDTH),
            _rows(2 * d),
            _const((FOX_WIDTH, d)),
            _const((SGU_WIDTH, d)),
            _const((d, d)),
            _const((1, d)),
            _const((d, D_FF)),
            _const((D_FF, d)),
            _const((1, d)),
        ],
        out_specs=_rows(d),
        out_shape=jax.ShapeDtypeStruct((n_tok, d), F32),
        compiler_params=_params(1),
        name="post",
    )(xt, att, sb, gs, w_a, w_b, w_o, g2, w_up, w_down, gf)


def kernel(x, norm1_g, w_in, b_f, ln_v_g, ln_v_b, w_sgu, b_sgu, w_a, w_b, w_o,
           norm2_g, w_up, w_down, normf_g):
    bsz, seq, d = x.shape
    assert d == D_MODEL and seq % TP == 0 and TP % TQ == 0 and TQ % SUB == 0 and TM % SUB == 0
    assert norm1_g.shape[0] == 1, "single-layer block"
    n_tok = bsz * seq
    xt = x.reshape(n_tok, d)

    w_all, w_ft = _prep(jnp.swapaxes(w_in[0], 0, 1))
    bf_rows = jnp.pad(jnp.tile(b_f[0], N_PARTS), (0, GATE_ROWS - N_PARTS * FOX_HEADS))
    bf_pad = jnp.broadcast_to(bf_rows[:, None], (GATE_ROWS, SUB))
    bm = jnp.repeat(jnp.transpose(b_sgu[0]), HEAD_DIM, axis=1)

    qa, ka, vt, sb, gs = _pre(xt, bsz, seq, norm1_g.reshape(1, d), w_all, w_ft, bf_pad,
                              ln_v_g.reshape(1, SGU_WIDTH), ln_v_b.reshape(1, SGU_WIDTH),
                              w_sgu[0], bm)
    att, (wa16, wb16, wo16, wu16, wd16) = _attn(
        qa, ka, vt, [w_a[0], w_b[0], w_o[0], w_up[0], w_down[0]])
    out = _post(xt, att.reshape(n_tok, FOX_WIDTH), sb, gs, wa16, wb16, wo16,
                norm2_g.reshape(1, d), wu16, wd16, normf_g.reshape(1, d))
    return out.reshape(bsz, seq, d)
```

```python
import functools
import math

import jax
import jax.numpy as jnp
from jax import lax
from jax.experimental import pallas as pl
from jax.experimental.pallas import tpu as pltpu

D_MODEL = 1024
HEAD_DIM = 64
FOX_HEADS = 8
FOX_WIDTH = FOX_HEADS * HEAD_DIM
SGU_GROUPS = 8
SGU_WIDTH = 512
SGU_LEN = 128
CHUNK = 64
D_FF = 4 * D_MODEL
EPS = 1e-6

LANES = 128
TM = 512
TP = 512
TQ = 512
QC = 256
PAIR = 2
N_CHUNKS = TQ // QC
N_CHAINS = PAIR * N_CHUNKS
BF16_SUBLANES = 16
SUB = 256
VT_ROWS = 128
PREP_COLS = 1536
LOG2E = math.log2(math.e)
VMEM_LIMIT = 56 * 1024 * 1024

_F0 = 3 * FOX_WIDTH
_Q0, _K0, _V0 = 0, FOX_WIDTH, 2 * FOX_WIDTH
_U0 = _F0
_GA0 = _U0 + 2 * SGU_WIDTH
_GB0 = _GA0 + D_MODEL
_W_COLS = _GB0 + D_MODEL
N_PARTS = 3
GATE_ROWS = 32

BF16 = jnp.bfloat16
F32 = jnp.float32


def _dot(a, b):
    return jnp.dot(a, b, preferred_element_type=F32)


def _gelu_tanh(x):
    c = math.sqrt(2.0 / math.pi)
    half = 0.5 * x
    return half * jnp.tanh(x * ((x * x) * (c * 0.044715) + c)) + half


def _sigmoid(x):
    return 0.5 * jnp.tanh(0.5 * x) + 0.5


def _split3(x):
    hi = x.astype(BF16).astype(F32)
    r = x - hi
    mid = r.astype(BF16).astype(F32)
    return hi, mid, r - mid


def _prep_kernel(a_ref, b_ref, g_ref, w_ref, wft_ref):
    a = a_ref[...]
    past_gates = jnp.concatenate([a[FOX_HEADS:], b_ref[...]], axis=0)
    rows = jnp.where(pl.program_id(0) < _F0 // PREP_COLS, a, past_gates)
    w_ref[...] = rows.T.astype(BF16)
    g = g_ref[...]
    pad = jnp.zeros((GATE_ROWS - N_PARTS * FOX_HEADS, g.shape[1]), g.dtype)
    wft_ref[...] = jnp.concatenate([g] * N_PARTS + [pad], axis=0).astype(BF16)


def _pre_kernel(x_ref, g1_ref, w_ref, wft_ref, bf_ref, lng_ref, lnb_ref, wm_ref, bm_ref,
                qa_ref, ka_ref, vt_ref, sb_ref, gs_ref, carry_ref, *, tiles_per_seq):
    i = pl.program_id(0)

    @pl.when(i % tiles_per_seq == 0)
    def _():
        carry_ref[...] = jnp.zeros_like(carry_ref)

    s_i = lax.broadcasted_iota(jnp.int32, (SUB, SUB), 0)
    t_i = lax.broadcasted_iota(jnp.int32, (SUB, SUB), 1)
    tri = jnp.where(s_i <= t_i, 1.0, 0.0).astype(BF16)
    grow = lax.broadcasted_iota(jnp.int32, (GATE_ROWS, SUB), 0)
    head_pad = jnp.zeros((LANES - FOX_HEADS, SUB), F32)
    lane = lax.broadcasted_iota(jnp.int32, (SUB, LANES), 1)
    low = lane < HEAD_DIM
    ones_row = jnp.where(lax.broadcasted_iota(jnp.int32, (VT_ROWS - HEAD_DIM, SUB), 0) == 0,
                         1.0, 0.0)
    n_win = SUB // SGU_LEN
    wi = lax.broadcasted_iota(jnp.int32, (SGU_LEN, SGU_LEN), 0) // CHUNK
    wj = lax.broadcasted_iota(jnp.int32, (SGU_LEN, SGU_LEN), 1) // CHUNK
    wmask = wj <= wi
    lane_w = lax.broadcasted_iota(jnp.int32, (SGU_LEN, n_win * LANES), 1)
    low_w = (lane_w % LANES) < HEAD_DIM

    def proj(h, c0, width):
        return _dot(h, w_ref[:, c0:c0 + width])

    carry = carry_ref[...]
    for sub in range(TP // SUB):
        rows = slice(sub * SUB, (sub + 1) * SUB)
        kblk, kcols = divmod(sub * SUB, TQ)
        x = x_ref[rows, :]
        ms = jnp.mean(x * x, axis=-1, keepdims=True)
        h = (x * lax.rsqrt(ms + EPS) * g1_ref[...]).astype(BF16)

        z = lax.dot_general(wft_ref[...], h, (((1,), (1,)), ((), ())),
                            preferred_element_type=F32) + bf_ref[...]
        usv = proj(h, _U0, 2 * SGU_WIDTH)
        k_all = proj(h, _K0, FOX_WIDTH)

        logf = jnp.minimum(z, 0.0) - jnp.log(1.0 + jnp.exp(-jnp.abs(z)))
        hi, mid, lo = _split3(logf)
        part = jnp.where(grow < FOX_HEADS, hi, jnp.where(grow < 2 * FOX_HEADS, mid, lo))
        cl = _dot(part.astype(BF16), tri)
        ct = carry + (cl[:FOX_HEADS] + cl[FOX_HEADS:2 * FOX_HEADS]
                      + cl[2 * FOX_HEADS:3 * FOX_HEADS])
        carry = jnp.broadcast_to(ct[:, SUB - 1:SUB], ct.shape)
        c = jnp.concatenate([ct, head_pad], axis=0).T
        nc_parts = _split3(c * (-LOG2E))

        q_all = proj(h, _Q0, FOX_WIDTH) * (HEAD_DIM ** -0.5 * LOG2E)
        g_a = proj(h, _GA0, D_MODEL)
        v_all = proj(h, _V0, FOX_WIDTH)

        u = _gelu_tanh(usv[:, :SGU_WIDTH])
        sv = _gelu_tanh(usv[:, SGU_WIDTH:])
        mu = jnp.mean(sv, axis=-1, keepdims=True)
        xc = sv - mu
        var = jnp.mean(xc * xc, axis=-1, keepdims=True)
        svn = (xc * lax.rsqrt(var + EPS) * lng_ref[...] + lnb_ref[...]).astype(BF16)
        for jp in range(SGU_GROUPS // 2):
            sl = slice(LANES * jp, LANES * (jp + 1))
            chunk = svn[:, sl]
            rhs = jnp.concatenate(
                [chunk[SGU_LEN * w:SGU_LEN * (w + 1), :] for w in range(n_win)], axis=1)
            wa = jnp.where(wmask, wm_ref[2 * jp], 0.0).astype(BF16)
            wb = jnp.where(wmask, wm_ref[2 * jp + 1], 0.0).astype(BF16)
            zero = jnp.zeros_like(rhs)
            stacked = jnp.concatenate([jnp.where(low_w, rhs, zero), jnp.where(low_w, zero, rhs)],
                                      axis=0)
            mixed = _dot(jnp.concatenate([wa, wb], axis=1), stacked)
            mixed = mixed + jnp.concatenate([bm_ref[:, sl]] * n_win, axis=1)
            mixed = jnp.concatenate(
                [mixed[:, LANES * w:LANES * (w + 1)] for w in range(n_win)], axis=0)
            sb_ref[rows, sl] = (u[:, sl] * mixed).astype(BF16)
        gs_ref[rows, :D_MODEL] = g_a.astype(BF16)

        g_b = proj(h, _GB0, D_MODEL)
        for jp in range(FOX_HEADS // 2):
            sl = slice(LANES * jp, LANES * (jp + 1))
            qc, kc = q_all[:, sl], k_all[:, sl]
            vt = v_all[:, sl].T
            for par in range(2):
                hd = 2 * jp + par
                a0 = HEAD_DIM if par == 0 else 0
                data = low if par == 0 else jnp.logical_not(low)
                in_aug = (lane >= a0) & (lane < a0 + N_PARTS)
                qa = jnp.where(data, qc, jnp.where(in_aug, 1.0, 0.0))
                hi, mid, lo = [jnp.broadcast_to(p[:, hd:hd + 1], (SUB, LANES)) for p in nc_parts]
                aug = jnp.where(lane == a0, hi,
                                jnp.where(lane == a0 + 1, mid,
                                          jnp.where(lane == a0 + 2, lo, 0.0)))
                ka = jnp.where(data, kc, aug)
                qa_ref[0, hd, rows, :] = qa.astype(BF16)
                ka_ref[0, hd, rows, :] = ka.astype(BF16)
                vta = jnp.concatenate([vt[par * HEAD_DIM:(par + 1) * HEAD_DIM], ones_row], axis=0)
                vt_ref[0, hd, kblk, :, kcols:kcols + SUB] = vta.astype(BF16)
        gs_ref[rows, D_MODEL:] = g_b.astype(BF16)
    carry_ref[...] = carry


def _attn_kernel(qa_ref, ka_ref, vt_ref, *refs, n_q, n_cast):
    cast_in, (o_ref, *cast_out), (acc_ref, st_ref) = (
        refs[:n_cast], refs[n_cast:2 * n_cast + 1], refs[2 * n_cast + 1:])
    for src, dst in zip(cast_in, cast_out):
        dst[...] = src[...].astype(BF16)

    k_i = lax.broadcasted_iota(jnp.int32, (QC, QC), 0)
    q_i = lax.broadcasted_iota(jnp.int32, (QC, QC), 1)
    causal = k_i <= q_i
    halves = [slice(h * QC, (h + 1) * QC) for h in range(N_CHUNKS)]

    def needed(c, tile, blk, half):
        return not (blk == tile and half > c)

    def score_half(ch, tile, blk, half):
        hd, c = divmod(ch, N_CHUNKS)
        q0 = tile * TQ + c * QC
        k0 = blk * TQ + half * QC
        st = lax.dot_general(ka_ref[0, hd, pl.ds(k0, QC), :], qa_ref[0, hd, pl.ds(q0, QC), :],
                             (((1,), (1,)), ((), ())), preferred_element_type=F32)
        if blk == tile and half == c:
            st = jnp.where(causal, st, -jnp.inf)
        st_ref[ch, halves[half], :] = st
        return jnp.max(st, axis=0, keepdims=True)

    def chain_step(ch, tile, blk, m_old, cm, nxt):
        hd, c = divmod(ch, N_CHUNKS)
        m_new = cm if m_old is None else jnp.maximum(m_old, cm)
        pv, cm_next = [], []
        for half in range(N_CHUNKS):
            use = needed(c, tile, blk, half)
            if use:
                p = jnp.exp2(st_ref[ch, halves[half], :] - m_new).astype(BF16)
            if nxt is not None and needed(c, *nxt, half):
                cm_next.append(score_half(ch, *nxt, half))
            if use:
                pv.append(_dot(vt_ref[0, hd, blk, :, halves[half]], p))
        pv = sum(pv[1:], pv[0])
        if m_old is None:
            acc_ref[ch] = pv
        else:
            acc_ref[ch] = acc_ref[ch] * jnp.exp2(m_old - m_new) + pv
        return m_new, (functools.reduce(jnp.maximum, cm_next) if cm_next else None)

    steps = [(qi, j) for qi in range(n_q) for j in range(qi + 1)]
    ms = [None] * N_CHAINS
    cms = [functools.reduce(jnp.maximum,
                            [score_half(ch, *steps[0], half) for half in range(N_CHUNKS)
                             if needed(ch % N_CHUNKS, *steps[0], half)])
           for ch in range(N_CHAINS)]
    for n, (qi, j) in enumerate(steps):
        nxt = steps[n + 1] if n + 1 < len(steps) else None
        for ch in range(N_CHAINS):
            ms[ch], cms[ch] = chain_step(ch, qi, j, ms[ch], cms[ch], nxt)
        if j == qi:
            for c in range(N_CHUNKS):
                chains = [N_CHUNKS * hd + c for hd in range(PAIR)]
                o_t = jnp.concatenate(
                    [acc_ref[ch, :HEAD_DIM, :] / acc_ref[ch, HEAD_DIM:HEAD_DIM + 1, :]
                     for ch in chains], axis=0)
                o_ref[0, pl.ds(qi * TQ + c * QC, QC), :] = o_t.T.astype(BF16)
            ms = [None] * N_CHAINS


def _post_kernel(x_ref, at_ref, sb_ref, gs_ref, wa_ref, wb_ref, wo_ref, g2_ref, wu_ref,
                 wd_ref, gf_ref, o_ref, *, ff_chunk):
    subs = [slice(s * SUB, (s + 1) * SUB) for s in range(TM // SUB)]
    merged = []
    for rows in subs:
        ya = _dot(at_ref[rows, :], wa_ref[...])
        yb = _dot(sb_ref[rows, :], wb_ref[...])
        merged.append((_sigmoid(gs_ref[rows, :D_MODEL].astype(F32)) * ya
                       + _sigmoid(gs_ref[rows, D_MODEL:].astype(F32)) * yb).astype(BF16))
    ys = [x_ref[rows, :] + _dot(m, wo_ref[...]) for rows, m in zip(subs, merged)]
    hs = []
    for y in ys:
        ms = jnp.mean(y * y, axis=-1, keepdims=True)
        hs.append((y * lax.rsqrt(ms + EPS) * g2_ref[...]).astype(BF16))
    for c0 in range(0, D_FF, ff_chunk):
        acts = []
        for h in hs:
            a = jnp.maximum(_dot(h, wu_ref[:, c0:c0 + ff_chunk]), 0.0)
            acts.append((a * a).astype(BF16))
        ys = [y + _dot(a, wd_ref[c0:c0 + ff_chunk, :]) for y, a in zip(ys, acts)]
    for rows, y in zip(subs, ys):
        ms = jnp.mean(y * y, axis=-1, keepdims=True)
        o_ref[rows, :] = y * lax.rsqrt(ms + EPS) * gf_ref[...]


def _params(n_axes, ordered=False):
    semantics = "arbitrary" if ordered else "parallel"
    return pltpu.CompilerParams(dimension_semantics=(semantics,) * n_axes,
                                vmem_limit_bytes=VMEM_LIMIT)


def _const(shape):
    return pl.BlockSpec(shape, lambda *_: (0,) * len(shape), pipeline_mode=pl.Buffered(1))


def _rows(width, rows=TM):
    return pl.BlockSpec((rows, width), lambda i: (i, 0))


def _prep(wt):
    n, d = wt.shape
    assert n == _W_COLS + FOX_HEADS and _F0 % PREP_COLS == 0 and _W_COLS % PREP_COLS == 0
    per_tile = PREP_COLS // FOX_HEADS
    return pl.pallas_call(
        _prep_kernel,
        grid=(_W_COLS // PREP_COLS,),
        in_specs=[pl.BlockSpec((PREP_COLS, d), lambda t: (t, 0)),
                  pl.BlockSpec((FOX_HEADS, d), lambda t: ((t + 1) * per_tile, 0)),
                  pl.BlockSpec((FOX_HEADS, d), lambda t: (_F0 // FOX_HEADS, 0))],
        out_specs=[pl.BlockSpec((d, PREP_COLS), lambda t: (0, t)),
                   pl.BlockSpec((GATE_ROWS, d), lambda t: (0, 0))],
        out_shape=[jax.ShapeDtypeStruct((d, _W_COLS), BF16),
                   jax.ShapeDtypeStruct((GATE_ROWS, d), BF16)],
        compiler_params=_params(1, ordered=True),
        name="prep",
    )(wt, wt, wt)


def _pre(xt, bsz, seq, g1, w_all, w_ft, bf_pad, ln_g, ln_b, w_sgu, bm):
    n_tok, d = xt.shape
    tiles_per_seq = seq // TP
    head_map = lambda i: (i // tiles_per_seq, 0, i % tiles_per_seq, 0)
    return pl.pallas_call(
        functools.partial(_pre_kernel, tiles_per_seq=tiles_per_seq),
        grid=(n_tok // TP,),
        in_specs=[
            _rows(d, TP),
            _const((1, d)),
            _const((d, _W_COLS)),
            _const((GATE_ROWS, d)),
            _const((GATE_ROWS, SUB)),
            _const((1, SGU_WIDTH)),
            _const((1, SGU_WIDTH)),
            _const((SGU_GROUPS, SGU_LEN, SGU_LEN)),
            _const((SGU_LEN, SGU_WIDTH)),
        ],
        out_specs=[
            pl.BlockSpec((1, FOX_HEADS, TP, LANES), head_map),
            pl.BlockSpec((1, FOX_HEADS, TP, LANES), head_map),
            pl.BlockSpec((1, FOX_HEADS, TP // TQ, VT_ROWS, TQ), lambda i: head_map(i) + (0,)),
            _rows(SGU_WIDTH, TP),
            _rows(2 * d, TP),
        ],
        out_shape=[
            jax.ShapeDtypeStruct((bsz, FOX_HEADS, seq, LANES), BF16),
            jax.ShapeDtypeStruct((bsz, FOX_HEADS, seq, LANES), BF16),
            jax.ShapeDtypeStruct((bsz, FOX_HEADS, seq // TQ, VT_ROWS, TQ), BF16),
            jax.ShapeDtypeStruct((n_tok, SGU_WIDTH), BF16),
            jax.ShapeDtypeStruct((n_tok, 2 * d), BF16),
        ],
        scratch_shapes=[pltpu.VMEM((FOX_HEADS, SUB), F32)],
        compiler_params=_params(1, ordered=True),
        name="pre",
    )(xt, g1, w_all, w_ft, bf_pad, ln_g, ln_b, w_sgu, bm)


def _attn(qa, ka, vt, weights):
    bsz, _, seq, _ = qa.shape
    pairs = FOX_HEADS // PAIR
    n_steps = bsz * pairs
    assert all(w.shape[0] % (n_steps * BF16_SUBLANES) == 0 for w in weights)
    slab = lambda w: pl.BlockSpec((w.shape[0] // n_steps, w.shape[1]),
                                  lambda b, p: (b * pairs + p, 0))
    att, *cast = pl.pallas_call(
        functools.partial(_attn_kernel, n_q=seq // TQ, n_cast=len(weights)),
        grid=(bsz, pairs),
        in_specs=[
            pl.BlockSpec((1, PAIR, seq, LANES), lambda b, p: (b, p, 0, 0)),
            pl.BlockSpec((1, PAIR, seq, LANES), lambda b, p: (b, p, 0, 0)),
            pl.BlockSpec((1, PAIR, seq // TQ, VT_ROWS, TQ), lambda b, p: (b, p, 0, 0, 0)),
        ] + [slab(w) for w in weights],
        out_specs=[pl.BlockSpec((1, seq, LANES), lambda b, p: (b, 0, p))]
        + [slab(w) for w in weights],
        out_shape=[jax.ShapeDtypeStruct((bsz, seq, FOX_WIDTH), BF16)]
        + [jax.ShapeDtypeStruct(w.shape, BF16) for w in weights],
        scratch_shapes=[pltpu.VMEM((N_CHAINS, VT_ROWS, QC), F32),
                        pltpu.VMEM((N_CHAINS, TQ, QC), F32)],
        compiler_params=_params(2),
        name="attn",
    )(qa, ka, vt, *weights)
    return att, cast


def _post(xt, att, sb, gs, w_a, w_b, w_o, g2, w_up, w_down, gf):
    n_tok, d = xt.shape
    return pl.pallas_call(
        functools.partial(_post_kernel, ff_chunk=1024),
        grid=(n_tok // TM,),
        in_specs=[
            _rows(d),
            _rows(FOX_WIDTH),
            _rows(SGU_WIDTH),
            _rows(2 * d),
            _const((FOX_WIDTH, d)),
            _const((SGU_WIDTH, d)),
            _const((d, d)),
            _const((1, d)),
            _const((d, D_FF)),
            _const((D_FF, d)),
            _const((1, d)),
        ],
        out_specs=_rows(d),
        out_shape=jax.ShapeDtypeStruct((n_tok, d), F32),
        compiler_params=_params(1),
        name="post",
    )(xt, att, sb, gs, w_a, w_b, w_o, g2, w_up, w_down, gf)


def kernel(x, norm1_g, w_in, b_f, ln_v_g, ln_v_b, w_sgu, b_sgu, w_a, w_b, w_o,
           norm2_g, w_up, w_down, normf_g):
    bsz, seq, d = x.shape
    assert d == D_MODEL and seq % TP == 0 and TP % TQ == 0 and TQ % SUB == 0 and TM % SUB == 0
    assert norm1_g.shape[0] == 1, "single-layer block"
    n_tok = bsz * seq
    xt = x.reshape(n_tok, d)

    w_all, w_ft = _prep(jnp.swapaxes(w_in[0], 0, 1))
    bf_rows = jnp.pad(jnp.tile(b_f[0], N_PARTS), (0, GATE_ROWS - N_PARTS * FOX_HEADS))
    bf_pad = jnp.broadcast_to(bf_rows[:, None], (GATE_ROWS, SUB))
    bm = jnp.repeat(jnp.transpose(b_sgu[0]), HEAD_DIM, axis=1)

    qa, ka, vt, sb, gs = _pre(xt, bsz, seq, norm1_g.reshape(1, d), w_all, w_ft, bf_pad,
                              ln_v_g.reshape(1, SGU_WIDTH), ln_v_b.reshape(1, SGU_WIDTH),
                              w_sgu[0], bm)
    att, (wa16, wb16, wo16, wu16, wd16) = _attn(
        qa, ka, vt, [w_a[0], w_b[0], w_o[0], w_up[0], w_down[0]])
    out = _post(xt, att.reshape(n_tok, FOX_WIDTH), sb, gs, wa16, wb16, wo16,
                norm2_g.reshape(1, d), wu16, wd16, normf_g.reshape(1, d))
    return out.reshape(bsz, seq, d)
```

```python
import functools
import math

import jax
import jax.numpy as jnp
from jax import lax
from jax.experimental import pallas as pl
from jax.experimental.pallas import tpu as pltpu

D_MODEL = 1024
HEAD_DIM = 64
FOX_HEADS = 8
FOX_WIDTH = FOX_HEADS * HEAD_DIM
SGU_GROUPS = 8
SGU_WIDTH = 512
SGU_LEN = 128
CHUNK = 64
D_FF = 4 * D_MODEL
EPS = 1e-6

LANES = 128
TM = 512
TP = 512
TQ = 512
QC = 256
PAIR = 2
N_CHUNKS = TQ // QC
N_CHAINS = PAIR * N_CHUNKS
BF16_SUBLANES = 16
SUB = 256
VT_ROWS = 128
PREP_COLS = 768
LOG2E = math.log2(math.e)
VMEM_LIMIT = 56 * 1024 * 1024

_F0 = 3 * FOX_WIDTH
_Q0, _K0, _V0 = 0, FOX_WIDTH, 2 * FOX_WIDTH
_U0 = _F0
_GA0 = _U0 + 2 * SGU_WIDTH
_GB0 = _GA0 + D_MODEL
_W_COLS = _GB0 + D_MODEL
N_PARTS = 3
GATE_ROWS = 32

BF16 = jnp.bfloat16
F32 = jnp.float32


def _dot(a, b):
    return jnp.dot(a, b, preferred_element_type=F32)


def _gelu_tanh(x):
    c = math.sqrt(2.0 / math.pi)
    half = 0.5 * x
    return half * jnp.tanh(x * ((x * x) * (c * 0.044715) + c)) + half


def _sigmoid(x):
    return 0.5 * jnp.tanh(0.5 * x) + 0.5


def _split3(x):
    hi = x.astype(BF16).astype(F32)
    r = x - hi
    mid = r.astype(BF16).astype(F32)
    return hi, mid, r - mid


def _load_weights(wt_hbm, w_ref, wft_ref, stage_ref, gate_ref, sem):
    n_chunks = _W_COLS // PREP_COLS

    def chunk(c):
        first_row = c * PREP_COLS + (FOX_HEADS if c * PREP_COLS >= _F0 else 0)
        return pltpu.make_async_copy(wt_hbm.at[pl.ds(first_row, PREP_COLS), :],
                                     stage_ref.at[c % 2], sem.at[c % 2])

    gates = pltpu.make_async_copy(wt_hbm.at[pl.ds(_F0, FOX_HEADS), :], gate_ref, sem.at[2])
    gates.start()
    chunk(0).start()
    for c in range(n_chunks):
        if c + 1 < n_chunks:
            chunk(c + 1).start()
        chunk(c).wait()
        w_ref[:, c * PREP_COLS:(c + 1) * PREP_COLS] = stage_ref[c % 2].T.astype(BF16)
    gates.wait()
    g = gate_ref[...]
    pad = jnp.zeros((GATE_ROWS - N_PARTS * FOX_HEADS, g.shape[1]), g.dtype)
    wft_ref[...] = jnp.concatenate([g] * N_PARTS + [pad], axis=0).astype(BF16)


def _pre_kernel(x_ref, g1_ref, wt_hbm, bf_ref, lng_ref, lnb_ref, wm_ref, bm_ref,
                qa_ref, ka_ref, vt_ref, sb_ref, gs_ref,
                carry_ref, w_ref, wft_ref, stage_ref, gate_ref, sem, *, tiles_per_seq):
    i = pl.program_id(0)

    @pl.when(i == 0)
    def _():
        _load_weights(wt_hbm, w_ref, wft_ref, stage_ref, gate_ref, sem)

    @pl.when(i % tiles_per_seq == 0)
    def _():
        carry_ref[...] = jnp.zeros_like(carry_ref)

    s_i = lax.broadcasted_iota(jnp.int32, (SUB, SUB), 0)
    t_i = lax.broadcasted_iota(jnp.int32, (SUB, SUB), 1)
    tri = jnp.where(s_i <= t_i, 1.0, 0.0).astype(BF16)
    grow = lax.broadcasted_iota(jnp.int32, (GATE_ROWS, SUB), 0)
    head_pad = jnp.zeros((LANES - FOX_HEADS, SUB), F32)
    lane = lax.broadcasted_iota(jnp.int32, (SUB, LANES), 1)
    low = lane < HEAD_DIM
    ones_row = jnp.where(lax.broadcasted_iota(jnp.int32, (VT_ROWS - HEAD_DIM, SUB), 0) == 0,
                         1.0, 0.0)
    n_win = SUB // SGU_LEN
    wi = lax.broadcasted_iota(jnp.int32, (SGU_LEN, SGU_LEN), 0) // CHUNK
    wj = lax.broadcasted_iota(jnp.int32, (SGU_LEN, SGU_LEN), 1) // CHUNK
    wmask = wj <= wi
    lane_w = lax.broadcasted_iota(jnp.int32, (SGU_LEN, n_win * LANES), 1)
    low_w = (lane_w % LANES) < HEAD_DIM

    def proj(h, c0, width):
        return _dot(h, w_ref[:, c0:c0 + width])

    carry = carry_ref[...]
    for sub in range(TP // SUB):
        rows = slice(sub * SUB, (sub + 1) * SUB)
        kblk, kcols = divmod(sub * SUB, TQ)
        x = x_ref[rows, :]
        ms = jnp.mean(x * x, axis=-1, keepdims=True)
        h = (x * lax.rsqrt(ms + EPS) * g1_ref[...]).astype(BF16)

        z = lax.dot_general(wft_ref[...], h, (((1,), (1,)), ((), ())),
                            preferred_element_type=F32) + bf_ref[...]
        usv = proj(h, _U0, 2 * SGU_WIDTH)
        k_all = proj(h, _K0, FOX_WIDTH)

        logf = jnp.minimum(z, 0.0) - jnp.log(1.0 + jnp.exp(-jnp.abs(z)))
        hi, mid, lo = _split3(logf)
        part = jnp.where(grow < FOX_HEADS, hi, jnp.where(grow < 2 * FOX_HEADS, mid, lo))
        cl = _dot(part.astype(BF16), tri)
        ct = carry + (cl[:FOX_HEADS] + cl[FOX_HEADS:2 * FOX_HEADS]
                      + cl[2 * FOX_HEADS:3 * FOX_HEADS])
        carry = jnp.broadcast_to(ct[:, SUB - 1:SUB], ct.shape)
        c = jnp.concatenate([ct, head_pad], axis=0).T
        nc_parts = _split3(c * (-LOG2E))

        q_all = proj(h, _Q0, FOX_WIDTH) * (HEAD_DIM ** -0.5 * LOG2E)
        g_a = proj(h, _GA0, D_MODEL)
        v_all = proj(h, _V0, FOX_WIDTH)

        u = _gelu_tanh(usv[:, :SGU_WIDTH])
        sv = _gelu_tanh(usv[:, SGU_WIDTH:])
        mu = jnp.mean(sv, axis=-1, keepdims=True)
        xc = sv - mu
        var = jnp.mean(xc * xc, axis=-1, keepdims=True)
        svn = (xc * lax.rsqrt(var + EPS) * lng_ref[...] + lnb_ref[...]).astype(BF16)
        for jp in range(SGU_GROUPS // 2):
            sl = slice(LANES * jp, LANES * (jp + 1))
            chunk = svn[:, sl]
            rhs = jnp.concatenate(
                [chunk[SGU_LEN * w:SGU_LEN * (w + 1), :] for w in range(n_win)], axis=1)
            wa = jnp.where(wmask, wm_ref[2 * jp], 0.0).astype(BF16)
            wb = jnp.where(wmask, wm_ref[2 * jp + 1], 0.0).astype(BF16)
            zero = jnp.zeros_like(rhs)
            stacked = jnp.concatenate([jnp.where(low_w, rhs, zero), jnp.where(low_w, zero, rhs)],
                                      axis=0)
            mixed = _dot(jnp.concatenate([wa, wb], axis=1), stacked)
            mixed = mixed + jnp.concatenate([bm_ref[:, sl]] * n_win, axis=1)
            mixed = jnp.concatenate(
                [mixed[:, LANES * w:LANES * (w + 1)] for w in range(n_win)], axis=0)
            sb_ref[rows, sl] = (u[:, sl] * mixed).astype(BF16)
        gs_ref[rows, :D_MODEL] = g_a.astype(BF16)

        g_b = proj(h, _GB0, D_MODEL)
        for jp in range(FOX_HEADS // 2):
            sl = slice(LANES * jp, LANES * (jp + 1))
            qc, kc = q_all[:, sl], k_all[:, sl]
            vt = v_all[:, sl].T
            for par in range(2):
                hd = 2 * jp + par
                a0 = HEAD_DIM if par == 0 else 0
                data = low if par == 0 else jnp.logical_not(low)
                in_aug = (lane >= a0) & (lane < a0 + N_PARTS)
                qa = jnp.where(data, qc, jnp.where(in_aug, 1.0, 0.0))
                hi, mid, lo = [jnp.broadcast_to(p[:, hd:hd + 1], (SUB, LANES)) for p in nc_parts]
                aug = jnp.where(lane == a0, hi,
                                jnp.where(lane == a0 + 1, mid,
                                          jnp.where(lane == a0 + 2, lo, 0.0)))
                ka = jnp.where(data, kc, aug)
                qa_ref[0, hd, rows, :] = qa.astype(BF16)
                ka_ref[0, hd, rows, :] = ka.astype(BF16)
                vta = jnp.concatenate([vt[par * HEAD_DIM:(par + 1) * HEAD_DIM], ones_row], axis=0)
                vt_ref[0, hd, kblk, :, kcols:kcols + SUB] = vta.astype(BF16)
        gs_ref[rows, D_MODEL:] = g_b.astype(BF16)
    carry_ref[...] = carry


def _attn_kernel(qa_ref, ka_ref, vt_ref, *refs, n_q, n_cast):
    cast_in, (o_ref, *cast_out), (acc_ref, st_ref) = (
        refs[:n_cast], refs[n_cast:2 * n_cast + 1], refs[2 * n_cast + 1:])
    for src, dst in zip(cast_in, cast_out):
        dst[...] = src[...].astype(BF16)

    k_i = lax.broadcasted_iota(jnp.int32, (QC, QC), 0)
    q_i = lax.broadcasted_iota(jnp.int32, (QC, QC), 1)
    causal = k_i <= q_i
    halves = [slice(h * QC, (h + 1) * QC) for h in range(N_CHUNKS)]

    def needed(c, tile, blk, half):
        return not (blk == tile and half > c)

    def score_half(ch, tile, blk, half):
        hd, c = divmod(ch, N_CHUNKS)
        q0 = tile * TQ + c * QC
        k0 = blk * TQ + half * QC
        st = lax.dot_general(ka_ref[0, hd, pl.ds(k0, QC), :], qa_ref[0, hd, pl.ds(q0, QC), :],
                             (((1,), (1,)), ((), ())), preferred_element_type=F32)
        if blk == tile and half == c:
            st = jnp.where(causal, st, -jnp.inf)
        st_ref[ch, halves[half], :] = st
        return jnp.max(st, axis=0, keepdims=True)

    def chain_step(ch, tile, blk, m_old, cm, nxt):
        hd, c = divmod(ch, N_CHUNKS)
        m_new = cm if m_old is None else jnp.maximum(m_old, cm)
        pv, cm_next = [], []
        for half in range(N_CHUNKS):
            use = needed(c, tile, blk, half)
            if use:
                p = jnp.exp2(st_ref[ch, halves[half], :] - m_new).astype(BF16)
            if nxt is not None and needed(c, *nxt, half):
                cm_next.append(score_half(ch, *nxt, half))
            if use:
                pv.append(_dot(vt_ref[0, hd, blk, :, halves[half]], p))
        pv = sum(pv[1:], pv[0])
        if m_old is None:
            acc_ref[ch] = pv
        else:
            acc_ref[ch] = acc_ref[ch] * jnp.exp2(m_old - m_new) + pv
        return m_new, (functools.reduce(jnp.maximum, cm_next) if cm_next else None)

    steps = [(qi, j) for qi in range(n_q) for j in range(qi + 1)]
    ms = [None] * N_CHAINS
    cms = [functools.reduce(jnp.maximum,
                            [score_half(ch, *steps[0], half) for half in range(N_CHUNKS)
                             if needed(ch % N_CHUNKS, *steps[0], half)])
           for ch in range(N_CHAINS)]
    for n, (qi, j) in enumerate(steps):
        nxt = steps[n + 1] if n + 1 < len(steps) else None
        for ch in range(N_CHAINS):
            ms[ch], cms[ch] = chain_step(ch, qi, j, ms[ch], cms[ch], nxt)
        if j == qi:
            for c in range(N_CHUNKS):
                chains = [N_CHUNKS * hd + c for hd in range(PAIR)]
                o_t = jnp.concatenate(
                    [acc_ref[ch, :HEAD_DIM, :] / acc_ref[ch, HEAD_DIM:HEAD_DIM + 1, :]
                     for ch in chains], axis=0)
                o_ref[0, pl.ds(qi * TQ + c * QC, QC), :] = o_t.T.astype(BF16)
            ms = [None] * N_CHAINS


def _post_kernel(x_ref, at_ref, sb_ref, gs_ref, wa_ref, wb_ref, wo_ref, g2_ref, wu_ref,
                 wd_ref, gf_ref, o_ref, *, ff_chunk):
    subs = [slice(s * SUB, (s + 1) * SUB) for s in range(TM // SUB)]
    merged = []
    for rows in subs:
        ya = _dot(at_ref[rows, :], wa_ref[...])
        yb = _dot(sb_ref[rows, :], wb_ref[...])
        merged.append((_sigmoid(gs_ref[rows, :D_MODEL].astype(F32)) * ya
                       + _sigmoid(gs_ref[rows, D_MODEL:].astype(F32)) * yb).astype(BF16))
    ys = [x_ref[rows, :] + _dot(m, wo_ref[...]) for rows, m in zip(subs, merged)]
    hs = []
    for y in ys:
        ms = jnp.mean(y * y, axis=-1, keepdims=True)
        hs.append((y * lax.rsqrt(ms + EPS) * g2_ref[...]).astype(BF16))
    for c0 in range(0, D_FF, ff_chunk):
        acts = []
        for h in hs:
            a = jnp.maximum(_dot(h, wu_ref[:, c0:c0 + ff_chunk]), 0.0)
            acts.append((a * a).astype(BF16))
        ys = [y + _dot(a, wd_ref[c0:c0 + ff_chunk, :]) for y, a in zip(ys, acts)]
    for rows, y in zip(subs, ys):
        ms = jnp.mean(y * y, axis=-1, keepdims=True)
        o_ref[rows, :] = y * lax.rsqrt(ms + EPS) * gf_ref[...]


def _params(n_axes, ordered=False):
    semantics = "arbitrary" if ordered else "parallel"
    return pltpu.CompilerParams(dimension_semantics=(semantics,) * n_axes,
                                vmem_limit_bytes=VMEM_LIMIT)


def _const(shape):
    return pl.BlockSpec(shape, lambda *_: (0,) * len(shape), pipeline_mode=pl.Buffered(1))


def _rows(width, rows=TM):
    return pl.BlockSpec((rows, width), lambda i: (i, 0))


def _pre(xt, bsz, seq, g1, wt, bf_pad, ln_g, ln_b, w_sgu, bm):
    n_tok, d = xt.shape
    assert wt.shape == (_W_COLS + FOX_HEADS, d) and _F0 % PREP_COLS == 0 and _W_COLS % PREP_COLS == 0
    tiles_per_seq = seq // TP
    head_map = lambda i: (i // tiles_per_seq, 0, i % tiles_per_seq, 0)
    return pl.pallas_call(
        functools.partial(_pre_kernel, tiles_per_seq=tiles_per_seq),
        grid=(n_tok // TP,),
        in_specs=[
            _rows(d, TP),
            _const((1, d)),
            pl.BlockSpec(memory_space=pl.ANY),
            _const((GATE_ROWS, SUB)),
            _const((1, SGU_WIDTH)),
            _const((1, SGU_WIDTH)),
            _const((SGU_GROUPS, SGU_LEN, SGU_LEN)),
            _const((SGU_LEN, SGU_WIDTH)),
        ],
        out_specs=[
            pl.BlockSpec((1, FOX_HEADS, TP, LANES), head_map),
            pl.BlockSpec((1, FOX_HEADS, TP, LANES), head_map),
            pl.BlockSpec((1, FOX_HEADS, TP // TQ, VT_ROWS, TQ), lambda i: head_map(i) + (0,)),
            _rows(SGU_WIDTH, TP),
            _rows(2 * d, TP),
        ],
        out_shape=[
            jax.ShapeDtypeStruct((bsz, FOX_HEADS, seq, LANES), BF16),
            jax.ShapeDtypeStruct((bsz, FOX_HEADS, seq, LANES), BF16),
            jax.ShapeDtypeStruct((bsz, FOX_HEADS, seq // TQ, VT_ROWS, TQ), BF16),
            jax.ShapeDtypeStruct((n_tok, SGU_WIDTH), BF16),
            jax.ShapeDtypeStruct((n_tok, 2 * d), BF16),
        ],
        scratch_shapes=[pltpu.VMEM((FOX_HEADS, SUB), F32),
                        pltpu.VMEM((d, _W_COLS), BF16),
                        pltpu.VMEM((GATE_ROWS, d), BF16),
                        pltpu.VMEM((2, PREP_COLS, d), F32),
                        pltpu.VMEM((FOX_HEADS, d), F32),
                        pltpu.SemaphoreType.DMA((3,))],
        compiler_params=_params(1, ordered=True),
        name="pre",
    )(xt, g1, wt, bf_pad, ln_g, ln_b, w_sgu, bm)


def _attn(qa, ka, vt, weights):
    bsz, _, seq, _ = qa.shape
    pairs = FOX_HEADS // PAIR
    n_steps = bsz * pairs
    assert all(w.shape[0] % (n_steps * BF16_SUBLANES) == 0 for w in weights)
    slab = lambda w: pl.BlockSpec((w.shape[0] // n_steps, w.shape[1]),
                                  lambda b, p: (b * pairs + p, 0))
    att, *cast = pl.pallas_call(
        functools.partial(_attn_kernel, n_q=seq // TQ, n_cast=len(weights)),
        grid=(bsz, pairs),
        in_specs=[
            pl.BlockSpec((1, PAIR, seq, LANES), lambda b, p: (b, p, 0, 0)),
            pl.BlockSpec((1, PAIR, seq, LANES), lambda b, p: (b, p, 0, 0)),
            pl.BlockSpec((1, PAIR, seq // TQ, VT_ROWS, TQ), lambda b, p: (b, p, 0, 0, 0)),
        ] + [slab(w) for w in weights],
        out_specs=[pl.BlockSpec((1, seq, LANES), lambda b, p: (b, 0, p))]
        + [slab(w) for w in weights],
        out_shape=[jax.ShapeDtypeStruct((bsz, seq, FOX_WIDTH), BF16)]
        + [jax.ShapeDtypeStruct(w.shape, BF16) for w in weights],
        scratch_shapes=[pltpu.VMEM((N_CHAINS, VT_ROWS, QC), F32),
                        pltpu.VMEM((N_CHAINS, TQ, QC), F32)],
        compiler_params=_params(2),
        name="attn",
    )(qa, ka, vt, *weights)
    return att, cast


def _post(xt, att, sb, gs, w_a, w_b, w_o, g2, w_up, w_down, gf):
    n_tok, d = xt.shape
    return pl.pallas_call(
        functools.partial(_post_kernel, ff_chunk=1024),
        grid=(n_tok // TM,),
        in_specs=[
            _rows(d),
            _rows(FOX_WIDTH),
            _rows(SGU_WIDTH),
            _rows(2 * d),
            _const((FOX_WIDTH, d)),
            _const((SGU_WIDTH, d)),
            _const((d, d)),
            _const((1, d)),
            _const((d, D_FF)),
            _const((D_FF, d)),
            _const((1, d)),
        ],
        out_specs=_rows(d),
        out_shape=jax.ShapeDtypeStruct((n_tok, d), F32),
        compiler_params=_params(1),
        name="post",
    )(xt, att, sb, gs, w_a, w_b, w_o, g2, w_up, w_down, gf)


def kernel(x, norm1_g, w_in, b_f, ln_v_g, ln_v_b, w_sgu, b_sgu, w_a, w_b, w_o,
           norm2_g, w_up, w_down, normf_g):
    bsz, seq, d = x.shape
    assert d == D_MODEL and seq % TP == 0 and TP % TQ == 0 and TQ % SUB == 0 and TM % SUB == 0
    assert norm1_g.shape[0] == 1, "single-layer block"
    n_tok = bsz * seq
    xt = x.reshape(n_tok, d)

    wt = jnp.swapaxes(w_in[0], 0, 1)
    bf_rows = jnp.pad(jnp.tile(b_f[0], N_PARTS), (0, GATE_ROWS - N_PARTS * FOX_HEADS))
    bf_pad = jnp.broadcast_to(bf_rows[:, None], (GATE_ROWS, SUB))
    bm = jnp.repeat(jnp.transpose(b_sgu[0]), HEAD_DIM, axis=1)

    qa, ka, vt, sb, gs = _pre(xt, bsz, seq, norm1_g.reshape(1, d), wt, bf_pad,
                              ln_v_g.reshape(1, SGU_WIDTH), ln_v_b.reshape(1, SGU_WIDTH),
                              w_sgu[0], bm)
    att, (wa16, wb16, wo16, wu16, wd16) = _attn(
        qa, ka, vt, [w_a[0], w_b[0], w_o[0], w_up[0], w_down[0]])
    out = _post(xt, att.reshape(n_tok, FOX_WIDTH), sb, gs, wa16, wb16, wo16,
                norm2_g.reshape(1, d), wu16, wd16, normf_g.reshape(1, d))
    return out.reshape(bsz, seq, d)
```

```python
import functools
import math

import jax
import jax.numpy as jnp
from jax import lax
from jax.experimental import pallas as pl
from jax.experimental.pallas import tpu as pltpu

D_MODEL = 1024
HEAD_DIM = 64
FOX_HEADS = 8
FOX_WIDTH = FOX_HEADS * HEAD_DIM
SGU_GROUPS = 8
SGU_WIDTH = 512
SGU_LEN = 128
CHUNK = 64
D_FF = 4 * D_MODEL
EPS = 1e-6

LANES = 128
TM = 512
TP = 512
TQ = 512
QC = 256
PAIR = 2
N_CHUNKS = TQ // QC
N_CHAINS = PAIR * N_CHUNKS
BF16_SUBLANES = 16
SUB = 256
VT_ROWS = 128
PREP_COLS = 768
LOG2E = math.log2(math.e)
VMEM_LIMIT = 56 * 1024 * 1024

_F0 = 3 * FOX_WIDTH
_Q0, _K0, _V0 = 0, FOX_WIDTH, 2 * FOX_WIDTH
_U0 = _F0
_GA0 = _U0 + 2 * SGU_WIDTH
_GB0 = _GA0 + D_MODEL
_W_COLS = _GB0 + D_MODEL
N_PARTS = 3
GATE_ROWS = 32

BF16 = jnp.bfloat16
F32 = jnp.float32


def _dot(a, b):
    return jnp.dot(a, b, preferred_element_type=F32)


def _gelu_tanh(x):
    c = math.sqrt(2.0 / math.pi)
    half = 0.5 * x
    return half * jnp.tanh(x * ((x * x) * (c * 0.044715) + c)) + half


def _sigmoid(x):
    return 0.5 * jnp.tanh(0.5 * x) + 0.5


def _split3(x):
    hi = x.astype(BF16).astype(F32)
    r = x - hi
    mid = r.astype(BF16).astype(F32)
    return hi, mid, r - mid


def _load_weights(wt_hbm, w_ref, wft_ref, stage_ref, gate_ref, sem):
    n_chunks = _W_COLS // PREP_COLS

    def chunk(c):
        first_row = c * PREP_COLS + (FOX_HEADS if c * PREP_COLS >= _F0 else 0)
        return pltpu.make_async_copy(wt_hbm.at[pl.ds(first_row, PREP_COLS), :],
                                     stage_ref.at[c % 2], sem.at[c % 2])

    gates = pltpu.make_async_copy(wt_hbm.at[pl.ds(_F0, FOX_HEADS), :], gate_ref, sem.at[2])
    gates.start()
    chunk(0).start()
    for c in range(n_chunks):
        if c + 1 < n_chunks:
            chunk(c + 1).start()
        chunk(c).wait()
        w_ref[:, c * PREP_COLS:(c + 1) * PREP_COLS] = stage_ref[c % 2].T.astype(BF16)
    gates.wait()
    g = gate_ref[...]
    pad = jnp.zeros((GATE_ROWS - N_PARTS * FOX_HEADS, g.shape[1]), g.dtype)
    wft_ref[...] = jnp.concatenate([g] * N_PARTS + [pad], axis=0).astype(BF16)


def _expand_biases(bfs_ref, bs_ref, bf_ref, bm_ref):
    grow = lax.broadcasted_iota(jnp.int32, bf_ref.shape, 0)
    bf = jnp.zeros(bf_ref.shape, F32)
    for h in range(FOX_HEADS):
        bf = jnp.where((grow % FOX_HEADS == h) & (grow < N_PARTS * FOX_HEADS), bfs_ref[0, h], bf)
    bf_ref[...] = bf
    b = bs_ref[...]
    bt = jnp.concatenate([b, jnp.zeros((LANES - SGU_GROUPS, SGU_LEN), F32)], axis=0).T
    lane = lax.broadcasted_iota(jnp.int32, (SGU_LEN, LANES), 1)
    for jp in range(SGU_GROUPS // 2):
        ga, gb = [jnp.broadcast_to(bt[:, g:g + 1], (SGU_LEN, LANES)) for g in (2 * jp, 2 * jp + 1)]
        bm_ref[:, LANES * jp:LANES * (jp + 1)] = jnp.where(lane < HEAD_DIM, ga, gb)


def _pre_kernel(x_ref, g1_ref, wt_hbm, bfs_ref, lng_ref, lnb_ref, wm_ref, bs_ref,
                qa_ref, ka_ref, vt_ref, sb_ref, gs_ref,
                carry_ref, w_ref, wft_ref, stage_ref, gate_ref, sem, bf_ref, bm_ref,
                *, tiles_per_seq):
    i = pl.program_id(0)

    @pl.when(i == 0)
    def _():
        _load_weights(wt_hbm, w_ref, wft_ref, stage_ref, gate_ref, sem)
        _expand_biases(bfs_ref, bs_ref, bf_ref, bm_ref)

    @pl.when(i % tiles_per_seq == 0)
    def _():
        carry_ref[...] = jnp.zeros_like(carry_ref)

    s_i = lax.broadcasted_iota(jnp.int32, (SUB, SUB), 0)
    t_i = lax.broadcasted_iota(jnp.int32, (SUB, SUB), 1)
    tri = jnp.where(s_i <= t_i, 1.0, 0.0).astype(BF16)
    grow = lax.broadcasted_iota(jnp.int32, (GATE_ROWS, SUB), 0)
    head_pad = jnp.zeros((LANES - FOX_HEADS, SUB), F32)
    lane = lax.broadcasted_iota(jnp.int32, (SUB, LANES), 1)
    low = lane < HEAD_DIM
    ones_row = jnp.where(lax.broadcasted_iota(jnp.int32, (VT_ROWS - HEAD_DIM, SUB), 0) == 0,
                         1.0, 0.0)
    n_win = SUB // SGU_LEN
    wi = lax.broadcasted_iota(jnp.int32, (SGU_LEN, SGU_LEN), 0) // CHUNK
    wj = lax.broadcasted_iota(jnp.int32, (SGU_LEN, SGU_LEN), 1) // CHUNK
    wmask = wj <= wi
    lane_w = lax.broadcasted_iota(jnp.int32, (SGU_LEN, n_win * LANES), 1)
    low_w = (lane_w % LANES) < HEAD_DIM

    def proj(h, c0, width):
        return _dot(h, w_ref[:, c0:c0 + width])

    carry = carry_ref[...]
    for sub in range(TP // SUB):
        rows = slice(sub * SUB, (sub + 1) * SUB)
        kblk, kcols = divmod(sub * SUB, TQ)
        x = x_ref[rows, :]
        ms = jnp.mean(x * x, axis=-1, keepdims=True)
        h = (x * lax.rsqrt(ms + EPS) * g1_ref[...]).astype(BF16)

        z = lax.dot_general(wft_ref[...], h, (((1,), (1,)), ((), ())),
                            preferred_element_type=F32) + bf_ref[...]
        usv = proj(h, _U0, 2 * SGU_WIDTH)
        k_all = proj(h, _K0, FOX_WIDTH)

        logf = jnp.minimum(z, 0.0) - jnp.log(1.0 + jnp.exp(-jnp.abs(z)))
        hi, mid, lo = _split3(logf)
        part = jnp.where(grow < FOX_HEADS, hi, jnp.where(grow < 2 * FOX_HEADS, mid, lo))
        cl = _dot(part.astype(BF16), tri)
        ct = carry + (cl[:FOX_HEADS] + cl[FOX_HEADS:2 * FOX_HEADS]
                      + cl[2 * FOX_HEADS:3 * FOX_HEADS])
        carry = jnp.broadcast_to(ct[:, SUB - 1:SUB], ct.shape)
        c = jnp.concatenate([ct, head_pad], axis=0).T
        nc_parts = _split3(c * (-LOG2E))

        q_all = proj(h, _Q0, FOX_WIDTH) * (HEAD_DIM ** -0.5 * LOG2E)
        g_a = proj(h, _GA0, D_MODEL)
        v_all = proj(h, _V0, FOX_WIDTH)

        u = _gelu_tanh(usv[:, :SGU_WIDTH])
        sv = _gelu_tanh(usv[:, SGU_WIDTH:])
        mu = jnp.mean(sv, axis=-1, keepdims=True)
        xc = sv - mu
        var = jnp.mean(xc * xc, axis=-1, keepdims=True)
        svn = (xc * lax.rsqrt(var + EPS) * lng_ref[...] + lnb_ref[...]).astype(BF16)
        for jp in range(SGU_GROUPS // 2):
            sl = slice(LANES * jp, LANES * (jp + 1))
            chunk = svn[:, sl]
            rhs = jnp.concatenate(
                [chunk[SGU_LEN * w:SGU_LEN * (w + 1), :] for w in range(n_win)], axis=1)
            wa = jnp.where(wmask, wm_ref[2 * jp], 0.0).astype(BF16)
            wb = jnp.where(wmask, wm_ref[2 * jp + 1], 0.0).astype(BF16)
            zero = jnp.zeros_like(rhs)
            stacked = jnp.concatenate([jnp.where(low_w, rhs, zero), jnp.where(low_w, zero, rhs)],
                                      axis=0)
            mixed = _dot(jnp.concatenate([wa, wb], axis=1), stacked)
            mixed = mixed + jnp.concatenate([bm_ref[:, sl]] * n_win, axis=1)
            mixed = jnp.concatenate(
                [mixed[:, LANES * w:LANES * (w + 1)] for w in range(n_win)], axis=0)
            sb_ref[rows, sl] = (u[:, sl] * mixed).astype(BF16)
        gs_ref[rows, :D_MODEL] = g_a.astype(BF16)

        g_b = proj(h, _GB0, D_MODEL)
        for jp in range(FOX_HEADS // 2):
            sl = slice(LANES * jp, LANES * (jp + 1))
            qc, kc = q_all[:, sl], k_all[:, sl]
            vt = v_all[:, sl].T
            for par in range(2):
                hd = 2 * jp + par
                a0 = HEAD_DIM if par == 0 else 0
                data = low if par == 0 else jnp.logical_not(low)
                in_aug = (lane >= a0) & (lane < a0 + N_PARTS)
                qa = jnp.where(data, qc, jnp.where(in_aug, 1.0, 0.0))
                hi, mid, lo = [jnp.broadcast_to(p[:, hd:hd + 1], (SUB, LANES)) for p in nc_parts]
                aug = jnp.where(lane == a0, hi,
                                jnp.where(lane == a0 + 1, mid,
                                          jnp.where(lane == a0 + 2, lo, 0.0)))
                ka = jnp.where(data, kc, aug)
                qa_ref[0, hd, rows, :] = qa.astype(BF16)
                ka_ref[0, hd, rows, :] = ka.astype(BF16)
                vta = jnp.concatenate([vt[par * HEAD_DIM:(par + 1) * HEAD_DIM], ones_row], axis=0)
                vt_ref[0, hd, kblk, :, kcols:kcols + SUB] = vta.astype(BF16)
        gs_ref[rows, D_MODEL:] = g_b.astype(BF16)
    carry_ref[...] = carry


def _attn_kernel(qa_ref, ka_ref, vt_ref, *refs, n_q, n_cast):
    cast_in, (o_ref, *cast_out), (acc_ref, st_ref) = (
        refs[:n_cast], refs[n_cast:2 * n_cast + 1], refs[2 * n_cast + 1:])
    for src, dst in zip(cast_in, cast_out):
        dst[...] = src[...].astype(BF16)

    k_i = lax.broadcasted_iota(jnp.int32, (QC, QC), 0)
    q_i = lax.broadcasted_iota(jnp.int32, (QC, QC), 1)
    causal = k_i <= q_i
    halves = [slice(h * QC, (h + 1) * QC) for h in range(N_CHUNKS)]

    def needed(c, tile, blk, half):
        return not (blk == tile and half > c)

    def score_half(ch, tile, blk, half):
        hd, c = divmod(ch, N_CHUNKS)
        q0 = tile * TQ + c * QC
        k0 = blk * TQ + half * QC
        st = lax.dot_general(ka_ref[0, hd, pl.ds(k0, QC), :], qa_ref[0, hd, pl.ds(q0, QC), :],
                             (((1,), (1,)), ((), ())), preferred_element_type=F32)
        if blk == tile and half == c:
            st = jnp.where(causal, st, -jnp.inf)
        st_ref[ch, halves[half], :] = st
        return jnp.max(st, axis=0, keepdims=True)

    def chain_step(ch, tile, blk, m_old, cm, nxt):
        hd, c = divmod(ch, N_CHUNKS)
        m_new = cm if m_old is None else jnp.maximum(m_old, cm)
        pv, cm_next = [], []
        for half in range(N_CHUNKS):
            use = needed(c, tile, blk, half)
            if use:
                p = jnp.exp2(st_ref[ch, halves[half], :] - m_new).astype(BF16)
            if nxt is not None and needed(c, *nxt, half):
                cm_next.append(score_half(ch, *nxt, half))
            if use:
                pv.append(_dot(vt_ref[0, hd, blk, :, halves[half]], p))
        pv = sum(pv[1:], pv[0])
        if m_old is None:
            acc_ref[ch] = pv
        else:
            acc_ref[ch] = acc_ref[ch] * jnp.exp2(m_old - m_new) + pv
        return m_new, (functools.reduce(jnp.maximum, cm_next) if cm_next else None)

    steps = [(qi, j) for qi in range(n_q) for j in range(qi + 1)]
    ms = [None] * N_CHAINS
    cms = [functools.reduce(jnp.maximum,
                            [score_half(ch, *steps[0], half) for half in range(N_CHUNKS)
                             if needed(ch % N_CHUNKS, *steps[0], half)])
           for ch in range(N_CHAINS)]
    for n, (qi, j) in enumerate(steps):
        nxt = steps[n + 1] if n + 1 < len(steps) else None
        for ch in range(N_CHAINS):
            ms[ch], cms[ch] = chain_step(ch, qi, j, ms[ch], cms[ch], nxt)
        if j == qi:
            for c in range(N_CHUNKS):
                chains = [N_CHUNKS * hd + c for hd in range(PAIR)]
                o_t = jnp.concatenate(
                    [acc_ref[ch, :HEAD_DIM, :] / acc_ref[ch, HEAD_DIM:HEAD_DIM + 1, :]
                     for ch in chains], axis=0)
                o_ref[0, pl.ds(qi * TQ + c * QC, QC), :] = o_t.T.astype(BF16)
            ms = [None] * N_CHAINS


def _post_kernel(x_ref, at_ref, sb_ref, gs_ref, wa_ref, wb_ref, wo_ref, g2_ref, wu_ref,
                 wd_ref, gf_ref, o_ref, *, ff_chunk):
    subs = [slice(s * SUB, (s + 1) * SUB) for s in range(TM // SUB)]
    merged = []
    for rows in subs:
        ya = _dot(at_ref[rows, :], wa_ref[...])
        yb = _dot(sb_ref[rows, :], wb_ref[...])
        merged.append((_sigmoid(gs_ref[rows, :D_MODEL].astype(F32)) * ya
                       + _sigmoid(gs_ref[rows, D_MODEL:].astype(F32)) * yb).astype(BF16))
    ys = [x_ref[rows, :] + _dot(m, wo_ref[...]) for rows, m in zip(subs, merged)]
    hs = []
    for y in ys:
        ms = jnp.mean(y * y, axis=-1, keepdims=True)
        hs.append((y * lax.rsqrt(ms + EPS) * g2_ref[...]).astype(BF16))
    for c0 in range(0, D_FF, ff_chunk):
        acts = []
        for h in hs:
            a = jnp.maximum(_dot(h, wu_ref[:, c0:c0 + ff_chunk]), 0.0)
            acts.append((a * a).astype(BF16))
        ys = [y + _dot(a, wd_ref[c0:c0 + ff_chunk, :]) for y, a in zip(ys, acts)]
    for rows, y in zip(subs, ys):
        ms = jnp.mean(y * y, axis=-1, keepdims=True)
        o_ref[rows, :] = y * lax.rsqrt(ms + EPS) * gf_ref[...]


def _params(n_axes, ordered=False):
    semantics = "arbitrary" if ordered else "parallel"
    return pltpu.CompilerParams(dimension_semantics=(semantics,) * n_axes,
                                vmem_limit_bytes=VMEM_LIMIT)


def _const(shape):
    return pl.BlockSpec(shape, lambda *_: (0,) * len(shape), pipeline_mode=pl.Buffered(1))


def _rows(width, rows=TM):
    return pl.BlockSpec((rows, width), lambda i: (i, 0))


def _pre(xt, bsz, seq, g1, wt, b_f, ln_g, ln_b, w_sgu, b_sgu):
    n_tok, d = xt.shape
    assert wt.shape == (_W_COLS + FOX_HEADS, d) and _F0 % PREP_COLS == 0 and _W_COLS % PREP_COLS == 0
    tiles_per_seq = seq // TP
    head_map = lambda i: (i // tiles_per_seq, 0, i % tiles_per_seq, 0)
    return pl.pallas_call(
        functools.partial(_pre_kernel, tiles_per_seq=tiles_per_seq),
        grid=(n_tok // TP,),
        in_specs=[
            _rows(d, TP),
            _const((1, d)),
            pl.BlockSpec(memory_space=pl.ANY),
            pl.BlockSpec(memory_space=pltpu.SMEM),
            _const((1, SGU_WIDTH)),
            _const((1, SGU_WIDTH)),
            _const((SGU_GROUPS, SGU_LEN, SGU_LEN)),
            _const((SGU_GROUPS, SGU_LEN)),
        ],
        out_specs=[
            pl.BlockSpec((1, FOX_HEADS, TP, LANES), head_map),
            pl.BlockSpec((1, FOX_HEADS, TP, LANES), head_map),
            pl.BlockSpec((1, FOX_HEADS, TP // TQ, VT_ROWS, TQ), lambda i: head_map(i) + (0,)),
            _rows(SGU_WIDTH, TP),
            _rows(2 * d, TP),
        ],
        out_shape=[
            jax.ShapeDtypeStruct((bsz, FOX_HEADS, seq, LANES), BF16),
            jax.ShapeDtypeStruct((bsz, FOX_HEADS, seq, LANES), BF16),
            jax.ShapeDtypeStruct((bsz, FOX_HEADS, seq // TQ, VT_ROWS, TQ), BF16),
            jax.ShapeDtypeStruct((n_tok, SGU_WIDTH), BF16),
            jax.ShapeDtypeStruct((n_tok, 2 * d), BF16),
        ],
        scratch_shapes=[pltpu.VMEM((FOX_HEADS, SUB), F32),
                        pltpu.VMEM((d, _W_COLS), BF16),
                        pltpu.VMEM((GATE_ROWS, d), BF16),
                        pltpu.VMEM((2, PREP_COLS, d), F32),
                        pltpu.VMEM((FOX_HEADS, d), F32),
                        pltpu.SemaphoreType.DMA((3,)),
                        pltpu.VMEM((GATE_ROWS, SUB), F32),
                        pltpu.VMEM((SGU_LEN, SGU_WIDTH), F32)],
        compiler_params=_params(1, ordered=True),
        name="pre",
    )(xt, g1, wt, b_f, ln_g, ln_b, w_sgu, b_sgu)


def _attn(qa, ka, vt, weights):
    bsz, _, seq, _ = qa.shape
    pairs = FOX_HEADS // PAIR
    n_steps = bsz * pairs
    assert all(w.shape[0] % (n_steps * BF16_SUBLANES) == 0 for w in weights)
    slab = lambda w: pl.BlockSpec((w.shape[0] // n_steps, w.shape[1]),
                                  lambda b, p: (b * pairs + p, 0))
    att, *cast = pl.pallas_call(
        functools.partial(_attn_kernel, n_q=seq // TQ, n_cast=len(weights)),
        grid=(bsz, pairs),
        in_specs=[
            pl.BlockSpec((1, PAIR, seq, LANES), lambda b, p: (b, p, 0, 0)),
            pl.BlockSpec((1, PAIR, seq, LANES), lambda b, p: (b, p, 0, 0)),
            pl.BlockSpec((1, PAIR, seq // TQ, VT_ROWS, TQ), lambda b, p: (b, p, 0, 0, 0)),
        ] + [slab(w) for w in weights],
        out_specs=[pl.BlockSpec((1, seq, LANES), lambda b, p: (b, 0, p))]
        + [slab(w) for w in weights],
        out_shape=[jax.ShapeDtypeStruct((bsz, seq, FOX_WIDTH), BF16)]
        + [jax.ShapeDtypeStruct(w.shape, BF16) for w in weights],
        scratch_shapes=[pltpu.VMEM((N_CHAINS, VT_ROWS, QC), F32),
                        pltpu.VMEM((N_CHAINS, TQ, QC), F32)],
        compiler_params=_params(2),
        name="attn",
    )(qa, ka, vt, *weights)
    return att, cast


def _post(xt, att, sb, gs, w_a, w_b, w_o, g2, w_up, w_down, gf):
    n_tok, d = xt.shape
    return pl.pallas_call(
        functools.partial(_post_kernel, ff_chunk=1024),
        grid=(n_tok // TM,),
        in_specs=[
            _rows(d),
            _rows(FOX_WIDTH),
            _rows(SGU_WIDTH),
            _rows(2 * d),
            _const((FOX_WIDTH, d)),
            _const((SGU_WIDTH, d)),
            _const((d, d)),
            _const((1, d)),
            _const((d, D_FF)),
            _const((D_FF, d)),
            _const((1, d)),
        ],
        out_specs=_rows(d),
        out_shape=jax.ShapeDtypeStruct((n_tok, d), F32),
        compiler_params=_params(1),
        name="post",
    )(xt, att, sb, gs, w_a, w_b, w_o, g2, w_up, w_down, gf)


def kernel(x, norm1_g, w_in, b_f, ln_v_g, ln_v_b, w_sgu, b_sgu, w_a, w_b, w_o,
           norm2_g, w_up, w_down, normf_g):
    bsz, seq, d = x.shape
    assert d == D_MODEL and seq % TP == 0 and TP % TQ == 0 and TQ % SUB == 0 and TM % SUB == 0
    assert norm1_g.shape[0] == 1, "single-layer block"
    n_tok = bsz * seq
    xt = x.reshape(n_tok, d)

    wt = jnp.swapaxes(w_in[0], 0, 1)

    qa, ka, vt, sb, gs = _pre(xt, bsz, seq, norm1_g.reshape(1, d), wt, b_f,
                              ln_v_g.reshape(1, SGU_WIDTH), ln_v_b.reshape(1, SGU_WIDTH),
                              w_sgu[0], b_sgu[0])
    att, (wa16, wb16, wo16, wu16, wd16) = _attn(
        qa, ka, vt, [w_a[0], w_b[0], w_o[0], w_up[0], w_down[0]])
    out = _post(xt, att.reshape(n_tok, FOX_WIDTH), sb, gs, wa16, wb16, wo16,
                norm2_g.reshape(1, d), wu16, wd16, normf_g.reshape(1, d))
    return out.reshape(bsz, seq, d)
```

```python
import functools
import math

import jax
import jax.numpy as jnp
from jax import lax
from jax.experimental import pallas as pl
from jax.experimental.pallas import tpu as pltpu

D_MODEL = 1024
HEAD_DIM = 64
FOX_HEADS = 8
FOX_WIDTH = FOX_HEADS * HEAD_DIM
SGU_GROUPS = 8
SGU_WIDTH = 512
SGU_LEN = 128
CHUNK = 64
D_FF = 4 * D_MODEL
EPS = 1e-6

LANES = 128
TM = 512
TP = 512
TQ = 512
QC = 256
PAIR = 2
N_CHUNKS = TQ // QC
N_CHAINS = PAIR * N_CHUNKS
BF16_SUBLANES = 16
SUB = 256
VT_ROWS = 128
LOG2E = math.log2(math.e)
VMEM_LIMIT = 56 * 1024 * 1024

_F0 = 3 * FOX_WIDTH
_Q0, _K0, _V0 = 0, FOX_WIDTH, 2 * FOX_WIDTH
_U0 = _F0
_GA0 = _U0 + 2 * SGU_WIDTH
_GB0 = _GA0 + D_MODEL
_W_COLS = _GB0 + D_MODEL
_W_PIECES = ((_U0, 2 * SGU_WIDTH), (_K0, FOX_WIDTH), (_Q0, FOX_WIDTH), (_GA0, D_MODEL),
             (_V0, FOX_WIDTH), (_GB0, D_MODEL))
N_PARTS = 3
GATE_ROWS = 32

BF16 = jnp.bfloat16
F32 = jnp.float32


def _dot(a, b):
    return jnp.dot(a, b, preferred_element_type=F32)


def _gelu_tanh(x):
    c = math.sqrt(2.0 / math.pi)
    half = 0.5 * x
    return half * jnp.tanh(x * ((x * x) * (c * 0.044715) + c)) + half


def _sigmoid(x):
    return 0.5 * jnp.tanh(0.5 * x) + 0.5


def _split3(x):
    hi = x.astype(BF16).astype(F32)
    r = x - hi
    mid = r.astype(BF16).astype(F32)
    return hi, mid, r - mid


def _weight_copies(wt_hbm, stage_ref, gate_ref, sem):
    copies = {}
    for n, (c0, width) in enumerate(_W_PIECES):
        first_row = c0 + (FOX_HEADS if c0 >= _F0 else 0)
        copies[c0] = (pltpu.make_async_copy(wt_hbm.at[pl.ds(first_row, width), :],
                                            stage_ref.at[pl.ds(c0, width), :], sem.at[n]), width)
    gates = pltpu.make_async_copy(wt_hbm.at[pl.ds(_F0, FOX_HEADS), :], gate_ref,
                                  sem.at[len(_W_PIECES)])
    return copies, gates


def _expand_biases(bfs_ref, bs_ref, bf_ref, bm_ref):
    grow = lax.broadcasted_iota(jnp.int32, bf_ref.shape, 0)
    bf = jnp.zeros(bf_ref.shape, F32)
    for h in range(FOX_HEADS):
        bf = jnp.where((grow % FOX_HEADS == h) & (grow < N_PARTS * FOX_HEADS), bfs_ref[0, h], bf)
    bf_ref[...] = bf
    b = bs_ref[...]
    bt = jnp.concatenate([b, jnp.zeros((LANES - SGU_GROUPS, SGU_LEN), F32)], axis=0).T
    lane = lax.broadcasted_iota(jnp.int32, (SGU_LEN, LANES), 1)
    for jp in range(SGU_GROUPS // 2):
        ga, gb = [jnp.broadcast_to(bt[:, g:g + 1], (SGU_LEN, LANES)) for g in (2 * jp, 2 * jp + 1)]
        bm_ref[:, LANES * jp:LANES * (jp + 1)] = jnp.where(lane < HEAD_DIM, ga, gb)


def _pre_step(x_ref, g1_ref, lng_ref, lnb_ref, wm_ref, qa_ref, ka_ref, vt_ref, sb_ref, gs_ref,
              carry_ref, w_ref, wft_ref, bf_ref, bm_ref, fetch):
    s_i = lax.broadcasted_iota(jnp.int32, (SUB, SUB), 0)
    t_i = lax.broadcasted_iota(jnp.int32, (SUB, SUB), 1)
    tri = jnp.where(s_i <= t_i, 1.0, 0.0).astype(BF16)
    grow = lax.broadcasted_iota(jnp.int32, (GATE_ROWS, SUB), 0)
    head_pad = jnp.zeros((LANES - FOX_HEADS, SUB), F32)
    lane = lax.broadcasted_iota(jnp.int32, (SUB, LANES), 1)
    low = lane < HEAD_DIM
    ones_row = jnp.where(lax.broadcasted_iota(jnp.int32, (VT_ROWS - HEAD_DIM, SUB), 0) == 0,
                         1.0, 0.0)
    n_win = SUB // SGU_LEN
    wi = lax.broadcasted_iota(jnp.int32, (SGU_LEN, SGU_LEN), 0) // CHUNK
    wj = lax.broadcasted_iota(jnp.int32, (SGU_LEN, SGU_LEN), 1) // CHUNK
    wmask = wj <= wi
    lane_w = lax.broadcasted_iota(jnp.int32, (SGU_LEN, n_win * LANES), 1)
    low_w = (lane_w % LANES) < HEAD_DIM

    def proj(h, c0, width):
        fetch(c0)
        return _dot(h, w_ref[:, c0:c0 + width])

    carry = carry_ref[...]
    for sub in range(TP // SUB):
        rows = slice(sub * SUB, (sub + 1) * SUB)
        kblk, kcols = divmod(sub * SUB, TQ)
        x = x_ref[rows, :]
        ms = jnp.mean(x * x, axis=-1, keepdims=True)
        h = (x * lax.rsqrt(ms + EPS) * g1_ref[...]).astype(BF16)

        z = lax.dot_general(wft_ref[...], h, (((1,), (1,)), ((), ())),
                            preferred_element_type=F32) + bf_ref[...]
        usv = proj(h, _U0, 2 * SGU_WIDTH)
        k_all = proj(h, _K0, FOX_WIDTH)

        logf = jnp.minimum(z, 0.0) - jnp.log(1.0 + jnp.exp(-jnp.abs(z)))
        hi, mid, lo = _split3(logf)
        part = jnp.where(grow < FOX_HEADS, hi, jnp.where(grow < 2 * FOX_HEADS, mid, lo))
        cl = _dot(part.astype(BF16), tri)
        ct = carry + (cl[:FOX_HEADS] + cl[FOX_HEADS:2 * FOX_HEADS]
                      + cl[2 * FOX_HEADS:3 * FOX_HEADS])
        carry = jnp.broadcast_to(ct[:, SUB - 1:SUB], ct.shape)
        c = jnp.concatenate([ct, head_pad], axis=0).T
        nc_parts = _split3(c * (-LOG2E))

        q_all = proj(h, _Q0, FOX_WIDTH) * (HEAD_DIM ** -0.5 * LOG2E)
        g_a = proj(h, _GA0, D_MODEL)
        v_all = proj(h, _V0, FOX_WIDTH)

        u = _gelu_tanh(usv[:, :SGU_WIDTH])
        sv = _gelu_tanh(usv[:, SGU_WIDTH:])
        mu = jnp.mean(sv, axis=-1, keepdims=True)
        xc = sv - mu
        var = jnp.mean(xc * xc, axis=-1, keepdims=True)
        svn = (xc * lax.rsqrt(var + EPS) * lng_ref[...] + lnb_ref[...]).astype(BF16)
        for jp in range(SGU_GROUPS // 2):
            sl = slice(LANES * jp, LANES * (jp + 1))
            chunk = svn[:, sl]
            rhs = jnp.concatenate(
                [chunk[SGU_LEN * w:SGU_LEN * (w + 1), :] for w in range(n_win)], axis=1)
            wa = jnp.where(wmask, wm_ref[2 * jp], 0.0).astype(BF16)
            wb = jnp.where(wmask, wm_ref[2 * jp + 1], 0.0).astype(BF16)
            zero = jnp.zeros_like(rhs)
            stacked = jnp.concatenate([jnp.where(low_w, rhs, zero), jnp.where(low_w, zero, rhs)],
                                      axis=0)
            mixed = _dot(jnp.concatenate([wa, wb], axis=1), stacked)
            mixed = mixed + jnp.concatenate([bm_ref[:, sl]] * n_win, axis=1)
            mixed = jnp.concatenate(
                [mixed[:, LANES * w:LANES * (w + 1)] for w in range(n_win)], axis=0)
            sb_ref[rows, sl] = (u[:, sl] * mixed).astype(BF16)
        gs_ref[rows, :D_MODEL] = g_a.astype(BF16)

        g_b = proj(h, _GB0, D_MODEL)
        for jp in range(FOX_HEADS // 2):
            sl = slice(LANES * jp, LANES * (jp + 1))
            qc, kc = q_all[:, sl], k_all[:, sl]
            vt = v_all[:, sl].T
            for par in range(2):
                hd = 2 * jp + par
                a0 = HEAD_DIM if par == 0 else 0
                data = low if par == 0 else jnp.logical_not(low)
                in_aug = (lane >= a0) & (lane < a0 + N_PARTS)
                qa = jnp.where(data, qc, jnp.where(in_aug, 1.0, 0.0))
                hi, mid, lo = [jnp.broadcast_to(p[:, hd:hd + 1], (SUB, LANES)) for p in nc_parts]
                aug = jnp.where(lane == a0, hi,
                                jnp.where(lane == a0 + 1, mid,
                                          jnp.where(lane == a0 + 2, lo, 0.0)))
                ka = jnp.where(data, kc, aug)
                qa_ref[0, hd, rows, :] = qa.astype(BF16)
                ka_ref[0, hd, rows, :] = ka.astype(BF16)
                vta = jnp.concatenate([vt[par * HEAD_DIM:(par + 1) * HEAD_DIM], ones_row], axis=0)
                vt_ref[0, hd, kblk, :, kcols:kcols + SUB] = vta.astype(BF16)
        gs_ref[rows, D_MODEL:] = g_b.astype(BF16)
    carry_ref[...] = carry


def _pre_kernel(x_ref, g1_ref, wt_hbm, bfs_ref, lng_ref, lnb_ref, wm_ref, bs_ref,
                qa_ref, ka_ref, vt_ref, sb_ref, gs_ref,
                carry_ref, w_ref, wft_ref, stage_ref, gate_ref, sem, bf_ref, bm_ref,
                *, tiles_per_seq):
    i = pl.program_id(0)
    step = functools.partial(_pre_step, x_ref, g1_ref, lng_ref, lnb_ref, wm_ref, qa_ref, ka_ref,
                             vt_ref, sb_ref, gs_ref, carry_ref, w_ref, wft_ref, bf_ref, bm_ref)

    @pl.when(i % tiles_per_seq == 0)
    def _():
        carry_ref[...] = jnp.zeros_like(carry_ref)

    @pl.when(i == 0)
    def _():
        copies, gates = _weight_copies(wt_hbm, stage_ref, gate_ref, sem)
        gates.start()
        for copy, _ in copies.values():
            copy.start()
        _expand_biases(bfs_ref, bs_ref, bf_ref, bm_ref)
        gates.wait()
        g = gate_ref[...]
        pad = jnp.zeros((GATE_ROWS - N_PARTS * FOX_HEADS, g.shape[1]), g.dtype)
        wft_ref[...] = jnp.concatenate([g] * N_PARTS + [pad], axis=0).astype(BF16)

        def fetch(c0):
            if c0 in copies:
                copy, width = copies.pop(c0)
                copy.wait()
                w_ref[:, c0:c0 + width] = stage_ref[c0:c0 + width, :].T.astype(BF16)

        step(fetch)
        assert not copies, "every started copy is waited in the first step"

    @pl.when(i != 0)
    def _():
        step(lambda c0: None)


def _attn_kernel(qa_ref, ka_ref, vt_ref, *refs, n_q, n_cast):
    cast_in, (o_ref, *cast_out), (acc_ref, st_ref) = (
        refs[:n_cast], refs[n_cast:2 * n_cast + 1], refs[2 * n_cast + 1:])
    for src, dst in zip(cast_in, cast_out):
        dst[...] = src[...].astype(BF16)

    k_i = lax.broadcasted_iota(jnp.int32, (QC, QC), 0)
    q_i = lax.broadcasted_iota(jnp.int32, (QC, QC), 1)
    causal = k_i <= q_i
    halves = [slice(h * QC, (h + 1) * QC) for h in range(N_CHUNKS)]

    def needed(c, tile, blk, half):
        return not (blk == tile and half > c)

    def score_half(ch, tile, blk, half):
        hd, c = divmod(ch, N_CHUNKS)
        q0 = tile * TQ + c * QC
        k0 = blk * TQ + half * QC
        st = lax.dot_general(ka_ref[0, hd, pl.ds(k0, QC), :], qa_ref[0, hd, pl.ds(q0, QC), :],
                             (((1,), (1,)), ((), ())), preferred_element_type=F32)
        if blk == tile and half == c:
            st = jnp.where(causal, st, -jnp.inf)
        st_ref[ch, halves[half], :] = st
        return jnp.max(st, axis=0, keepdims=True)

    def chain_step(ch, tile, blk, m_old, cm, nxt):
        hd, c = divmod(ch, N_CHUNKS)
        m_new = cm if m_old is None else jnp.maximum(m_old, cm)
        pv, cm_next = [], []
        for half in range(N_CHUNKS):
            use = needed(c, tile, blk, half)
            if use:
                p = jnp.exp2(st_ref[ch, halves[half], :] - m_new).astype(BF16)
            if nxt is not None and needed(c, *nxt, half):
                cm_next.append(score_half(ch, *nxt, half))
            if use:
                pv.append(_dot(vt_ref[0, hd, blk, :, halves[half]], p))
        pv = sum(pv[1:], pv[0])
        if m_old is None:
            acc_ref[ch] = pv
        else:
            acc_ref[ch] = acc_ref[ch] * jnp.exp2(m_old - m_new) + pv
        return m_new, (functools.reduce(jnp.maximum, cm_next) if cm_next else None)

    steps = [(qi, j) for qi in range(n_q) for j in range(qi + 1)]
    ms = [None] * N_CHAINS
    cms = [functools.reduce(jnp.maximum,
                            [score_half(ch, *steps[0], half) for half in range(N_CHUNKS)
                             if needed(ch % N_CHUNKS, *steps[0], half)])
           for ch in range(N_CHAINS)]
    for n, (qi, j) in enumerate(steps):
        nxt = steps[n + 1] if n + 1 < len(steps) else None
        for ch in range(N_CHAINS):
            ms[ch], cms[ch] = chain_step(ch, qi, j, ms[ch], cms[ch], nxt)
        if j == qi:
            for c in range(N_CHUNKS):
                chains = [N_CHUNKS * hd + c for hd in range(PAIR)]
                o_t = jnp.concatenate(
                    [acc_ref[ch, :HEAD_DIM, :] / acc_ref[ch, HEAD_DIM:HEAD_DIM + 1, :]
                     for ch in chains], axis=0)
                o_ref[0, pl.ds(qi * TQ + c * QC, QC), :] = o_t.T.astype(BF16)
            ms = [None] * N_CHAINS


def _post_kernel(x_ref, at_ref, sb_ref, gs_ref, wa_ref, wb_ref, wo_ref, g2_ref, wu_ref,
                 wd_ref, gf_ref, o_ref, *, ff_chunk):
    subs = [slice(s * SUB, (s + 1) * SUB) for s in range(TM // SUB)]
    merged = []
    for rows in subs:
        ya = _dot(at_ref[rows, :], wa_ref[...])
        yb = _dot(sb_ref[rows, :], wb_ref[...])
        merged.append((_sigmoid(gs_ref[rows, :D_MODEL].astype(F32)) * ya
                       + _sigmoid(gs_ref[rows, D_MODEL:].astype(F32)) * yb).astype(BF16))
    ys = [x_ref[rows, :] + _dot(m, wo_ref[...]) for rows, m in zip(subs, merged)]
    hs = []
    for y in ys:
        ms = jnp.mean(y * y, axis=-1, keepdims=True)
        hs.append((y * lax.rsqrt(ms + EPS) * g2_ref[...]).astype(BF16))
    for c0 in range(0, D_FF, ff_chunk):
        acts = []
        for h in hs:
            a = jnp.maximum(_dot(h, wu_ref[:, c0:c0 + ff_chunk]), 0.0)
            acts.append((a * a).astype(BF16))
        ys = [y + _dot(a, wd_ref[c0:c0 + ff_chunk, :]) for y, a in zip(ys, acts)]
    for rows, y in zip(subs, ys):
        ms = jnp.mean(y * y, axis=-1, keepdims=True)
        o_ref[rows, :] = y * lax.rsqrt(ms + EPS) * gf_ref[...]


def _params(n_axes, ordered=False):
    semantics = "arbitrary" if ordered else "parallel"
    return pltpu.CompilerParams(dimension_semantics=(semantics,) * n_axes,
                                vmem_limit_bytes=VMEM_LIMIT)


def _const(shape):
    return pl.BlockSpec(shape, lambda *_: (0,) * len(shape), pipeline_mode=pl.Buffered(1))


def _rows(width, rows=TM):
    return pl.BlockSpec((rows, width), lambda i: (i, 0))


def _pre(xt, bsz, seq, g1, wt, b_f, ln_g, ln_b, w_sgu, b_sgu):
    n_tok, d = xt.shape
    assert wt.shape == (_W_COLS + FOX_HEADS, d)
    tiles_per_seq = seq // TP
    head_map = lambda i: (i // tiles_per_seq, 0, i % tiles_per_seq, 0)
    return pl.pallas_call(
        functools.partial(_pre_kernel, tiles_per_seq=tiles_per_seq),
        grid=(n_tok // TP,),
        in_specs=[
            _rows(d, TP),
            _const((1, d)),
            pl.BlockSpec(memory_space=pl.ANY),
            pl.BlockSpec(memory_space=pltpu.SMEM),
            _const((1, SGU_WIDTH)),
            _const((1, SGU_WIDTH)),
            _const((SGU_GROUPS, SGU_LEN, SGU_LEN)),
            _const((SGU_GROUPS, SGU_LEN)),
        ],
        out_specs=[
            pl.BlockSpec((1, FOX_HEADS, TP, LANES), head_map),
            pl.BlockSpec((1, FOX_HEADS, TP, LANES), head_map),
            pl.BlockSpec((1, FOX_HEADS, TP // TQ, VT_ROWS, TQ), lambda i: head_map(i) + (0,)),
            _rows(SGU_WIDTH, TP),
            _rows(2 * d, TP),
        ],
        out_shape=[
            jax.ShapeDtypeStruct((bsz, FOX_HEADS, seq, LANES), BF16),
            jax.ShapeDtypeStruct((bsz, FOX_HEADS, seq, LANES), BF16),
            jax.ShapeDtypeStruct((bsz, FOX_HEADS, seq // TQ, VT_ROWS, TQ), BF16),
            jax.ShapeDtypeStruct((n_tok, SGU_WIDTH), BF16),
            jax.ShapeDtypeStruct((n_tok, 2 * d), BF16),
        ],
        scratch_shapes=[pltpu.VMEM((FOX_HEADS, SUB), F32),
                        pltpu.VMEM((d, _W_COLS), BF16),
                        pltpu.VMEM((GATE_ROWS, d), BF16),
                        pltpu.VMEM((_W_COLS, d), F32),
                        pltpu.VMEM((FOX_HEADS, d), F32),
                        pltpu.SemaphoreType.DMA((len(_W_PIECES) + 1,)),
                        pltpu.VMEM((GATE_ROWS, SUB), F32),
                        pltpu.VMEM((SGU_LEN, SGU_WIDTH), F32)],
        compiler_params=_params(1, ordered=True),
        name="pre",
    )(xt, g1, wt, b_f, ln_g, ln_b, w_sgu, b_sgu)


def _attn(qa, ka, vt, weights):
    bsz, _, seq, _ = qa.shape
    pairs = FOX_HEADS // PAIR
    n_steps = bsz * pairs
    assert all(w.shape[0] % (n_steps * BF16_SUBLANES) == 0 for w in weights)
    slab = lambda w: pl.BlockSpec((w.shape[0] // n_steps, w.shape[1]),
                                  lambda b, p: (b * pairs + p, 0))
    att, *cast = pl.pallas_call(
        functools.partial(_attn_kernel, n_q=seq // TQ, n_cast=len(weights)),
        grid=(bsz, pairs),
        in_specs=[
            pl.BlockSpec((1, PAIR, seq, LANES), lambda b, p: (b, p, 0, 0)),
            pl.BlockSpec((1, PAIR, seq, LANES), lambda b, p: (b, p, 0, 0)),
            pl.BlockSpec((1, PAIR, seq // TQ, VT_ROWS, TQ), lambda b, p: (b, p, 0, 0, 0)),
        ] + [slab(w) for w in weights],
        out_specs=[pl.BlockSpec((1, seq, LANES), lambda b, p: (b, 0, p))]
        + [slab(w) for w in weights],
        out_shape=[jax.ShapeDtypeStruct((bsz, seq, FOX_WIDTH), BF16)]
        + [jax.ShapeDtypeStruct(w.shape, BF16) for w in weights],
        scratch_shapes=[pltpu.VMEM((N_CHAINS, VT_ROWS, QC), F32),
                        pltpu.VMEM((N_CHAINS, TQ, QC), F32)],
        compiler_params=_params(2),
        name="attn",
    )(qa, ka, vt, *weights)
    return att, cast


def _post(xt, att, sb, gs, w_a, w_b, w_o, g2, w_up, w_down, gf):
    n_tok, d = xt.shape
    return pl.pallas_call(
        functools.partial(_post_kernel, ff_chunk=1024),
        grid=(n_tok // TM,),
        in_specs=[
            _rows(d),
            _rows(FOX_WIDTH),
            _rows(SGU_WIDTH),
            _rows(2 * d),
            _const((FOX_WIDTH, d)),
            _const((SGU_WIDTH, d)),
            _const((d, d)),
            _const((1, d)),
            _const((d, D_FF)),
            _const((D_FF, d)),
            _const((1, d)),
        ],
        out_specs=_rows(d),
        out_shape=jax.ShapeDtypeStruct((n_tok, d), F32),
        compiler_params=_params(1),
        name="post",
    )(xt, att, sb, gs, w_a, w_b, w_o, g2, w_up, w_down, gf)


def kernel(x, norm1_g, w_in, b_f, ln_v_g, ln_v_b, w_sgu, b_sgu, w_a, w_b, w_o,
           norm2_g, w_up, w_down, normf_g):
    bsz, seq, d = x.shape
    assert d == D_MODEL and seq % TP == 0 and TP % TQ == 0 and TQ % SUB == 0 and TM % SUB == 0
    assert norm1_g.shape[0] == 1, "single-layer block"
    n_tok = bsz * seq
    xt = x.reshape(n_tok, d)

    wt = jnp.swapaxes(w_in[0], 0, 1)

    qa, ka, vt, sb, gs = _pre(xt, bsz, seq, norm1_g.reshape(1, d), wt, b_f,
                              ln_v_g.reshape(1, SGU_WIDTH), ln_v_b.reshape(1, SGU_WIDTH),
                              w_sgu[0], b_sgu[0])
    att, (wa16, wb16, wo16, wu16, wd16) = _attn(
        qa, ka, vt, [w_a[0], w_b[0], w_o[0], w_up[0], w_down[0]])
    out = _post(xt, att.reshape(n_tok, FOX_WIDTH), sb, gs, wa16, wb16, wo16,
                norm2_g.reshape(1, d), wu16, wd16, normf_g.reshape(1, d))
    return out.reshape(bsz, seq, d)
```

```python
import functools
import math

import jax
import jax.numpy as jnp
from jax import lax
from jax.experimental import pallas as pl
from jax.experimental.pallas import tpu as pltpu

D_MODEL = 1024
HEAD_DIM = 64
FOX_HEADS = 8
FOX_WIDTH = FOX_HEADS * HEAD_DIM
SGU_GROUPS = 8
SGU_WIDTH = 512
SGU_LEN = 128
CHUNK = 64
D_FF = 4 * D_MODEL
EPS = 1e-6

LANES = 128
TM = 512
TP = 512
TQ = 512
QC = 256
PAIR = 2
N_CHUNKS = TQ // QC
N_CHAINS = PAIR * N_CHUNKS
BF16_SUBLANES = 16
SUB = 256
VT_ROWS = 128
PREP_COLS = 768
LOG2E = math.log2(math.e)
VMEM_LIMIT = 56 * 1024 * 1024

_F0 = 3 * FOX_WIDTH
_Q0, _K0, _V0 = 0, FOX_WIDTH, 2 * FOX_WIDTH
_U0 = _F0
_GA0 = _U0 + 2 * SGU_WIDTH
_GB0 = _GA0 + D_MODEL
_W_COLS = _GB0 + D_MODEL
N_PARTS = 3
GATE_ROWS = 32

BF16 = jnp.bfloat16
F32 = jnp.float32


def _dot(a, b):
    return jnp.dot(a, b, preferred_element_type=F32)


def _gelu_tanh(x):
    c = math.sqrt(2.0 / math.pi)
    half = 0.5 * x
    return half * jnp.tanh(x * ((x * x) * (c * 0.044715) + c)) + half


def _sigmoid(x):
    return 0.5 * jnp.tanh(0.5 * x) + 0.5


def _split3(x):
    hi = x.astype(BF16).astype(F32)
    r = x - hi
    mid = r.astype(BF16).astype(F32)
    return hi, mid, r - mid


def _load_weights(wt_hbm, w_ref, wft_ref, stage_ref, gate_ref, sem):
    n_chunks = _W_COLS // PREP_COLS

    def chunk(c):
        first_row = c * PREP_COLS + (FOX_HEADS if c * PREP_COLS >= _F0 else 0)
        return pltpu.make_async_copy(wt_hbm.at[pl.ds(first_row, PREP_COLS), :],
                                     stage_ref.at[c % 2], sem.at[c % 2])

    gates = pltpu.make_async_copy(wt_hbm.at[pl.ds(_F0, FOX_HEADS), :], gate_ref, sem.at[2])
    gates.start()
    chunk(0).start()
    for c in range(n_chunks):
        if c + 1 < n_chunks:
            chunk(c + 1).start()
        chunk(c).wait()
        w_ref[:, c * PREP_COLS:(c + 1) * PREP_COLS] = stage_ref[c % 2].T.astype(BF16)
    gates.wait()
    g = gate_ref[...]
    pad = jnp.zeros((GATE_ROWS - N_PARTS * FOX_HEADS, g.shape[1]), g.dtype)
    wft_ref[...] = jnp.concatenate([g] * N_PARTS + [pad], axis=0).astype(BF16)


def _expand_biases(bfs_ref, bs_ref, bf_ref, bm_ref):
    grow = lax.broadcasted_iota(jnp.int32, bf_ref.shape, 0)
    bf = jnp.zeros(bf_ref.shape, F32)
    for h in range(FOX_HEADS):
        bf = jnp.where((grow % FOX_HEADS == h) & (grow < N_PARTS * FOX_HEADS), bfs_ref[0, h], bf)
    bf_ref[...] = bf
    b = bs_ref[...]
    bt = jnp.concatenate([b, jnp.zeros((LANES - SGU_GROUPS, SGU_LEN), F32)], axis=0).T
    lane = lax.broadcasted_iota(jnp.int32, (SGU_LEN, LANES), 1)
    for jp in range(SGU_GROUPS // 2):
        ga, gb = [jnp.broadcast_to(bt[:, g:g + 1], (SGU_LEN, LANES)) for g in (2 * jp, 2 * jp + 1)]
        bm_ref[:, LANES * jp:LANES * (jp + 1)] = jnp.where(lane < HEAD_DIM, ga, gb)


def _pre_kernel(x_ref, g1_ref, wt_hbm, bfs_ref, lng_ref, lnb_ref, wm_ref, bs_ref,
                qa_ref, ka_ref, vt_ref, sb_ref, gs_ref,
                carry_ref, w_ref, wft_ref, stage_ref, gate_ref, sem, bf_ref, bm_ref,
                *, tiles_per_seq):
    i = pl.program_id(0)

    @pl.when(i == 0)
    def _():
        _load_weights(wt_hbm, w_ref, wft_ref, stage_ref, gate_ref, sem)
        _expand_biases(bfs_ref, bs_ref, bf_ref, bm_ref)

    @pl.when(i % tiles_per_seq == 0)
    def _():
        carry_ref[...] = jnp.zeros_like(carry_ref)

    s_i = lax.broadcasted_iota(jnp.int32, (SUB, SUB), 0)
    t_i = lax.broadcasted_iota(jnp.int32, (SUB, SUB), 1)
    tri = jnp.where(s_i <= t_i, 1.0, 0.0).astype(BF16)
    grow = lax.broadcasted_iota(jnp.int32, (GATE_ROWS, SUB), 0)
    head_pad = jnp.zeros((LANES - FOX_HEADS, SUB), F32)
    lane = lax.broadcasted_iota(jnp.int32, (SUB, LANES), 1)
    low = lane < HEAD_DIM
    ones_row = jnp.where(lax.broadcasted_iota(jnp.int32, (VT_ROWS - HEAD_DIM, SUB), 0) == 0,
                         1.0, 0.0)
    n_win = SUB // SGU_LEN
    wi = lax.broadcasted_iota(jnp.int32, (SGU_LEN, SGU_LEN), 0) // CHUNK
    wj = lax.broadcasted_iota(jnp.int32, (SGU_LEN, SGU_LEN), 1) // CHUNK
    wmask = wj <= wi
    lane_w = lax.broadcasted_iota(jnp.int32, (SGU_LEN, n_win * LANES), 1)
    low_w = (lane_w % LANES) < HEAD_DIM

    def proj(h, c0, width):
        return _dot(h, w_ref[:, c0:c0 + width])

    carry = carry_ref[...]
    for sub in range(TP // SUB):
        rows = slice(sub * SUB, (sub + 1) * SUB)
        kblk, kcols = divmod(sub * SUB, TQ)
        x = x_ref[rows, :]
        ms = jnp.mean(x * x, axis=-1, keepdims=True)
        h = (x * lax.rsqrt(ms + EPS) * g1_ref[...]).astype(BF16)

        z = lax.dot_general(wft_ref[...], h, (((1,), (1,)), ((), ())),
                            preferred_element_type=F32) + bf_ref[...]
        usv = proj(h, _U0, 2 * SGU_WIDTH)
        k_all = proj(h, _K0, FOX_WIDTH)

        logf = jnp.minimum(z, 0.0) - jnp.log(1.0 + jnp.exp(-jnp.abs(z)))
        hi, mid, lo = _split3(logf)
        part = jnp.where(grow < FOX_HEADS, hi, jnp.where(grow < 2 * FOX_HEADS, mid, lo))
        cl = _dot(part.astype(BF16), tri)
        ct = carry + (cl[:FOX_HEADS] + cl[FOX_HEADS:2 * FOX_HEADS]
                      + cl[2 * FOX_HEADS:3 * FOX_HEADS])
        carry = jnp.broadcast_to(ct[:, SUB - 1:SUB], ct.shape)
        c = jnp.concatenate([ct, head_pad], axis=0).T
        nc_parts = _split3(c * (-LOG2E))

        q_all = proj(h, _Q0, FOX_WIDTH) * (HEAD_DIM ** -0.5 * LOG2E)
        g_a = proj(h, _GA0, D_MODEL)
        v_all = proj(h, _V0, FOX_WIDTH)

        u = _gelu_tanh(usv[:, :SGU_WIDTH])
        sv = _gelu_tanh(usv[:, SGU_WIDTH:])
        mu = jnp.mean(sv, axis=-1, keepdims=True)
        xc = sv - mu
        var = jnp.mean(xc * xc, axis=-1, keepdims=True)
        svn = (xc * lax.rsqrt(var + EPS) * lng_ref[...] + lnb_ref[...]).astype(BF16)
        for jp in range(SGU_GROUPS // 2):
            sl = slice(LANES * jp, LANES * (jp + 1))
            chunk = svn[:, sl]
            rhs = jnp.concatenate(
                [chunk[SGU_LEN * w:SGU_LEN * (w + 1), :] for w in range(n_win)], axis=1)
            wa = jnp.where(wmask, wm_ref[2 * jp], 0.0).astype(BF16)
            wb = jnp.where(wmask, wm_ref[2 * jp + 1], 0.0).astype(BF16)
            zero = jnp.zeros_like(rhs)
            stacked = jnp.concatenate([jnp.where(low_w, rhs, zero), jnp.where(low_w, zero, rhs)],
                                      axis=0)
            mixed = _dot(jnp.concatenate([wa, wb], axis=1), stacked)
            mixed = mixed + jnp.concatenate([bm_ref[:, sl]] * n_win, axis=1)
            mixed = jnp.concatenate(
                [mixed[:, LANES * w:LANES * (w + 1)] for w in range(n_win)], axis=0)
            sb_ref[rows, sl] = (u[:, sl] * mixed).astype(BF16)
        gs_ref[rows, :D_MODEL] = g_a.astype(BF16)

        g_b = proj(h, _GB0, D_MODEL)
        for jp in range(FOX_HEADS // 2):
            sl = slice(LANES * jp, LANES * (jp + 1))
            qc, kc = q_all[:, sl], k_all[:, sl]
            vt = v_all[:, sl].T
            for par in range(2):
                hd = 2 * jp + par
                a0 = HEAD_DIM if par == 0 else 0
                data = low if par == 0 else jnp.logical_not(low)
                in_aug = (lane >= a0) & (lane < a0 + N_PARTS)
                qa = jnp.where(data, qc, jnp.where(in_aug, 1.0, 0.0))
                hi, mid, lo = [jnp.broadcast_to(p[:, hd:hd + 1], (SUB, LANES)) for p in nc_parts]
                aug = jnp.where(lane == a0, hi,
                                jnp.where(lane == a0 + 1, mid,
                                          jnp.where(lane == a0 + 2, lo, 0.0)))
                ka = jnp.where(data, kc, aug)
                qa_ref[0, hd, rows, :] = qa.astype(BF16)
                ka_ref[0, hd, rows, :] = ka.astype(BF16)
                vta = jnp.concatenate([vt[par * HEAD_DIM:(par + 1) * HEAD_DIM], ones_row], axis=0)
                vt_ref[0, hd, kblk, :, kcols:kcols + SUB] = vta.astype(BF16)
        gs_ref[rows, D_MODEL:] = g_b.astype(BF16)
    carry_ref[...] = carry


def _attn_kernel(qa_ref, ka_ref, vt_ref, *refs, n_q, n_cast):
    cast_in, (o_ref, *cast_out), (acc_ref, st_ref) = (
        refs[:n_cast], refs[n_cast:2 * n_cast + 1], refs[2 * n_cast + 1:])
    for src, dst in zip(cast_in, cast_out):
        dst[...] = src[...].astype(BF16)

    k_i = lax.broadcasted_iota(jnp.int32, (QC, QC), 0)
    q_i = lax.broadcasted_iota(jnp.int32, (QC, QC), 1)
    causal = k_i <= q_i
    halves = [slice(h * QC, (h + 1) * QC) for h in range(N_CHUNKS)]

    def needed(c, tile, blk, half):
        return not (blk == tile and half > c)

    def score_half(ch, tile, blk, half):
        hd, c = divmod(ch, N_CHUNKS)
        q0 = tile * TQ + c * QC
        k0 = blk * TQ + half * QC
        st = lax.dot_general(ka_ref[0, hd, pl.ds(k0, QC), :], qa_ref[0, hd, pl.ds(q0, QC), :],
                             (((1,), (1,)), ((), ())), preferred_element_type=F32)
        if blk == tile and half == c:
            st = jnp.where(causal, st, -jnp.inf)
        st_ref[ch, halves[half], :] = st
        return jnp.max(st, axis=0, keepdims=True)

    def chain_step(ch, tile, blk, m_old, cm, nxt):
        hd, c = divmod(ch, N_CHUNKS)
        m_new = cm if m_old is None else jnp.maximum(m_old, cm)
        pv, cm_next = [], []
        for half in range(N_CHUNKS):
            use = needed(c, tile, blk, half)
            if use:
                p = jnp.exp2(st_ref[ch, halves[half], :] - m_new).astype(BF16)
            if nxt is not None and needed(c, *nxt, half):
                cm_next.append(score_half(ch, *nxt, half))
            if use:
                pv.append(_dot(vt_ref[0, hd, blk, :, halves[half]], p))
        pv = sum(pv[1:], pv[0])
        if m_old is None:
            acc_ref[ch] = pv
        else:
            acc_ref[ch] = acc_ref[ch] * jnp.exp2(m_old - m_new) + pv
        return m_new, (functools.reduce(jnp.maximum, cm_next) if cm_next else None)

    steps = [(qi, j) for qi in range(n_q) for j in range(qi + 1)]
    ms = [None] * N_CHAINS
    cms = [functools.reduce(jnp.maximum,
                            [score_half(ch, *steps[0], half) for half in range(N_CHUNKS)
                             if needed(ch % N_CHUNKS, *steps[0], half)])
           for ch in range(N_CHAINS)]
    for n, (qi, j) in enumerate(steps):
        nxt = steps[n + 1] if n + 1 < len(steps) else None
        for ch in sorted(range(N_CHAINS), key=lambda ch: (ch % N_CHUNKS, ch)):
            ms[ch], cms[ch] = chain_step(ch, qi, j, ms[ch], cms[ch], nxt)
        if j == qi:
            for c in range(N_CHUNKS):
                chains = [N_CHUNKS * hd + c for hd in range(PAIR)]
                o_t = jnp.concatenate(
                    [acc_ref[ch, :HEAD_DIM, :] / acc_ref[ch, HEAD_DIM:HEAD_DIM + 1, :]
                     for ch in chains], axis=0)
                o_ref[0, pl.ds(qi * TQ + c * QC, QC), :] = o_t.T.astype(BF16)
            ms = [None] * N_CHAINS


def _post_kernel(x_ref, at_ref, sb_ref, gs_ref, wa_ref, wb_ref, wo_ref, g2_ref, wu_ref,
                 wd_ref, gf_ref, o_ref, *, ff_chunk):
    subs = [slice(s * SUB, (s + 1) * SUB) for s in range(TM // SUB)]
    merged = []
    for rows in subs:
        ya = _dot(at_ref[rows, :], wa_ref[...])
        yb = _dot(sb_ref[rows, :], wb_ref[...])
        merged.append((_sigmoid(gs_ref[rows, :D_MODEL].astype(F32)) * ya
                       + _sigmoid(gs_ref[rows, D_MODEL:].astype(F32)) * yb).astype(BF16))
    ys = [x_ref[rows, :] + _dot(m, wo_ref[...]) for rows, m in zip(subs, merged)]
    hs = []
    for y in ys:
        ms = jnp.mean(y * y, axis=-1, keepdims=True)
        hs.append((y * lax.rsqrt(ms + EPS) * g2_ref[...]).astype(BF16))
    for c0 in range(0, D_FF, ff_chunk):
        acts = []
        for h in hs:
            a = jnp.maximum(_dot(h, wu_ref[:, c0:c0 + ff_chunk]), 0.0)
            acts.append((a * a).astype(BF16))
        ys = [y + _dot(a, wd_ref[c0:c0 + ff_chunk, :]) for y, a in zip(ys, acts)]
    for rows, y in zip(subs, ys):
        ms = jnp.mean(y * y, axis=-1, keepdims=True)
        o_ref[rows, :] = y * lax.rsqrt(ms + EPS) * gf_ref[...]


def _params(n_axes, ordered=False):
    semantics = "arbitrary" if ordered else "parallel"
    return pltpu.CompilerParams(dimension_semantics=(semantics,) * n_axes,
                                vmem_limit_bytes=VMEM_LIMIT)


def _const(shape):
    return pl.BlockSpec(shape, lambda *_: (0,) * len(shape), pipeline_mode=pl.Buffered(1))


def _rows(width, rows=TM):
    return pl.BlockSpec((rows, width), lambda i: (i, 0))


def _pre(xt, bsz, seq, g1, wt, b_f, ln_g, ln_b, w_sgu, b_sgu):
    n_tok, d = xt.shape
    assert wt.shape == (_W_COLS + FOX_HEADS, d) and _F0 % PREP_COLS == 0 and _W_COLS % PREP_COLS == 0
    tiles_per_seq = seq // TP
    head_map = lambda i: (i // tiles_per_seq, 0, i % tiles_per_seq, 0)
    return pl.pallas_call(
        functools.partial(_pre_kernel, tiles_per_seq=tiles_per_seq),
        grid=(n_tok // TP,),
        in_specs=[
            _rows(d, TP),
            _const((1, d)),
            pl.BlockSpec(memory_space=pl.ANY),
            pl.BlockSpec(memory_space=pltpu.SMEM),
            _const((1, SGU_WIDTH)),
            _const((1, SGU_WIDTH)),
            _const((SGU_GROUPS, SGU_LEN, SGU_LEN)),
            _const((SGU_GROUPS, SGU_LEN)),
        ],
        out_specs=[
            pl.BlockSpec((1, FOX_HEADS, TP, LANES), head_map),
            pl.BlockSpec((1, FOX_HEADS, TP, LANES), head_map),
            pl.BlockSpec((1, FOX_HEADS, TP // TQ, VT_ROWS, TQ), lambda i: head_map(i) + (0,)),
            _rows(SGU_WIDTH, TP),
            _rows(2 * d, TP),
        ],
        out_shape=[
            jax.ShapeDtypeStruct((bsz, FOX_HEADS, seq, LANES), BF16),
            jax.ShapeDtypeStruct((bsz, FOX_HEADS, seq, LANES), BF16),
            jax.ShapeDtypeStruct((bsz, FOX_HEADS, seq // TQ, VT_ROWS, TQ), BF16),
            jax.ShapeDtypeStruct((n_tok, SGU_WIDTH), BF16),
            jax.ShapeDtypeStruct((n_tok, 2 * d), BF16),
        ],
        scratch_shapes=[pltpu.VMEM((FOX_HEADS, SUB), F32),
                        pltpu.VMEM((d, _W_COLS), BF16),
                        pltpu.VMEM((GATE_ROWS, d), BF16),
                        pltpu.VMEM((2, PREP_COLS, d), F32),
                        pltpu.VMEM((FOX_HEADS, d), F32),
                        pltpu.SemaphoreType.DMA((3,)),
                        pltpu.VMEM((GATE_ROWS, SUB), F32),
                        pltpu.VMEM((SGU_LEN, SGU_WIDTH), F32)],
        compiler_params=_params(1, ordered=True),
        name="pre",
    )(xt, g1, wt, b_f, ln_g, ln_b, w_sgu, b_sgu)


def _attn(qa, ka, vt, weights):
    bsz, _, seq, _ = qa.shape
    pairs = FOX_HEADS // PAIR
    n_steps = bsz * pairs
    assert all(w.shape[0] % (n_steps * BF16_SUBLANES) == 0 for w in weights)
    slab = lambda w: pl.BlockSpec((w.shape[0] // n_steps, w.shape[1]),
                                  lambda b, p: (b * pairs + p, 0))
    att, *cast = pl.pallas_call(
        functools.partial(_attn_kernel, n_q=seq // TQ, n_cast=len(weights)),
        grid=(bsz, pairs),
        in_specs=[
            pl.BlockSpec((1, PAIR, seq, LANES), lambda b, p: (b, p, 0, 0)),
            pl.BlockSpec((1, PAIR, seq, LANES), lambda b, p: (b, p, 0, 0)),
            pl.BlockSpec((1, PAIR, seq // TQ, VT_ROWS, TQ), lambda b, p: (b, p, 0, 0, 0)),
        ] + [slab(w) for w in weights],
        out_specs=[pl.BlockSpec((1, seq, LANES), lambda b, p: (b, 0, p))]
        + [slab(w) for w in weights],
        out_shape=[jax.ShapeDtypeStruct((bsz, seq, FOX_WIDTH), BF16)]
        + [jax.ShapeDtypeStruct(w.shape, BF16) for w in weights],
        scratch_shapes=[pltpu.VMEM((N_CHAINS, VT_ROWS, QC), F32),
                        pltpu.VMEM((N_CHAINS, TQ, QC), F32)],
        compiler_params=_params(2),
        name="attn",
    )(qa, ka, vt, *weights)
    return att, cast


def _post(xt, att, sb, gs, w_a, w_b, w_o, g2, w_up, w_down, gf):
    n_tok, d = xt.shape
    return pl.pallas_call(
        functools.partial(_post_kernel, ff_chunk=1024),
        grid=(n_tok // TM,),
        in_specs=[
            _rows(d),
            _rows(FOX_WIDTH),
            _rows(SGU_WIDTH),
            _rows(2 * d),
            _const((FOX_WIDTH, d)),
            _const((SGU_WIDTH, d)),
            _const((d, d)),
            _const((1, d)),
            _const((d, D_FF)),
            _const((D_FF, d)),
            _const((1, d)),
        ],
        out_specs=_rows(d),
        out_shape=jax.ShapeDtypeStruct((n_tok, d), F32),
        compiler_params=_params(1),
        name="post",
    )(xt, att, sb, gs, w_a, w_b, w_o, g2, w_up, w_down, gf)


def kernel(x, norm1_g, w_in, b_f, ln_v_g, ln_v_b, w_sgu, b_sgu, w_a, w_b, w_o,
           norm2_g, w_up, w_down, normf_g):
    bsz, seq, d = x.shape
    assert d == D_MODEL and seq % TP == 0 and TP % TQ == 0 and TQ % SUB == 0 and TM % SUB == 0
    assert norm1_g.shape[0] == 1, "single-layer block"
    n_tok = bsz * seq
    xt = x.reshape(n_tok, d)

    wt = jnp.swapaxes(w_in[0], 0, 1)

    qa, ka, vt, sb, gs = _pre(xt, bsz, seq, norm1_g.reshape(1, d), wt, b_f,
                              ln_v_g.reshape(1, SGU_WIDTH), ln_v_b.reshape(1, SGU_WIDTH),
                              w_sgu[0], b_sgu[0])
    att, (wa16, wb16, wo16, wu16, wd16) = _attn(
        qa, ka, vt, [w_a[0], w_b[0], w_o[0], w_up[0], w_down[0]])
    out = _post(xt, att.reshape(n_tok, FOX_WIDTH), sb, gs, wa16, wb16, wo16,
                norm2_g.reshape(1, d), wu16, wd16, normf_g.reshape(1, d))
    return out.reshape(bsz, seq, d)
```

```python
import functools
import math

import jax
import jax.numpy as jnp
from jax import lax
from jax.experimental import pallas as pl
from jax.experimental.pallas import tpu as pltpu

D_MODEL = 1024
HEAD_DIM = 64
FOX_HEADS = 8
FOX_WIDTH = FOX_HEADS * HEAD_DIM
SGU_GROUPS = 8
SGU_WIDTH = 512
SGU_LEN = 128
CHUNK = 64
D_FF = 4 * D_MODEL
EPS = 1e-6

LANES = 128
TM = 512
TP = 512
TQ = 512
QC = 256
PAIR = 2
N_CHUNKS = TQ // QC
N_CHAINS = PAIR * N_CHUNKS
BF16_SUBLANES = 16
SUB = 256
VT_ROWS = 128
PREP_COLS = 768
LOG2E = math.log2(math.e)
VMEM_LIMIT = 56 * 1024 * 1024

_F0 = 3 * FOX_WIDTH
_Q0, _K0, _V0 = 0, FOX_WIDTH, 2 * FOX_WIDTH
_U0 = _F0
_GA0 = _U0 + 2 * SGU_WIDTH
_GB0 = _GA0 + D_MODEL
_W_COLS = _GB0 + D_MODEL
N_PARTS = 3
GATE_ROWS = 32

BF16 = jnp.bfloat16
F32 = jnp.float32


def _dot(a, b):
    return jnp.dot(a, b, preferred_element_type=F32)


def _gelu_tanh(x):
    c = math.sqrt(2.0 / math.pi)
    half = 0.5 * x
    return half * jnp.tanh(x * ((x * x) * (c * 0.044715) + c)) + half


def _sigmoid(x):
    return 0.5 * jnp.tanh(0.5 * x) + 0.5


def _split3(x):
    hi = x.astype(BF16).astype(F32)
    r = x - hi
    mid = r.astype(BF16).astype(F32)
    return hi, mid, r - mid


def _load_weights(wt_hbm, w_ref, wft_ref, stage_ref, gate_ref, sem):
    n_chunks = _W_COLS // PREP_COLS

    def chunk(c):
        first_row = c * PREP_COLS + (FOX_HEADS if c * PREP_COLS >= _F0 else 0)
        return pltpu.make_async_copy(wt_hbm.at[pl.ds(first_row, PREP_COLS), :],
                                     stage_ref.at[c % 2], sem.at[c % 2])

    gates = pltpu.make_async_copy(wt_hbm.at[pl.ds(_F0, FOX_HEADS), :], gate_ref, sem.at[2])
    gates.start(priority=1)
    chunk(0).start()
    for c in range(n_chunks):
        if c + 1 < n_chunks:
            chunk(c + 1).start(priority=(c + 1) % 2)
        chunk(c).wait()
        w_ref[:, c * PREP_COLS:(c + 1) * PREP_COLS] = stage_ref[c % 2].T.astype(BF16)
    gates.wait()
    g = gate_ref[...]
    pad = jnp.zeros((GATE_ROWS - N_PARTS * FOX_HEADS, g.shape[1]), g.dtype)
    wft_ref[...] = jnp.concatenate([g] * N_PARTS + [pad], axis=0).astype(BF16)


def _expand_biases(bfs_ref, bs_ref, bf_ref, bm_ref):
    grow = lax.broadcasted_iota(jnp.int32, bf_ref.shape, 0)
    bf = jnp.zeros(bf_ref.shape, F32)
    for h in range(FOX_HEADS):
        bf = jnp.where((grow % FOX_HEADS == h) & (grow < N_PARTS * FOX_HEADS), bfs_ref[0, h], bf)
    bf_ref[...] = bf
    b = bs_ref[...]
    bt = jnp.concatenate([b, jnp.zeros((LANES - SGU_GROUPS, SGU_LEN), F32)], axis=0).T
    lane = lax.broadcasted_iota(jnp.int32, (SGU_LEN, LANES), 1)
    for jp in range(SGU_GROUPS // 2):
        ga, gb = [jnp.broadcast_to(bt[:, g:g + 1], (SGU_LEN, LANES)) for g in (2 * jp, 2 * jp + 1)]
        bm_ref[:, LANES * jp:LANES * (jp + 1)] = jnp.where(lane < HEAD_DIM, ga, gb)


def _pre_kernel(x_ref, g1_ref, wt_hbm, bfs_ref, lng_ref, lnb_ref, wm_ref, bs_ref,
                qa_ref, ka_ref, vt_ref, sb_ref, gs_ref,
                carry_ref, w_ref, wft_ref, stage_ref, gate_ref, sem, bf_ref, bm_ref,
                *, tiles_per_seq):
    i = pl.program_id(0)

    @pl.when(i == 0)
    def _():
        _load_weights(wt_hbm, w_ref, wft_ref, stage_ref, gate_ref, sem)
        _expand_biases(bfs_ref, bs_ref, bf_ref, bm_ref)

    @pl.when(i % tiles_per_seq == 0)
    def _():
        carry_ref[...] = jnp.zeros_like(carry_ref)

    s_i = lax.broadcasted_iota(jnp.int32, (SUB, SUB), 0)
    t_i = lax.broadcasted_iota(jnp.int32, (SUB, SUB), 1)
    tri = jnp.where(s_i <= t_i, 1.0, 0.0).astype(BF16)
    grow = lax.broadcasted_iota(jnp.int32, (GATE_ROWS, SUB), 0)
    head_pad = jnp.zeros((LANES - FOX_HEADS, SUB), F32)
    lane = lax.broadcasted_iota(jnp.int32, (SUB, LANES), 1)
    low = lane < HEAD_DIM
    ones_row = jnp.where(lax.broadcasted_iota(jnp.int32, (VT_ROWS - HEAD_DIM, SUB), 0) == 0,
                         1.0, 0.0)
    n_win = SUB // SGU_LEN
    wi = lax.broadcasted_iota(jnp.int32, (SGU_LEN, SGU_LEN), 0) // CHUNK
    wj = lax.broadcasted_iota(jnp.int32, (SGU_LEN, SGU_LEN), 1) // CHUNK
    wmask = wj <= wi
    lane_w = lax.broadcasted_iota(jnp.int32, (SGU_LEN, n_win * LANES), 1)
    low_w = (lane_w % LANES) < HEAD_DIM

    def proj(h, c0, width):
        return _dot(h, w_ref[:, c0:c0 + width])

    carry = carry_ref[...]
    for sub in range(TP // SUB):
        rows = slice(sub * SUB, (sub + 1) * SUB)
        kblk, kcols = divmod(sub * SUB, TQ)
        x = x_ref[rows, :]
        ms = jnp.mean(x * x, axis=-1, keepdims=True)
        h = (x * lax.rsqrt(ms + EPS) * g1_ref[...]).astype(BF16)

        z = lax.dot_general(wft_ref[...], h, (((1,), (1,)), ((), ())),
                            preferred_element_type=F32) + bf_ref[...]
        usv = proj(h, _U0, 2 * SGU_WIDTH)
        k_all = proj(h, _K0, FOX_WIDTH)

        logf = jnp.minimum(z, 0.0) - jnp.log(1.0 + jnp.exp(-jnp.abs(z)))
        hi, mid, lo = _split3(logf)
        part = jnp.where(grow < FOX_HEADS, hi, jnp.where(grow < 2 * FOX_HEADS, mid, lo))
        cl = _dot(part.astype(BF16), tri)
        ct = carry + (cl[:FOX_HEADS] + cl[FOX_HEADS:2 * FOX_HEADS]
                      + cl[2 * FOX_HEADS:3 * FOX_HEADS])
        carry = jnp.broadcast_to(ct[:, SUB - 1:SUB], ct.shape)
        c = jnp.concatenate([ct, head_pad], axis=0).T
        nc_parts = _split3(c * (-LOG2E))

        q_all = proj(h, _Q0, FOX_WIDTH) * (HEAD_DIM ** -0.5 * LOG2E)
        g_a = proj(h, _GA0, D_MODEL)
        v_all = proj(h, _V0, FOX_WIDTH)

        u = _gelu_tanh(usv[:, :SGU_WIDTH])
        sv = _gelu_tanh(usv[:, SGU_WIDTH:])
        mu = jnp.mean(sv, axis=-1, keepdims=True)
        xc = sv - mu
        var = jnp.mean(xc * xc, axis=-1, keepdims=True)
        svn = (xc * lax.rsqrt(var + EPS) * lng_ref[...] + lnb_ref[...]).astype(BF16)
        for jp in range(SGU_GROUPS // 2):
            sl = slice(LANES * jp, LANES * (jp + 1))
            chunk = svn[:, sl]
            rhs = jnp.concatenate(
                [chunk[SGU_LEN * w:SGU_LEN * (w + 1), :] for w in range(n_win)], axis=1)
            wa = jnp.where(wmask, wm_ref[2 * jp], 0.0).astype(BF16)
            wb = jnp.where(wmask, wm_ref[2 * jp + 1], 0.0).astype(BF16)
            zero = jnp.zeros_like(rhs)
            stacked = jnp.concatenate([jnp.where(low_w, rhs, zero), jnp.where(low_w, zero, rhs)],
                                      axis=0)
            mixed = _dot(jnp.concatenate([wa, wb], axis=1), stacked)
            mixed = mixed + jnp.concatenate([bm_ref[:, sl]] * n_win, axis=1)
            mixed = jnp.concatenate(
                [mixed[:, LANES * w:LANES * (w + 1)] for w in range(n_win)], axis=0)
            sb_ref[rows, sl] = (u[:, sl] * mixed).astype(BF16)
        gs_ref[rows, :D_MODEL] = g_a.astype(BF16)

        g_b = proj(h, _GB0, D_MODEL)
        for jp in range(FOX_HEADS // 2):
            sl = slice(LANES * jp, LANES * (jp + 1))
            qc, kc = q_all[:, sl], k_all[:, sl]
            vt = v_all[:, sl].T
            for par in range(2):
                hd = 2 * jp + par
                a0 = HEAD_DIM if par == 0 else 0
                data = low if par == 0 else jnp.logical_not(low)
                in_aug = (lane >= a0) & (lane < a0 + N_PARTS)
                qa = jnp.where(data, qc, jnp.where(in_aug, 1.0, 0.0))
                hi, mid, lo = [jnp.broadcast_to(p[:, hd:hd + 1], (SUB, LANES)) for p in nc_parts]
                aug = jnp.where(lane == a0, hi,
                                jnp.where(lane == a0 + 1, mid,
                                          jnp.where(lane == a0 + 2, lo, 0.0)))
                ka = jnp.where(data, kc, aug)
                qa_ref[0, hd, rows, :] = qa.astype(BF16)
                ka_ref[0, hd, rows, :] = ka.astype(BF16)
                vta = jnp.concatenate([vt[par * HEAD_DIM:(par + 1) * HEAD_DIM], ones_row], axis=0)
                vt_ref[0, hd, kblk, :, kcols:kcols + SUB] = vta.astype(BF16)
        gs_ref[rows, D_MODEL:] = g_b.astype(BF16)
    carry_ref[...] = carry


def _attn_kernel(qa_ref, ka_ref, vt_ref, *refs, n_q, n_cast):
    cast_in, (o_ref, *cast_out), (acc_ref, st_ref) = (
        refs[:n_cast], refs[n_cast:2 * n_cast + 1], refs[2 * n_cast + 1:])
    for src, dst in zip(cast_in, cast_out):
        dst[...] = src[...].astype(BF16)

    k_i = lax.broadcasted_iota(jnp.int32, (QC, QC), 0)
    q_i = lax.broadcasted_iota(jnp.int32, (QC, QC), 1)
    causal = k_i <= q_i
    halves = [slice(h * QC, (h + 1) * QC) for h in range(N_CHUNKS)]

    def needed(c, tile, blk, half):
        return not (blk == tile and half > c)

    def score_half(ch, tile, blk, half):
        hd, c = divmod(ch, N_CHUNKS)
        q0 = tile * TQ + c * QC
        k0 = blk * TQ + half * QC
        st = lax.dot_general(ka_ref[0, hd, pl.ds(k0, QC), :], qa_ref[0, hd, pl.ds(q0, QC), :],
                             (((1,), (1,)), ((), ())), preferred_element_type=F32)
        if blk == tile and half == c:
            st = jnp.where(causal, st, -jnp.inf)
        st_ref[ch, halves[half], :] = st
        return jnp.max(st, axis=0, keepdims=True)

    def chain_step(ch, tile, blk, m_old, cm, nxt):
        hd, c = divmod(ch, N_CHUNKS)
        m_new = cm if m_old is None else jnp.maximum(m_old, cm)
        pv, cm_next = [], []
        for half in range(N_CHUNKS):
            use = needed(c, tile, blk, half)
            if use:
                p = jnp.exp2(st_ref[ch, halves[half], :] - m_new).astype(BF16)
            if nxt is not None and needed(c, *nxt, half):
                cm_next.append(score_half(ch, *nxt, half))
            if use:
                pv.append(_dot(vt_ref[0, hd, blk, :, halves[half]], p))
        pv = sum(pv[1:], pv[0])
        if m_old is None:
            acc_ref[ch] = pv
        else:
            acc_ref[ch] = acc_ref[ch] * jnp.exp2(m_old - m_new) + pv
        return m_new, (functools.reduce(jnp.maximum, cm_next) if cm_next else None)

    steps = [(qi, j) for qi in range(n_q) for j in range(qi + 1)]
    ms = [None] * N_CHAINS
    cms = [functools.reduce(jnp.maximum,
                            [score_half(ch, *steps[0], half) for half in range(N_CHUNKS)
                             if needed(ch % N_CHUNKS, *steps[0], half)])
           for ch in range(N_CHAINS)]
    for n, (qi, j) in enumerate(steps):
        nxt = steps[n + 1] if n + 1 < len(steps) else None
        for ch in range(N_CHAINS):
            ms[ch], cms[ch] = chain_step(ch, qi, j, ms[ch], cms[ch], nxt)
        if j == qi:
            for c in range(N_CHUNKS):
                chains = [N_CHUNKS * hd + c for hd in range(PAIR)]
                o_t = jnp.concatenate(
                    [acc_ref[ch, :HEAD_DIM, :] / acc_ref[ch, HEAD_DIM:HEAD_DIM + 1, :]
                     for ch in chains], axis=0)
                o_ref[0, pl.ds(qi * TQ + c * QC, QC), :] = o_t.T.astype(BF16)
            ms = [None] * N_CHAINS


def _post_kernel(x_ref, at_ref, sb_ref, gs_ref, wa_ref, wb_ref, wo_ref, g2_ref, wu_ref,
                 wd_ref, gf_ref, o_ref, *, ff_chunk):
    subs = [slice(s * SUB, (s + 1) * SUB) for s in range(TM // SUB)]
    merged = []
    for rows in subs:
        ya = _dot(at_ref[rows, :], wa_ref[...])
        yb = _dot(sb_ref[rows, :], wb_ref[...])
        merged.append((_sigmoid(gs_ref[rows, :D_MODEL].astype(F32)) * ya
                       + _sigmoid(gs_ref[rows, D_MODEL:].astype(F32)) * yb).astype(BF16))
    ys = [x_ref[rows, :] + _dot(m, wo_ref[...]) for rows, m in zip(subs, merged)]
    hs = []
    for y in ys:
        ms = jnp.mean(y * y, axis=-1, keepdims=True)
        hs.append((y * lax.rsqrt(ms + EPS) * g2_ref[...]).astype(BF16))
    for c0 in range(0, D_FF, ff_chunk):
        acts = []
        for h in hs:
            a = jnp.maximum(_dot(h, wu_ref[:, c0:c0 + ff_chunk]), 0.0)
            acts.append((a * a).astype(BF16))
        ys = [y + _dot(a, wd_ref[c0:c0 + ff_chunk, :]) for y, a in zip(ys, acts)]
    for rows, y in zip(subs, ys):
        ms = jnp.mean(y * y, axis=-1, keepdims=True)
        o_ref[rows, :] = y * lax.rsqrt(ms + EPS) * gf_ref[...]


def _params(n_axes, ordered=False):
    semantics = "arbitrary" if ordered else "parallel"
    return pltpu.CompilerParams(dimension_semantics=(semantics,) * n_axes,
                                vmem_limit_bytes=VMEM_LIMIT)


def _const(shape):
    return pl.BlockSpec(shape, lambda *_: (0,) * len(shape), pipeline_mode=pl.Buffered(1))


def _rows(width, rows=TM):
    return pl.BlockSpec((rows, width), lambda i: (i, 0))


def _pre(xt, bsz, seq, g1, wt, b_f, ln_g, ln_b, w_sgu, b_sgu):
    n_tok, d = xt.shape
    assert wt.shape == (_W_COLS + FOX_HEADS, d) and _F0 % PREP_COLS == 0 and _W_COLS % PREP_COLS == 0
    tiles_per_seq = seq // TP
    head_map = lambda i: (i // tiles_per_seq, 0, i % tiles_per_seq, 0)
    return pl.pallas_call(
        functools.partial(_pre_kernel, tiles_per_seq=tiles_per_seq),
        grid=(n_tok // TP,),
        in_specs=[
            _rows(d, TP),
            _const((1, d)),
            pl.BlockSpec(memory_space=pl.ANY),
            pl.BlockSpec(memory_space=pltpu.SMEM),
            _const((1, SGU_WIDTH)),
            _const((1, SGU_WIDTH)),
            _const((SGU_GROUPS, SGU_LEN, SGU_LEN)),
            _const((SGU_GROUPS, SGU_LEN)),
        ],
        out_specs=[
            pl.BlockSpec((1, FOX_HEADS, TP, LANES), head_map),
            pl.BlockSpec((1, FOX_HEADS, TP, LANES), head_map),
            pl.BlockSpec((1, FOX_HEADS, TP // TQ, VT_ROWS, TQ), lambda i: head_map(i) + (0,)),
            _rows(SGU_WIDTH, TP),
            _rows(2 * d, TP),
        ],
        out_shape=[
            jax.ShapeDtypeStruct((bsz, FOX_HEADS, seq, LANES), BF16),
            jax.ShapeDtypeStruct((bsz, FOX_HEADS, seq, LANES), BF16),
            jax.ShapeDtypeStruct((bsz, FOX_HEADS, seq // TQ, VT_ROWS, TQ), BF16),
            jax.ShapeDtypeStruct((n_tok, SGU_WIDTH), BF16),
            jax.ShapeDtypeStruct((n_tok, 2 * d), BF16),
        ],
        scratch_shapes=[pltpu.VMEM((FOX_HEADS, SUB), F32),
                        pltpu.VMEM((d, _W_COLS), BF16),
                        pltpu.VMEM((GATE_ROWS, d), BF16),
                        pltpu.VMEM((2, PREP_COLS, d), F32),
                        pltpu.VMEM((FOX_HEADS, d), F32),
                        pltpu.SemaphoreType.DMA((3,)),
                        pltpu.VMEM((GATE_ROWS, SUB), F32),
                        pltpu.VMEM((SGU_LEN, SGU_WIDTH), F32)],
        compiler_params=_params(1, ordered=True),
        name="pre",
    )(xt, g1, wt, b_f, ln_g, ln_b, w_sgu, b_sgu)


def _attn(qa, ka, vt, weights):
    bsz, _, seq, _ = qa.shape
    pairs = FOX_HEADS // PAIR
    n_steps = bsz * pairs
    assert all(w.shape[0] % (n_steps * BF16_SUBLANES) == 0 for w in weights)
    slab = lambda w: pl.BlockSpec((w.shape[0] // n_steps, w.shape[1]),
                                  lambda b, p: (b * pairs + p, 0))
    att, *cast = pl.pallas_call(
        functools.partial(_attn_kernel, n_q=seq // TQ, n_cast=len(weights)),
        grid=(bsz, pairs),
        in_specs=[
            pl.BlockSpec((1, PAIR, seq, LANES), lambda b, p: (b, p, 0, 0)),
            pl.BlockSpec((1, PAIR, seq, LANES), lambda b, p: (b, p, 0, 0)),
            pl.BlockSpec((1, PAIR, seq // TQ, VT_ROWS, TQ), lambda b, p: (b, p, 0, 0, 0)),
        ] + [slab(w) for w in weights],
        out_specs=[pl.BlockSpec((1, seq, LANES), lambda b, p: (b, 0, p))]
        + [slab(w) for w in weights],
        out_shape=[jax.ShapeDtypeStruct((bsz, seq, FOX_WIDTH), BF16)]
        + [jax.ShapeDtypeStruct(w.shape, BF16) for w in weights],
        scratch_shapes=[pltpu.VMEM((N_CHAINS, VT_ROWS, QC), F32),
                        pltpu.VMEM((N_CHAINS, TQ, QC), F32)],
        compiler_params=_params(2),
        name="attn",
    )(qa, ka, vt, *weights)
    return att, cast


def _post(xt, att, sb, gs, w_a, w_b, w_o, g2, w_up, w_down, gf):
    n_tok, d = xt.shape
    return pl.pallas_call(
        functools.partial(_post_kernel, ff_chunk=1024),
        grid=(n_tok // TM,),
        in_specs=[
            _rows(d),
            _rows(FOX_WIDTH),
            _rows(SGU_WIDTH),
            _rows(2 * d),
            _const((FOX_WIDTH, d)),
            _const((SGU_WIDTH, d)),
            _const((d, d)),
            _const((1, d)),
            _const((d, D_FF)),
            _const((D_FF, d)),
            _const((1, d)),
        ],
        out_specs=_rows(d),
        out_shape=jax.ShapeDtypeStruct((n_tok, d), F32),
        compiler_params=_params(1),
        name="post",
    )(xt, att, sb, gs, w_a, w_b, w_o, g2, w_up, w_down, gf)


def kernel(x, norm1_g, w_in, b_f, ln_v_g, ln_v_b, w_sgu, b_sgu, w_a, w_b, w_o,
           norm2_g, w_up, w_down, normf_g):
    bsz, seq, d = x.shape
    assert d == D_MODEL and seq % TP == 0 and TP % TQ == 0 and TQ % SUB == 0 and TM % SUB == 0
    assert norm1_g.shape[0] == 1, "single-layer block"
    n_tok = bsz * seq
    xt = x.reshape(n_tok, d)

    wt = jnp.swapaxes(w_in[0], 0, 1)

    qa, ka, vt, sb, gs = _pre(xt, bsz, seq, norm1_g.reshape(1, d), wt, b_f,
                              ln_v_g.reshape(1, SGU_WIDTH), ln_v_b.reshape(1, SGU_WIDTH),
                              w_sgu[0], b_sgu[0])
    att, (wa16, wb16, wo16, wu16, wd16) = _attn(
        qa, ka, vt, [w_a[0], w_b[0], w_o[0], w_up[0], w_down[0]])
    out = _post(xt, att.reshape(n_tok, FOX_WIDTH), sb, gs, wa16, wb16, wo16,
                norm2_g.reshape(1, d), wu16, wd16, normf_g.reshape(1, d))
    return out.reshape(bsz, seq, d)
```

```python
import functools
import math

import jax
import jax.numpy as jnp
from jax import lax
from jax.experimental import pallas as pl
from jax.experimental.pallas import tpu as pltpu

D_MODEL = 1024
HEAD_DIM = 64
FOX_HEADS = 8
FOX_WIDTH = FOX_HEADS * HEAD_DIM
SGU_GROUPS = 8
SGU_WIDTH = 512
SGU_LEN = 128
CHUNK = 64
D_FF = 4 * D_MODEL
EPS = 1e-6

LANES = 128
TM = 512
TP = 512
TQ = 512
QC = 256
PAIR = 2
N_CHUNKS = TQ // QC
N_CHAINS = PAIR * N_CHUNKS
BF16_SUBLANES = 16
SUB = 256
VT_ROWS = 128
PREP_COLS = 768
LOG2E = math.log2(math.e)
VMEM_LIMIT = 56 * 1024 * 1024

_F0 = 3 * FOX_WIDTH
_Q0, _K0, _V0 = 0, FOX_WIDTH, 2 * FOX_WIDTH
_U0 = _F0
_GA0 = _U0 + 2 * SGU_WIDTH
_GB0 = _GA0 + D_MODEL
_W_COLS = _GB0 + D_MODEL
N_PARTS = 3
GATE_ROWS = 32

BF16 = jnp.bfloat16
F32 = jnp.float32


def _dot(a, b):
    return jnp.dot(a, b, preferred_element_type=F32)


def _gelu_tanh(x):
    c = math.sqrt(2.0 / math.pi)
    half = 0.5 * x
    return half * jnp.tanh(x * ((x * x) * (c * 0.044715) + c)) + half


def _sigmoid(x):
    return 0.5 * jnp.tanh(0.5 * x) + 0.5


def _split3(x):
    hi = x.astype(BF16).astype(F32)
    r = x - hi
    mid = r.astype(BF16).astype(F32)
    return hi, mid, r - mid


def _load_weights(wt_hbm, w_ref, wft_ref, stage_ref, gate_ref, sem):
    n_chunks = _W_COLS // PREP_COLS

    def chunk(c):
        first_row = c * PREP_COLS + (FOX_HEADS if c * PREP_COLS >= _F0 else 0)
        return pltpu.make_async_copy(wt_hbm.at[pl.ds(first_row, PREP_COLS), :],
                                     stage_ref.at[c % 2], sem.at[c % 2])

    gates = pltpu.make_async_copy(wt_hbm.at[pl.ds(_F0, FOX_HEADS), :], gate_ref, sem.at[2])
    gates.start()
    chunk(0).start()
    for c in range(n_chunks):
        if c + 1 < n_chunks:
            chunk(c + 1).start()
        chunk(c).wait()
        w_ref[:, c * PREP_COLS:(c + 1) * PREP_COLS] = stage_ref[c % 2].T.astype(BF16)
    gates.wait()
    g = gate_ref[...]
    pad = jnp.zeros((GATE_ROWS - N_PARTS * FOX_HEADS, g.shape[1]), g.dtype)
    wft_ref[...] = jnp.concatenate([g] * N_PARTS + [pad], axis=0).astype(BF16)


def _expand_biases(bfs_ref, bs_ref, bf_ref, bm_ref):
    grow = lax.broadcasted_iota(jnp.int32, bf_ref.shape, 0)
    bf = jnp.zeros(bf_ref.shape, F32)
    for h in range(FOX_HEADS):
        bf = jnp.where((grow % FOX_HEADS == h) & (grow < N_PARTS * FOX_HEADS), bfs_ref[0, h], bf)
    bf_ref[...] = bf
    b = bs_ref[...]
    bt = jnp.concatenate([b, jnp.zeros((LANES - SGU_GROUPS, SGU_LEN), F32)], axis=0).T
    lane = lax.broadcasted_iota(jnp.int32, (SGU_LEN, LANES), 1)
    for jp in range(SGU_GROUPS // 2):
        ga, gb = [jnp.broadcast_to(bt[:, g:g + 1], (SGU_LEN, LANES)) for g in (2 * jp, 2 * jp + 1)]
        bm_ref[:, LANES * jp:LANES * (jp + 1)] = jnp.where(lane < HEAD_DIM, ga, gb)


def _pre_kernel(x_ref, g1_ref, wt_hbm, bfs_ref, lng_ref, lnb_ref, wm_ref, bs_ref,
                qa_ref, ka_ref, vt_ref, sb_ref, gs_ref,
                carry_ref, w_ref, wft_ref, stage_ref, gate_ref, sem, bf_ref, bm_ref,
                *, tiles_per_seq):
    i = pl.program_id(0)

    @pl.when(i == 0)
    def _():
        _load_weights(wt_hbm, w_ref, wft_ref, stage_ref, gate_ref, sem)
        _expand_biases(bfs_ref, bs_ref, bf_ref, bm_ref)

    @pl.when(i % tiles_per_seq == 0)
    def _():
        carry_ref[...] = jnp.zeros_like(carry_ref)

    s_i = lax.broadcasted_iota(jnp.int32, (SUB, SUB), 0)
    t_i = lax.broadcasted_iota(jnp.int32, (SUB, SUB), 1)
    tri = jnp.where(s_i <= t_i, 1.0, 0.0).astype(BF16)
    grow = lax.broadcasted_iota(jnp.int32, (GATE_ROWS, SUB), 0)
    head_pad = jnp.zeros((LANES - FOX_HEADS, SUB), F32)
    lane = lax.broadcasted_iota(jnp.int32, (SUB, LANES), 1)
    low = lane < HEAD_DIM
    ones_row = jnp.where(lax.broadcasted_iota(jnp.int32, (VT_ROWS - HEAD_DIM, SUB), 0) == 0,
                         1.0, 0.0)
    n_win = SUB // SGU_LEN
    wi = lax.broadcasted_iota(jnp.int32, (SGU_LEN, SGU_LEN), 0) // CHUNK
    wj = lax.broadcasted_iota(jnp.int32, (SGU_LEN, SGU_LEN), 1) // CHUNK
    wmask = wj <= wi
    lane_w = lax.broadcasted_iota(jnp.int32, (SGU_LEN, n_win * LANES), 1)
    low_w = (lane_w % LANES) < HEAD_DIM

    def proj(h, c0, width):
        return _dot(h, w_ref[:, c0:c0 + width])

    carry = carry_ref[:FOX_HEADS, :]
    for sub in range(TP // SUB):
        rows = slice(sub * SUB, (sub + 1) * SUB)
        kblk, kcols = divmod(sub * SUB, TQ)
        x = x_ref[rows, :]
        ms = jnp.mean(x * x, axis=-1, keepdims=True)
        h = (x * lax.rsqrt(ms + EPS) * g1_ref[...]).astype(BF16)

        z = lax.dot_general(wft_ref[...], h, (((1,), (1,)), ((), ())),
                            preferred_element_type=F32) + bf_ref[...]
        usv = proj(h, _U0, 2 * SGU_WIDTH)
        k_all = proj(h, _K0, FOX_WIDTH)

        logf = jnp.minimum(z, 0.0) - jnp.log(1.0 + jnp.exp(-jnp.abs(z)))
        hi, mid, lo = _split3(logf)
        part = jnp.where(grow < FOX_HEADS, hi, jnp.where(grow < 2 * FOX_HEADS, mid, lo))
        cl = _dot(part.astype(BF16), tri)
        ct = carry + (cl[:FOX_HEADS] + cl[FOX_HEADS:2 * FOX_HEADS]
                      + cl[2 * FOX_HEADS:3 * FOX_HEADS])
        carry = jnp.broadcast_to(ct[:, SUB - 1:SUB], ct.shape)
        c = jnp.concatenate([ct, head_pad], axis=0).T
        nc_parts = _split3(c * (-LOG2E))

        q_all = proj(h, _Q0, FOX_WIDTH) * (HEAD_DIM ** -0.5 * LOG2E)
        g_a = proj(h, _GA0, D_MODEL)
        v_all = proj(h, _V0, FOX_WIDTH)

        u = _gelu_tanh(usv[:, :SGU_WIDTH])
        sv = _gelu_tanh(usv[:, SGU_WIDTH:])
        mu = jnp.mean(sv, axis=-1, keepdims=True)
        xc = sv - mu
        var = jnp.mean(xc * xc, axis=-1, keepdims=True)
        svn = (xc * lax.rsqrt(var + EPS) * lng_ref[...] + lnb_ref[...]).astype(BF16)
        for jp in range(SGU_GROUPS // 2):
            sl = slice(LANES * jp, LANES * (jp + 1))
            chunk = svn[:, sl]
            rhs = jnp.concatenate(
                [chunk[SGU_LEN * w:SGU_LEN * (w + 1), :] for w in range(n_win)], axis=1)
            wa = jnp.where(wmask, wm_ref[2 * jp], 0.0).astype(BF16)
            wb = jnp.where(wmask, wm_ref[2 * jp + 1], 0.0).astype(BF16)
            zero = jnp.zeros_like(rhs)
            stacked = jnp.concatenate([jnp.where(low_w, rhs, zero), jnp.where(low_w, zero, rhs)],
                                      axis=0)
            mixed = _dot(jnp.concatenate([wa, wb], axis=1), stacked)
            mixed = mixed + jnp.concatenate([bm_ref[:, sl]] * n_win, axis=1)
            mixed = jnp.concatenate(
                [mixed[:, LANES * w:LANES * (w + 1)] for w in range(n_win)], axis=0)
            sb_ref[rows, sl] = (u[:, sl] * mixed).astype(BF16)
        gs_ref[rows, :D_MODEL] = g_a.astype(BF16)

        g_b = proj(h, _GB0, D_MODEL)
        for jp in range(FOX_HEADS // 2):
            sl = slice(LANES * jp, LANES * (jp + 1))
            qc, kc = q_all[:, sl], k_all[:, sl]
            vt = v_all[:, sl].T
            for par in range(2):
                hd = 2 * jp + par
                a0 = HEAD_DIM if par == 0 else 0
                data = low if par == 0 else jnp.logical_not(low)
                in_aug = (lane >= a0) & (lane < a0 + N_PARTS)
                qa = jnp.where(data, qc, jnp.where(in_aug, 1.0, 0.0))
                hi, mid, lo = [jnp.broadcast_to(p[:, hd:hd + 1], (SUB, LANES)) for p in nc_parts]
                aug = jnp.where(lane == a0, hi,
                                jnp.where(lane == a0 + 1, mid,
                                          jnp.where(lane == a0 + 2, lo, 0.0)))
                ka = jnp.where(data, kc, aug)
                qa_ref[0, hd, rows, :] = qa.astype(BF16)
                ka_ref[0, hd, rows, :] = ka.astype(BF16)
                vta = jnp.concatenate([vt[par * HEAD_DIM:(par + 1) * HEAD_DIM], ones_row], axis=0)
                vt_ref[0, hd, kblk, :, kcols:kcols + SUB] = vta.astype(BF16)
        gs_ref[rows, D_MODEL:] = g_b.astype(BF16)
    carry_ref[:FOX_HEADS, :] = carry


def _attn_kernel(qa_ref, ka_ref, vt_ref, *refs, n_q, n_cast):
    cast_in, (o_ref, *cast_out), (acc_ref, st_ref) = (
        refs[:n_cast], refs[n_cast:2 * n_cast + 1], refs[2 * n_cast + 1:])
    for src, dst in zip(cast_in, cast_out):
        dst[...] = src[...].astype(BF16)

    k_i = lax.broadcasted_iota(jnp.int32, (QC, QC), 0)
    q_i = lax.broadcasted_iota(jnp.int32, (QC, QC), 1)
    causal = k_i <= q_i
    halves = [slice(h * QC, (h + 1) * QC) for h in range(N_CHUNKS)]

    def needed(c, tile, blk, half):
        return not (blk == tile and half > c)

    def score_half(ch, tile, blk, half):
        hd, c = divmod(ch, N_CHUNKS)
        q0 = tile * TQ + c * QC
        k0 = blk * TQ + half * QC
        st = lax.dot_general(ka_ref[0, hd, pl.ds(k0, QC), :], qa_ref[0, hd, pl.ds(q0, QC), :],
                             (((1,), (1,)), ((), ())), preferred_element_type=F32)
        if blk == tile and half == c:
            st = jnp.where(causal, st, -jnp.inf)
        st_ref[ch, halves[half], :] = st
        return jnp.max(st, axis=0, keepdims=True)

    def chain_step(ch, tile, blk, m_old, cm, nxt):
        hd, c = divmod(ch, N_CHUNKS)
        m_new = cm if m_old is None else jnp.maximum(m_old, cm)
        pv, cm_next = [], []
        for half in range(N_CHUNKS):
            use = needed(c, tile, blk, half)
            if use:
                p = jnp.exp2(st_ref[ch, halves[half], :] - m_new).astype(BF16)
            if nxt is not None and needed(c, *nxt, half):
                cm_next.append(score_half(ch, *nxt, half))
            if use:
                pv.append(_dot(vt_ref[0, hd, blk, :, halves[half]], p))
        pv = sum(pv[1:], pv[0])
        if m_old is None:
            acc_ref[ch] = pv
        else:
            acc_ref[ch] = acc_ref[ch] * jnp.exp2(m_old - m_new) + pv
        return m_new, (functools.reduce(jnp.maximum, cm_next) if cm_next else None)

    steps = [(qi, j) for qi in range(n_q) for j in range(qi + 1)]
    ms = [None] * N_CHAINS
    cms = [functools.reduce(jnp.maximum,
                            [score_half(ch, *steps[0], half) for half in range(N_CHUNKS)
                             if needed(ch % N_CHUNKS, *steps[0], half)])
           for ch in range(N_CHAINS)]
    for n, (qi, j) in enumerate(steps):
        nxt = steps[n + 1] if n + 1 < len(steps) else None
        for ch in range(N_CHAINS):
            ms[ch], cms[ch] = chain_step(ch, qi, j, ms[ch], cms[ch], nxt)
        if j == qi:
            for c in range(N_CHUNKS):
                chains = [N_CHUNKS * hd + c for hd in range(PAIR)]
                o_t = jnp.concatenate(
                    [acc_ref[ch, :HEAD_DIM, :] / acc_ref[ch, HEAD_DIM:HEAD_DIM + 1, :]
                     for ch in chains], axis=0)
                o_ref[0, pl.ds(qi * TQ + c * QC, QC), :] = o_t.T.astype(BF16)
            ms = [None] * N_CHAINS


def _post_kernel(x_ref, at_ref, sb_ref, gs_ref, wa_ref, wb_ref, wo_ref, g2_ref, wu_ref,
                 wd_ref, gf_ref, o_ref, *, ff_chunk):
    subs = [slice(s * SUB, (s + 1) * SUB) for s in range(TM // SUB)]
    merged = []
    for rows in subs:
        ya = _dot(at_ref[rows, :], wa_ref[...])
        yb = _dot(sb_ref[rows, :], wb_ref[...])
        merged.append((_sigmoid(gs_ref[rows, :D_MODEL].astype(F32)) * ya
                       + _sigmoid(gs_ref[rows, D_MODEL:].astype(F32)) * yb).astype(BF16))
    ys = [x_ref[rows, :] + _dot(m, wo_ref[...]) for rows, m in zip(subs, merged)]
    hs = []
    for y in ys:
        ms = jnp.mean(y * y, axis=-1, keepdims=True)
        hs.append((y * lax.rsqrt(ms + EPS) * g2_ref[...]).astype(BF16))
    for c0 in range(0, D_FF, ff_chunk):
        acts = []
        for h in hs:
            a = jnp.maximum(_dot(h, wu_ref[:, c0:c0 + ff_chunk]), 0.0)
            acts.append((a * a).astype(BF16))
        ys = [y + _dot(a, wd_ref[c0:c0 + ff_chunk, :]) for y, a in zip(ys, acts)]
    for rows, y in zip(subs, ys):
        ms = jnp.mean(y * y, axis=-1, keepdims=True)
        o_ref[rows, :] = y * lax.rsqrt(ms + EPS) * gf_ref[...]


def _params(n_axes, ordered=False):
    semantics = "arbitrary" if ordered else "parallel"
    return pltpu.CompilerParams(dimension_semantics=(semantics,) * n_axes,
                                vmem_limit_bytes=VMEM_LIMIT)


def _const(shape):
    return pl.BlockSpec(shape, lambda *_: (0,) * len(shape), pipeline_mode=pl.Buffered(1))


def _rows(width, rows=TM):
    return pl.BlockSpec((rows, width), lambda i: (i, 0))


def _pre(xt, bsz, seq, g1, wt, b_f, ln_g, ln_b, w_sgu, b_sgu):
    n_tok, d = xt.shape
    assert wt.shape == (_W_COLS + FOX_HEADS, d) and _F0 % PREP_COLS == 0 and _W_COLS % PREP_COLS == 0
    tiles_per_seq = seq // TP
    head_map = lambda i: (i // tiles_per_seq, 0, i % tiles_per_seq, 0)
    return pl.pallas_call(
        functools.partial(_pre_kernel, tiles_per_seq=tiles_per_seq),
        grid=(n_tok // TP,),
        in_specs=[
            _rows(d, TP),
            _const((1, d)),
            pl.BlockSpec(memory_space=pl.ANY),
            pl.BlockSpec(memory_space=pltpu.SMEM),
            _const((1, SGU_WIDTH)),
            _const((1, SGU_WIDTH)),
            _const((SGU_GROUPS, SGU_LEN, SGU_LEN)),
            _const((SGU_GROUPS, SGU_LEN)),
        ],
        out_specs=[
            pl.BlockSpec((1, FOX_HEADS, TP, LANES), head_map),
            pl.BlockSpec((1, FOX_HEADS, TP, LANES), head_map),
            pl.BlockSpec((1, FOX_HEADS, TP // TQ, VT_ROWS, TQ), lambda i: head_map(i) + (0,)),
            _rows(SGU_WIDTH, TP),
            _rows(2 * d, TP),
        ],
        out_shape=[
            jax.ShapeDtypeStruct((bsz, FOX_HEADS, seq, LANES), BF16),
            jax.ShapeDtypeStruct((bsz, FOX_HEADS, seq, LANES), BF16),
            jax.ShapeDtypeStruct((bsz, FOX_HEADS, seq // TQ, VT_ROWS, TQ), BF16),
            jax.ShapeDtypeStruct((n_tok, SGU_WIDTH), BF16),
            jax.ShapeDtypeStruct((n_tok, 2 * d), BF16),
        ],
        scratch_shapes=[pltpu.VMEM((2 * FOX_HEADS, SUB), F32),
                        pltpu.VMEM((d, _W_COLS), BF16),
                        pltpu.VMEM((GATE_ROWS, d), BF16),
                        pltpu.VMEM((2, PREP_COLS, d), F32),
                        pltpu.VMEM((FOX_HEADS, d), F32),
                        pltpu.SemaphoreType.DMA((3,)),
                        pltpu.VMEM((GATE_ROWS, SUB), F32),
                        pltpu.VMEM((SGU_LEN, SGU_WIDTH), F32)],
        compiler_params=_params(1, ordered=True),
        name="pre",
    )(xt, g1, wt, b_f, ln_g, ln_b, w_sgu, b_sgu)


def _attn(qa, ka, vt, weights):
    bsz, _, seq, _ = qa.shape
    pairs = FOX_HEADS // PAIR
    n_steps = bsz * pairs
    assert all(w.shape[0] % (n_steps * BF16_SUBLANES) == 0 for w in weights)
    slab = lambda w: pl.BlockSpec((w.shape[0] // n_steps, w.shape[1]),
                                  lambda b, p: (b * pairs + p, 0))
    att, *cast = pl.pallas_call(
        functools.partial(_attn_kernel, n_q=seq // TQ, n_cast=len(weights)),
        grid=(bsz, pairs),
        in_specs=[
            pl.BlockSpec((1, PAIR, seq, LANES), lambda b, p: (b, p, 0, 0)),
            pl.BlockSpec((1, PAIR, seq, LANES), lambda b, p: (b, p, 0, 0)),
            pl.BlockSpec((1, PAIR, seq // TQ, VT_ROWS, TQ), lambda b, p: (b, p, 0, 0, 0)),
        ] + [slab(w) for w in weights],
        out_specs=[pl.BlockSpec((1, seq, LANES), lambda b, p: (b, 0, p))]
        + [slab(w) for w in weights],
        out_shape=[jax.ShapeDtypeStruct((bsz, seq, FOX_WIDTH), BF16)]
        + [jax.ShapeDtypeStruct(w.shape, BF16) for w in weights],
        scratch_shapes=[pltpu.VMEM((N_CHAINS, VT_ROWS, QC), F32),
                        pltpu.VMEM((N_CHAINS, TQ, QC), F32)],
        compiler_params=_params(2),
        name="attn",
    )(qa, ka, vt, *weights)
    return att, cast


def _post(xt, att, sb, gs, w_a, w_b, w_o, g2, w_up, w_down, gf):
    n_tok, d = xt.shape
    return pl.pallas_call(
        functools.partial(_post_kernel, ff_chunk=1024),
        grid=(n_tok // TM,),
        in_specs=[
            _rows(d),
            _rows(FOX_WIDTH),
            _rows(SGU_WIDTH),
            _rows(2 * d),
            _const((FOX_WIDTH, d)),
            _const((SGU_WIDTH, d)),
            _const((d, d)),
            _const((1, d)),
            _const((d, D_FF)),
            _const((D_FF, d)),
            _const((1, d)),
        ],
        out_specs=_rows(d),
        out_shape=jax.ShapeDtypeStruct((n_tok, d), F32),
        compiler_params=_params(1),
        name="post",
    )(xt, att, sb, gs, w_a, w_b, w_o, g2, w_up, w_down, gf)


def kernel(x, norm1_g, w_in, b_f, ln_v_g, ln_v_b, w_sgu, b_sgu, w_a, w_b, w_o,
           norm2_g, w_up, w_down, normf_g):
    bsz, seq, d = x.shape
    assert d == D_MODEL and seq % TP == 0 and TP % TQ == 0 and TQ % SUB == 0 and TM % SUB == 0
    assert norm1_g.shape[0] == 1, "single-layer block"
    n_tok = bsz * seq
    xt = x.reshape(n_tok, d)

    wt = jnp.swapaxes(w_in[0], 0, 1)

    qa, ka, vt, sb, gs = _pre(xt, bsz, seq, norm1_g.reshape(1, d), wt, b_f,
                              ln_v_g.reshape(1, SGU_WIDTH), ln_v_b.reshape(1, SGU_WIDTH),
                              w_sgu[0], b_sgu[0])
    att, (wa16, wb16, wo16, wu16, wd16) = _attn(
        qa, ka, vt, [w_a[0], w_b[0], w_o[0], w_up[0], w_down[0]])
    out = _post(xt, att.reshape(n_tok, FOX_WIDTH), sb, gs, wa16, wb16, wo16,
                norm2_g.reshape(1, d), wu16, wd16, normf_g.reshape(1, d))
    return out.reshape(bsz, seq, d)
```
